```python
import jax, jax.numpy as jnp
from jax import lax
import numpy as np

D_MODEL = 1024
BATCH = 8
SEQ = 8192
DEPTH = 1
DEC_BATCH = 8
DEC_SEQ = 2048
PAST_LEN = 128

D_MIX = D_MODEL
POOL_WIDTH = D_MIX // 2
POOL_WINDOWS = (2, 4, 8, 16)
N_POOL_GROUPS = len(POOL_WINDOWS)
POOL_GROUP = POOL_WIDTH // N_POOL_GROUPS
N_HEADS = 8
V_HEAD_DIM = (D_MIX - POOL_WIDTH) // N_HEADS
QK_NOPE_DIM = 64
QK_ROPE_DIM = 32
QK_HEAD_DIM = QK_NOPE_DIM + QK_ROPE_DIM
Q_LORA_RANK = 256
KV_LORA_RANK = 128
IN_COLS = POOL_WIDTH + Q_LORA_RANK + KV_LORA_RANK + QK_ROPE_DIM
ROPE_THETA = 10000.0
Q_BLOCK = 128
N_EXPERTS = 64
TOP_K = 6
N_GROUPS = 8
TOPK_GROUPS = 4
EXPERT_FF = 256
SHARED_FF = 256
ROUTED_SCALE = 2.5
EXPERT_BLOCK = 256
EPS = 1e-6

kernel_name = "hybrid_pool_mla_moe_adaln_encoder"


def rmsnorm(x, w):
    xf = x.astype(jnp.float32)
    return xf * lax.rsqrt(jnp.mean(xf * xf, axis=-1, keepdims=True) + EPS) * w.astype(jnp.float32)


def rope_tail(t, pos):
    half = QK_ROPE_DIM // 2
    inv_freq = ROPE_THETA ** (-jnp.arange(half, dtype=jnp.float32) / half)
    ang = pos[:, None] * inv_freq[None, :]
    cos = jnp.cos(ang)[None, :, None, :]
    sin = jnp.sin(ang)[None, :, None, :]
    t_nope, t1, t2 = jnp.split(t, [QK_NOPE_DIM, QK_NOPE_DIM + half], axis=-1)
    return jnp.concatenate([t_nope, t1 * cos - t2 * sin, t1 * sin + t2 * cos], axis=-1)


def pool_mixer(u, pool_w, pool_scale):
    B, S, _ = u.shape
    ug = u.reshape(B, S, N_POOL_GROUPS, POOL_GROUP)
    cs = jnp.concatenate([jnp.zeros((B, 1, N_POOL_GROUPS, POOL_GROUP), jnp.float32),
                          jnp.cumsum(ug, axis=1)], axis=1)
    win = jnp.array(POOL_WINDOWS, jnp.int32)
    left = win // 2
    right = win - 1 - left
    p = jnp.arange(S, dtype=jnp.int32)[:, None]
    hi = jnp.minimum(p + right[None, :] + 1, S)
    lo = jnp.maximum(p - left[None, :], 0)
    gi = jnp.arange(N_POOL_GROUPS)[None, :]
    cnt = (hi - lo).astype(jnp.float32)
    mean = (cs[:, hi, gi] - cs[:, lo, gi]) / cnt[None, :, :, None]
    d = mean - ug
    out = jnp.einsum('bsgc,gcd->bsgd', d, pool_w).reshape(B, S, POOL_WIDTH)
    return out * pool_scale


def blocked_attention(q, k, v):
    B, S, H, Dq = q.shape
    nb = S // Q_BLOCK
    qb = q.reshape(B, nb, Q_BLOCK, H, Dq).transpose(1, 0, 2, 3, 4)
    scale = QK_HEAD_DIM ** -0.5

    def one_block(qblk):
        s = jnp.einsum('bqhd,bkhd->bhqk', qblk, k) * scale
        pr = jax.nn.softmax(s.astype(jnp.float32), axis=-1)
        return jnp.einsum('bhqk,bkhd->bqhd', pr, v)

    o = lax.map(one_block, qb)
    return o.transpose(1, 0, 2, 3, 4).reshape(B, S, H * V_HEAD_DIM)


def token_mix(h, pos, w_in, pool_w, pool_scale, q_a_norm_w, w_q_b, kv_a_norm_w, w_kv_b,
              q_norm_w, k_norm_w, w_o):
    B, S, _ = h.shape
    z = h @ w_in
    u, cq, ckv, kpe = jnp.split(z, [POOL_WIDTH, POOL_WIDTH + Q_LORA_RANK,
                                    POOL_WIDTH + Q_LORA_RANK + KV_LORA_RANK], axis=-1)
    pool_out = pool_mixer(u, pool_w, pool_scale)
    q = (rmsnorm(cq, q_a_norm_w) @ w_q_b).reshape(B, S, N_HEADS, QK_HEAD_DIM)
    kv = (rmsnorm(ckv, kv_a_norm_w) @ w_kv_b).reshape(B, S, N_HEADS, QK_NOPE_DIM + V_HEAD_DIM)
    k_nope, v = jnp.split(kv, [QK_NOPE_DIM], axis=-1)
    k = jnp.concatenate([k_nope, jnp.broadcast_to(kpe[:, :, None, :], (B, S, N_HEADS, QK_ROPE_DIM))], axis=-1)
    q = rope_tail(rmsnorm(q, q_norm_w), pos)
    k = rope_tail(rmsnorm(k, k_norm_w), pos)
    attn_out = blocked_attention(q, k, v)
    return jnp.concatenate([pool_out, attn_out], axis=-1) @ w_o


def route(t, w_router, router_bias):
    T = t.shape[0]
    scores = jax.nn.sigmoid(t.astype(jnp.float32) @ w_router.astype(jnp.float32))
    choice = scores + router_bias
    grp = choice.reshape(T, N_GROUPS, N_EXPERTS // N_GROUPS)
    grp_score = lax.top_k(grp, 2)[0].sum(-1)
    _, top_g = lax.top_k(grp_score, TOPK_GROUPS)
    gmask = jnp.any(top_g[:, :, None] == jnp.arange(N_GROUPS)[None, None, :], axis=1)
    emask = jnp.repeat(gmask, N_EXPERTS // N_GROUPS, axis=-1)
    masked = jnp.where(emask, choice, -jnp.inf)
    _, idx = lax.top_k(masked, TOP_K)
    wts = jnp.take_along_axis(scores, idx, axis=-1)
    wts = wts / jnp.sum(wts, axis=-1, keepdims=True) * ROUTED_SCALE
    return idx, wts


def routed_experts(t, idx, wts, w_gate, w_up, w_down):
    T, D = t.shape
    A = T * TOP_K
    flat_e = idx.reshape(A)
    order = jnp.argsort(flat_e)
    se = flat_e[order]
    counts = jnp.bincount(flat_e, length=N_EXPERTS)
    padded = (counts + EXPERT_BLOCK - 1) // EXPERT_BLOCK * EXPERT_BLOCK
    pad_end = jnp.cumsum(padded)
    pad_start = pad_end - padded
    start = jnp.cumsum(counts) - counts
    dest = pad_start[se] + jnp.arange(A) - start[se]
    n_blocks = -(-A // EXPERT_BLOCK) + N_EXPERTS
    n_rows = n_blocks * EXPERT_BLOCK
    row_tok = jnp.full((n_rows,), T, jnp.int32).at[dest].set((order // TOP_K).astype(jnp.int32))
    row_w = jnp.zeros((n_rows,), jnp.float32).at[dest].set(wts.reshape(A)[order])
    blk_exp = jnp.minimum(jnp.searchsorted(pad_end, jnp.arange(n_blocks) * EXPERT_BLOCK, side='right'),
                          N_EXPERTS - 1)
    t_pad = jnp.concatenate([t, jnp.zeros((1, D), t.dtype)], axis=0)

    def step(acc, inp):
        e, toks, w = inp
        xb = t_pad[toks]
        hb = jax.nn.silu(xb @ w_gate[e]) * (xb @ w_up[e])
        yb = (hb @ w_down[e]).astype(jnp.float32)
        return acc.at[toks].add(yb * w[:, None]), None

    acc, _ = lax.scan(step, jnp.zeros((T + 1, D), jnp.float32),
                      (blk_exp, row_tok.reshape(n_blocks, EXPERT_BLOCK), row_w.reshape(n_blocks, EXPERT_BLOCK)))
    return acc[:T]


def moe(h, w_router, router_bias, w_gate, w_up, w_down, ws_gate, ws_up, ws_down):
    B, S, D = h.shape
    t = h.reshape(B * S, D)
    idx, wts = route(t, w_router, router_bias)
    routed = routed_experts(t, idx, wts, w_gate, w_up, w_down)
    shared = (jax.nn.silu(t @ ws_gate) * (t @ ws_up)) @ ws_down
    return (routed + shared).reshape(B, S, D)


def layer(x, c, pos, w_ada, b_ada, norm1_w, w_in, pool_w, pool_scale, q_a_norm_w, w_q_b,
          kv_a_norm_w, w_kv_b, q_norm_w, k_norm_w, w_o, norm2_w, w_router, router_bias,
          w_gate, w_up, w_down, ws_gate, ws_up, ws_down):
    B = x.shape[0]
    mod = (jax.nn.silu(c) @ w_ada + b_ada).reshape(B, 6, 1, D_MODEL)
    shift1, scale1, gate1, shift2, scale2, gate2 = [mod[:, i] for i in range(6)]
    h = rmsnorm(x, norm1_w) * (1.0 + scale1) + shift1
    x = x + gate1 * token_mix(h, pos, w_in, pool_w, pool_scale, q_a_norm_w, w_q_b, kv_a_norm_w,
                              w_kv_b, q_norm_w, k_norm_w, w_o)
    h2 = rmsnorm(x, norm2_w) * (1.0 + scale2) + shift2
    x = x + gate2 * moe(h2, w_router, router_bias, w_gate, w_up, w_down, ws_gate, ws_up, ws_down)
    return x


def encoder(x_in, c, weights):
    x = x_in.astype(jnp.float32)
    c = c.astype(jnp.float32)
    pos = jnp.arange(x.shape[1], dtype=jnp.float32)
    for l in range(DEPTH):
        x = layer(x, c, pos, *[w[l] for w in weights])
    return x.astype(x_in.dtype)


def setup_inputs(seed: int = 0) -> dict:
    key = jax.random.key(seed)
    ks = jax.random.split(key, 32)
    L = DEPTH
    nrm = lambda k, shape, s: jax.random.normal(k, shape, jnp.float32) * s
    gain = lambda k, shape: 1.0 + 0.02 * jax.random.normal(k, shape, jnp.float32)
    return {
        "x_prompt": nrm(ks[0], (BATCH, SEQ, D_MODEL), 1.0),
        "x_sample": nrm(ks[1], (DEC_BATCH, DEC_SEQ, D_MODEL), 1.0),
        "c_prompt": nrm(ks[2], (BATCH, D_MODEL), 1.0),
        "c_sample": nrm(ks[3], (DEC_BATCH, D_MODEL), 1.0),
        "w_ada": nrm(ks[4], (L, D_MODEL, 6 * D_MODEL), 0.5 * D_MODEL ** -0.5),
        "b_ada": nrm(ks[5], (L, 6 * D_MODEL), 0.02),
        "norm1_w": gain(ks[6], (L, D_MODEL)),
        "w_in": nrm(ks[7], (L, D_MODEL, IN_COLS), D_MODEL ** -0.5),
        "pool_w": nrm(ks[8], (L, N_POOL_GROUPS, POOL_GROUP, POOL_GROUP), POOL_GROUP ** -0.5),
        "pool_scale": gain(ks[9], (L, POOL_WIDTH)),
        "q_a_norm_w": gain(ks[10], (L, Q_LORA_RANK)),
        "w_q_b": nrm(ks[11], (L, Q_LORA_RANK, N_HEADS * QK_HEAD_DIM), Q_LORA_RANK ** -0.5),
        "kv_a_norm_w": gain(ks[12], (L, KV_LORA_RANK)),
        "w_kv_b": nrm(ks[13], (L, KV_LORA_RANK, N_HEADS * (QK_NOPE_DIM + V_HEAD_DIM)), KV_LORA_RANK ** -0.5),
        "q_norm_w": gain(ks[14], (L, QK_HEAD_DIM)),
        "k_norm_w": gain(ks[15], (L, QK_HEAD_DIM)),
        "w_o": nrm(ks[16], (L, D_MIX, D_MODEL), D_MIX ** -0.5),
        "norm2_w": gain(ks[17], (L, D_MODEL)),
        "w_router": nrm(ks[18], (L, D_MODEL, N_EXPERTS), D_MODEL ** -0.5),
        "router_bias": nrm(ks[19], (L, N_EXPERTS), 0.01),
        "w_gate": nrm(ks[20], (L, N_EXPERTS, D_MODEL, EXPERT_FF), D_MODEL ** -0.5),
        "w_up": nrm(ks[21], (L, N_EXPERTS, D_MODEL, EXPERT_FF), D_MODEL ** -0.5),
        "w_down": nrm(ks[22], (L, N_EXPERTS, EXPERT_FF, D_MODEL), EXPERT_FF ** -0.5),
        "ws_gate": nrm(ks[23], (L, D_MODEL, SHARED_FF), D_MODEL ** -0.5),
        "ws_up": nrm(ks[24], (L, D_MODEL, SHARED_FF), D_MODEL ** -0.5),
        "ws_down": nrm(ks[25], (L, SHARED_FF, D_MODEL), SHARED_FF ** -0.5),
    }


def reference(x_prompt, x_sample, c_prompt, c_sample, w_ada, b_ada, norm1_w, w_in, pool_w, pool_scale,
              q_a_norm_w, w_q_b, kv_a_norm_w, w_kv_b, q_norm_w, k_norm_w, w_o, norm2_w, w_router,
              router_bias, w_gate, w_up, w_down, ws_gate, ws_up, ws_down):
    weights = (w_ada, b_ada, norm1_w, w_in, pool_w, pool_scale, q_a_norm_w, w_q_b, kv_a_norm_w, w_kv_b,
               q_norm_w, k_norm_w, w_o, norm2_w, w_router, router_bias, w_gate, w_up, w_down,
               ws_gate, ws_up, ws_down)
    y_prompt = encoder(x_prompt, c_prompt, weights)
    y_sample = encoder(x_sample, c_sample, weights)
    return (y_prompt, y_sample)
```

```python
import functools

import jax
import jax.numpy as jnp
from jax import lax
from jax.experimental import pallas as pl
from jax.experimental.pallas import tpu as pltpu

D_MODEL = 1024
POOL_WIDTH = 512
POOL_WINDOWS = (2, 4, 8, 16)
POOL_GROUP = 128
N_HEADS = 8
V_HEAD_DIM = 64
QK_NOPE_DIM = 64
QK_ROPE_DIM = 32
QK_HEAD_DIM = 96
Q_LORA_RANK = 256
KV_LORA_RANK = 128
ROPE_THETA = 10000.0
N_EXPERTS = 64
TOP_K = 6
N_GROUPS = 8
TOPK_GROUPS = 4
EXPERT_FF = 256
SHARED_FF = 256
ROUTED_SCALE = 2.5
EPS = 1e-6

LANES = 128
HEAD_PAD = 128
HALO = 16
ROW_BLOCK = 256
TOPK_PAD = 8
VMEM_LIMIT = 48 * 1024 * 1024

F32 = jnp.float32
BF16 = jnp.bfloat16


def _cparams(sem):
    return pltpu.CompilerParams(dimension_semantics=sem, vmem_limit_bytes=VMEM_LIMIT)


def _silu(x):
    return x * (1.0 / (1.0 + jnp.exp(-x)))


def _adaln_kernel(c_ref, w_ref, b_ref, o_ref):
    c = c_ref[...]
    o_ref[...] = jnp.dot(_silu(c), w_ref[...], preferred_element_type=F32,
                         precision=lax.Precision.HIGHEST) + b_ref[...]


def _adaln(c, w_ada, b_ada):
    nb, d = c.shape
    n = w_ada.shape[1]
    tn = 1536
    return pl.pallas_call(
        _adaln_kernel,
        grid=(n // tn,),
        in_specs=[pl.BlockSpec((nb, d), lambda j: (0, 0)),
                  pl.BlockSpec((d, tn), lambda j: (0, j)),
                  pl.BlockSpec((1, tn), lambda j: (0, j))],
        out_specs=pl.BlockSpec((nb, tn), lambda j: (0, j)),
        out_shape=jax.ShapeDtypeStruct((nb, n), F32),
        compiler_params=_cparams(("arbitrary",)),
        name="adaln",
    )(c, w_ada, b_ada.reshape(1, n))


def _head_norm_rope(t, gain, cos, sin_a, sin_b):
    r = lax.rsqrt(jnp.sum(t * t, axis=-1, keepdims=True) * (1.0 / QK_HEAD_DIM) + EPS)
    tn = t * r * gain
    return (tn * cos + pltpu.roll(tn, HEAD_PAD - QK_ROPE_DIM // 2, 1) * sin_a
            + pltpu.roll(tn, QK_ROPE_DIM // 2, 1) * sin_b)


def _inproj_kernel(x_ref, mod_ref, n1w_ref, win_ref, qan_ref, wq_ref, kvan_ref, wk_ref, wv_ref,
                   qnw_ref, knw_ref, cos_ref, sa_ref, sb_ref, u_ref, q_ref, k_ref, v_ref):
    x = x_ref[0]
    shift1 = mod_ref[0, 0:1, :]
    scale1 = mod_ref[0, 1:2, :]
    r = lax.rsqrt(jnp.mean(x * x, axis=-1, keepdims=True) + EPS)
    h = x * r * n1w_ref[...] * (1.0 + scale1) + shift1
    z = jnp.dot(h.astype(BF16), win_ref[...], preferred_element_type=F32)
    u_ref[0] = z[:, :POOL_WIDTH].astype(BF16)

    cq = z[:, POOL_WIDTH:POOL_WIDTH + Q_LORA_RANK]
    cqn = cq * lax.rsqrt(jnp.mean(cq * cq, axis=-1, keepdims=True) + EPS) * qan_ref[...]
    q = jnp.dot(cqn.astype(BF16), wq_ref[...], preferred_element_type=F32)

    c0 = POOL_WIDTH + Q_LORA_RANK
    ckv = z[:, c0:c0 + KV_LORA_RANK]
    ckvn = (ckv * lax.rsqrt(jnp.mean(ckv * ckv, axis=-1, keepdims=True) + EPS)
            * kvan_ref[...]).astype(BF16)
    kk = jnp.dot(ckvn, wk_ref[...], preferred_element_type=F32)
    v_ref[0] = jnp.dot(ckvn, wv_ref[...], preferred_element_type=F32).astype(BF16)
    kpe = z[:, c0 + KV_LORA_RANK:]

    cos = cos_ref[...]
    sa = sa_ref[...]
    sb = sb_ref[...]
    qg = qnw_ref[...]
    kg = knw_ref[...]
    for hd in range(N_HEADS):
        sl = slice(hd * HEAD_PAD, (hd + 1) * HEAD_PAD)
        q_ref[0, :, sl] = _head_norm_rope(q[:, sl], qg, cos, sa, sb).astype(BF16)
        k_ref[0, :, sl] = _head_norm_rope(kk[:, sl] + kpe, kg, cos, sa, sb).astype(BF16)


def _inproj(x, mod, p, ts):
    b, s, d = x.shape
    qk_w = N_HEADS * HEAD_PAD
    const = lambda shape: pl.BlockSpec(shape, lambda bi, i: (0,) * len(shape))
    row_tab = pl.BlockSpec((ts, HEAD_PAD), lambda bi, i: (i, 0))
    return pl.pallas_call(
        _inproj_kernel,
        grid=(b, s // ts),
        in_specs=[pl.BlockSpec((1, ts, d), lambda bi, i: (bi, i, 0)),
                  pl.BlockSpec((1, 6, d), lambda bi, i: (bi, 0, 0)),
                  const((1, d)), const((d, d)), const((1, Q_LORA_RANK)),
                  const((Q_LORA_RANK, qk_w)), const((1, KV_LORA_RANK)),
                  const((KV_LORA_RANK, qk_w)), const((KV_LORA_RANK, N_HEADS * V_HEAD_DIM)),
                  const((1, HEAD_PAD)), const((1, HEAD_PAD)), row_tab, row_tab, row_tab],
        out_specs=[pl.BlockSpec((1, ts, POOL_WIDTH), lambda bi, i: (bi, i, 0)),
                   pl.BlockSpec((1, ts, qk_w), lambda bi, i: (bi, i, 0)),
                   pl.BlockSpec((1, ts, qk_w), lambda bi, i: (bi, i, 0)),
                   pl.BlockSpec((1, ts, N_HEADS * V_HEAD_DIM), lambda bi, i: (bi, i, 0))],
        out_shape=[jax.ShapeDtypeStruct((b, s, POOL_WIDTH), BF16),
                   jax.ShapeDtypeStruct((b, s, qk_w), BF16),
                   jax.ShapeDtypeStruct((b, s, qk_w), BF16),
                   jax.ShapeDtypeStruct((b, s, N_HEADS * V_HEAD_DIM), BF16)],
        compiler_params=_cparams(("parallel", "parallel")),
        name="inproj",
    )(x, mod, p["norm1_w"], p["w_in"], p["q_a_norm_w"], p["w_q"], p["kv_a_norm_w"], p["w_k"],
      p["w_v"], p["q_norm_w"], p["k_norm_w"], p["rope_cos"], p["rope_sa"], p["rope_sb"])


def _attn_kernel(q_ref, k_ref, v_ref, o_ref, *, tk, nk):
    tq = q_ref.shape[1]
    q0 = q_ref[0, :, :HEAD_PAD]
    q1 = q_ref[0, :, HEAD_PAD:]
    first = lax.broadcasted_iota(jnp.int32, (1, 2 * V_HEAD_DIM), 1) < V_HEAD_DIM
    nt = (((1,), (1,)), ((), ()))

    def body(c, carry):
        m0, l0, m1, l1, acc = carry
        off = pl.multiple_of(c * tk, tk)
        ks = k_ref[0, pl.ds(off, tk), :]
        vs = v_ref[0, pl.ds(off, tk), :]
        zero = jnp.zeros_like(vs)
        v0 = jnp.where(first, vs, zero)
        v1 = jnp.where(first, zero, vs)
        s0 = lax.dot_general(q0, ks[:, :HEAD_PAD], nt, preferred_element_type=F32)
        s1 = lax.dot_general(q1, ks[:, HEAD_PAD:], nt, preferred_element_type=F32)
        m0n = jnp.maximum(m0, jnp.max(s0, axis=-1, keepdims=True))
        m1n = jnp.maximum(m1, jnp.max(s1, axis=-1, keepdims=True))
        a0 = jnp.exp(m0 - m0n)
        a1 = jnp.exp(m1 - m1n)
        p0 = jnp.exp(s0 - m0n)
        p1 = jnp.exp(s1 - m1n)
        l0 = a0 * l0 + jnp.sum(p0, axis=-1, keepdims=True)
        l1 = a1 * l1 + jnp.sum(p1, axis=-1, keepdims=True)
        pv = (jnp.dot(p0.astype(BF16), v0, preferred_element_type=F32)
              + jnp.dot(p1.astype(BF16), v1, preferred_element_type=F32))
        acc = acc * jnp.where(first, a0, a1) + pv
        return m0n, l0, m1n, l1, acc

    neg = jnp.full((tq, 1), -jnp.inf, F32)
    zl = jnp.zeros((tq, 1), F32)
    m0, l0, m1, l1, acc = lax.fori_loop(
        0, nk, body, (neg, zl, neg, zl, jnp.zeros((tq, 2 * V_HEAD_DIM), F32)))
    o_ref[0] = (acc * jnp.where(first, 1.0 / l0, 1.0 / l1)).astype(BF16)


def _attention(q, k, v, tq, tk):
    b, s, _ = q.shape
    return pl.pallas_call(
        functools.partial(_attn_kernel, tk=tk, nk=s // tk),
        grid=(b, N_HEADS // 2, s // tq),
        in_specs=[pl.BlockSpec((1, tq, 2 * HEAD_PAD), lambda bi, j, i: (bi, i, j)),
                  pl.BlockSpec((1, s, 2 * HEAD_PAD), lambda bi, j, i: (bi, 0, j)),
                  pl.BlockSpec((1, s, 2 * V_HEAD_DIM), lambda bi, j, i: (bi, 0, j))],
        out_specs=pl.BlockSpec((1, tq, 2 * V_HEAD_DIM), lambda bi, j, i: (bi, i, j)),
        out_shape=jax.ShapeDtypeStruct((b, s, N_HEADS * V_HEAD_DIM), BF16),
        compiler_params=_cparams(("parallel", "parallel", "parallel")),
        name="attn",
    )(q, k, v)


def _postmix_kernel(u_ref, up_ref, un_ref, a_ref, x_ref, mod_ref, pw_ref, ps_ref, wo_ref, n2w_ref,
                    wrh_ref, wrl_ref, x1_ref, h2_ref, lg_ref, *, seq):
    i = pl.program_id(1)
    ts = u_ref.shape[1]
    ext_rows = ts + 2 * HALO
    ext = jnp.concatenate([up_ref[0], u_ref[0], un_ref[0]], axis=0).astype(F32)
    pos = i * ts - HALO + lax.broadcasted_iota(jnp.int32, (ext_rows, 1), 0)
    ext = jnp.where((pos >= 0) & (pos < seq), ext, 0.0)
    p = i * ts + lax.broadcasted_iota(jnp.int32, (ts, 1), 0)

    outs = []
    for g, w in enumerate(POOL_WINDOWS):
        left = w // 2
        right = w - 1 - left
        t = ext[:, g * POOL_GROUP:(g + 1) * POOL_GROUP]
        step = 1
        while step < w:
            t = t + pltpu.roll(t, ext_rows - step, 0)
            step *= 2
        win = pltpu.roll(t, left, 0)[HALO:HALO + ts]
        cnt = (jnp.minimum(p + right + 1, seq) - jnp.maximum(p - left, 0)).astype(F32)
        d = win / cnt - ext[HALO:HALO + ts, g * POOL_GROUP:(g + 1) * POOL_GROUP]
        outs.append(jnp.dot(d.astype(BF16), pw_ref[g], preferred_element_type=F32))
    pool = (jnp.concatenate(outs, axis=-1) * ps_ref[...]).astype(BF16)

    mix = (jnp.dot(pool, wo_ref[:POOL_WIDTH, :], preferred_element_type=F32)
           + jnp.dot(a_ref[0], wo_ref[POOL_WIDTH:, :], preferred_element_type=F32))
    x1 = x_ref[0] + mod_ref[0, 2:3, :] * mix
    x1_ref[0] = x1
    r = lax.rsqrt(jnp.mean(x1 * x1, axis=-1, keepdims=True) + EPS)
    h2 = x1 * r * n2w_ref[...] * (1.0 + mod_ref[0, 4:5, :]) + mod_ref[0, 3:4, :]
    h2_ref[0] = h2
    hi = h2.astype(BF16)
    lo = (h2 - hi.astype(F32)).astype(BF16)
    nt = (((1,), (1,)), ((), ()))
    lg_ref[...] = (lax.dot_general(wrh_ref[...], hi, nt, preferred_element_type=F32)
                   + lax.dot_general(wrl_ref[...], hi, nt, preferred_element_type=F32)
                   + lax.dot_general(wrh_ref[...], lo, nt, preferred_element_type=F32))


def _postmix(u, attn, x, mod, p, ts):
    b, s, d = x.shape
    nt = s // ts
    hb = ts // HALO
    const = lambda shape: pl.BlockSpec(shape, lambda bi, i: (0,) * len(shape))
    return pl.pallas_call(
        functools.partial(_postmix_kernel, seq=s),
        grid=(b, nt),
        in_specs=[pl.BlockSpec((1, ts, POOL_WIDTH), lambda bi, i: (bi, i, 0)),
                  pl.BlockSpec((1, HALO, POOL_WIDTH), lambda bi, i: (bi, jnp.maximum(i * hb - 1, 0), 0)),
                  pl.BlockSpec((1, HALO, POOL_WIDTH),
                               lambda bi, i: (bi, jnp.minimum((i + 1) * hb, s // HALO - 1), 0)),
                  pl.BlockSpec((1, ts, POOL_WIDTH), lambda bi, i: (bi, i, 0)),
                  pl.BlockSpec((1, ts, d), lambda bi, i: (bi, i, 0)),
                  pl.BlockSpec((1, 6, d), lambda bi, i: (bi, 0, 0)),
                  const((len(POOL_WINDOWS), POOL_GROUP, POOL_GROUP)), const((1, POOL_WIDTH)),
                  const((d, d)), const((1, d)), const((N_EXPERTS, d)), const((N_EXPERTS, d))],
        out_specs=[pl.BlockSpec((1, ts, d), lambda bi, i: (bi, i, 0)),
                   pl.BlockSpec((1, ts, d), lambda bi, i: (bi, i, 0)),
                   pl.BlockSpec((N_EXPERTS, ts), lambda bi, i: (0, bi * nt + i))],
        out_shape=[jax.ShapeDtypeStruct((b, s, d), F32),
                   jax.ShapeDtypeStruct((b, s, d), F32),
                   jax.ShapeDtypeStruct((N_EXPERTS, b * s), F32)],
        compiler_params=_cparams(("parallel", "parallel")),
        name="postmix",
    )(u, u, u, attn, x, mod, p["pool_w"], p["pool_scale"], p["w_o"], p["norm2_w"],
      p["w_router_hi"], p["w_router_lo"])


def _router_kernel(lg_ref, bias_ref, tri_ref, idx_ref, wts_ref, pos_ref, cnt_ref, carry_ref):
    @pl.when(pl.program_id(0) == 0)
    def _():
        carry_ref[...] = jnp.zeros_like(carry_ref)

    ts = lg_ref.shape[1]
    gsz = N_EXPERTS // N_GROUPS
    ninf = -jnp.inf
    scores = 1.0 / (1.0 + jnp.exp(-lg_ref[...]))
    choice = scores + bias_ref[...]
    sub = lax.broadcasted_iota(jnp.int32, (gsz, ts), 0)

    gs_rows = []
    for g in range(N_GROUPS):
        grp = choice[g * gsz:(g + 1) * gsz]
        m1 = jnp.max(grp, axis=0, keepdims=True)
        i1 = jnp.min(jnp.where(grp == m1, sub, gsz), axis=0, keepdims=True)
        m2 = jnp.max(jnp.where(sub == i1, ninf, grp), axis=0, keepdims=True)
        gs_rows.append(m1 + m2)
    gs = jnp.concatenate(gs_rows, axis=0)

    rank = jnp.zeros((N_GROUPS, ts), jnp.int32)
    for g in range(N_GROUPS):
        row = gs[g:g + 1]
        beats = (row > gs) | ((row == gs) & (sub > g))
        rank = rank + beats.astype(jnp.int32)
    gsel = rank < TOPK_GROUPS

    masked = jnp.concatenate(
        [jnp.where(gsel[g:g + 1], choice[g * gsz:(g + 1) * gsz], ninf) for g in range(N_GROUPS)],
        axis=0)
    eio = lax.broadcasted_iota(jnp.int32, (N_EXPERTS, ts), 0)
    idx_rows, w_rows, hits = [], [], []
    for _ in range(TOP_K):
        m = jnp.max(masked, axis=0, keepdims=True)
        i = jnp.min(jnp.where(masked == m, eio, N_EXPERTS), axis=0, keepdims=True)
        hit = eio == i
        w_rows.append(jnp.sum(jnp.where(hit, scores, 0.0), axis=0, keepdims=True))
        masked = jnp.where(hit, ninf, masked)
        idx_rows.append(i)
        hits.append(hit)

    wsum = functools.reduce(lambda a, c: a + c, w_rows)
    pad_i = [jnp.zeros((1, ts), jnp.int32)] * (TOPK_PAD - TOP_K)
    pad_f = [jnp.zeros((1, ts), F32)] * (TOPK_PAD - TOP_K)
    idx_ref[...] = jnp.concatenate(idx_rows + pad_i, axis=0)
    wts_ref[...] = jnp.concatenate([w / wsum * ROUTED_SCALE for w in w_rows] + pad_f, axis=0)

    sel = functools.reduce(lambda a, c: a | c, hits)
    onehot = jnp.where(sel, 1.0, 0.0).astype(BF16)
    run = jnp.dot(onehot, tri_ref[...], preferred_element_type=F32) + carry_ref[:, 0:1]
    pos_rows = [jnp.sum(jnp.where(h, run - 1.0, 0.0), axis=0, keepdims=True).astype(jnp.int32)
                for h in hits]
    pos_ref[...] = jnp.concatenate(pos_rows + pad_i, axis=0)
    total = run[:, ts - 1:ts]
    carry_ref[...] = jnp.broadcast_to(total, carry_ref.shape)
    cnt_ref[...] = jnp.broadcast_to(total, cnt_ref.shape)


def _router(logits_t, router_bias, ts):
    t = logits_t.shape[1]
    tri = jnp.triu(jnp.ones((ts, ts), BF16))
    tok = pl.BlockSpec((TOPK_PAD, ts), lambda i: (0, i))
    return pl.pallas_call(
        _router_kernel,
        grid=(t // ts,),
        in_specs=[pl.BlockSpec((N_EXPERTS, ts), lambda i: (0, i)),
                  pl.BlockSpec((N_EXPERTS, 1), lambda i: (0, 0)),
                  pl.BlockSpec((ts, ts), lambda i: (0, 0))],
        out_specs=[tok, tok, tok, pl.BlockSpec((N_EXPERTS, LANES), lambda i: (0, 0))],
        out_shape=[jax.ShapeDtypeStruct((TOPK_PAD, t), jnp.int32),
                   jax.ShapeDtypeStruct((TOPK_PAD, t), F32),
                   jax.ShapeDtypeStruct((TOPK_PAD, t), jnp.int32),
                   jax.ShapeDtypeStruct((N_EXPERTS, LANES), F32)],
        scratch_shapes=[pltpu.VMEM((N_EXPERTS, LANES), F32)],
        compiler_params=_cparams(("arbitrary",)),
        name="router",
    )(logits_t, router_bias.reshape(N_EXPERTS, 1), tri)


def _row_copy(src_ref, src_row, dst_ref, dst_row, sem):
    return pltpu.make_async_copy(src_ref.at[pl.ds(src_row, 1), :], dst_ref.at[pl.ds(dst_row, 1), :], sem)


def _dispatch_kernel(dest_ref, h2_ref, xs_ref, sem):
    tt = h2_ref.shape[0]

    def issue(j, _):
        for k in range(TOP_K):
            _row_copy(h2_ref, j, xs_ref, dest_ref[0, 0, j * TOP_K + k], sem).start()
        return 0

    lax.fori_loop(0, tt, issue, 0)

    def drain(j, _):
        for k in range(TOP_K):
            _row_copy(h2_ref, 0, xs_ref, 0, sem).wait()
        return 0

    lax.fori_loop(0, tt, drain, 0)


def _dispatch(dest, h2, n_rows, tt):
    t, d = h2.shape
    return pl.pallas_call(
        _dispatch_kernel,
        grid=(t // tt,),
        in_specs=[pl.BlockSpec((1, 1, tt * TOP_K), lambda i: (i, 0, 0), memory_space=pltpu.SMEM),
                  pl.BlockSpec((tt, d), lambda i: (i, 0))],
        out_specs=pl.BlockSpec(memory_space=pl.ANY),
        out_shape=jax.ShapeDtypeStruct((n_rows, d), F32),
        scratch_shapes=[pltpu.SemaphoreType.DMA(())],
        compiler_params=_cparams(("arbitrary",)),
        name="dispatch",
    )(dest.reshape(t // tt, 1, tt * TOP_K), h2)


def _experts_kernel(blk_exp_ref, n_used_ref, xs_ref, wg_ref, wu_ref, wd_ref, ys_ref):
    @pl.when(pl.program_id(0) < n_used_ref[0])
    def _():
        xb = xs_ref[...].astype(BF16)
        g = jnp.dot(xb, wg_ref[0], preferred_element_type=F32)
        u = jnp.dot(xb, wu_ref[0], preferred_element_type=F32)
        hb = (_silu(g) * u).astype(BF16)
        ys_ref[...] = jnp.dot(hb, wd_ref[0], preferred_element_type=F32)


def _experts(blk_exp, n_used, xs, w_gate, w_up, w_down):
    n_rows, d = xs.shape
    n_blocks = n_rows // ROW_BLOCK
    row = lambda i, be, nu: (jnp.minimum(i, nu[0] - 1), 0)
    wsel = lambda i, be, nu: (be[i], 0, 0)
    return pl.pallas_call(
        _experts_kernel,
        grid_spec=pltpu.PrefetchScalarGridSpec(
            num_scalar_prefetch=2,
            grid=(n_blocks,),
            in_specs=[pl.BlockSpec((ROW_BLOCK, d), row),
                      pl.BlockSpec((1, d, EXPERT_FF), wsel),
                      pl.BlockSpec((1, d, EXPERT_FF), wsel),
                      pl.BlockSpec((1, EXPERT_FF, d), wsel)],
            out_specs=pl.BlockSpec((ROW_BLOCK, d), row)),
        out_shape=jax.ShapeDtypeStruct((n_rows, d), F32),
        compiler_params=_cparams(("arbitrary",)),
        name="experts",
    )(blk_exp, n_used, xs, w_gate, w_up, w_down)


def _combine_kernel(dest_ref, ys_ref, w_ref, h2_ref, x1_ref, mod_ref, wsg_ref, wsu_ref, wsd_ref,
                    out_ref, buf_ref, sem):
    tt = h2_ref.shape[0]

    def issue(j, _):
        for k in range(TOP_K):
            pltpu.make_async_copy(ys_ref.at[pl.ds(dest_ref[0, 0, j * TOP_K + k], 1), :],
                                  buf_ref.at[k, pl.ds(j, 1), :], sem).start()
        return 0

    lax.fori_loop(0, tt, issue, 0)

    hb = h2_ref[...].astype(BF16)
    g = jnp.dot(hb, wsg_ref[...], preferred_element_type=F32)
    u = jnp.dot(hb, wsu_ref[...], preferred_element_type=F32)
    acc = jnp.dot((_silu(g) * u).astype(BF16), wsd_ref[...], preferred_element_type=F32)

    def drain(j, _):
        for k in range(TOP_K):
            pltpu.make_async_copy(ys_ref.at[pl.ds(0, 1), :], buf_ref.at[0, pl.ds(0, 1), :], sem).wait()
        return 0

    lax.fori_loop(0, tt, drain, 0)

    w = w_ref[...]
    for k in range(TOP_K):
        acc = acc + buf_ref[k] * w[:, k:k + 1]
    out_ref[...] = x1_ref[...] + mod_ref[0, 5:6, :] * acc


def _combine(dest, ys, wts_t, h2, x1, mod, p, tt, seq):
    t, d = h2.shape
    per_seq = seq // tt
    tok = pl.BlockSpec((tt, d), lambda i: (i, 0))
    const = lambda shape: pl.BlockSpec(shape, lambda i: (0,) * len(shape))
    return pl.pallas_call(
        _combine_kernel,
        grid=(t // tt,),
        in_specs=[pl.BlockSpec((1, 1, tt * TOP_K), lambda i: (i, 0, 0), memory_space=pltpu.SMEM),
                  pl.BlockSpec(memory_space=pl.ANY),
                  pl.BlockSpec((tt, TOPK_PAD), lambda i: (i, 0)),
                  tok, tok,
                  pl.BlockSpec((1, 6, d), lambda i: (i // per_seq, 0, 0)),
                  const((d, SHARED_FF)), const((d, SHARED_FF)), const((SHARED_FF, d))],
        out_specs=tok,
        out_shape=jax.ShapeDtypeStruct((t, d), F32),
        scratch_shapes=[pltpu.VMEM((TOP_K, tt, d), F32), pltpu.SemaphoreType.DMA(())],
        compiler_params=_cparams(("arbitrary",)),
        name="combine",
    )(dest.reshape(t // tt, 1, tt * TOP_K), ys, wts_t, h2, x1, mod,
      p["ws_gate"], p["ws_up"], p["ws_down"])


def _rope_tables(s):
    half = QK_ROPE_DIM // 2
    inv_freq = ROPE_THETA ** (-jnp.arange(half, dtype=F32) / half)
    ang = jnp.arange(s, dtype=F32)[:, None] * inv_freq[None, :]
    cos, sin = jnp.cos(ang), jnp.sin(ang)
    z = lambda n: jnp.zeros((s, n), F32)
    tab_cos = jnp.concatenate([jnp.ones((s, QK_NOPE_DIM), F32), cos, cos, z(HEAD_PAD - QK_HEAD_DIM)], 1)
    tab_sa = jnp.concatenate([z(QK_NOPE_DIM), -sin, z(HEAD_PAD - QK_NOPE_DIM - half)], 1)
    tab_sb = jnp.concatenate([z(QK_NOPE_DIM + half), sin, z(HEAD_PAD - QK_HEAD_DIM)], 1)
    return tab_cos, tab_sa, tab_sb


def _prep_weights(norm1_w, w_in, pool_w, pool_scale, q_a_norm_w, w_q_b, kv_a_norm_w, w_kv_b,
                  q_norm_w, k_norm_w, w_o, norm2_w, w_router, w_gate, w_up, w_down,
                  ws_gate, ws_up, ws_down):
    d = D_MODEL
    c0 = POOL_WIDTH + Q_LORA_RANK + KV_LORA_RANK
    pad_h = HEAD_PAD - QK_HEAD_DIM
    w_in_p = jnp.concatenate(
        [w_in[:, :c0], jnp.zeros((d, QK_NOPE_DIM), F32), w_in[:, c0:], jnp.zeros((d, pad_h), F32)], 1)
    w_q = jnp.pad(w_q_b.reshape(Q_LORA_RANK, N_HEADS, QK_HEAD_DIM), ((0, 0), (0, 0), (0, pad_h)))
    kv = w_kv_b.reshape(KV_LORA_RANK, N_HEADS, QK_NOPE_DIM + V_HEAD_DIM)
    w_k = jnp.pad(kv[:, :, :QK_NOPE_DIM], ((0, 0), (0, 0), (0, HEAD_PAD - QK_NOPE_DIM)))
    w_v = kv[:, :, QK_NOPE_DIM:]
    w_r_t = w_router.T
    w_r_hi = w_r_t.astype(BF16)
    w_r_lo = (w_r_t - w_r_hi.astype(F32)).astype(BF16)
    return dict(
        norm1_w=norm1_w.reshape(1, d), w_in=w_in_p.astype(BF16),
        q_a_norm_w=q_a_norm_w.reshape(1, -1), w_q=w_q.reshape(Q_LORA_RANK, -1).astype(BF16),
        kv_a_norm_w=kv_a_norm_w.reshape(1, -1), w_k=w_k.reshape(KV_LORA_RANK, -1).astype(BF16),
        w_v=w_v.reshape(KV_LORA_RANK, -1).astype(BF16),
        q_norm_w=jnp.pad(q_norm_w * QK_HEAD_DIM ** -0.5, (0, pad_h)).reshape(1, HEAD_PAD),
        k_norm_w=jnp.pad(k_norm_w, (0, pad_h)).reshape(1, HEAD_PAD),
        pool_w=pool_w.astype(BF16), pool_scale=pool_scale.reshape(1, -1), w_o=w_o.astype(BF16),
        norm2_w=norm2_w.reshape(1, d), w_router_hi=w_r_hi, w_router_lo=w_r_lo,
        w_gate=w_gate.astype(BF16), w_up=w_up.astype(BF16), w_down=w_down.astype(BF16),
        ws_gate=ws_gate.astype(BF16), ws_up=ws_up.astype(BF16), ws_down=ws_down.astype(BF16))


def _tile(n, pref):
    return pref if n % pref == 0 else n


def _encoder(x, mod, router_bias, p):
    b, s, d = x.shape
    t = b * s
    ts = _tile(s, 512)
    p = dict(p)
    p["rope_cos"], p["rope_sa"], p["rope_sb"] = _rope_tables(s)

    u, q, k, v = _inproj(x, mod, p, ts)
    attn = _attention(q, k, v, _tile(s, 512), _tile(s, 512))
    x1, h2, logits_t = _postmix(u, attn, x, mod, p, ts)
    idx, wts, pos, cnt = _router(logits_t, router_bias, _tile(t, 512))

    counts = cnt[:, 0].astype(jnp.int32)
    padded = (counts + ROW_BLOCK - 1) // ROW_BLOCK * ROW_BLOCK
    pad_end = jnp.cumsum(padded)
    pad_start = pad_end - padded
    dest = (pad_start[idx[:TOP_K]] + pos[:TOP_K]).T.reshape(t * TOP_K)
    n_blocks = -(-t * TOP_K // ROW_BLOCK) + N_EXPERTS
    blk_exp = jnp.minimum(
        jnp.searchsorted(pad_end, jnp.arange(n_blocks, dtype=jnp.int32) * ROW_BLOCK, side="right"),
        N_EXPERTS - 1).astype(jnp.int32)
    n_used = (pad_end[-1:] // ROW_BLOCK).astype(jnp.int32)

    tt = _tile(s, 256)
    h2f = h2.reshape(t, d)
    xs = _dispatch(dest, h2f, n_blocks * ROW_BLOCK, tt)
    ys = _experts(blk_exp, n_used, xs, p["w_gate"], p["w_up"], p["w_down"])
    out = _combine(dest, ys, wts.T, h2f, x1.reshape(t, d), mod, p, tt, s)
    return out.reshape(b, s, d)


def kernel(x_prompt, x_sample, c_prompt, c_sample, w_ada, b_ada, norm1_w, w_in, pool_w, pool_scale,
           q_a_norm_w, w_q_b, kv_a_norm_w, w_kv_b, q_norm_w, k_norm_w, w_o, norm2_w, w_router,
           router_bias, w_gate, w_up, w_down, ws_gate, ws_up, ws_down):
    assert w_ada.shape[0] == 1, "single-layer encoder"
    p = _prep_weights(norm1_w[0], w_in[0], pool_w[0], pool_scale[0], q_a_norm_w[0], w_q_b[0],
                      kv_a_norm_w[0], w_kv_b[0], q_norm_w[0], k_norm_w[0], w_o[0], norm2_w[0],
                      w_router[0], w_gate[0], w_up[0], w_down[0], ws_gate[0], ws_up[0], ws_down[0])
    nb = x_prompt.shape[0]
    c = jnp.concatenate([c_prompt, c_sample], axis=0).astype(F32)
    mod = _adaln(c, w_ada[0], b_ada[0]).reshape(c.shape[0], 6, D_MODEL)
    y_prompt = _encoder(x_prompt, mod[:nb], router_bias[0], p)
    y_sample = _encoder(x_sample, mod[nb:], router_bias[0], p)
    return (y_prompt, y_sample)
```

```python
import functools

import jax
import jax.numpy as jnp
from jax import lax
from jax.experimental import pallas as pl
from jax.experimental.pallas import tpu as pltpu

D_MODEL = 1024
POOL_WIDTH = 512
POOL_WINDOWS = (2, 4, 8, 16)
POOL_GROUP = 128
N_HEADS = 8
V_HEAD_DIM = 64
QK_NOPE_DIM = 64
QK_ROPE_DIM = 32
QK_HEAD_DIM = 96
Q_LORA_RANK = 256
KV_LORA_RANK = 128
ROPE_THETA = 10000.0
N_EXPERTS = 64
TOP_K = 6
N_GROUPS = 8
TOPK_GROUPS = 4
EXPERT_FF = 256
SHARED_FF = 256
ROUTED_SCALE = 2.5
EPS = 1e-6

LANES = 128
HEAD_PAD = 128
HALO = 16
ROW_BLOCK = 256
TOPK_PAD = 8
V_AUG = 80
SHIFT_LANE = QK_HEAD_DIM
MAX_FIXED_SHIFT = 40.0
LOG2E = 1.4426950408889634
VMEM_LIMIT = 48 * 1024 * 1024

F32 = jnp.float32
BF16 = jnp.bfloat16


def _cparams(sem):
    return pltpu.CompilerParams(dimension_semantics=sem, vmem_limit_bytes=VMEM_LIMIT)


def _silu(x):
    return x * (1.0 / (1.0 + jnp.exp(-x)))


def _adaln_kernel(c_ref, w_ref, b_ref, o_ref):
    c = c_ref[...]
    o_ref[...] = jnp.dot(_silu(c), w_ref[...], preferred_element_type=F32,
                         precision=lax.Precision.HIGHEST) + b_ref[...]


def _adaln(c, w_ada, b_ada):
    nb, d = c.shape
    n = w_ada.shape[1]
    tn = 1536
    return pl.pallas_call(
        _adaln_kernel,
        grid=(n // tn,),
        in_specs=[pl.BlockSpec((nb, d), lambda j: (0, 0)),
                  pl.BlockSpec((d, tn), lambda j: (0, j)),
                  pl.BlockSpec((1, tn), lambda j: (0, j))],
        out_specs=pl.BlockSpec((nb, tn), lambda j: (0, j)),
        out_shape=jax.ShapeDtypeStruct((nb, n), F32),
        compiler_params=_cparams(("arbitrary",)),
        name="adaln",
    )(c, w_ada, b_ada.reshape(1, n))


def _head_norm_rope(t, gain, cos, sin_a, sin_b, spare):
    r = lax.rsqrt(jnp.sum(t * t, axis=-1, keepdims=True) * (1.0 / QK_HEAD_DIM) + EPS)
    tn = t * r * gain
    return (tn * cos + pltpu.roll(tn, HEAD_PAD - QK_ROPE_DIM // 2, 1) * sin_a
            + pltpu.roll(tn, QK_ROPE_DIM // 2, 1) * sin_b + spare)


def _inproj_kernel(x_ref, mod_ref, n1w_ref, win_ref, qan_ref, wq_ref, kvan_ref, wk_ref, wv_ref,
                   qnw_ref, knw_ref, qsp_ref, ksp_ref, cos_ref, sa_ref, sb_ref,
                   u_ref, q_ref, k_ref, v_ref):
    x = x_ref[0]
    shift1 = mod_ref[0, 0:1, :]
    scale1 = mod_ref[0, 1:2, :]
    r = lax.rsqrt(jnp.mean(x * x, axis=-1, keepdims=True) + EPS)
    h = x * r * n1w_ref[...] * (1.0 + scale1) + shift1
    z = jnp.dot(h.astype(BF16), win_ref[...], preferred_element_type=F32)
    u_ref[0] = z[:, :POOL_WIDTH].astype(BF16)

    cq = z[:, POOL_WIDTH:POOL_WIDTH + Q_LORA_RANK]
    cqn = cq * lax.rsqrt(jnp.mean(cq * cq, axis=-1, keepdims=True) + EPS) * qan_ref[...]
    q = jnp.dot(cqn.astype(BF16), wq_ref[...], preferred_element_type=F32)

    c0 = POOL_WIDTH + Q_LORA_RANK
    ckv = z[:, c0:c0 + KV_LORA_RANK]
    ckvn = (ckv * lax.rsqrt(jnp.mean(ckv * ckv, axis=-1, keepdims=True) + EPS)
            * kvan_ref[...]).astype(BF16)
    kk = jnp.dot(ckvn, wk_ref[...], preferred_element_type=F32)
    v_ref[0] = jnp.dot(ckvn, wv_ref[...], preferred_element_type=F32).astype(BF16)
    kpe = z[:, c0 + KV_LORA_RANK:]

    cos = cos_ref[...]
    sa = sa_ref[...]
    sb = sb_ref[...]
    qg = qnw_ref[...]
    kg = knw_ref[...]
    qsp = qsp_ref[...]
    ksp = ksp_ref[...]
    for hd in range(N_HEADS):
        sl = slice(hd * HEAD_PAD, (hd + 1) * HEAD_PAD)
        q_ref[0, :, sl] = _head_norm_rope(q[:, sl], qg, cos, sa, sb, qsp).astype(BF16)
        k_ref[0, :, sl] = _head_norm_rope(kk[:, sl] + kpe, kg, cos, sa, sb, ksp).astype(BF16)


def _inproj(x, mod, p, ts):
    b, s, d = x.shape
    qk_w = N_HEADS * HEAD_PAD
    const = lambda shape: pl.BlockSpec(shape, lambda bi, i: (0,) * len(shape))
    row_tab = pl.BlockSpec((ts, HEAD_PAD), lambda bi, i: (i, 0))
    return pl.pallas_call(
        _inproj_kernel,
        grid=(b, s // ts),
        in_specs=[pl.BlockSpec((1, ts, d), lambda bi, i: (bi, i, 0)),
                  pl.BlockSpec((1, 6, d), lambda bi, i: (bi, 0, 0)),
                  const((1, d)), const((d, d)), const((1, Q_LORA_RANK)),
                  const((Q_LORA_RANK, qk_w)), const((1, KV_LORA_RANK)),
                  const((KV_LORA_RANK, qk_w)), const((KV_LORA_RANK, N_HEADS * V_HEAD_DIM)),
                  const((1, HEAD_PAD)), const((1, HEAD_PAD)), const((1, HEAD_PAD)),
                  const((1, HEAD_PAD)), row_tab, row_tab, row_tab],
        out_specs=[pl.BlockSpec((1, ts, POOL_WIDTH), lambda bi, i: (bi, i, 0)),
                   pl.BlockSpec((1, ts, qk_w), lambda bi, i: (bi, i, 0)),
                   pl.BlockSpec((1, ts, qk_w), lambda bi, i: (bi, i, 0)),
                   pl.BlockSpec((1, ts, N_HEADS * V_HEAD_DIM), lambda bi, i: (bi, i, 0))],
        out_shape=[jax.ShapeDtypeStruct((b, s, POOL_WIDTH), BF16),
                   jax.ShapeDtypeStruct((b, s, qk_w), BF16),
                   jax.ShapeDtypeStruct((b, s, qk_w), BF16),
                   jax.ShapeDtypeStruct((b, s, N_HEADS * V_HEAD_DIM), BF16)],
        compiler_params=_cparams(("parallel", "parallel")),
        name="inproj",
    )(x, mod, p["norm1_w"], p["w_in"], p["q_a_norm_w"], p["w_q"], p["kv_a_norm_w"], p["w_k"],
      p["w_v"], p["q_norm_w"], p["k_norm_w"], p["q_spare"], p["k_spare"],
      p["rope_cos"], p["rope_sa"], p["rope_sb"])


def _attn_shift_kernel(qt_ref, k_ref, vt_ref, o_ref, acc_a, acc_b, *, tk, nk):
    acc_a[...] = jnp.zeros_like(acc_a)
    acc_b[...] = jnp.zeros_like(acc_b)
    qa = qt_ref[0, :HEAD_PAD, :]
    qb = qt_ref[0, HEAD_PAD:, :]

    def body(c, _):
        off = pl.multiple_of(c * tk, tk)
        ks = k_ref[0, pl.ds(off, tk), :]
        vts = vt_ref[0, :, pl.ds(off, tk)]
        pa = jnp.exp2(jnp.dot(ks[:, :HEAD_PAD], qa, preferred_element_type=F32)).astype(BF16)
        pb = jnp.exp2(jnp.dot(ks[:, HEAD_PAD:], qb, preferred_element_type=F32)).astype(BF16)
        acc_a[...] += jnp.dot(vts[:V_AUG], pa, preferred_element_type=F32)
        acc_b[...] += jnp.dot(vts[V_AUG:], pb, preferred_element_type=F32)
        return 0

    lax.fori_loop(0, nk, body, 0)
    oa = acc_a[:V_HEAD_DIM] / acc_a[V_HEAD_DIM:V_HEAD_DIM + 1]
    ob = acc_b[:V_HEAD_DIM] / acc_b[V_HEAD_DIM:V_HEAD_DIM + 1]
    o_ref[0] = jnp.concatenate([oa, ob], axis=0).T.astype(BF16)


def _attn_online_kernel(qt_ref, k_ref, vt_ref, o_ref, acc_a, acc_b, m_ref, *, tk, nk):
    acc_a[...] = jnp.zeros_like(acc_a)
    acc_b[...] = jnp.zeros_like(acc_b)
    m_ref[...] = jnp.full_like(m_ref, -jnp.inf)
    qa = qt_ref[0, :HEAD_PAD, :]
    qb = qt_ref[0, HEAD_PAD:, :]

    def one_head(kh, q, vth, acc, row):
        s = jnp.dot(kh, q, preferred_element_type=F32)
        m_old = m_ref[row:row + 1, :]
        m_new = jnp.maximum(m_old, jnp.max(s, axis=0, keepdims=True))
        p = jnp.exp2(s - m_new).astype(BF16)
        acc[...] = acc[...] * jnp.exp2(m_old - m_new) + jnp.dot(vth, p, preferred_element_type=F32)
        m_ref[row:row + 1, :] = m_new

    def body(c, _):
        off = pl.multiple_of(c * tk, tk)
        ks = k_ref[0, pl.ds(off, tk), :]
        vts = vt_ref[0, :, pl.ds(off, tk)]
        one_head(ks[:, :HEAD_PAD], qa, vts[:V_AUG], acc_a, 0)
        one_head(ks[:, HEAD_PAD:], qb, vts[V_AUG:], acc_b, 1)
        return 0

    lax.fori_loop(0, nk, body, 0)
    oa = acc_a[:V_HEAD_DIM] / acc_a[V_HEAD_DIM:V_HEAD_DIM + 1]
    ob = acc_b[:V_HEAD_DIM] / acc_b[V_HEAD_DIM:V_HEAD_DIM + 1]
    o_ref[0] = jnp.concatenate([oa, ob], axis=0).T.astype(BF16)


def _attention(qt, k, vt, tq, tk, online):
    b, _, s = qt.shape
    scratch = [pltpu.VMEM((V_AUG, tq), F32), pltpu.VMEM((V_AUG, tq), F32)]
    if online:
        body = functools.partial(_attn_online_kernel, tk=tk, nk=s // tk)
        scratch.append(pltpu.VMEM((8, tq), F32))
    else:
        body = functools.partial(_attn_shift_kernel, tk=tk, nk=s // tk)
    return pl.pallas_call(
        body,
        grid=(b, N_HEADS // 2, s // tq),
        in_specs=[pl.BlockSpec((1, 2 * HEAD_PAD, tq), lambda bi, j, i: (bi, j, i)),
                  pl.BlockSpec((1, s, 2 * HEAD_PAD), lambda bi, j, i: (bi, 0, j)),
                  pl.BlockSpec((1, 2 * V_AUG, s), lambda bi, j, i: (bi, j, 0))],
        out_specs=pl.BlockSpec((1, tq, 2 * V_HEAD_DIM), lambda bi, j, i: (bi, i, j)),
        out_shape=jax.ShapeDtypeStruct((b, s, N_HEADS * V_HEAD_DIM), BF16),
        scratch_shapes=scratch,
        compiler_params=_cparams(("parallel", "parallel", "arbitrary")),
        name="attn_online" if online else "attn_shift",
    )(qt, k, vt)


def _postmix_kernel(u_ref, up_ref, un_ref, a_ref, x_ref, mod_ref, pw_ref, ps_ref, wo_ref, n2w_ref,
                    wrh_ref, wrl_ref, x1_ref, h2_ref, lg_ref, *, seq):
    i = pl.program_id(1)
    ts = u_ref.shape[1]
    ext_rows = ts + 2 * HALO
    ext = jnp.concatenate([up_ref[0], u_ref[0], un_ref[0]], axis=0).astype(F32)
    pos = i * ts - HALO + lax.broadcasted_iota(jnp.int32, (ext_rows, 1), 0)
    ext = jnp.where((pos >= 0) & (pos < seq), ext, 0.0)
    p = i * ts + lax.broadcasted_iota(jnp.int32, (ts, 1), 0)

    outs = []
    for g, w in enumerate(POOL_WINDOWS):
        left = w // 2
        right = w - 1 - left
        t = ext[:, g * POOL_GROUP:(g + 1) * POOL_GROUP]
        step = 1
        while step < w:
            t = t + pltpu.roll(t, ext_rows - step, 0)
            step *= 2
        win = pltpu.roll(t, left, 0)[HALO:HALO + ts]
        cnt = (jnp.minimum(p + right + 1, seq) - jnp.maximum(p - left, 0)).astype(F32)
        d = win / cnt - ext[HALO:HALO + ts, g * POOL_GROUP:(g + 1) * POOL_GROUP]
        outs.append(jnp.dot(d.astype(BF16), pw_ref[g], preferred_element_type=F32))
    pool = (jnp.concatenate(outs, axis=-1) * ps_ref[...]).astype(BF16)

    mix = (jnp.dot(pool, wo_ref[:POOL_WIDTH, :], preferred_element_type=F32)
           + jnp.dot(a_ref[0], wo_ref[POOL_WIDTH:, :], preferred_element_type=F32))
    x1 = x_ref[0] + mod_ref[0, 2:3, :] * mix
    x1_ref[0] = x1
    r = lax.rsqrt(jnp.mean(x1 * x1, axis=-1, keepdims=True) + EPS)
    h2 = x1 * r * n2w_ref[...] * (1.0 + mod_ref[0, 4:5, :]) + mod_ref[0, 3:4, :]
    h2_ref[0] = h2
    hi = h2.astype(BF16)
    lo = (h2 - hi.astype(F32)).astype(BF16)
    nt = (((1,), (1,)), ((), ()))
    lg_ref[...] = (lax.dot_general(wrh_ref[...], hi, nt, preferred_element_type=F32)
                   + lax.dot_general(wrl_ref[...], hi, nt, preferred_element_type=F32)
                   + lax.dot_general(wrh_ref[...], lo, nt, preferred_element_type=F32))


def _postmix(u, attn, x, mod, p, ts):
    b, s, d = x.shape
    nt = s // ts
    hb = ts // HALO
    const = lambda shape: pl.BlockSpec(shape, lambda bi, i: (0,) * len(shape))
    return pl.pallas_call(
        functools.partial(_postmix_kernel, seq=s),
        grid=(b, nt),
        in_specs=[pl.BlockSpec((1, ts, POOL_WIDTH), lambda bi, i: (bi, i, 0)),
                  pl.BlockSpec((1, HALO, POOL_WIDTH), lambda bi, i: (bi, jnp.maximum(i * hb - 1, 0), 0)),
                  pl.BlockSpec((1, HALO, POOL_WIDTH),
                               lambda bi, i: (bi, jnp.minimum((i + 1) * hb, s // HALO - 1), 0)),
                  pl.BlockSpec((1, ts, POOL_WIDTH), lambda bi, i: (bi, i, 0)),
                  pl.BlockSpec((1, ts, d), lambda bi, i: (bi, i, 0)),
                  pl.BlockSpec((1, 6, d), lambda bi, i: (bi, 0, 0)),
                  const((len(POOL_WINDOWS), POOL_GROUP, POOL_GROUP)), const((1, POOL_WIDTH)),
                  const((d, d)), const((1, d)), const((N_EXPERTS, d)), const((N_EXPERTS, d))],
        out_specs=[pl.BlockSpec((1, ts, d), lambda bi, i: (bi, i, 0)),
                   pl.BlockSpec((1, ts, d), lambda bi, i: (bi, i, 0)),
                   pl.BlockSpec((N_EXPERTS, ts), lambda bi, i: (0, bi * nt + i))],
        out_shape=[jax.ShapeDtypeStruct((b, s, d), F32),
                   jax.ShapeDtypeStruct((b, s, d), F32),
                   jax.ShapeDtypeStruct((N_EXPERTS, b * s), F32)],
        compiler_params=_cparams(("parallel", "parallel")),
        name="postmix",
    )(u, u, u, attn, x, mod, p["pool_w"], p["pool_scale"], p["w_o"], p["norm2_w"],
      p["w_router_hi"], p["w_router_lo"])


def _router_kernel(lg_ref, bias_ref, tri_ref, idx_ref, wts_ref, pos_ref, cnt_ref, carry_ref):
    @pl.when(pl.program_id(0) == 0)
    def _():
        carry_ref[...] = jnp.zeros_like(carry_ref)

    ts = lg_ref.shape[1]
    gsz = N_EXPERTS // N_GROUPS
    ninf = -jnp.inf
    scores = 1.0 / (1.0 + jnp.exp(-lg_ref[...]))
    choice = scores + bias_ref[...]
    sub = lax.broadcasted_iota(jnp.int32, (gsz, ts), 0)

    gs_rows = []
    for g in range(N_GROUPS):
        grp = choice[g * gsz:(g + 1) * gsz]
        m1 = jnp.max(grp, axis=0, keepdims=True)
        i1 = jnp.min(jnp.where(grp == m1, sub, gsz), axis=0, keepdims=True)
        m2 = jnp.max(jnp.where(sub == i1, ninf, grp), axis=0, keepdims=True)
        gs_rows.append(m1 + m2)
    gs = jnp.concatenate(gs_rows, axis=0)

    rank = jnp.zeros((N_GROUPS, ts), jnp.int32)
    for g in range(N_GROUPS):
        row = gs[g:g + 1]
        beats = (row > gs) | ((row == gs) & (sub > g))
        rank = rank + beats.astype(jnp.int32)
    gsel = rank < TOPK_GROUPS

    masked = jnp.concatenate(
        [jnp.where(gsel[g:g + 1], choice[g * gsz:(g + 1) * gsz], ninf) for g in range(N_GROUPS)],
        axis=0)
    eio = lax.broadcasted_iota(jnp.int32, (N_EXPERTS, ts), 0)
    idx_rows, w_rows, hits = [], [], []
    for _ in range(TOP_K):
        m = jnp.max(masked, axis=0, keepdims=True)
        i = jnp.min(jnp.where(masked == m, eio, N_EXPERTS), axis=0, keepdims=True)
        hit = eio == i
        w_rows.append(jnp.sum(jnp.where(hit, scores, 0.0), axis=0, keepdims=True))
        masked = jnp.where(hit, ninf, masked)
        idx_rows.append(i)
        hits.append(hit)

    wsum = functools.reduce(lambda a, c: a + c, w_rows)
    pad_i = [jnp.zeros((1, ts), jnp.int32)] * (TOPK_PAD - TOP_K)
    pad_f = [jnp.zeros((1, ts), F32)] * (TOPK_PAD - TOP_K)
    idx_ref[...] = jnp.concatenate(idx_rows + pad_i, axis=0)
    wts_ref[...] = jnp.concatenate([w / wsum * ROUTED_SCALE for w in w_rows] + pad_f, axis=0)

    sel = functools.reduce(lambda a, c: a | c, hits)
    onehot = jnp.where(sel, 1.0, 0.0).astype(BF16)
    run = jnp.dot(onehot, tri_ref[...], preferred_element_type=F32) + carry_ref[:, 0:1]
    pos_rows = [jnp.sum(jnp.where(h, run - 1.0, 0.0), axis=0, keepdims=True).astype(jnp.int32)
                for h in hits]
    pos_ref[...] = jnp.concatenate(pos_rows + pad_i, axis=0)
    total = run[:, ts - 1:ts]
    carry_ref[...] = jnp.broadcast_to(total, carry_ref.shape)
    cnt_ref[...] = jnp.broadcast_to(total, cnt_ref.shape)


def _router(logits_t, router_bias, ts):
    t = logits_t.shape[1]
    tri = jnp.triu(jnp.ones((ts, ts), BF16))
    tok = pl.BlockSpec((TOPK_PAD, ts), lambda i: (0, i))
    return pl.pallas_call(
        _router_kernel,
        grid=(t // ts,),
        in_specs=[pl.BlockSpec((N_EXPERTS, ts), lambda i: (0, i)),
                  pl.BlockSpec((N_EXPERTS, 1), lambda i: (0, 0)),
                  pl.BlockSpec((ts, ts), lambda i: (0, 0))],
        out_specs=[tok, tok, tok, pl.BlockSpec((N_EXPERTS, LANES), lambda i: (0, 0))],
        out_shape=[jax.ShapeDtypeStruct((TOPK_PAD, t), jnp.int32),
                   jax.ShapeDtypeStruct((TOPK_PAD, t), F32),
                   jax.ShapeDtypeStruct((TOPK_PAD, t), jnp.int32),
                   jax.ShapeDtypeStruct((N_EXPERTS, LANES), F32)],
        scratch_shapes=[pltpu.VMEM((N_EXPERTS, LANES), F32)],
        compiler_params=_cparams(("arbitrary",)),
        name="router",
    )(logits_t, router_bias.reshape(N_EXPERTS, 1), tri)


def _dest_kernel(idx_ref, pos_ref, start_ref, dest_ref):
    ts = idx_ref.shape[1]
    eio = lax.broadcasted_iota(jnp.int32, (N_EXPERTS, ts), 0)
    start = start_ref[...]
    rows = [pos_ref[k:k + 1, :]
            + jnp.sum(jnp.where(eio == idx_ref[k:k + 1, :], start, 0), axis=0, keepdims=True)
            for k in range(TOPK_PAD)]
    dest_ref[...] = jnp.concatenate(rows, axis=0)


def _dest_rows(idx, pos, pad_start, ts):
    t = idx.shape[1]
    tok = pl.BlockSpec((TOPK_PAD, ts), lambda i: (0, i))
    return pl.pallas_call(
        _dest_kernel,
        grid=(t // ts,),
        in_specs=[tok, tok, pl.BlockSpec((N_EXPERTS, 1), lambda i: (0, 0))],
        out_specs=tok,
        out_shape=jax.ShapeDtypeStruct((TOPK_PAD, t), jnp.int32),
        compiler_params=_cparams(("parallel",)),
        name="dest_rows",
    )(idx, pos, pad_start.reshape(N_EXPERTS, 1))


def _row_copy(src_ref, src_row, dst_ref, dst_row, sem):
    return pltpu.make_async_copy(src_ref.at[pl.ds(src_row, 1), :], dst_ref.at[pl.ds(dst_row, 1), :], sem)


def _dispatch_kernel(dest_ref, h2_ref, xs_ref, sem):
    tt = h2_ref.shape[0]

    def issue(j, _):
        for k in range(TOP_K):
            _row_copy(h2_ref, j, xs_ref, dest_ref[0, 0, j * TOP_K + k], sem).start()
        return 0

    lax.fori_loop(0, tt, issue, 0)

    def drain(j, _):
        for k in range(TOP_K):
            _row_copy(h2_ref, 0, xs_ref, 0, sem).wait()
        return 0

    lax.fori_loop(0, tt, drain, 0)


def _dispatch(dest, h2, n_rows, tt):
    t, d = h2.shape
    return pl.pallas_call(
        _dispatch_kernel,
        grid=(t // tt,),
        in_specs=[pl.BlockSpec((1, 1, tt * TOP_K), lambda i: (i, 0, 0), memory_space=pltpu.SMEM),
                  pl.BlockSpec((tt, d), lambda i: (i, 0))],
        out_specs=pl.BlockSpec(memory_space=pl.ANY),
        out_shape=jax.ShapeDtypeStruct((n_rows, d), F32),
        scratch_shapes=[pltpu.SemaphoreType.DMA(())],
        compiler_params=_cparams(("arbitrary",)),
        name="dispatch",
    )(dest.reshape(t // tt, 1, tt * TOP_K), h2)


def _experts_kernel(blk_exp_ref, n_used_ref, xs_ref, wg_ref, wu_ref, wd_ref, ys_ref):
    @pl.when(pl.program_id(0) < n_used_ref[0])
    def _():
        xb = xs_ref[...].astype(BF16)
        g = jnp.dot(xb, wg_ref[0], preferred_element_type=F32)
        u = jnp.dot(xb, wu_ref[0], preferred_element_type=F32)
        hb = (_silu(g) * u).astype(BF16)
        ys_ref[...] = jnp.dot(hb, wd_ref[0], preferred_element_type=F32)


def _experts(blk_exp, n_used, xs, w_gate, w_up, w_down):
    n_rows, d = xs.shape
    n_blocks = n_rows // ROW_BLOCK
    row = lambda i, be, nu: (jnp.minimum(i, nu[0] - 1), 0)
    wsel = lambda i, be, nu: (be[i], 0, 0)
    return pl.pallas_call(
        _experts_kernel,
        grid_spec=pltpu.PrefetchScalarGridSpec(
            num_scalar_prefetch=2,
            grid=(n_blocks,),
            in_specs=[pl.BlockSpec((ROW_BLOCK, d), row),
                      pl.BlockSpec((1, d, EXPERT_FF), wsel),
                      pl.BlockSpec((1, d, EXPERT_FF), wsel),
                      pl.BlockSpec((1, EXPERT_FF, d), wsel)],
            out_specs=pl.BlockSpec((ROW_BLOCK, d), row)),
        out_shape=jax.ShapeDtypeStruct((n_rows, d), F32),
        compiler_params=_cparams(("arbitrary",)),
        name="experts",
    )(blk_exp, n_used, xs, w_gate, w_up, w_down)


def _combine_kernel(dest_ref, ys_ref, w_ref, h2_ref, x1_ref, mod_ref, wsg_ref, wsu_ref, wsd_ref,
                    out_ref, buf_ref, sem):
    tt = h2_ref.shape[0]

    def issue(j, _):
        for k in range(TOP_K):
            pltpu.make_async_copy(ys_ref.at[pl.ds(dest_ref[0, 0, j * TOP_K + k], 1), :],
                                  buf_ref.at[k, pl.ds(j, 1), :], sem).start()
        return 0

    lax.fori_loop(0, tt, issue, 0)

    hb = h2_ref[...].astype(BF16)
    g = jnp.dot(hb, wsg_ref[...], preferred_element_type=F32)
    u = jnp.dot(hb, wsu_ref[...], preferred_element_type=F32)
    acc = jnp.dot((_silu(g) * u).astype(BF16), wsd_ref[...], preferred_element_type=F32)

    def drain(j, _):
        for k in range(TOP_K):
            pltpu.make_async_copy(ys_ref.at[pl.ds(0, 1), :], buf_ref.at[0, pl.ds(0, 1), :], sem).wait()
        return 0

    lax.fori_loop(0, tt, drain, 0)

    w = w_ref[...]
    for k in range(TOP_K):
        acc = acc + buf_ref[k] * w[:, k:k + 1]
    out_ref[...] = x1_ref[...] + mod_ref[0, 5:6, :] * acc


def _combine(dest, ys, wts_t, h2, x1, mod, p, tt, seq):
    t, d = h2.shape
    per_seq = seq // tt
    tok = pl.BlockSpec((tt, d), lambda i: (i, 0))
    const = lambda shape: pl.BlockSpec(shape, lambda i: (0,) * len(shape))
    return pl.pallas_call(
        _combine_kernel,
        grid=(t // tt,),
        in_specs=[pl.BlockSpec((1, 1, tt * TOP_K), lambda i: (i, 0, 0), memory_space=pltpu.SMEM),
                  pl.BlockSpec(memory_space=pl.ANY),
                  pl.BlockSpec((tt, TOPK_PAD), lambda i: (i, 0)),
                  tok, tok,
                  pl.BlockSpec((1, 6, d), lambda i: (i // per_seq, 0, 0)),
                  const((d, SHARED_FF)), const((d, SHARED_FF)), const((SHARED_FF, d))],
        out_specs=tok,
        out_shape=jax.ShapeDtypeStruct((t, d), F32),
        scratch_shapes=[pltpu.VMEM((TOP_K, tt, d), F32), pltpu.SemaphoreType.DMA(())],
        compiler_params=_cparams(("arbitrary",)),
        name="combine",
    )(dest.reshape(t // tt, 1, tt * TOP_K), ys, wts_t, h2, x1, mod,
      p["ws_gate"], p["ws_up"], p["ws_down"])


def _rope_tables(s):
    half = QK_ROPE_DIM // 2
    inv_freq = ROPE_THETA ** (-jnp.arange(half, dtype=F32) / half)
    ang = jnp.arange(s, dtype=F32)[:, None] * inv_freq[None, :]
    cos, sin = jnp.cos(ang), jnp.sin(ang)
    z = lambda n: jnp.zeros((s, n), F32)
    tab_cos = jnp.concatenate([jnp.ones((s, QK_NOPE_DIM), F32), cos, cos, z(HEAD_PAD - QK_HEAD_DIM)], 1)
    tab_sa = jnp.concatenate([z(QK_NOPE_DIM), -sin, z(HEAD_PAD - QK_NOPE_DIM - half)], 1)
    tab_sb = jnp.concatenate([z(QK_NOPE_DIM + half), sin, z(HEAD_PAD - QK_HEAD_DIM)], 1)
    return tab_cos, tab_sa, tab_sb


def _prep_weights(norm1_w, w_in, pool_w, pool_scale, q_a_norm_w, w_q_b, kv_a_norm_w, w_kv_b,
                  q_norm_w, k_norm_w, w_o, norm2_w, w_router, w_gate, w_up, w_down,
                  ws_gate, ws_up, ws_down):
    d = D_MODEL
    c0 = POOL_WIDTH + Q_LORA_RANK + KV_LORA_RANK
    pad_h = HEAD_PAD - QK_HEAD_DIM
    w_in_p = jnp.concatenate(
        [w_in[:, :c0], jnp.zeros((d, QK_NOPE_DIM), F32), w_in[:, c0:], jnp.zeros((d, pad_h), F32)], 1)
    w_q = jnp.pad(w_q_b.reshape(Q_LORA_RANK, N_HEADS, QK_HEAD_DIM), ((0, 0), (0, 0), (0, pad_h)))
    kv = w_kv_b.reshape(KV_LORA_RANK, N_HEADS, QK_NOPE_DIM + V_HEAD_DIM)
    w_k = jnp.pad(kv[:, :, :QK_NOPE_DIM], ((0, 0), (0, 0), (0, HEAD_PAD - QK_NOPE_DIM)))
    w_v = kv[:, :, QK_NOPE_DIM:]
    w_r_t = w_router.T
    w_r_hi = w_r_t.astype(BF16)
    w_r_lo = (w_r_t - w_r_hi.astype(F32)).astype(BF16)
    q_gain = q_norm_w * (QK_HEAD_DIM ** -0.5 * LOG2E)
    bound = QK_HEAD_DIM * jnp.max(jnp.abs(q_gain)) * jnp.max(jnp.abs(k_norm_w)) * 1.02 + 0.25
    bound = bound.astype(BF16).astype(F32)
    lane = jnp.arange(HEAD_PAD) == SHIFT_LANE
    return dict(
        score_bound=bound,
        q_spare=jnp.where(lane, 1.0, 0.0).astype(F32).reshape(1, HEAD_PAD),
        k_spare=jnp.where(lane, -bound, 0.0).astype(F32).reshape(1, HEAD_PAD),
        norm1_w=norm1_w.reshape(1, d), w_in=w_in_p.astype(BF16),
        q_a_norm_w=q_a_norm_w.reshape(1, -1), w_q=w_q.reshape(Q_LORA_RANK, -1).astype(BF16),
        kv_a_norm_w=kv_a_norm_w.reshape(1, -1), w_k=w_k.reshape(KV_LORA_RANK, -1).astype(BF16),
        w_v=w_v.reshape(KV_LORA_RANK, -1).astype(BF16),
        q_norm_w=jnp.pad(q_gain, (0, pad_h)).reshape(1, HEAD_PAD),
        k_norm_w=jnp.pad(k_norm_w, (0, pad_h)).reshape(1, HEAD_PAD),
        pool_w=pool_w.astype(BF16), pool_scale=pool_scale.reshape(1, -1), w_o=w_o.astype(BF16),
        norm2_w=norm2_w.reshape(1, d), w_router_hi=w_r_hi, w_router_lo=w_r_lo,
        w_gate=w_gate.astype(BF16), w_up=w_up.astype(BF16), w_down=w_down.astype(BF16),
        ws_gate=ws_gate.astype(BF16), ws_up=ws_up.astype(BF16), ws_down=ws_down.astype(BF16))


def _tile(n, pref):
    return pref if n % pref == 0 else n


def _encoder(x, mod, router_bias, p):
    b, s, d = x.shape
    t = b * s
    ts = _tile(s, 512)
    p = dict(p)
    p["rope_cos"], p["rope_sa"], p["rope_sb"] = _rope_tables(s)

    u, q, k, v = _inproj(x, mod, p, ts)
    qt = jnp.swapaxes(q, 1, 2)
    vt = jnp.swapaxes(v, 1, 2).reshape(b, N_HEADS, V_HEAD_DIM, s)
    vt = jnp.concatenate([vt, jnp.ones((b, N_HEADS, V_AUG - V_HEAD_DIM, s), BF16)], axis=2)
    vt = vt.reshape(b, N_HEADS * V_AUG, s)
    tq, tk = _tile(s, 512), _tile(s, 512)
    attn = lax.cond(p["score_bound"] <= MAX_FIXED_SHIFT,
                    lambda a, c, e: _attention(a, c, e, tq, tk, online=False),
                    lambda a, c, e: _attention(a, c, e, tq, tk, online=True),
                    qt, k, vt)
    x1, h2, logits_t = _postmix(u, attn, x, mod, p, ts)
    tr = _tile(t, 512)
    idx, wts, pos, cnt = _router(logits_t, router_bias, tr)

    counts = cnt[:, 0].astype(jnp.int32)
    padded = (counts + ROW_BLOCK - 1) // ROW_BLOCK * ROW_BLOCK
    pad_end = jnp.cumsum(padded)
    dest = _dest_rows(idx, pos, pad_end - padded, tr)[:TOP_K].T.reshape(t * TOP_K)
    n_blocks = -(-t * TOP_K // ROW_BLOCK) + N_EXPERTS
    blk_row = jnp.arange(n_blocks, dtype=jnp.int32)[:, None] * ROW_BLOCK
    blk_exp = jnp.minimum(jnp.sum((pad_end[None, :] <= blk_row).astype(jnp.int32), axis=1),
                          N_EXPERTS - 1)
    n_used = (pad_end[-1:] // ROW_BLOCK).astype(jnp.int32)

    tt = _tile(s, 256)
    h2f = h2.reshape(t, d)
    xs = _dispatch(dest, h2f, n_blocks * ROW_BLOCK, tt)
    ys = _experts(blk_exp, n_used, xs, p["w_gate"], p["w_up"], p["w_down"])
    out = _combine(dest, ys, wts.T, h2f, x1.reshape(t, d), mod, p, tt, s)
    return out.reshape(b, s, d)


def kernel(x_prompt, x_sample, c_prompt, c_sample, w_ada, b_ada, norm1_w, w_in, pool_w, pool_scale,
           q_a_norm_w, w_q_b, kv_a_norm_w, w_kv_b, q_norm_w, k_norm_w, w_o, norm2_w, w_router,
           router_bias, w_gate, w_up, w_down, ws_gate, ws_up, ws_down):
    assert w_ada.shape[0] == 1, "single-layer encoder"
    p = _prep_weights(norm1_w[0], w_in[0], pool_w[0], pool_scale[0], q_a_norm_w[0], w_q_b[0],
                      kv_a_norm_w[0], w_kv_b[0], q_norm_w[0], k_norm_w[0], w_o[0], norm2_w[0],
                      w_router[0], w_gate[0], w_up[0], w_down[0], ws_gate[0], ws_up[0], ws_down[0])
    nb = x_prompt.shape[0]
    c = jnp.concatenate([c_prompt, c_sample], axis=0).astype(F32)
    mod = _adaln(c, w_ada[0], b_ada[0]).reshape(c.shape[0], 6, D_MODEL)
    y_prompt = _encoder(x_prompt, mod[:nb], router_bias[0], p)
    y_sample = _encoder(x_sample, mod[nb:], router_bias[0], p)
    return (y_prompt, y_sample)
```

```python
import functools

import jax
import jax.numpy as jnp
from jax import lax
from jax.experimental import pallas as pl
from jax.experimental.pallas import tpu as pltpu
from jax.experimental.pallas import tpu_sc as plsc

D_MODEL = 1024
POOL_WIDTH = 512
POOL_WINDOWS = (2, 4, 8, 16)
POOL_GROUP = 128
N_HEADS = 8
V_HEAD_DIM = 64
QK_NOPE_DIM = 64
QK_ROPE_DIM = 32
QK_HEAD_DIM = 96
Q_LORA_RANK = 256
KV_LORA_RANK = 128
ROPE_THETA = 10000.0
N_EXPERTS = 64
TOP_K = 6
N_GROUPS = 8
TOPK_GROUPS = 4
EXPERT_FF = 256
SHARED_FF = 256
ROUTED_SCALE = 2.5
EPS = 1e-6

LANES = 128
HEAD_PAD = 128
HALO = 16
ROW_BLOCK = 256
TOPK_PAD = 8
V_AUG = 80
SHIFT_LANE = QK_HEAD_DIM
MAX_FIXED_SHIFT = 40.0
LOG2E = 1.4426950408889634
PACK_W = D_MODEL // 2
SC_CORES = 2
SC_WORKERS = 32
SC_CHUNK = 128
VMEM_LIMIT = 48 * 1024 * 1024

F32 = jnp.float32
BF16 = jnp.bfloat16


def _cparams(sem):
    return pltpu.CompilerParams(dimension_semantics=sem, vmem_limit_bytes=VMEM_LIMIT)


def _silu(x):
    return x * (1.0 / (1.0 + jnp.exp(-x)))


def _adaln_kernel(c_ref, w_ref, b_ref, o_ref):
    c = c_ref[...]
    o_ref[...] = jnp.dot(_silu(c), w_ref[...], preferred_element_type=F32,
                         precision=lax.Precision.HIGHEST) + b_ref[...]


def _adaln(c, w_ada, b_ada):
    nb, d = c.shape
    n = w_ada.shape[1]
    tn = 1536
    return pl.pallas_call(
        _adaln_kernel,
        grid=(n // tn,),
        in_specs=[pl.BlockSpec((nb, d), lambda j: (0, 0)),
                  pl.BlockSpec((d, tn), lambda j: (0, j)),
                  pl.BlockSpec((1, tn), lambda j: (0, j))],
        out_specs=pl.BlockSpec((nb, tn), lambda j: (0, j)),
        out_shape=jax.ShapeDtypeStruct((nb, n), F32),
        compiler_params=_cparams(("arbitrary",)),
        name="adaln",
    )(c, w_ada, b_ada.reshape(1, n))


def _head_norm_rope(t, gain, cos, sin_a, sin_b, spare):
    r = lax.rsqrt(jnp.sum(t * t, axis=-1, keepdims=True) * (1.0 / QK_HEAD_DIM) + EPS)
    tn = t * r * gain
    return (tn * cos + pltpu.roll(tn, HEAD_PAD - QK_ROPE_DIM // 2, 1) * sin_a
            + pltpu.roll(tn, QK_ROPE_DIM // 2, 1) * sin_b + spare)


def _inproj_kernel(x_ref, mod_ref, n1w_ref, win_ref, qan_ref, wq_ref, kvan_ref, wk_ref, wv_ref,
                   qnw_ref, knw_ref, qsp_ref, ksp_ref, cos_ref, sa_ref, sb_ref,
                   u_ref, q_ref, k_ref, v_ref):
    x = x_ref[0]
    shift1 = mod_ref[0, 0:1, :]
    scale1 = mod_ref[0, 1:2, :]
    r = lax.rsqrt(jnp.mean(x * x, axis=-1, keepdims=True) + EPS)
    h = x * r * n1w_ref[...] * (1.0 + scale1) + shift1
    z = jnp.dot(h.astype(BF16), win_ref[...], preferred_element_type=F32)
    u_ref[0] = z[:, :POOL_WIDTH].astype(BF16)

    cq = z[:, POOL_WIDTH:POOL_WIDTH + Q_LORA_RANK]
    cqn = cq * lax.rsqrt(jnp.mean(cq * cq, axis=-1, keepdims=True) + EPS) * qan_ref[...]
    q = jnp.dot(cqn.astype(BF16), wq_ref[...], preferred_element_type=F32)

    c0 = POOL_WIDTH + Q_LORA_RANK
    ckv = z[:, c0:c0 + KV_LORA_RANK]
    ckvn = (ckv * lax.rsqrt(jnp.mean(ckv * ckv, axis=-1, keepdims=True) + EPS)
            * kvan_ref[...]).astype(BF16)
    kk = jnp.dot(ckvn, wk_ref[...], preferred_element_type=F32)
    v_ref[0] = jnp.dot(ckvn, wv_ref[...], preferred_element_type=F32).astype(BF16)
    kpe = z[:, c0 + KV_LORA_RANK:]

    cos = cos_ref[...]
    sa = sa_ref[...]
    sb = sb_ref[...]
    qg = qnw_ref[...]
    kg = knw_ref[...]
    qsp = qsp_ref[...]
    ksp = ksp_ref[...]
    for hd in range(N_HEADS):
        sl = slice(hd * HEAD_PAD, (hd + 1) * HEAD_PAD)
        q_ref[0, :, sl] = _head_norm_rope(q[:, sl], qg, cos, sa, sb, qsp).astype(BF16)
        k_ref[0, :, sl] = _head_norm_rope(kk[:, sl] + kpe, kg, cos, sa, sb, ksp).astype(BF16)


def _inproj(x, mod, p, ts):
    b, s, d = x.shape
    qk_w = N_HEADS * HEAD_PAD
    const = lambda shape: pl.BlockSpec(shape, lambda bi, i: (0,) * len(shape))
    row_tab = pl.BlockSpec((ts, HEAD_PAD), lambda bi, i: (i, 0))
    return pl.pallas_call(
        _inproj_kernel,
        grid=(b, s // ts),
        in_specs=[pl.BlockSpec((1, ts, d), lambda bi, i: (bi, i, 0)),
                  pl.BlockSpec((1, 6, d), lambda bi, i: (bi, 0, 0)),
                  const((1, d)), const((d, d)), const((1, Q_LORA_RANK)),
                  const((Q_LORA_RANK, qk_w)), const((1, KV_LORA_RANK)),
                  const((KV_LORA_RANK, qk_w)), const((KV_LORA_RANK, N_HEADS * V_HEAD_DIM)),
                  const((1, HEAD_PAD)), const((1, HEAD_PAD)), const((1, HEAD_PAD)),
                  const((1, HEAD_PAD)), row_tab, row_tab, row_tab],
        out_specs=[pl.BlockSpec((1, ts, POOL_WIDTH), lambda bi, i: (bi, i, 0)),
                   pl.BlockSpec((1, ts, qk_w), lambda bi, i: (bi, i, 0)),
                   pl.BlockSpec((1, ts, qk_w), lambda bi, i: (bi, i, 0)),
                   pl.BlockSpec((1, ts, N_HEADS * V_HEAD_DIM), lambda bi, i: (bi, i, 0))],
        out_shape=[jax.ShapeDtypeStruct((b, s, POOL_WIDTH), BF16),
                   jax.ShapeDtypeStruct((b, s, qk_w), BF16),
                   jax.ShapeDtypeStruct((b, s, qk_w), BF16),
                   jax.ShapeDtypeStruct((b, s, N_HEADS * V_HEAD_DIM), BF16)],
        compiler_params=_cparams(("parallel", "parallel")),
        name="inproj",
    )(x, mod, p["norm1_w"], p["w_in"], p["q_a_norm_w"], p["w_q"], p["kv_a_norm_w"], p["w_k"],
      p["w_v"], p["q_norm_w"], p["k_norm_w"], p["q_spare"], p["k_spare"],
      p["rope_cos"], p["rope_sa"], p["rope_sb"])


def _attn_shift_kernel(qt_ref, k_ref, vt_ref, o_ref, acc_a, acc_b, *, tk, nk):
    acc_a[...] = jnp.zeros_like(acc_a)
    acc_b[...] = jnp.zeros_like(acc_b)
    qa = qt_ref[0, :HEAD_PAD, :]
    qb = qt_ref[0, HEAD_PAD:, :]

    def body(c, _):
        off = pl.multiple_of(c * tk, tk)
        ks = k_ref[0, pl.ds(off, tk), :]
        vts = vt_ref[0, :, pl.ds(off, tk)]
        pa = jnp.exp2(jnp.dot(ks[:, :HEAD_PAD], qa, preferred_element_type=F32)).astype(BF16)
        pb = jnp.exp2(jnp.dot(ks[:, HEAD_PAD:], qb, preferred_element_type=F32)).astype(BF16)
        acc_a[...] += jnp.dot(vts[:V_AUG], pa, preferred_element_type=F32)
        acc_b[...] += jnp.dot(vts[V_AUG:], pb, preferred_element_type=F32)
        return 0

    lax.fori_loop(0, nk, body, 0)
    oa = acc_a[:V_HEAD_DIM] / acc_a[V_HEAD_DIM:V_HEAD_DIM + 1]
    ob = acc_b[:V_HEAD_DIM] / acc_b[V_HEAD_DIM:V_HEAD_DIM + 1]
    o_ref[0] = jnp.concatenate([oa, ob], axis=0).T.astype(BF16)


def _attn_online_kernel(qt_ref, k_ref, vt_ref, o_ref, acc_a, acc_b, m_ref, *, tk, nk):
    acc_a[...] = jnp.zeros_like(acc_a)
    acc_b[...] = jnp.zeros_like(acc_b)
    m_ref[...] = jnp.full_like(m_ref, -jnp.inf)
    qa = qt_ref[0, :HEAD_PAD, :]
    qb = qt_ref[0, HEAD_PAD:, :]

    def one_head(kh, q, vth, acc, row):
        s = jnp.dot(kh, q, preferred_element_type=F32)
        m_old = m_ref[row:row + 1, :]
        m_new = jnp.maximum(m_old, jnp.max(s, axis=0, keepdims=True))
        p = jnp.exp2(s - m_new).astype(BF16)
        acc[...] = acc[...] * jnp.exp2(m_old - m_new) + jnp.dot(vth, p, preferred_element_type=F32)
        m_ref[row:row + 1, :] = m_new

    def body(c, _):
        off = pl.multiple_of(c * tk, tk)
        ks = k_ref[0, pl.ds(off, tk), :]
        vts = vt_ref[0, :, pl.ds(off, tk)]
        one_head(ks[:, :HEAD_PAD], qa, vts[:V_AUG], acc_a, 0)
        one_head(ks[:, HEAD_PAD:], qb, vts[V_AUG:], acc_b, 1)
        return 0

    lax.fori_loop(0, nk, body, 0)
    oa = acc_a[:V_HEAD_DIM] / acc_a[V_HEAD_DIM:V_HEAD_DIM + 1]
    ob = acc_b[:V_HEAD_DIM] / acc_b[V_HEAD_DIM:V_HEAD_DIM + 1]
    o_ref[0] = jnp.concatenate([oa, ob], axis=0).T.astype(BF16)


def _attention(qt, k, vt, tq, tk, online):
    b, _, s = qt.shape
    scratch = [pltpu.VMEM((V_AUG, tq), F32), pltpu.VMEM((V_AUG, tq), F32)]
    if online:
        body = functools.partial(_attn_online_kernel, tk=tk, nk=s // tk)
        scratch.append(pltpu.VMEM((8, tq), F32))
    else:
        body = functools.partial(_attn_shift_kernel, tk=tk, nk=s // tk)
    return pl.pallas_call(
        body,
        grid=(b, N_HEADS // 2, s // tq),
        in_specs=[pl.BlockSpec((1, 2 * HEAD_PAD, tq), lambda bi, j, i: (bi, j, i)),
                  pl.BlockSpec((1, s, 2 * HEAD_PAD), lambda bi, j, i: (bi, 0, j)),
                  pl.BlockSpec((1, 2 * V_AUG, s), lambda bi, j, i: (bi, j, 0))],
        out_specs=pl.BlockSpec((1, tq, 2 * V_HEAD_DIM), lambda bi, j, i: (bi, i, j)),
        out_shape=jax.ShapeDtypeStruct((b, s, N_HEADS * V_HEAD_DIM), BF16),
        scratch_shapes=scratch,
        compiler_params=_cparams(("parallel", "parallel", "arbitrary")),
        name="attn_online" if online else "attn_shift",
    )(qt, k, vt)


def _postmix_kernel(u_ref, up_ref, un_ref, a_ref, x_ref, mod_ref, pw_ref, ps_ref, wo_ref, n2w_ref,
                    wrh_ref, wrl_ref, x1_ref, h2_ref, lg_ref, *, seq):
    i = pl.program_id(1)
    ts = u_ref.shape[1]
    ext_rows = ts + 2 * HALO
    ext = jnp.concatenate([up_ref[0], u_ref[0], un_ref[0]], axis=0).astype(F32)
    pos = i * ts - HALO + lax.broadcasted_iota(jnp.int32, (ext_rows, 1), 0)
    ext = jnp.where((pos >= 0) & (pos < seq), ext, 0.0)
    p = i * ts + lax.broadcasted_iota(jnp.int32, (ts, 1), 0)

    outs = []
    for g, w in enumerate(POOL_WINDOWS):
        left = w // 2
        right = w - 1 - left
        t = ext[:, g * POOL_GROUP:(g + 1) * POOL_GROUP]
        step = 1
        while step < w:
            t = t + pltpu.roll(t, ext_rows - step, 0)
            step *= 2
        win = pltpu.roll(t, left, 0)[HALO:HALO + ts]
        cnt = (jnp.minimum(p + right + 1, seq) - jnp.maximum(p - left, 0)).astype(F32)
        d = win / cnt - ext[HALO:HALO + ts, g * POOL_GROUP:(g + 1) * POOL_GROUP]
        outs.append(jnp.dot(d.astype(BF16), pw_ref[g], preferred_element_type=F32))
    pool = (jnp.concatenate(outs, axis=-1) * ps_ref[...]).astype(BF16)

    mix = (jnp.dot(pool, wo_ref[:POOL_WIDTH, :], preferred_element_type=F32)
           + jnp.dot(a_ref[0], wo_ref[POOL_WIDTH:, :], preferred_element_type=F32))
    x1 = x_ref[0] + mod_ref[0, 2:3, :] * mix
    x1_ref[0] = x1
    r = lax.rsqrt(jnp.mean(x1 * x1, axis=-1, keepdims=True) + EPS)
    h2 = x1 * r * n2w_ref[...] * (1.0 + mod_ref[0, 4:5, :]) + mod_ref[0, 3:4, :]
    h2_ref[0] = _pack_rows(h2)
    hi = h2.astype(BF16)
    lo = (h2 - hi.astype(F32)).astype(BF16)
    nt = (((1,), (1,)), ((), ()))
    lg_ref[...] = (lax.dot_general(wrh_ref[...], hi, nt, preferred_element_type=F32)
                   + lax.dot_general(wrl_ref[...], hi, nt, preferred_element_type=F32)
                   + lax.dot_general(wrh_ref[...], lo, nt, preferred_element_type=F32))


def _postmix(u, attn, x, mod, p, ts):
    b, s, d = x.shape
    nt = s // ts
    hb = ts // HALO
    const = lambda shape: pl.BlockSpec(shape, lambda bi, i: (0,) * len(shape))
    return pl.pallas_call(
        functools.partial(_postmix_kernel, seq=s),
        grid=(b, nt),
        in_specs=[pl.BlockSpec((1, ts, POOL_WIDTH), lambda bi, i: (bi, i, 0)),
                  pl.BlockSpec((1, HALO, POOL_WIDTH), lambda bi, i: (bi, jnp.maximum(i * hb - 1, 0), 0)),
                  pl.BlockSpec((1, HALO, POOL_WIDTH),
                               lambda bi, i: (bi, jnp.minimum((i + 1) * hb, s // HALO - 1), 0)),
                  pl.BlockSpec((1, ts, POOL_WIDTH), lambda bi, i: (bi, i, 0)),
                  pl.BlockSpec((1, ts, d), lambda bi, i: (bi, i, 0)),
                  pl.BlockSpec((1, 6, d), lambda bi, i: (bi, 0, 0)),
                  const((len(POOL_WINDOWS), POOL_GROUP, POOL_GROUP)), const((1, POOL_WIDTH)),
                  const((d, d)), const((1, d)), const((N_EXPERTS, d)), const((N_EXPERTS, d))],
        out_specs=[pl.BlockSpec((1, ts, d), lambda bi, i: (bi, i, 0)),
                   pl.BlockSpec((1, ts, PACK_W), lambda bi, i: (bi, i, 0)),
                   pl.BlockSpec((N_EXPERTS, ts), lambda bi, i: (0, bi * nt + i))],
        out_shape=[jax.ShapeDtypeStruct((b, s, d), F32),
                   jax.ShapeDtypeStruct((b, s, PACK_W), jnp.int32),
                   jax.ShapeDtypeStruct((N_EXPERTS, b * s), F32)],
        compiler_params=_cparams(("parallel", "parallel")),
        name="postmix",
    )(u, u, u, attn, x, mod, p["pool_w"], p["pool_scale"], p["w_o"], p["norm2_w"],
      p["w_router_hi"], p["w_router_lo"])


def _router_kernel(lg_ref, bias_ref, tri_ref, idx_ref, wts_ref, pos_ref, cnt_ref, carry_ref):
    @pl.when(pl.program_id(0) == 0)
    def _():
        carry_ref[...] = jnp.zeros_like(carry_ref)

    ts = lg_ref.shape[1]
    gsz = N_EXPERTS // N_GROUPS
    ninf = -jnp.inf
    scores = 1.0 / (1.0 + jnp.exp(-lg_ref[...]))
    choice = scores + bias_ref[...]
    sub = lax.broadcasted_iota(jnp.int32, (gsz, ts), 0)

    gs_rows = []
    for g in range(N_GROUPS):
        grp = choice[g * gsz:(g + 1) * gsz]
        m1 = jnp.max(grp, axis=0, keepdims=True)
        i1 = jnp.min(jnp.where(grp == m1, sub, gsz), axis=0, keepdims=True)
        m2 = jnp.max(jnp.where(sub == i1, ninf, grp), axis=0, keepdims=True)
        gs_rows.append(m1 + m2)
    gs = jnp.concatenate(gs_rows, axis=0)

    rank = jnp.zeros((N_GROUPS, ts), jnp.int32)
    for g in range(N_GROUPS):
        row = gs[g:g + 1]
        beats = (row > gs) | ((row == gs) & (sub > g))
        rank = rank + beats.astype(jnp.int32)
    gsel = rank < TOPK_GROUPS

    masked = jnp.concatenate(
        [jnp.where(gsel[g:g + 1], choice[g * gsz:(g + 1) * gsz], ninf) for g in range(N_GROUPS)],
        axis=0)
    eio = lax.broadcasted_iota(jnp.int32, (N_EXPERTS, ts), 0)
    idx_rows, w_rows, hits = [], [], []
    for _ in range(TOP_K):
        m = jnp.max(masked, axis=0, keepdims=True)
        i = jnp.min(jnp.where(masked == m, eio, N_EXPERTS), axis=0, keepdims=True)
        hit = eio == i
        w_rows.append(jnp.sum(jnp.where(hit, scores, 0.0), axis=0, keepdims=True))
        masked = jnp.where(hit, ninf, masked)
        idx_rows.append(i)
        hits.append(hit)

    wsum = functools.reduce(lambda a, c: a + c, w_rows)
    pad_i = [jnp.zeros((1, ts), jnp.int32)] * (TOPK_PAD - TOP_K)
    pad_f = [jnp.zeros((1, ts), F32)] * (TOPK_PAD - TOP_K)
    idx_ref[...] = jnp.concatenate(idx_rows + pad_i, axis=0)
    wts_ref[...] = jnp.concatenate([w / wsum * ROUTED_SCALE for w in w_rows] + pad_f, axis=0)

    sel = functools.reduce(lambda a, c: a | c, hits)
    onehot = jnp.where(sel, 1.0, 0.0).astype(BF16)
    run = jnp.dot(onehot, tri_ref[...], preferred_element_type=F32) + carry_ref[:, 0:1]
    pos_rows = [jnp.sum(jnp.where(h, run - 1.0, 0.0), axis=0, keepdims=True).astype(jnp.int32)
                for h in hits]
    pos_ref[...] = jnp.concatenate(pos_rows + pad_i, axis=0)
    total = run[:, ts - 1:ts]
    carry_ref[...] = jnp.broadcast_to(total, carry_ref.shape)
    cnt_ref[...] = jnp.broadcast_to(total, cnt_ref.shape)


def _router(logits_t, router_bias, ts):
    t = logits_t.shape[1]
    tri = jnp.triu(jnp.ones((ts, ts), BF16))
    tok = pl.BlockSpec((TOPK_PAD, ts), lambda i: (0, i))
    return pl.pallas_call(
        _router_kernel,
        grid=(t // ts,),
        in_specs=[pl.BlockSpec((N_EXPERTS, ts), lambda i: (0, i)),
                  pl.BlockSpec((N_EXPERTS, 1), lambda i: (0, 0)),
                  pl.BlockSpec((ts, ts), lambda i: (0, 0))],
        out_specs=[tok, tok, tok, pl.BlockSpec((N_EXPERTS, LANES), lambda i: (0, 0))],
        out_shape=[jax.ShapeDtypeStruct((TOPK_PAD, t), jnp.int32),
                   jax.ShapeDtypeStruct((TOPK_PAD, t), F32),
                   jax.ShapeDtypeStruct((TOPK_PAD, t), jnp.int32),
                   jax.ShapeDtypeStruct((N_EXPERTS, LANES), F32)],
        scratch_shapes=[pltpu.VMEM((N_EXPERTS, LANES), F32)],
        compiler_params=_cparams(("arbitrary",)),
        name="router",
    )(logits_t, router_bias.reshape(N_EXPERTS, 1), tri)


def _dest_kernel(idx_ref, pos_ref, start_ref, dest_ref):
    ts = idx_ref.shape[1]
    eio = lax.broadcasted_iota(jnp.int32, (N_EXPERTS, ts), 0)
    start = start_ref[...]
    rows = [pos_ref[k:k + 1, :]
            + jnp.sum(jnp.where(eio == idx_ref[k:k + 1, :], start, 0), axis=0, keepdims=True)
            for k in range(TOPK_PAD)]
    dest_ref[...] = jnp.concatenate(rows, axis=0)


def _dest_rows(idx, pos, pad_start, ts):
    t = idx.shape[1]
    tok = pl.BlockSpec((TOPK_PAD, ts), lambda i: (0, i))
    return pl.pallas_call(
        _dest_kernel,
        grid=(t // ts,),
        in_specs=[tok, tok, pl.BlockSpec((N_EXPERTS, 1), lambda i: (0, 0))],
        out_specs=tok,
        out_shape=jax.ShapeDtypeStruct((TOPK_PAD, t), jnp.int32),
        compiler_params=_cparams(("parallel",)),
        name="dest_rows",
    )(idx, pos, pad_start.reshape(N_EXPERTS, 1))


def _pack_rows(x):
    bits = lax.bitcast_convert_type(x.astype(BF16).astype(F32), jnp.int32)
    return bits[:, :PACK_W] | lax.shift_right_logical(bits[:, PACK_W:], 16)


def _unpack_rows(words):
    hi = lax.bitcast_convert_type(words & jnp.int32(-65536), F32)
    lo = lax.bitcast_convert_type(lax.shift_left(words, 16), F32)
    return jnp.concatenate([hi, lo], axis=1)


def _sc_mesh():
    return plsc.VectorSubcoreMesh(core_axis_name="core", subcore_axis_name="subcore")


def _sc_worker_base(rows_per_worker):
    return (lax.axis_index("subcore") * SC_CORES + lax.axis_index("core")) * rows_per_worker


def _sc_scatter_rows(x, dest, n_rows):
    t, w = x.shape
    per_worker = t // SC_WORKERS
    assert per_worker * SC_WORKERS == t and per_worker % SC_CHUNK == 0

    @functools.partial(
        pl.kernel, out_type=jax.ShapeDtypeStruct((n_rows, w), x.dtype), mesh=_sc_mesh(),
        scratch_types=[pltpu.VMEM((TOPK_PAD, SC_CHUNK), jnp.int32), pltpu.VMEM((SC_CHUNK, w), x.dtype),
                       pltpu.SemaphoreType.DMA])
    def scatter(x_hbm, dest_hbm, out_hbm, idx_v, rows_v, sem):
        base = _sc_worker_base(per_worker)

        @pl.loop(0, per_worker // SC_CHUNK)
        def _(c):
            off = pl.multiple_of(base + c * SC_CHUNK, SC_CHUNK)
            pltpu.sync_copy(dest_hbm.at[:, pl.ds(off, SC_CHUNK)], idx_v)
            pltpu.sync_copy(x_hbm.at[pl.ds(off, SC_CHUNK)], rows_v)
            copies = [pltpu.async_copy(rows_v, out_hbm.at[idx_v.at[k]], sem) for k in range(TOP_K)]
            for cp in copies:
                cp.wait()

    return scatter(x, dest)


def _sc_gather_rows(table, idx):
    m = idx.shape[0]
    w = table.shape[1]
    per_worker = m // SC_WORKERS
    assert per_worker * SC_WORKERS == m and per_worker % SC_CHUNK == 0

    @functools.partial(
        pl.kernel, out_type=jax.ShapeDtypeStruct((m, w), table.dtype), mesh=_sc_mesh(),
        scratch_types=[pltpu.VMEM((SC_CHUNK,), jnp.int32), pltpu.VMEM((SC_CHUNK, w), table.dtype),
                       pltpu.SemaphoreType.DMA])
    def gather(table_hbm, idx_hbm, out_hbm, idx_v, rows_v, sem):
        base = _sc_worker_base(per_worker)

        @pl.loop(0, per_worker // SC_CHUNK)
        def _(c):
            off = pl.multiple_of(base + c * SC_CHUNK, SC_CHUNK)
            pltpu.sync_copy(idx_hbm.at[pl.ds(off, SC_CHUNK)], idx_v)
            pltpu.async_copy(table_hbm.at[idx_v], rows_v, sem).wait()
            pltpu.sync_copy(rows_v, out_hbm.at[pl.ds(off, SC_CHUNK)])

    return gather(table, idx)


def _experts_kernel(blk_exp_ref, n_used_ref, xs_ref, wg_ref, wu_ref, wd_ref, ys_ref):
    @pl.when(pl.program_id(0) < n_used_ref[0])
    def _():
        xb = _unpack_rows(xs_ref[...]).astype(BF16)
        g = jnp.dot(xb, wg_ref[0], preferred_element_type=F32)
        u = jnp.dot(xb, wu_ref[0], preferred_element_type=F32)
        hb = (_silu(g) * u).astype(BF16)
        ys_ref[...] = _pack_rows(jnp.dot(hb, wd_ref[0], preferred_element_type=F32))


def _experts(blk_exp, n_used, xs, w_gate, w_up, w_down):
    n_rows, w = xs.shape
    d = D_MODEL
    n_blocks = n_rows // ROW_BLOCK
    row = lambda i, be, nu: (jnp.minimum(i, nu[0] - 1), 0)
    wsel = lambda i, be, nu: (be[i], 0, 0)
    return pl.pallas_call(
        _experts_kernel,
        grid_spec=pltpu.PrefetchScalarGridSpec(
            num_scalar_prefetch=2,
            grid=(n_blocks,),
            in_specs=[pl.BlockSpec((ROW_BLOCK, w), row),
                      pl.BlockSpec((1, d, EXPERT_FF), wsel),
                      pl.BlockSpec((1, d, EXPERT_FF), wsel),
                      pl.BlockSpec((1, EXPERT_FF, d), wsel)],
            out_specs=pl.BlockSpec((ROW_BLOCK, w), row)),
        out_shape=jax.ShapeDtypeStruct((n_rows, w), jnp.int32),
        compiler_params=_cparams(("arbitrary",)),
        name="experts",
    )(blk_exp, n_used, xs, w_gate, w_up, w_down)


def _combine_kernel(g_ref, w_ref, h2_ref, x1_ref, mod_ref, wsg_ref, wsu_ref, wsd_ref, out_ref):
    hb = _unpack_rows(h2_ref[...]).astype(BF16)
    g = jnp.dot(hb, wsg_ref[...], preferred_element_type=F32)
    u = jnp.dot(hb, wsu_ref[...], preferred_element_type=F32)
    acc = jnp.dot((_silu(g) * u).astype(BF16), wsd_ref[...], preferred_element_type=F32)
    w = w_ref[...]
    for k in range(TOP_K):
        acc = acc + _unpack_rows(g_ref[k]) * w[:, k:k + 1]
    out_ref[...] = x1_ref[...] + mod_ref[0, 5:6, :] * acc


def _combine(gathered, wts_t, h2p, x1, mod, p, tt, seq):
    t, d = x1.shape
    per_seq = seq // tt
    tok = pl.BlockSpec((tt, d), lambda i: (i, 0))
    const = lambda shape: pl.BlockSpec(shape, lambda i: (0,) * len(shape))
    return pl.pallas_call(
        _combine_kernel,
        grid=(t // tt,),
        in_specs=[pl.BlockSpec((TOP_K, tt, PACK_W), lambda i: (0, i, 0)),
                  pl.BlockSpec((tt, TOPK_PAD), lambda i: (i, 0)),
                  pl.BlockSpec((tt, PACK_W), lambda i: (i, 0)),
                  tok,
                  pl.BlockSpec((1, 6, d), lambda i: (i // per_seq, 0, 0)),
                  const((d, SHARED_FF)), const((d, SHARED_FF)), const((SHARED_FF, d))],
        out_specs=tok,
        out_shape=jax.ShapeDtypeStruct((t, d), F32),
        compiler_params=_cparams(("parallel",)),
        name="combine",
    )(gathered, wts_t, h2p, x1, mod, p["ws_gate"], p["ws_up"], p["ws_down"])


def _rope_tables(s):
    half = QK_ROPE_DIM // 2
    inv_freq = ROPE_THETA ** (-jnp.arange(half, dtype=F32) / half)
    ang = jnp.arange(s, dtype=F32)[:, None] * inv_freq[None, :]
    cos, sin = jnp.cos(ang), jnp.sin(ang)
    z = lambda n: jnp.zeros((s, n), F32)
    tab_cos = jnp.concatenate([jnp.ones((s, QK_NOPE_DIM), F32), cos, cos, z(HEAD_PAD - QK_HEAD_DIM)], 1)
    tab_sa = jnp.concatenate([z(QK_NOPE_DIM), -sin, z(HEAD_PAD - QK_NOPE_DIM - half)], 1)
    tab_sb = jnp.concatenate([z(QK_NOPE_DIM + half), sin, z(HEAD_PAD - QK_HEAD_DIM)], 1)
    return tab_cos, tab_sa, tab_sb


def _prep_weights(norm1_w, w_in, pool_w, pool_scale, q_a_norm_w, w_q_b, kv_a_norm_w, w_kv_b,
                  q_norm_w, k_norm_w, w_o, norm2_w, w_router, w_gate, w_up, w_down,
                  ws_gate, ws_up, ws_down):
    d = D_MODEL
    c0 = POOL_WIDTH + Q_LORA_RANK + KV_LORA_RANK
    pad_h = HEAD_PAD - QK_HEAD_DIM
    w_in_p = jnp.concatenate(
        [w_in[:, :c0], jnp.zeros((d, QK_NOPE_DIM), F32), w_in[:, c0:], jnp.zeros((d, pad_h), F32)], 1)
    w_q = jnp.pad(w_q_b.reshape(Q_LORA_RANK, N_HEADS, QK_HEAD_DIM), ((0, 0), (0, 0), (0, pad_h)))
    kv = w_kv_b.reshape(KV_LORA_RANK, N_HEADS, QK_NOPE_DIM + V_HEAD_DIM)
    w_k = jnp.pad(kv[:, :, :QK_NOPE_DIM], ((0, 0), (0, 0), (0, HEAD_PAD - QK_NOPE_DIM)))
    w_v = kv[:, :, QK_NOPE_DIM:]
    w_r_t = w_router.T
    w_r_hi = w_r_t.astype(BF16)
    w_r_lo = (w_r_t - w_r_hi.astype(F32)).astype(BF16)
    q_gain = q_norm_w * (QK_HEAD_DIM ** -0.5 * LOG2E)
    bound = QK_HEAD_DIM * jnp.max(jnp.abs(q_gain)) * jnp.max(jnp.abs(k_norm_w)) * 1.02 + 0.25
    bound = bound.astype(BF16).astype(F32)
    lane = jnp.arange(HEAD_PAD) == SHIFT_LANE
    return dict(
        score_bound=bound,
        q_spare=jnp.where(lane, 1.0, 0.0).astype(F32).reshape(1, HEAD_PAD),
        k_spare=jnp.where(lane, -bound, 0.0).astype(F32).reshape(1, HEAD_PAD),
        norm1_w=norm1_w.reshape(1, d), w_in=w_in_p.astype(BF16),
        q_a_norm_w=q_a_norm_w.reshape(1, -1), w_q=w_q.reshape(Q_LORA_RANK, -1).astype(BF16),
        kv_a_norm_w=kv_a_norm_w.reshape(1, -1), w_k=w_k.reshape(KV_LORA_RANK, -1).astype(BF16),
        w_v=w_v.reshape(KV_LORA_RANK, -1).astype(BF16),
        q_norm_w=jnp.pad(q_gain, (0, pad_h)).reshape(1, HEAD_PAD),
        k_norm_w=jnp.pad(k_norm_w, (0, pad_h)).reshape(1, HEAD_PAD),
        pool_w=pool_w.astype(BF16), pool_scale=pool_scale.reshape(1, -1), w_o=w_o.astype(BF16),
        norm2_w=norm2_w.reshape(1, d), w_router_hi=w_r_hi, w_router_lo=w_r_lo,
        w_gate=w_gate.astype(BF16), w_up=w_up.astype(BF16), w_down=w_down.astype(BF16),
        ws_gate=ws_gate.astype(BF16), ws_up=ws_up.astype(BF16), ws_down=ws_down.astype(BF16))


def _tile(n, pref):
    return pref if n % pref == 0 else n


def _encoder(x, mod, router_bias, p):
    b, s, d = x.shape
    t = b * s
    ts = _tile(s, 512)
    p = dict(p)
    p["rope_cos"], p["rope_sa"], p["rope_sb"] = _rope_tables(s)

    u, q, k, v = _inproj(x, mod, p, ts)
    qt = jnp.swapaxes(q, 1, 2)
    vt = jnp.swapaxes(v, 1, 2).reshape(b, N_HEADS, V_HEAD_DIM, s)
    vt = jnp.concatenate([vt, jnp.ones((b, N_HEADS, V_AUG - V_HEAD_DIM, s), BF16)], axis=2)
    vt = vt.reshape(b, N_HEADS * V_AUG, s)
    tq, tk = _tile(s, 512), _tile(s, 512)
    attn = lax.cond(p["score_bound"] <= MAX_FIXED_SHIFT,
                    lambda a, c, e: _attention(a, c, e, tq, tk, online=False),
                    lambda a, c, e: _attention(a, c, e, tq, tk, online=True),
                    qt, k, vt)
    x1, h2p, logits_t = _postmix(u, attn, x, mod, p, ts)
    tr = _tile(t, 512)
    idx, wts, pos, cnt = _router(logits_t, router_bias, tr)

    counts = cnt[:, 0].astype(jnp.int32)
    padded = (counts + ROW_BLOCK - 1) // ROW_BLOCK * ROW_BLOCK
    pad_end = jnp.cumsum(padded)
    dest = _dest_rows(idx, pos, pad_end - padded, tr)
    n_blocks = -(-t * TOP_K // ROW_BLOCK) + N_EXPERTS
    blk_row = jnp.arange(n_blocks, dtype=jnp.int32)[:, None] * ROW_BLOCK
    blk_exp = jnp.minimum(jnp.sum((pad_end[None, :] <= blk_row).astype(jnp.int32), axis=1),
                          N_EXPERTS - 1)
    n_used = (pad_end[-1:] // ROW_BLOCK).astype(jnp.int32)

    tt = _tile(s, 256)
    h2f = h2p.reshape(t, PACK_W)
    xs = _sc_scatter_rows(h2f, dest, n_blocks * ROW_BLOCK)
    ys = _experts(blk_exp, n_used, xs, p["w_gate"], p["w_up"], p["w_down"])
    gathered = _sc_gather_rows(ys, dest[:TOP_K].reshape(TOP_K * t)).reshape(TOP_K, t, PACK_W)
    out = _combine(gathered, wts.T, h2f, x1.reshape(t, d), mod, p, tt, s)
    return out.reshape(b, s, d)


def kernel(x_prompt, x_sample, c_prompt, c_sample, w_ada, b_ada, norm1_w, w_in, pool_w, pool_scale,
           q_a_norm_w, w_q_b, kv_a_norm_w, w_kv_b, q_norm_w, k_norm_w, w_o, norm2_w, w_router,
           router_bias, w_gate, w_up, w_down, ws_gate, ws_up, ws_down):
    assert w_ada.shape[0] == 1, "single-layer encoder"
    p = _prep_weights(norm1_w[0], w_in[0], pool_w[0], pool_scale[0], q_a_norm_w[0], w_q_b[0],
                      kv_a_norm_w[0], w_kv_b[0], q_norm_w[0], k_norm_w[0], w_o[0], norm2_w[0],
                      w_router[0], w_gate[0], w_up[0], w_down[0], ws_gate[0], ws_up[0], ws_down[0])
    nb = x_prompt.shape[0]
    c = jnp.concatenate([c_prompt, c_sample], axis=0).astype(F32)
    mod = _adaln(c, w_ada[0], b_ada[0]).reshape(c.shape[0], 6, D_MODEL)
    y_prompt = _encoder(x_prompt, mod[:nb], router_bias[0], p)
    y_sample = _encoder(x_sample, mod[nb:], router_bias[0], p)
    return (y_prompt, y_sample)
```

```python
import functools

import jax
import jax.numpy as jnp
from jax import lax
from jax.experimental import pallas as pl
from jax.experimental.pallas import tpu as pltpu
from jax.experimental.pallas import tpu_sc as plsc

D_MODEL = 1024
POOL_WIDTH = 512
POOL_WINDOWS = (2, 4, 8, 16)
POOL_GROUP = 128
N_HEADS = 8
V_HEAD_DIM = 64
QK_NOPE_DIM = 64
QK_ROPE_DIM = 32
QK_HEAD_DIM = 96
Q_LORA_RANK = 256
KV_LORA_RANK = 128
ROPE_THETA = 10000.0
N_EXPERTS = 64
TOP_K = 6
N_GROUPS = 8
TOPK_GROUPS = 4
EXPERT_FF = 256
SHARED_FF = 256
ROUTED_SCALE = 2.5
EPS = 1e-6

LANES = 128
HEAD_PAD = 128
HALO = 16
ROW_BLOCK = 512
TOPK_PAD = 8
V_AUG = 80
SHIFT_LANE = QK_HEAD_DIM
MAX_FIXED_SHIFT = 40.0
LOG2E = 1.4426950408889634
PACK_W = D_MODEL // 2
SC_CORES = 2
SC_WORKERS = 32
SC_CHUNK = 128
VMEM_LIMIT = 48 * 1024 * 1024

F32 = jnp.float32
BF16 = jnp.bfloat16


def _cparams(sem):
    return pltpu.CompilerParams(dimension_semantics=sem, vmem_limit_bytes=VMEM_LIMIT)


def _silu(x):
    return x * (1.0 / (1.0 + jnp.exp(-x)))


def _adaln_kernel(c_ref, w_ref, b_ref, o_ref):
    c = c_ref[...]
    o_ref[...] = jnp.dot(_silu(c), w_ref[...], preferred_element_type=F32,
                         precision=lax.Precision.HIGHEST) + b_ref[...]


def _adaln(c, w_ada, b_ada):
    nb, d = c.shape
    n = w_ada.shape[1]
    tn = 1536
    return pl.pallas_call(
        _adaln_kernel,
        grid=(n // tn,),
        in_specs=[pl.BlockSpec((nb, d), lambda j: (0, 0)),
                  pl.BlockSpec((d, tn), lambda j: (0, j)),
                  pl.BlockSpec((1, tn), lambda j: (0, j))],
        out_specs=pl.BlockSpec((nb, tn), lambda j: (0, j)),
        out_shape=jax.ShapeDtypeStruct((nb, n), F32),
        compiler_params=_cparams(("arbitrary",)),
        name="adaln",
    )(c, w_ada, b_ada.reshape(1, n))


def _head_norm_rope(t, gain, cos, sin_a, sin_b, spare):
    r = lax.rsqrt(jnp.sum(t * t, axis=-1, keepdims=True) * (1.0 / QK_HEAD_DIM) + EPS)
    tn = t * r * gain
    return (tn * cos + pltpu.roll(tn, HEAD_PAD - QK_ROPE_DIM // 2, 1) * sin_a
            + pltpu.roll(tn, QK_ROPE_DIM // 2, 1) * sin_b + spare)


def _inproj_kernel(x_ref, mod_ref, n1w_ref, win_ref, qan_ref, wq_ref, kvan_ref, wk_ref, wv_ref,
                   qnw_ref, knw_ref, qsp_ref, ksp_ref, cos_ref, sa_ref, sb_ref,
                   u_ref, q_ref, k_ref, v_ref):
    x = x_ref[0]
    shift1 = mod_ref[0, 0:1, :]
    scale1 = mod_ref[0, 1:2, :]
    r = lax.rsqrt(jnp.mean(x * x, axis=-1, keepdims=True) + EPS)
    h = x * r * n1w_ref[...] * (1.0 + scale1) + shift1
    z = jnp.dot(h.astype(BF16), win_ref[...], preferred_element_type=F32)
    u_ref[0] = z[:, :POOL_WIDTH].astype(BF16)

    cq = z[:, POOL_WIDTH:POOL_WIDTH + Q_LORA_RANK]
    cqn = cq * lax.rsqrt(jnp.mean(cq * cq, axis=-1, keepdims=True) + EPS) * qan_ref[...]
    q = jnp.dot(cqn.astype(BF16), wq_ref[...], preferred_element_type=F32)

    c0 = POOL_WIDTH + Q_LORA_RANK
    ckv = z[:, c0:c0 + KV_LORA_RANK]
    ckvn = (ckv * lax.rsqrt(jnp.mean(ckv * ckv, axis=-1, keepdims=True) + EPS)
            * kvan_ref[...]).astype(BF16)
    kk = jnp.dot(ckvn, wk_ref[...], preferred_element_type=F32)
    v_ref[0] = jnp.dot(ckvn, wv_ref[...], preferred_element_type=F32).astype(BF16)
    kpe = z[:, c0 + KV_LORA_RANK:]

    cos = cos_ref[...]
    sa = sa_ref[...]
    sb = sb_ref[...]
    qg = qnw_ref[...]
    kg = knw_ref[...]
    qsp = qsp_ref[...]
    ksp = ksp_ref[...]
    for hd in range(N_HEADS):
        sl = slice(hd * HEAD_PAD, (hd + 1) * HEAD_PAD)
        q_ref[0, :, sl] = _head_norm_rope(q[:, sl], qg, cos, sa, sb, qsp).astype(BF16)
        k_ref[0, :, sl] = _head_norm_rope(kk[:, sl] + kpe, kg, cos, sa, sb, ksp).astype(BF16)


def _inproj(x, mod, p, ts):
    b, s, d = x.shape
    qk_w = N_HEADS * HEAD_PAD
    const = lambda shape: pl.BlockSpec(shape, lambda bi, i: (0,) * len(shape))
    row_tab = pl.BlockSpec((ts, HEAD_PAD), lambda bi, i: (i, 0))
    return pl.pallas_call(
        _inproj_kernel,
        grid=(b, s // ts),
        in_specs=[pl.BlockSpec((1, ts, d), lambda bi, i: (bi, i, 0)),
                  pl.BlockSpec((1, 6, d), lambda bi, i: (bi, 0, 0)),
                  const((1, d)), const((d, d)), const((1, Q_LORA_RANK)),
                  const((Q_LORA_RANK, qk_w)), const((1, KV_LORA_RANK)),
                  const((KV_LORA_RANK, qk_w)), const((KV_LORA_RANK, N_HEADS * V_HEAD_DIM)),
                  const((1, HEAD_PAD)), const((1, HEAD_PAD)), const((1, HEAD_PAD)),
                  const((1, HEAD_PAD)), row_tab, row_tab, row_tab],
        out_specs=[pl.BlockSpec((1, ts, POOL_WIDTH), lambda bi, i: (bi, i, 0)),
                   pl.BlockSpec((1, ts, qk_w), lambda bi, i: (bi, i, 0)),
                   pl.BlockSpec((1, ts, qk_w), lambda bi, i: (bi, i, 0)),
                   pl.BlockSpec((1, ts, N_HEADS * V_HEAD_DIM), lambda bi, i: (bi, i, 0))],
        out_shape=[jax.ShapeDtypeStruct((b, s, POOL_WIDTH), BF16),
                   jax.ShapeDtypeStruct((b, s, qk_w), BF16),
                   jax.ShapeDtypeStruct((b, s, qk_w), BF16),
                   jax.ShapeDtypeStruct((b, s, N_HEADS * V_HEAD_DIM), BF16)],
        compiler_params=_cparams(("parallel", "parallel")),
        name="inproj",
    )(x, mod, p["norm1_w"], p["w_in"], p["q_a_norm_w"], p["w_q"], p["kv_a_norm_w"], p["w_k"],
      p["w_v"], p["q_norm_w"], p["k_norm_w"], p["q_spare"], p["k_spare"],
      p["rope_cos"], p["rope_sa"], p["rope_sb"])


def _attn_kernel(online_ref, qt_ref, k_ref, vt_ref, o_ref, acc_a, acc_b, m_ref, *, tk, tk_online):
    s_len = k_ref.shape[1]
    acc_a[...] = jnp.zeros_like(acc_a)
    acc_b[...] = jnp.zeros_like(acc_b)
    qa = qt_ref[0, :HEAD_PAD, :]
    qb = qt_ref[0, HEAD_PAD:, :]

    @pl.when(online_ref[0] == 0)
    def _():
        def body(c, _):
            off = pl.multiple_of(c * tk, tk)
            ks = k_ref[0, pl.ds(off, tk), :]
            vts = vt_ref[0, :, pl.ds(off, tk)]
            pa = jnp.exp2(jnp.dot(ks[:, :HEAD_PAD], qa, preferred_element_type=F32)).astype(BF16)
            pb = jnp.exp2(jnp.dot(ks[:, HEAD_PAD:], qb, preferred_element_type=F32)).astype(BF16)
            acc_a[...] += jnp.dot(vts[:V_AUG], pa, preferred_element_type=F32)
            acc_b[...] += jnp.dot(vts[V_AUG:], pb, preferred_element_type=F32)
            return 0

        lax.fori_loop(0, s_len // tk, body, 0)

    @pl.when(online_ref[0] != 0)
    def _():
        m_ref[...] = jnp.full_like(m_ref, -jnp.inf)

        def one_head(kh, q, vth, acc, row):
            s = jnp.dot(kh, q, preferred_element_type=F32)
            m_old = m_ref[row:row + 1, :]
            m_new = jnp.maximum(m_old, jnp.max(s, axis=0, keepdims=True))
            p = jnp.exp2(s - m_new).astype(BF16)
            acc[...] = acc[...] * jnp.exp2(m_old - m_new) + jnp.dot(vth, p, preferred_element_type=F32)
            m_ref[row:row + 1, :] = m_new

        def body(c, _):
            off = pl.multiple_of(c * tk_online, tk_online)
            ks = k_ref[0, pl.ds(off, tk_online), :]
            vts = vt_ref[0, :, pl.ds(off, tk_online)]
            one_head(ks[:, :HEAD_PAD], qa, vts[:V_AUG], acc_a, 0)
            one_head(ks[:, HEAD_PAD:], qb, vts[V_AUG:], acc_b, 1)
            return 0

        lax.fori_loop(0, s_len // tk_online, body, 0)

    oa = acc_a[:V_HEAD_DIM] / acc_a[V_HEAD_DIM:V_HEAD_DIM + 1]
    ob = acc_b[:V_HEAD_DIM] / acc_b[V_HEAD_DIM:V_HEAD_DIM + 1]
    o_ref[0] = jnp.concatenate([oa, ob], axis=0).T.astype(BF16)


def _attention(online, qt, k, vt, tq, tk, tk_online):
    b, _, s = qt.shape
    return pl.pallas_call(
        functools.partial(_attn_kernel, tk=tk, tk_online=tk_online),
        grid_spec=pltpu.PrefetchScalarGridSpec(
            num_scalar_prefetch=1,
            grid=(b, N_HEADS // 2, s // tq),
            in_specs=[pl.BlockSpec((1, 2 * HEAD_PAD, tq), lambda bi, j, i, on: (bi, j, i)),
                      pl.BlockSpec((1, s, 2 * HEAD_PAD), lambda bi, j, i, on: (bi, 0, j)),
                      pl.BlockSpec((1, 2 * V_AUG, s), lambda bi, j, i, on: (bi, j, 0))],
            out_specs=pl.BlockSpec((1, tq, 2 * V_HEAD_DIM), lambda bi, j, i, on: (bi, i, j)),
            scratch_shapes=[pltpu.VMEM((V_AUG, tq), F32), pltpu.VMEM((V_AUG, tq), F32),
                            pltpu.VMEM((8, tq), F32)]),
        out_shape=jax.ShapeDtypeStruct((b, s, N_HEADS * V_HEAD_DIM), BF16),
        compiler_params=_cparams(("parallel", "parallel", "arbitrary")),
        name="attn",
    )(online, qt, k, vt)


def _postmix_kernel(u_ref, up_ref, un_ref, a_ref, x_ref, mod_ref, pw_ref, ps_ref, wo_ref, n2w_ref,
                    wrh_ref, wrl_ref, x1_ref, h2_ref, lg_ref, *, seq):
    i = pl.program_id(1)
    ts = u_ref.shape[1]
    ext_rows = ts + 2 * HALO
    ext = jnp.concatenate([up_ref[0], u_ref[0], un_ref[0]], axis=0).astype(F32)
    pos = i * ts - HALO + lax.broadcasted_iota(jnp.int32, (ext_rows, 1), 0)
    ext = jnp.where((pos >= 0) & (pos < seq), ext, 0.0)
    p = i * ts + lax.broadcasted_iota(jnp.int32, (ts, 1), 0)

    outs = []
    for g, w in enumerate(POOL_WINDOWS):
        left = w // 2
        right = w - 1 - left
        t = ext[:, g * POOL_GROUP:(g + 1) * POOL_GROUP]
        step = 1
        while step < w:
            t = t + pltpu.roll(t, ext_rows - step, 0)
            step *= 2
        win = pltpu.roll(t, left, 0)[HALO:HALO + ts]
        cnt = (jnp.minimum(p + right + 1, seq) - jnp.maximum(p - left, 0)).astype(F32)
        d = win / cnt - ext[HALO:HALO + ts, g * POOL_GROUP:(g + 1) * POOL_GROUP]
        outs.append(jnp.dot(d.astype(BF16), pw_ref[g], preferred_element_type=F32))
    pool = (jnp.concatenate(outs, axis=-1) * ps_ref[...]).astype(BF16)

    mix = (jnp.dot(pool, wo_ref[:POOL_WIDTH, :], preferred_element_type=F32)
           + jnp.dot(a_ref[0], wo_ref[POOL_WIDTH:, :], preferred_element_type=F32))
    x1 = x_ref[0] + mod_ref[0, 2:3, :] * mix
    x1_ref[0] = x1
    r = lax.rsqrt(jnp.mean(x1 * x1, axis=-1, keepdims=True) + EPS)
    h2 = x1 * r * n2w_ref[...] * (1.0 + mod_ref[0, 4:5, :]) + mod_ref[0, 3:4, :]
    h2_ref[0] = _pack_rows(h2)
    hi = h2.astype(BF16)
    lo = (h2 - hi.astype(F32)).astype(BF16)
    nt = (((1,), (1,)), ((), ()))
    lg_ref[...] = (lax.dot_general(wrh_ref[...], hi, nt, preferred_element_type=F32)
                   + lax.dot_general(wrl_ref[...], hi, nt, preferred_element_type=F32)
                   + lax.dot_general(wrh_ref[...], lo, nt, preferred_element_type=F32))


def _postmix(u, attn, x, mod, p, ts):
    b, s, d = x.shape
    nt = s // ts
    hb = ts // HALO
    const = lambda shape: pl.BlockSpec(shape, lambda bi, i: (0,) * len(shape))
    return pl.pallas_call(
        functools.partial(_postmix_kernel, seq=s),
        grid=(b, nt),
        in_specs=[pl.BlockSpec((1, ts, POOL_WIDTH), lambda bi, i: (bi, i, 0)),
                  pl.BlockSpec((1, HALO, POOL_WIDTH), lambda bi, i: (bi, jnp.maximum(i * hb - 1, 0), 0)),
                  pl.BlockSpec((1, HALO, POOL_WIDTH),
                               lambda bi, i: (bi, jnp.minimum((i + 1) * hb, s // HALO - 1), 0)),
                  pl.BlockSpec((1, ts, POOL_WIDTH), lambda bi, i: (bi, i, 0)),
                  pl.BlockSpec((1, ts, d), lambda bi, i: (bi, i, 0)),
                  pl.BlockSpec((1, 6, d), lambda bi, i: (bi, 0, 0)),
                  const((len(POOL_WINDOWS), POOL_GROUP, POOL_GROUP)), const((1, POOL_WIDTH)),
                  const((d, d)), const((1, d)), const((N_EXPERTS, d)), const((N_EXPERTS, d))],
        out_specs=[pl.BlockSpec((1, ts, d), lambda bi, i: (bi, i, 0)),
                   pl.BlockSpec((1, ts, PACK_W), lambda bi, i: (bi, i, 0)),
                   pl.BlockSpec((N_EXPERTS, ts), lambda bi, i: (0, bi * nt + i))],
        out_shape=[jax.ShapeDtypeStruct((b, s, d), F32),
                   jax.ShapeDtypeStruct((b, s, PACK_W), jnp.int32),
                   jax.ShapeDtypeStruct((N_EXPERTS, b * s), F32)],
        compiler_params=_cparams(("parallel", "parallel")),
        name="postmix",
    )(u, u, u, attn, x, mod, p["pool_w"], p["pool_scale"], p["w_o"], p["norm2_w"],
      p["w_router_hi"], p["w_router_lo"])


def _router_kernel(lg_ref, bias_ref, tri_ref, idx_ref, wts_ref, pos_ref, cnt_ref, carry_ref):
    @pl.when(pl.program_id(0) == 0)
    def _():
        carry_ref[...] = jnp.zeros_like(carry_ref)

    ts = lg_ref.shape[1]
    gsz = N_EXPERTS // N_GROUPS
    ninf = -jnp.inf
    scores = 1.0 / (1.0 + jnp.exp(-lg_ref[...]))
    choice = scores + bias_ref[...]
    sub = lax.broadcasted_iota(jnp.int32, (gsz, ts), 0)

    gs_rows = []
    for g in range(N_GROUPS):
        grp = choice[g * gsz:(g + 1) * gsz]
        m1 = jnp.max(grp, axis=0, keepdims=True)
        i1 = jnp.min(jnp.where(grp == m1, sub, gsz), axis=0, keepdims=True)
        m2 = jnp.max(jnp.where(sub == i1, ninf, grp), axis=0, keepdims=True)
        gs_rows.append(m1 + m2)
    gs = jnp.concatenate(gs_rows, axis=0)

    rank = jnp.zeros((N_GROUPS, ts), jnp.int32)
    for g in range(N_GROUPS):
        row = gs[g:g + 1]
        beats = (row > gs) | ((row == gs) & (sub > g))
        rank = rank + beats.astype(jnp.int32)
    gsel = rank < TOPK_GROUPS

    masked = jnp.concatenate(
        [jnp.where(gsel[g:g + 1], choice[g * gsz:(g + 1) * gsz], ninf) for g in range(N_GROUPS)],
        axis=0)
    eio = lax.broadcasted_iota(jnp.int32, (N_EXPERTS, ts), 0)
    idx_rows, w_rows, hits = [], [], []
    for _ in range(TOP_K):
        m = jnp.max(masked, axis=0, keepdims=True)
        i = jnp.min(jnp.where(masked == m, eio, N_EXPERTS), axis=0, keepdims=True)
        hit = eio == i
        w_rows.append(jnp.sum(jnp.where(hit, scores, 0.0), axis=0, keepdims=True))
        masked = jnp.where(hit, ninf, masked)
        idx_rows.append(i)
        hits.append(hit)

    wsum = functools.reduce(lambda a, c: a + c, w_rows)
    pad_i = [jnp.zeros((1, ts), jnp.int32)] * (TOPK_PAD - TOP_K)
    pad_f = [jnp.zeros((1, ts), F32)] * (TOPK_PAD - TOP_K)
    idx_ref[...] = jnp.concatenate(idx_rows + pad_i, axis=0)
    wts_ref[...] = jnp.concatenate([w / wsum * ROUTED_SCALE for w in w_rows] + pad_f, axis=0)

    sel = functools.reduce(lambda a, c: a | c, hits)
    onehot = jnp.where(sel, 1.0, 0.0).astype(BF16)
    run = jnp.dot(onehot, tri_ref[...], preferred_element_type=F32) + carry_ref[:, 0:1]
    pos_rows = [jnp.sum(jnp.where(h, run - 1.0, 0.0), axis=0, keepdims=True).astype(jnp.int32)
                for h in hits]
    pos_ref[...] = jnp.concatenate(pos_rows + pad_i, axis=0)
    total = run[:, ts - 1:ts]
    carry_ref[...] = jnp.broadcast_to(total, carry_ref.shape)
    cnt_ref[...] = jnp.broadcast_to(total, cnt_ref.shape)


def _router(logits_t, router_bias, ts):
    t = logits_t.shape[1]
    tri = jnp.triu(jnp.ones((ts, ts), BF16))
    tok = pl.BlockSpec((TOPK_PAD, ts), lambda i: (0, i))
    return pl.pallas_call(
        _router_kernel,
        grid=(t // ts,),
        in_specs=[pl.BlockSpec((N_EXPERTS, ts), lambda i: (0, i)),
                  pl.BlockSpec((N_EXPERTS, 1), lambda i: (0, 0)),
                  pl.BlockSpec((ts, ts), lambda i: (0, 0))],
        out_specs=[tok, tok, tok, pl.BlockSpec((N_EXPERTS, LANES), lambda i: (0, 0))],
        out_shape=[jax.ShapeDtypeStruct((TOPK_PAD, t), jnp.int32),
                   jax.ShapeDtypeStruct((TOPK_PAD, t), F32),
                   jax.ShapeDtypeStruct((TOPK_PAD, t), jnp.int32),
                   jax.ShapeDtypeStruct((N_EXPERTS, LANES), F32)],
        scratch_shapes=[pltpu.VMEM((N_EXPERTS, LANES), F32)],
        compiler_params=_cparams(("arbitrary",)),
        name="router",
    )(logits_t, router_bias.reshape(N_EXPERTS, 1), tri)


def _dest_kernel(idx_ref, pos_ref, start_ref, dest_ref):
    ts = idx_ref.shape[1]
    eio = lax.broadcasted_iota(jnp.int32, (N_EXPERTS, ts), 0)
    start = start_ref[...]
    rows = [pos_ref[k:k + 1, :]
            + jnp.sum(jnp.where(eio == idx_ref[k:k + 1, :], start, 0), axis=0, keepdims=True)
            for k in range(TOPK_PAD)]
    dest_ref[...] = jnp.concatenate(rows, axis=0)


def _dest_rows(idx, pos, pad_start, ts):
    t = idx.shape[1]
    tok = pl.BlockSpec((TOPK_PAD, ts), lambda i: (0, i))
    return pl.pallas_call(
        _dest_kernel,
        grid=(t // ts,),
        in_specs=[tok, tok, pl.BlockSpec((N_EXPERTS, 1), lambda i: (0, 0))],
        out_specs=tok,
        out_shape=jax.ShapeDtypeStruct((TOPK_PAD, t), jnp.int32),
        compiler_params=_cparams(("parallel",)),
        name="dest_rows",
    )(idx, pos, pad_start.reshape(N_EXPERTS, 1))


def _pack_rows(x):
    bits = lax.bitcast_convert_type(x.astype(BF16).astype(F32), jnp.int32)
    return bits[:, :PACK_W] | lax.shift_right_logical(bits[:, PACK_W:], 16)


def _unpack_rows(words):
    hi = lax.bitcast_convert_type(words & jnp.int32(-65536), F32)
    lo = lax.bitcast_convert_type(lax.shift_left(words, 16), F32)
    return jnp.concatenate([hi, lo], axis=1)


def _sc_mesh():
    return plsc.VectorSubcoreMesh(core_axis_name="core", subcore_axis_name="subcore")


def _sc_worker_base(rows_per_worker):
    return (lax.axis_index("subcore") * SC_CORES + lax.axis_index("core")) * rows_per_worker


def _sc_scatter_rows(x, dest, n_rows):
    t, w = x.shape
    per_worker = t // SC_WORKERS
    assert per_worker * SC_WORKERS == t and per_worker % SC_CHUNK == 0

    @functools.partial(
        pl.kernel, out_type=jax.ShapeDtypeStruct((n_rows, w), x.dtype), mesh=_sc_mesh(),
        scratch_types=[pltpu.VMEM((TOPK_PAD, SC_CHUNK), jnp.int32), pltpu.VMEM((SC_CHUNK, w), x.dtype),
                       pltpu.SemaphoreType.DMA])
    def scatter(x_hbm, dest_hbm, out_hbm, idx_v, rows_v, sem):
        base = _sc_worker_base(per_worker)

        @pl.loop(0, per_worker // SC_CHUNK)
        def _(c):
            off = pl.multiple_of(base + c * SC_CHUNK, SC_CHUNK)
            pltpu.sync_copy(dest_hbm.at[:, pl.ds(off, SC_CHUNK)], idx_v)
            pltpu.sync_copy(x_hbm.at[pl.ds(off, SC_CHUNK)], rows_v)
            copies = [pltpu.async_copy(rows_v, out_hbm.at[idx_v.at[k]], sem) for k in range(TOP_K)]
            for cp in copies:
                cp.wait()

    return scatter(x, dest)


def _sc_gather_rows(table, idx):
    m = idx.shape[0]
    w = table.shape[1]
    per_worker = m // SC_WORKERS
    assert per_worker * SC_WORKERS == m and per_worker % SC_CHUNK == 0

    @functools.partial(
        pl.kernel, out_type=jax.ShapeDtypeStruct((m, w), table.dtype), mesh=_sc_mesh(),
        scratch_types=[pltpu.VMEM((SC_CHUNK,), jnp.int32), pltpu.VMEM((SC_CHUNK, w), table.dtype),
                       pltpu.SemaphoreType.DMA])
    def gather(table_hbm, idx_hbm, out_hbm, idx_v, rows_v, sem):
        base = _sc_worker_base(per_worker)

        @pl.loop(0, per_worker // SC_CHUNK)
        def _(c):
            off = pl.multiple_of(base + c * SC_CHUNK, SC_CHUNK)
            pltpu.sync_copy(idx_hbm.at[pl.ds(off, SC_CHUNK)], idx_v)
            pltpu.async_copy(table_hbm.at[idx_v], rows_v, sem).wait()
            pltpu.sync_copy(rows_v, out_hbm.at[pl.ds(off, SC_CHUNK)])

    return gather(table, idx)


def _experts_kernel(blk_exp_ref, n_used_ref, xs_ref, wg_ref, wu_ref, wd_ref, ys_ref):
    @pl.when(pl.program_id(0) < n_used_ref[0])
    def _():
        xb = _unpack_rows(xs_ref[...]).astype(BF16)
        g = jnp.dot(xb, wg_ref[0], preferred_element_type=F32)
        u = jnp.dot(xb, wu_ref[0], preferred_element_type=F32)
        hb = (_silu(g) * u).astype(BF16)
        ys_ref[...] = _pack_rows(jnp.dot(hb, wd_ref[0], preferred_element_type=F32))


def _experts(blk_exp, n_used, xs, w_gate, w_up, w_down):
    n_rows, w = xs.shape
    d = D_MODEL
    n_blocks = n_rows // ROW_BLOCK
    row = lambda i, be, nu: (jnp.minimum(i, nu[0] - 1), 0)
    wsel = lambda i, be, nu: (be[i], 0, 0)
    return pl.pallas_call(
        _experts_kernel,
        grid_spec=pltpu.PrefetchScalarGridSpec(
            num_scalar_prefetch=2,
            grid=(n_blocks,),
            in_specs=[pl.BlockSpec((ROW_BLOCK, w), row),
                      pl.BlockSpec((1, d, EXPERT_FF), wsel),
                      pl.BlockSpec((1, d, EXPERT_FF), wsel),
                      pl.BlockSpec((1, EXPERT_FF, d), wsel)],
            out_specs=pl.BlockSpec((ROW_BLOCK, w), row)),
        out_shape=jax.ShapeDtypeStruct((n_rows, w), jnp.int32),
        compiler_params=_cparams(("arbitrary",)),
        name="experts",
    )(blk_exp, n_used, xs, w_gate, w_up, w_down)


def _combine_kernel(g_ref, w_ref, h2_ref, x1_ref, mod_ref, wsg_ref, wsu_ref, wsd_ref, out_ref):
    hb = _unpack_rows(h2_ref[...]).astype(BF16)
    g = jnp.dot(hb, wsg_ref[...], preferred_element_type=F32)
    u = jnp.dot(hb, wsu_ref[...], preferred_element_type=F32)
    acc = jnp.dot((_silu(g) * u).astype(BF16), wsd_ref[...], preferred_element_type=F32)
    w = w_ref[...]
    for k in range(TOP_K):
        acc = acc + _unpack_rows(g_ref[k]) * w[:, k:k + 1]
    out_ref[...] = x1_ref[...] + mod_ref[0, 5:6, :] * acc


def _combine(gathered, wts_t, h2p, x1, mod, p, tt, seq):
    t, d = x1.shape
    per_seq = seq // tt
    tok = pl.BlockSpec((tt, d), lambda i: (i, 0))
    const = lambda shape: pl.BlockSpec(shape, lambda i: (0,) * len(shape))
    return pl.pallas_call(
        _combine_kernel,
        grid=(t // tt,),
        in_specs=[pl.BlockSpec((TOP_K, tt, PACK_W), lambda i: (0, i, 0)),
                  pl.BlockSpec((tt, TOPK_PAD), lambda i: (i, 0)),
                  pl.BlockSpec((tt, PACK_W), lambda i: (i, 0)),
                  tok,
                  pl.BlockSpec((1, 6, d), lambda i: (i // per_seq, 0, 0)),
                  const((d, SHARED_FF)), const((d, SHARED_FF)), const((SHARED_FF, d))],
        out_specs=tok,
        out_shape=jax.ShapeDtypeStruct((t, d), F32),
        compiler_params=_cparams(("parallel",)),
        name="combine",
    )(gathered, wts_t, h2p, x1, mod, p["ws_gate"], p["ws_up"], p["ws_down"])


def _rope_tables(s):
    half = QK_ROPE_DIM // 2
    inv_freq = ROPE_THETA ** (-jnp.arange(half, dtype=F32) / half)
    ang = jnp.arange(s, dtype=F32)[:, None] * inv_freq[None, :]
    cos, sin = jnp.cos(ang), jnp.sin(ang)
    z = lambda n: jnp.zeros((s, n), F32)
    tab_cos = jnp.concatenate([jnp.ones((s, QK_NOPE_DIM), F32), cos, cos, z(HEAD_PAD - QK_HEAD_DIM)], 1)
    tab_sa = jnp.concatenate([z(QK_NOPE_DIM), -sin, z(HEAD_PAD - QK_NOPE_DIM - half)], 1)
    tab_sb = jnp.concatenate([z(QK_NOPE_DIM + half), sin, z(HEAD_PAD - QK_HEAD_DIM)], 1)
    return tab_cos, tab_sa, tab_sb


def _prep_weights(norm1_w, w_in, pool_w, pool_scale, q_a_norm_w, w_q_b, kv_a_norm_w, w_kv_b,
                  q_norm_w, k_norm_w, w_o, norm2_w, w_router, w_gate, w_up, w_down,
                  ws_gate, ws_up, ws_down):
    d = D_MODEL
    c0 = POOL_WIDTH + Q_LORA_RANK + KV_LORA_RANK
    pad_h = HEAD_PAD - QK_HEAD_DIM
    w_in_p = jnp.concatenate(
        [w_in[:, :c0], jnp.zeros((d, QK_NOPE_DIM), F32), w_in[:, c0:], jnp.zeros((d, pad_h), F32)], 1)
    w_q = jnp.pad(w_q_b.reshape(Q_LORA_RANK, N_HEADS, QK_HEAD_DIM), ((0, 0), (0, 0), (0, pad_h)))
    kv = w_kv_b.reshape(KV_LORA_RANK, N_HEADS, QK_NOPE_DIM + V_HEAD_DIM)
    w_k = jnp.pad(kv[:, :, :QK_NOPE_DIM], ((0, 0), (0, 0), (0, HEAD_PAD - QK_NOPE_DIM)))
    w_v = kv[:, :, QK_NOPE_DIM:]
    w_r_t = w_router.T
    w_r_hi = w_r_t.astype(BF16)
    w_r_lo = (w_r_t - w_r_hi.astype(F32)).astype(BF16)
    q_gain = q_norm_w * (QK_HEAD_DIM ** -0.5 * LOG2E)
    bound = QK_HEAD_DIM * jnp.max(jnp.abs(q_gain)) * jnp.max(jnp.abs(k_norm_w)) * 1.02 + 0.25
    bound = bound.astype(BF16).astype(F32)
    lane = jnp.arange(HEAD_PAD) == SHIFT_LANE
    return dict(
        score_bound=bound,
        q_spare=jnp.where(lane, 1.0, 0.0).astype(F32).reshape(1, HEAD_PAD),
        k_spare=jnp.where(lane, -bound, 0.0).astype(F32).reshape(1, HEAD_PAD),
        norm1_w=norm1_w.reshape(1, d), w_in=w_in_p.astype(BF16),
        q_a_norm_w=q_a_norm_w.reshape(1, -1), w_q=w_q.reshape(Q_LORA_RANK, -1).astype(BF16),
        kv_a_norm_w=kv_a_norm_w.reshape(1, -1), w_k=w_k.reshape(KV_LORA_RANK, -1).astype(BF16),
        w_v=w_v.reshape(KV_LORA_RANK, -1).astype(BF16),
        q_norm_w=jnp.pad(q_gain, (0, pad_h)).reshape(1, HEAD_PAD),
        k_norm_w=jnp.pad(k_norm_w, (0, pad_h)).reshape(1, HEAD_PAD),
        pool_w=pool_w.astype(BF16), pool_scale=pool_scale.reshape(1, -1), w_o=w_o.astype(BF16),
        norm2_w=norm2_w.reshape(1, d), w_router_hi=w_r_hi, w_router_lo=w_r_lo,
        w_gate=w_gate.astype(BF16), w_up=w_up.astype(BF16), w_down=w_down.astype(BF16),
        ws_gate=ws_gate.astype(BF16), ws_up=ws_up.astype(BF16), ws_down=ws_down.astype(BF16))


def _tile(n, pref):
    return pref if n % pref == 0 else n


def _encoder(x, mod, router_bias, p):
    b, s, d = x.shape
    t = b * s
    ts = _tile(s, 512)
    p = dict(p)
    p["rope_cos"], p["rope_sa"], p["rope_sb"] = _rope_tables(s)

    u, q, k, v = _inproj(x, mod, p, ts)
    qt = jnp.swapaxes(q, 1, 2)
    vt = jnp.swapaxes(v, 1, 2).reshape(b, N_HEADS, V_HEAD_DIM, s)
    vt = jnp.concatenate([vt, jnp.ones((b, N_HEADS, V_AUG - V_HEAD_DIM, s), BF16)], axis=2)
    vt = vt.reshape(b, N_HEADS * V_AUG, s)
    online = (p["score_bound"] > MAX_FIXED_SHIFT).astype(jnp.int32).reshape(1)
    attn = _attention(online, qt, k, vt, _tile(s, 1024), _tile(s, 1024), _tile(s, 512))
    x1, h2p, logits_t = _postmix(u, attn, x, mod, p, ts)
    tr = _tile(t, 512)
    idx, wts, pos, cnt = _router(logits_t, router_bias, tr)

    counts = cnt[:, 0].astype(jnp.int32)
    padded = (counts + ROW_BLOCK - 1) // ROW_BLOCK * ROW_BLOCK
    pad_end = jnp.cumsum(padded)
    dest = _dest_rows(idx, pos, pad_end - padded, tr)
    n_blocks = -(-t * TOP_K // ROW_BLOCK) + N_EXPERTS
    blk_row = jnp.arange(n_blocks, dtype=jnp.int32)[:, None] * ROW_BLOCK
    blk_exp = jnp.minimum(jnp.sum((pad_end[None, :] <= blk_row).astype(jnp.int32), axis=1),
                          N_EXPERTS - 1)
    n_used = (pad_end[-1:] // ROW_BLOCK).astype(jnp.int32)

    tt = _tile(s, 256)
    h2f = h2p.reshape(t, PACK_W)
    xs = _sc_scatter_rows(h2f, dest, n_blocks * ROW_BLOCK)
    ys = _experts(blk_exp, n_used, xs, p["w_gate"], p["w_up"], p["w_down"])
    gathered = _sc_gather_rows(ys, dest[:TOP_K].reshape(TOP_K * t)).reshape(TOP_K, t, PACK_W)
    out = _combine(gathered, wts.T, h2f, x1.reshape(t, d), mod, p, tt, s)
    return out.reshape(b, s, d)


def kernel(x_prompt, x_sample, c_prompt, c_sample, w_ada, b_ada, norm1_w, w_in, pool_w, pool_scale,
           q_a_norm_w, w_q_b, kv_a_norm_w, w_kv_b, q_norm_w, k_norm_w, w_o, norm2_w, w_router,
           router_bias, w_gate, w_up, w_down, ws_gate, ws_up, ws_down):
    assert w_ada.shape[0] == 1, "single-layer encoder"
    p = _prep_weights(norm1_w[0], w_in[0], pool_w[0], pool_scale[0], q_a_norm_w[0], w_q_b[0],
                      kv_a_norm_w[0], w_kv_b[0], q_norm_w[0], k_norm_w[0], w_o[0], norm2_w[0],
                      w_router[0], w_gate[0], w_up[0], w_down[0], ws_gate[0], ws_up[0], ws_down[0])
    nb = x_prompt.shape[0]
    c = jnp.concatenate([c_prompt, c_sample], axis=0).astype(F32)
    mod = _adaln(c, w_ada[0], b_ada[0]).reshape(c.shape[0], 6, D_MODEL)
    y_prompt = _encoder(x_prompt, mod[:nb], router_bias[0], p)
    y_sample = _encoder(x_sample, mod[nb:], router_bias[0], p)
    return (y_prompt, y_sample)
```

```python
import functools

import jax
import jax.numpy as jnp
from jax import lax
from jax.experimental import pallas as pl
from jax.experimental.pallas import tpu as pltpu
from jax.experimental.pallas import tpu_sc as plsc

D_MODEL = 1024
POOL_WIDTH = 512
POOL_WINDOWS = (2, 4, 8, 16)
POOL_GROUP = 128
N_HEADS = 8
V_HEAD_DIM = 64
QK_NOPE_DIM = 64
QK_ROPE_DIM = 32
QK_HEAD_DIM = 96
Q_LORA_RANK = 256
KV_LORA_RANK = 128
ROPE_THETA = 10000.0
N_EXPERTS = 64
TOP_K = 6
N_GROUPS = 8
TOPK_GROUPS = 4
EXPERT_FF = 256
SHARED_FF = 256
ROUTED_SCALE = 2.5
EPS = 1e-6

LANES = 128
HEAD_PAD = 128
HALO = 16
ROW_BLOCK = 512
TOPK_PAD = 8
V_AUG = 80
SHIFT_LANE = QK_HEAD_DIM
MAX_FIXED_SHIFT = 40.0
LOG2E = 1.4426950408889634
PACK_W = D_MODEL // 2
SC_CORES = 2
SC_WORKERS = 32
SC_CHUNK = 128
VMEM_LIMIT = 48 * 1024 * 1024

F32 = jnp.float32
BF16 = jnp.bfloat16


def _cparams(sem):
    return pltpu.CompilerParams(dimension_semantics=sem, vmem_limit_bytes=VMEM_LIMIT)


def _silu(x):
    return x * (1.0 / (1.0 + jnp.exp(-x)))


def _adaln_kernel(c_ref, w_ref, b_ref, o_ref):
    c = c_ref[...]
    o_ref[...] = jnp.dot(_silu(c), w_ref[...], preferred_element_type=F32,
                         precision=lax.Precision.HIGHEST) + b_ref[...]


def _adaln(c, w_ada, b_ada):
    nb, d = c.shape
    n = w_ada.shape[1]
    tn = 1536
    return pl.pallas_call(
        _adaln_kernel,
        grid=(n // tn,),
        in_specs=[pl.BlockSpec((nb, d), lambda j: (0, 0)),
                  pl.BlockSpec((d, tn), lambda j: (0, j)),
                  pl.BlockSpec((1, tn), lambda j: (0, j))],
        out_specs=pl.BlockSpec((nb, tn), lambda j: (0, j)),
        out_shape=jax.ShapeDtypeStruct((nb, n), F32),
        compiler_params=_cparams(("arbitrary",)),
        name="adaln",
    )(c, w_ada, b_ada.reshape(1, n))


def _inproj_kernel(x_ref, mod_ref, n1w_ref, win_ref, qan_ref, wqt_ref, kvan_ref, wk_ref, wvt_ref,
                   qg_ref, kg_ref, ksp_ref, cos_ref, sa_ref, sb_ref, cost_ref, sint_ref,
                   u_ref, qt_ref, k_ref, vt_ref):
    ts = x_ref.shape[1]
    half = QK_ROPE_DIM // 2
    x = x_ref[0]
    shift1 = mod_ref[0, 0:1, :]
    scale1 = mod_ref[0, 1:2, :]
    r = lax.rsqrt(jnp.mean(x * x, axis=-1, keepdims=True) + EPS)
    h = x * r * n1w_ref[...] * (1.0 + scale1) + shift1
    z = jnp.dot(h.astype(BF16), win_ref[...], preferred_element_type=F32)
    u_ref[0] = z[:, :POOL_WIDTH].astype(BF16)

    cq = z[:, POOL_WIDTH:POOL_WIDTH + Q_LORA_RANK]
    cqn = cq * lax.rsqrt(jnp.mean(cq * cq, axis=-1, keepdims=True) + EPS) * qan_ref[...]
    qt = jnp.dot(wqt_ref[...], cqn.T.astype(BF16), preferred_element_type=F32)
    reps = ts // LANES
    qg = jnp.concatenate([qg_ref[...]] * reps, axis=1)
    cost = cost_ref[...]
    sint = sint_ref[...]
    spare = jnp.where(lax.broadcasted_iota(jnp.int32, (HEAD_PAD - QK_HEAD_DIM, ts), 0) == 0, 1.0, 0.0)
    for hd in range(N_HEADS):
        t = qt[hd * HEAD_PAD:(hd + 1) * HEAD_PAD]
        rn = lax.rsqrt(jnp.sum(t * t, axis=0, keepdims=True) * (1.0 / QK_HEAD_DIM) + EPS)
        tn = t * rn * qg
        t1 = tn[QK_NOPE_DIM:QK_NOPE_DIM + half]
        t2 = tn[QK_NOPE_DIM + half:QK_HEAD_DIM]
        out = jnp.concatenate([tn[:QK_NOPE_DIM], t1 * cost - t2 * sint, t1 * sint + t2 * cost, spare],
                              axis=0)
        qt_ref[0, hd * HEAD_PAD:(hd + 1) * HEAD_PAD, :] = out.astype(BF16)

    c0 = POOL_WIDTH + Q_LORA_RANK
    ckv = z[:, c0:c0 + KV_LORA_RANK]
    ckvn = ckv * lax.rsqrt(jnp.mean(ckv * ckv, axis=-1, keepdims=True) + EPS) * kvan_ref[...]

    vt = jnp.dot(wvt_ref[...], ckvn.T.astype(BF16), preferred_element_type=F32)
    ones = jnp.ones((V_AUG - V_HEAD_DIM, ts), BF16)
    for hd in range(N_HEADS):
        vt_ref[0, hd * V_AUG:hd * V_AUG + V_HEAD_DIM, :] = (
            vt[hd * V_HEAD_DIM:(hd + 1) * V_HEAD_DIM].astype(BF16))
        vt_ref[0, hd * V_AUG + V_HEAD_DIM:(hd + 1) * V_AUG, :] = ones

    kk = jnp.dot(ckvn.astype(BF16), wk_ref[...], preferred_element_type=F32)
    kpe = z[:, c0 + KV_LORA_RANK:]
    kg = kg_ref[...]
    ksp = ksp_ref[...]
    pe_ssq = jnp.sum(kpe * kpe, axis=-1, keepdims=True)
    pg = kpe * kg
    pe_rot = (pg * cos_ref[...] + pltpu.roll(pg, HEAD_PAD - half, 1) * sa_ref[...]
              + pltpu.roll(pg, half, 1) * sb_ref[...])
    for hd in range(N_HEADS):
        t = kk[:, hd * HEAD_PAD:(hd + 1) * HEAD_PAD]
        ssq = jnp.sum(t * t, axis=-1, keepdims=True) + pe_ssq
        rn = lax.rsqrt(ssq * (1.0 / QK_HEAD_DIM) + EPS)
        k_ref[0, :, hd * HEAD_PAD:(hd + 1) * HEAD_PAD] = ((t * kg + pe_rot) * rn + ksp).astype(BF16)


def _inproj(x, mod, p, ts):
    b, s, d = x.shape
    qk_w = N_HEADS * HEAD_PAD
    half = QK_ROPE_DIM // 2
    const = lambda shape: pl.BlockSpec(shape, lambda bi, i: (0,) * len(shape))
    row_tab = pl.BlockSpec((ts, HEAD_PAD), lambda bi, i: (i, 0))
    col_tab = pl.BlockSpec((half, ts), lambda bi, i: (0, i))
    return pl.pallas_call(
        _inproj_kernel,
        grid=(b, s // ts),
        in_specs=[pl.BlockSpec((1, ts, d), lambda bi, i: (bi, i, 0)),
                  pl.BlockSpec((1, 6, d), lambda bi, i: (bi, 0, 0)),
                  const((1, d)), const((d, d)), const((1, Q_LORA_RANK)),
                  const((qk_w, Q_LORA_RANK)), const((1, KV_LORA_RANK)),
                  const((KV_LORA_RANK, qk_w)), const((N_HEADS * V_HEAD_DIM, KV_LORA_RANK)),
                  const((HEAD_PAD, LANES)), const((1, HEAD_PAD)), const((1, HEAD_PAD)),
                  row_tab, row_tab, row_tab, col_tab, col_tab],
        out_specs=[pl.BlockSpec((1, ts, POOL_WIDTH), lambda bi, i: (bi, i, 0)),
                   pl.BlockSpec((1, qk_w, ts), lambda bi, i: (bi, 0, i)),
                   pl.BlockSpec((1, ts, qk_w), lambda bi, i: (bi, i, 0)),
                   pl.BlockSpec((1, N_HEADS * V_AUG, ts), lambda bi, i: (bi, 0, i))],
        out_shape=[jax.ShapeDtypeStruct((b, s, POOL_WIDTH), BF16),
                   jax.ShapeDtypeStruct((b, qk_w, s), BF16),
                   jax.ShapeDtypeStruct((b, s, qk_w), BF16),
                   jax.ShapeDtypeStruct((b, N_HEADS * V_AUG, s), BF16)],
        compiler_params=_cparams(("parallel", "parallel")),
        name="inproj",
    )(x, mod, p["norm1_w"], p["w_in"], p["q_a_norm_w"], p["w_q_t"], p["kv_a_norm_w"], p["w_k"],
      p["w_v_t"], p["q_gain_col"], p["k_norm_w"], p["k_spare"],
      p["rope_cos"], p["rope_sa"], p["rope_sb"], p["rope_cos_t"], p["rope_sin_t"])


def _attn_kernel(online_ref, qt_ref, k_ref, vt_ref, o_ref, acc_a, acc_b, m_ref, *, tk, tk_online):
    s_len = k_ref.shape[1]
    acc_a[...] = jnp.zeros_like(acc_a)
    acc_b[...] = jnp.zeros_like(acc_b)
    qa = qt_ref[0, :HEAD_PAD, :]
    qb = qt_ref[0, HEAD_PAD:, :]

    @pl.when(online_ref[0] == 0)
    def _():
        def body(c, _):
            off = pl.multiple_of(c * tk, tk)
            ks = k_ref[0, pl.ds(off, tk), :]
            vts = vt_ref[0, :, pl.ds(off, tk)]
            pa = jnp.exp2(jnp.dot(ks[:, :HEAD_PAD], qa, preferred_element_type=F32)).astype(BF16)
            pb = jnp.exp2(jnp.dot(ks[:, HEAD_PAD:], qb, preferred_element_type=F32)).astype(BF16)
            acc_a[...] += jnp.dot(vts[:V_AUG], pa, preferred_element_type=F32)
            acc_b[...] += jnp.dot(vts[V_AUG:], pb, preferred_element_type=F32)
            return 0

        lax.fori_loop(0, s_len // tk, body, 0)

    @pl.when(online_ref[0] != 0)
    def _():
        m_ref[...] = jnp.full_like(m_ref, -jnp.inf)

        def one_head(kh, q, vth, acc, row):
            s = jnp.dot(kh, q, preferred_element_type=F32)
            m_old = m_ref[row:row + 1, :]
            m_new = jnp.maximum(m_old, jnp.max(s, axis=0, keepdims=True))
            p = jnp.exp2(s - m_new).astype(BF16)
            acc[...] = acc[...] * jnp.exp2(m_old - m_new) + jnp.dot(vth, p, preferred_element_type=F32)
            m_ref[row:row + 1, :] = m_new

        def body(c, _):
            off = pl.multiple_of(c * tk_online, tk_online)
            ks = k_ref[0, pl.ds(off, tk_online), :]
            vts = vt_ref[0, :, pl.ds(off, tk_online)]
            one_head(ks[:, :HEAD_PAD], qa, vts[:V_AUG], acc_a, 0)
            one_head(ks[:, HEAD_PAD:], qb, vts[V_AUG:], acc_b, 1)
            return 0

        lax.fori_loop(0, s_len // tk_online, body, 0)

    oa = acc_a[:V_HEAD_DIM] / acc_a[V_HEAD_DIM:V_HEAD_DIM + 1]
    ob = acc_b[:V_HEAD_DIM] / acc_b[V_HEAD_DIM:V_HEAD_DIM + 1]
    o_ref[0] = jnp.concatenate([oa, ob], axis=0).T.astype(BF16)


def _attention(online, qt, k, vt, tq, tk, tk_online):
    b, _, s = qt.shape
    return pl.pallas_call(
        functools.partial(_attn_kernel, tk=tk, tk_online=tk_online),
        grid_spec=pltpu.PrefetchScalarGridSpec(
            num_scalar_prefetch=1,
            grid=(b, N_HEADS // 2, s // tq),
            in_specs=[pl.BlockSpec((1, 2 * HEAD_PAD, tq), lambda bi, j, i, on: (bi, j, i)),
                      pl.BlockSpec((1, s, 2 * HEAD_PAD), lambda bi, j, i, on: (bi, 0, j)),
                      pl.BlockSpec((1, 2 * V_AUG, s), lambda bi, j, i, on: (bi, j, 0))],
            out_specs=pl.BlockSpec((1, tq, 2 * V_HEAD_DIM), lambda bi, j, i, on: (bi, i, j)),
            scratch_shapes=[pltpu.VMEM((V_AUG, tq), F32), pltpu.VMEM((V_AUG, tq), F32),
                            pltpu.VMEM((8, tq), F32)]),
        out_shape=jax.ShapeDtypeStruct((b, s, N_HEADS * V_HEAD_DIM), BF16),
        compiler_params=_cparams(("parallel", "parallel", "arbitrary")),
        name="attn",
    )(online, qt, k, vt)


def _postmix_kernel(u_ref, up_ref, un_ref, a_ref, x_ref, mod_ref, pw_ref, ps_ref, wo_ref, n2w_ref,
                    wrh_ref, wrl_ref, x1_ref, h2_ref, lg_ref, *, seq):
    i = pl.program_id(1)
    ts = u_ref.shape[1]
    ext_rows = ts + 2 * HALO
    ext = jnp.concatenate([up_ref[0], u_ref[0], un_ref[0]], axis=0).astype(F32)
    pos = i * ts - HALO + lax.broadcasted_iota(jnp.int32, (ext_rows, 1), 0)
    ext = jnp.where((pos >= 0) & (pos < seq), ext, 0.0)
    p = i * ts + lax.broadcasted_iota(jnp.int32, (ts, 1), 0)

    outs = []
    for g, w in enumerate(POOL_WINDOWS):
        left = w // 2
        right = w - 1 - left
        t = ext[:, g * POOL_GROUP:(g + 1) * POOL_GROUP]
        step = 1
        while step < w:
            t = t + pltpu.roll(t, ext_rows - step, 0)
            step *= 2
        win = pltpu.roll(t, left, 0)[HALO:HALO + ts]
        cnt = (jnp.minimum(p + right + 1, seq) - jnp.maximum(p - left, 0)).astype(F32)
        d = win / cnt - ext[HALO:HALO + ts, g * POOL_GROUP:(g + 1) * POOL_GROUP]
        outs.append(jnp.dot(d.astype(BF16), pw_ref[g], preferred_element_type=F32))
    pool = (jnp.concatenate(outs, axis=-1) * ps_ref[...]).astype(BF16)

    mix = (jnp.dot(pool, wo_ref[:POOL_WIDTH, :], preferred_element_type=F32)
           + jnp.dot(a_ref[0], wo_ref[POOL_WIDTH:, :], preferred_element_type=F32))
    x1 = x_ref[0] + mod_ref[0, 2:3, :] * mix
    x1_ref[0] = x1
    r = lax.rsqrt(jnp.mean(x1 * x1, axis=-1, keepdims=True) + EPS)
    h2 = x1 * r * n2w_ref[...] * (1.0 + mod_ref[0, 4:5, :]) + mod_ref[0, 3:4, :]
    h2_ref[0] = _pack_rows(h2)
    hi = h2.astype(BF16)
    lo = (h2 - hi.astype(F32)).astype(BF16)
    nt = (((1,), (1,)), ((), ()))
    lg_ref[...] = (lax.dot_general(wrh_ref[...], hi, nt, preferred_element_type=F32)
                   + lax.dot_general(wrl_ref[...], hi, nt, preferred_element_type=F32)
                   + lax.dot_general(wrh_ref[...], lo, nt, preferred_element_type=F32))


def _postmix(u, attn, x, mod, p, ts):
    b, s, d = x.shape
    nt = s // ts
    hb = ts // HALO
    const = lambda shape: pl.BlockSpec(shape, lambda bi, i: (0,) * len(shape))
    return pl.pallas_call(
        functools.partial(_postmix_kernel, seq=s),
        grid=(b, nt),
        in_specs=[pl.BlockSpec((1, ts, POOL_WIDTH), lambda bi, i: (bi, i, 0)),
                  pl.BlockSpec((1, HALO, POOL_WIDTH), lambda bi, i: (bi, jnp.maximum(i * hb - 1, 0), 0)),
                  pl.BlockSpec((1, HALO, POOL_WIDTH),
                               lambda bi, i: (bi, jnp.minimum((i + 1) * hb, s // HALO - 1), 0)),
                  pl.BlockSpec((1, ts, POOL_WIDTH), lambda bi, i: (bi, i, 0)),
                  pl.BlockSpec((1, ts, d), lambda bi, i: (bi, i, 0)),
                  pl.BlockSpec((1, 6, d), lambda bi, i: (bi, 0, 0)),
                  const((len(POOL_WINDOWS), POOL_GROUP, POOL_GROUP)), const((1, POOL_WIDTH)),
                  const((d, d)), const((1, d)), const((N_EXPERTS, d)), const((N_EXPERTS, d))],
        out_specs=[pl.BlockSpec((1, ts, d), lambda bi, i: (bi, i, 0)),
                   pl.BlockSpec((1, ts, PACK_W), lambda bi, i: (bi, i, 0)),
                   pl.BlockSpec((N_EXPERTS, ts), lambda bi, i: (0, bi * nt + i))],
        out_shape=[jax.ShapeDtypeStruct((b, s, d), F32),
                   jax.ShapeDtypeStruct((b, s, PACK_W), jnp.int32),
                   jax.ShapeDtypeStruct((N_EXPERTS, b * s), F32)],
        compiler_params=_cparams(("parallel", "parallel")),
        name="postmix",
    )(u, u, u, attn, x, mod, p["pool_w"], p["pool_scale"], p["w_o"], p["norm2_w"],
      p["w_router_hi"], p["w_router_lo"])


def _router_kernel(lg_ref, bias_ref, tri_ref, idx_ref, wts_ref, pos_ref, cnt_ref, carry_ref):
    @pl.when(pl.program_id(0) == 0)
    def _():
        carry_ref[...] = jnp.zeros_like(carry_ref)

    ts = lg_ref.shape[1]
    gsz = N_EXPERTS // N_GROUPS
    ninf = -jnp.inf
    scores = 1.0 / (1.0 + jnp.exp(-lg_ref[...]))
    choice = scores + bias_ref[...]
    sub = lax.broadcasted_iota(jnp.int32, (gsz, ts), 0)

    gs_rows = []
    for g in range(N_GROUPS):
        grp = choice[g * gsz:(g + 1) * gsz]
        m1 = jnp.max(grp, axis=0, keepdims=True)
        i1 = jnp.min(jnp.where(grp == m1, sub, gsz), axis=0, keepdims=True)
        m2 = jnp.max(jnp.where(sub == i1, ninf, grp), axis=0, keepdims=True)
        gs_rows.append(m1 + m2)
    gs = jnp.concatenate(gs_rows, axis=0)

    rank = jnp.zeros((N_GROUPS, ts), jnp.int32)
    for g in range(N_GROUPS):
        row = gs[g:g + 1]
        beats = (row > gs) | ((row == gs) & (sub > g))
        rank = rank + beats.astype(jnp.int32)
    gsel = rank < TOPK_GROUPS

    masked = jnp.concatenate(
        [jnp.where(gsel[g:g + 1], choice[g * gsz:(g + 1) * gsz], ninf) for g in range(N_GROUPS)],
        axis=0)
    eio = lax.broadcasted_iota(jnp.int32, (N_EXPERTS, ts), 0)
    idx_rows, w_rows, hits = [], [], []
    for _ in range(TOP_K):
        m = jnp.max(masked, axis=0, keepdims=True)
        i = jnp.min(jnp.where(masked == m, eio, N_EXPERTS), axis=0, keepdims=True)
        hit = eio == i
        w_rows.append(jnp.sum(jnp.where(hit, scores, 0.0), axis=0, keepdims=True))
        masked = jnp.where(hit, ninf, masked)
        idx_rows.append(i)
        hits.append(hit)

    wsum = functools.reduce(lambda a, c: a + c, w_rows)
    pad_i = [jnp.zeros((1, ts), jnp.int32)] * (TOPK_PAD - TOP_K)
    pad_f = [jnp.zeros((1, ts), F32)] * (TOPK_PAD - TOP_K)
    idx_ref[...] = jnp.concatenate(idx_rows + pad_i, axis=0)
    wts_ref[...] = jnp.concatenate([w / wsum * ROUTED_SCALE for w in w_rows] + pad_f, axis=0)

    sel = functools.reduce(lambda a, c: a | c, hits)
    onehot = jnp.where(sel, 1.0, 0.0).astype(BF16)
    run = jnp.dot(onehot, tri_ref[...], preferred_element_type=F32) + carry_ref[:, 0:1]
    pos_rows = [jnp.sum(jnp.where(h, run - 1.0, 0.0), axis=0, keepdims=True).astype(jnp.int32)
                for h in hits]
    pos_ref[...] = jnp.concatenate(pos_rows + pad_i, axis=0)
    total = run[:, ts - 1:ts]
    carry_ref[...] = jnp.broadcast_to(total, carry_ref.shape)
    cnt_ref[...] = jnp.broadcast_to(total, cnt_ref.shape)


def _router(logits_t, router_bias, ts):
    t = logits_t.shape[1]
    tri = jnp.triu(jnp.ones((ts, ts), BF16))
    tok = pl.BlockSpec((TOPK_PAD, ts), lambda i: (0, i))
    return pl.pallas_call(
        _router_kernel,
        grid=(t // ts,),
        in_specs=[pl.BlockSpec((N_EXPERTS, ts), lambda i: (0, i)),
                  pl.BlockSpec((N_EXPERTS, 1), lambda i: (0, 0)),
                  pl.BlockSpec((ts, ts), lambda i: (0, 0))],
        out_specs=[tok, tok, tok, pl.BlockSpec((N_EXPERTS, LANES), lambda i: (0, 0))],
        out_shape=[jax.ShapeDtypeStruct((TOPK_PAD, t), jnp.int32),
                   jax.ShapeDtypeStruct((TOPK_PAD, t), F32),
                   jax.ShapeDtypeStruct((TOPK_PAD, t), jnp.int32),
                   jax.ShapeDtypeStruct((N_EXPERTS, LANES), F32)],
        scratch_shapes=[pltpu.VMEM((N_EXPERTS, LANES), F32)],
        compiler_params=_cparams(("arbitrary",)),
        name="router",
    )(logits_t, router_bias.reshape(N_EXPERTS, 1), tri)


def _dest_kernel(idx_ref, pos_ref, start_ref, dest_ref):
    ts = idx_ref.shape[1]
    eio = lax.broadcasted_iota(jnp.int32, (N_EXPERTS, ts), 0)
    start = start_ref[...]
    rows = [pos_ref[k:k + 1, :]
            + jnp.sum(jnp.where(eio == idx_ref[k:k + 1, :], start, 0), axis=0, keepdims=True)
            for k in range(TOPK_PAD)]
    dest_ref[...] = jnp.concatenate(rows, axis=0)


def _dest_rows(idx, pos, pad_start, ts):
    t = idx.shape[1]
    tok = pl.BlockSpec((TOPK_PAD, ts), lambda i: (0, i))
    return pl.pallas_call(
        _dest_kernel,
        grid=(t // ts,),
        in_specs=[tok, tok, pl.BlockSpec((N_EXPERTS, 1), lambda i: (0, 0))],
        out_specs=tok,
        out_shape=jax.ShapeDtypeStruct((TOPK_PAD, t), jnp.int32),
        compiler_params=_cparams(("parallel",)),
        name="dest_rows",
    )(idx, pos, pad_start.reshape(N_EXPERTS, 1))


def _pack_rows(x):
    bits = lax.bitcast_convert_type(x.astype(BF16).astype(F32), jnp.int32)
    return bits[:, :PACK_W] | lax.shift_right_logical(bits[:, PACK_W:], 16)


def _unpack_rows(words):
    hi = lax.bitcast_convert_type(words & jnp.int32(-65536), F32)
    lo = lax.bitcast_convert_type(lax.shift_left(words, 16), F32)
    return jnp.concatenate([hi, lo], axis=1)


def _sc_mesh():
    return plsc.VectorSubcoreMesh(core_axis_name="core", subcore_axis_name="subcore")


def _sc_worker_base(rows_per_worker):
    return (lax.axis_index("subcore") * SC_CORES + lax.axis_index("core")) * rows_per_worker


def _sc_scatter_rows(x, dest, n_rows):
    t, w = x.shape
    per_worker = t // SC_WORKERS
    assert per_worker * SC_WORKERS == t and per_worker % SC_CHUNK == 0

    @functools.partial(
        pl.kernel, out_type=jax.ShapeDtypeStruct((n_rows, w), x.dtype), mesh=_sc_mesh(),
        scratch_types=[pltpu.VMEM((TOPK_PAD, SC_CHUNK), jnp.int32), pltpu.VMEM((SC_CHUNK, w), x.dtype),
                       pltpu.SemaphoreType.DMA])
    def scatter(x_hbm, dest_hbm, out_hbm, idx_v, rows_v, sem):
        base = _sc_worker_base(per_worker)

        @pl.loop(0, per_worker // SC_CHUNK)
        def _(c):
            off = pl.multiple_of(base + c * SC_CHUNK, SC_CHUNK)
            pltpu.sync_copy(dest_hbm.at[:, pl.ds(off, SC_CHUNK)], idx_v)
            pltpu.sync_copy(x_hbm.at[pl.ds(off, SC_CHUNK)], rows_v)
            copies = [pltpu.async_copy(rows_v, out_hbm.at[idx_v.at[k]], sem) for k in range(TOP_K)]
            for cp in copies:
                cp.wait()

    return scatter(x, dest)


def _sc_gather_rows(table, idx):
    m = idx.shape[0]
    w = table.shape[1]
    per_worker = m // SC_WORKERS
    assert per_worker * SC_WORKERS == m and per_worker % SC_CHUNK == 0

    @functools.partial(
        pl.kernel, out_type=jax.ShapeDtypeStruct((m, w), table.dtype), mesh=_sc_mesh(),
        scratch_types=[pltpu.VMEM((SC_CHUNK,), jnp.int32), pltpu.VMEM((SC_CHUNK, w), table.dtype),
                       pltpu.SemaphoreType.DMA])
    def gather(table_hbm, idx_hbm, out_hbm, idx_v, rows_v, sem):
        base = _sc_worker_base(per_worker)

        @pl.loop(0, per_worker // SC_CHUNK)
        def _(c):
            off = pl.multiple_of(base + c * SC_CHUNK, SC_CHUNK)
            pltpu.sync_copy(idx_hbm.at[pl.ds(off, SC_CHUNK)], idx_v)
            pltpu.async_copy(table_hbm.at[idx_v], rows_v, sem).wait()
            pltpu.sync_copy(rows_v, out_hbm.at[pl.ds(off, SC_CHUNK)])

    return gather(table, idx)


def _experts_kernel(blk_exp_ref, n_used_ref, xs_ref, wg_ref, wu_ref, wd_ref, ys_ref):
    @pl.when(pl.program_id(0) < n_used_ref[0])
    def _():
        xb = _unpack_rows(xs_ref[...]).astype(BF16)
        g = jnp.dot(xb, wg_ref[0].astype(BF16), preferred_element_type=F32)
        u = jnp.dot(xb, wu_ref[0].astype(BF16), preferred_element_type=F32)
        hb = (_silu(g) * u).astype(BF16)
        ys_ref[...] = _pack_rows(jnp.dot(hb, wd_ref[0].astype(BF16), preferred_element_type=F32))


def _experts(blk_exp, n_used, xs, w_gate, w_up, w_down):
    n_rows, w = xs.shape
    d = D_MODEL
    n_blocks = n_rows // ROW_BLOCK
    row = lambda i, be, nu: (jnp.minimum(i, nu[0] - 1), 0)
    wsel = lambda i, be, nu: (be[i], 0, 0)
    return pl.pallas_call(
        _experts_kernel,
        grid_spec=pltpu.PrefetchScalarGridSpec(
            num_scalar_prefetch=2,
            grid=(n_blocks,),
            in_specs=[pl.BlockSpec((ROW_BLOCK, w), row),
                      pl.BlockSpec((1, d, EXPERT_FF), wsel),
                      pl.BlockSpec((1, d, EXPERT_FF), wsel),
                      pl.BlockSpec((1, EXPERT_FF, d), wsel)],
            out_specs=pl.BlockSpec((ROW_BLOCK, w), row)),
        out_shape=jax.ShapeDtypeStruct((n_rows, w), jnp.int32),
        compiler_params=_cparams(("arbitrary",)),
        name="experts",
    )(blk_exp, n_used, xs, w_gate, w_up, w_down)


def _combine_kernel(g_ref, w_ref, h2_ref, x1_ref, mod_ref, wsg_ref, wsu_ref, wsd_ref, out_ref):
    hb = _unpack_rows(h2_ref[...]).astype(BF16)
    g = jnp.dot(hb, wsg_ref[...], preferred_element_type=F32)
    u = jnp.dot(hb, wsu_ref[...], preferred_element_type=F32)
    acc = jnp.dot((_silu(g) * u).astype(BF16), wsd_ref[...], preferred_element_type=F32)
    w = w_ref[...]
    for k in range(TOP_K):
        acc = acc + _unpack_rows(g_ref[k]) * w[:, k:k + 1]
    out_ref[...] = x1_ref[...] + mod_ref[0, 5:6, :] * acc


def _combine(gathered, wts_t, h2p, x1, mod, p, tt, seq):
    t, d = x1.shape
    per_seq = seq // tt
    tok = pl.BlockSpec((tt, d), lambda i: (i, 0))
    const = lambda shape: pl.BlockSpec(shape, lambda i: (0,) * len(shape))
    return pl.pallas_call(
        _combine_kernel,
        grid=(t // tt,),
        in_specs=[pl.BlockSpec((TOP_K, tt, PACK_W), lambda i: (0, i, 0)),
                  pl.BlockSpec((tt, TOPK_PAD), lambda i: (i, 0)),
                  pl.BlockSpec((tt, PACK_W), lambda i: (i, 0)),
                  tok,
                  pl.BlockSpec((1, 6, d), lambda i: (i // per_seq, 0, 0)),
                  const((d, SHARED_FF)), const((d, SHARED_FF)), const((SHARED_FF, d))],
        out_specs=tok,
        out_shape=jax.ShapeDtypeStruct((t, d), F32),
        compiler_params=_cparams(("parallel",)),
        name="combine",
    )(gathered, wts_t, h2p, x1, mod, p["ws_gate"], p["ws_up"], p["ws_down"])


def _rope_tables(s):
    half = QK_ROPE_DIM // 2
    inv_freq = ROPE_THETA ** (-jnp.arange(half, dtype=F32) / half)
    ang = jnp.arange(s, dtype=F32)[:, None] * inv_freq[None, :]
    cos, sin = jnp.cos(ang), jnp.sin(ang)
    z = lambda n: jnp.zeros((s, n), F32)
    tab_cos = jnp.concatenate([jnp.ones((s, QK_NOPE_DIM), F32), cos, cos, z(HEAD_PAD - QK_HEAD_DIM)], 1)
    tab_sa = jnp.concatenate([z(QK_NOPE_DIM), -sin, z(HEAD_PAD - QK_NOPE_DIM - half)], 1)
    tab_sb = jnp.concatenate([z(QK_NOPE_DIM + half), sin, z(HEAD_PAD - QK_HEAD_DIM)], 1)
    return dict(rope_cos=tab_cos, rope_sa=tab_sa, rope_sb=tab_sb, rope_cos_t=cos.T, rope_sin_t=sin.T)


def _prep_weights(norm1_w, w_in, pool_w, pool_scale, q_a_norm_w, w_q_b, kv_a_norm_w, w_kv_b,
                  q_norm_w, k_norm_w, w_o, norm2_w, w_router, w_gate, w_up, w_down,
                  ws_gate, ws_up, ws_down):
    d = D_MODEL
    c0 = POOL_WIDTH + Q_LORA_RANK + KV_LORA_RANK
    pad_h = HEAD_PAD - QK_HEAD_DIM
    w_in_p = jnp.concatenate(
        [w_in[:, :c0], jnp.zeros((d, QK_NOPE_DIM), F32), w_in[:, c0:], jnp.zeros((d, pad_h), F32)], 1)
    w_q = jnp.pad(w_q_b.reshape(Q_LORA_RANK, N_HEADS, QK_HEAD_DIM), ((0, 0), (0, 0), (0, pad_h)))
    kv = w_kv_b.reshape(KV_LORA_RANK, N_HEADS, QK_NOPE_DIM + V_HEAD_DIM)
    w_k = jnp.pad(kv[:, :, :QK_NOPE_DIM], ((0, 0), (0, 0), (0, HEAD_PAD - QK_NOPE_DIM)))
    w_v = kv[:, :, QK_NOPE_DIM:]
    w_r_t = w_router.T
    w_r_hi = w_r_t.astype(BF16)
    w_r_lo = (w_r_t - w_r_hi.astype(F32)).astype(BF16)
    q_gain = q_norm_w * (QK_HEAD_DIM ** -0.5 * LOG2E)
    bound = QK_HEAD_DIM * jnp.max(jnp.abs(q_gain)) * jnp.max(jnp.abs(k_norm_w)) * 1.02 + 0.25
    bound = bound.astype(BF16).astype(F32)
    lane = jnp.arange(HEAD_PAD) == SHIFT_LANE
    return dict(
        score_bound=bound,
        k_spare=jnp.where(lane, -bound, 0.0).astype(F32).reshape(1, HEAD_PAD),
        norm1_w=norm1_w.reshape(1, d), w_in=w_in_p.astype(BF16),
        q_a_norm_w=q_a_norm_w.reshape(1, -1), w_q_t=w_q.reshape(Q_LORA_RANK, -1).T.astype(BF16),
        kv_a_norm_w=kv_a_norm_w.reshape(1, -1), w_k=w_k.reshape(KV_LORA_RANK, -1).astype(BF16),
        w_v_t=w_v.reshape(KV_LORA_RANK, -1).T.astype(BF16),
        q_gain_col=jnp.broadcast_to(jnp.pad(q_gain, (0, pad_h))[:, None], (HEAD_PAD, LANES)),
        k_norm_w=jnp.pad(k_norm_w, (0, pad_h)).reshape(1, HEAD_PAD),
        pool_w=pool_w.astype(BF16), pool_scale=pool_scale.reshape(1, -1), w_o=w_o.astype(BF16),
        norm2_w=norm2_w.reshape(1, d), w_router_hi=w_r_hi, w_router_lo=w_r_lo,
        w_gate=w_gate, w_up=w_up, w_down=w_down,
        ws_gate=ws_gate.astype(BF16), ws_up=ws_up.astype(BF16), ws_down=ws_down.astype(BF16))


def _tile(n, pref):
    return pref if n % pref == 0 else n


def _encoder(x, mod, router_bias, p):
    b, s, d = x.shape
    t = b * s
    ts = _tile(s, 512)
    p = dict(p, **_rope_tables(s))

    u, qt, k, vt = _inproj(x, mod, p, ts)
    online = (p["score_bound"] > MAX_FIXED_SHIFT).astype(jnp.int32).reshape(1)
    attn = _attention(online, qt, k, vt, _tile(s, 1024), _tile(s, 1024), _tile(s, 512))
    x1, h2p, logits_t = _postmix(u, attn, x, mod, p, ts)
    tr = _tile(t, 512)
    idx, wts, pos, cnt = _router(logits_t, router_bias, tr)

    counts = cnt[:, 0].astype(jnp.int32)
    padded = (counts + ROW_BLOCK - 1) // ROW_BLOCK * ROW_BLOCK
    pad_end = jnp.cumsum(padded)
    dest = _dest_rows(idx, pos, pad_end - padded, tr)
    n_blocks = -(-t * TOP_K // ROW_BLOCK) + N_EXPERTS
    blk_row = jnp.arange(n_blocks, dtype=jnp.int32)[:, None] * ROW_BLOCK
    blk_exp = jnp.minimum(jnp.sum((pad_end[None, :] <= blk_row).astype(jnp.int32), axis=1),
                          N_EXPERTS - 1)
    n_used = (pad_end[-1:] // ROW_BLOCK).astype(jnp.int32)

    tt = _tile(s, 256)
    h2f = h2p.reshape(t, PACK_W)
    xs = _sc_scatter_rows(h2f, dest, n_blocks * ROW_BLOCK)
    ys = _experts(blk_exp, n_used, xs, p["w_gate"], p["w_up"], p["w_down"])
    gathered = _sc_gather_rows(ys, dest[:TOP_K].reshape(TOP_K * t)).reshape(TOP_K, t, PACK_W)
    out = _combine(gathered, wts.T, h2f, x1.reshape(t, d), mod, p, tt, s)
    return out.reshape(b, s, d)


def kernel(x_prompt, x_sample, c_prompt, c_sample, w_ada, b_ada, norm1_w, w_in, pool_w, pool_scale,
           q_a_norm_w, w_q_b, kv_a_norm_w, w_kv_b, q_norm_w, k_norm_w, w_o, norm2_w, w_router,
           router_bias, w_gate, w_up, w_down, ws_gate, ws_up, ws_down):
    assert w_ada.shape[0] == 1, "single-layer encoder"
    p = _prep_weights(norm1_w[0], w_in[0], pool_w[0], pool_scale[0], q_a_norm_w[0], w_q_b[0],
                      kv_a_norm_w[0], w_kv_b[0], q_norm_w[0], k_norm_w[0], w_o[0], norm2_w[0],
                      w_router[0], w_gate[0], w_up[0], w_down[0], ws_gate[0], ws_up[0], ws_down[0])
    nb = x_prompt.shape[0]
    c = jnp.concatenate([c_prompt, c_sample], axis=0).astype(F32)
    mod = _adaln(c, w_ada[0], b_ada[0]).reshape(c.shape[0], 6, D_MODEL)
    y_prompt = _encoder(x_prompt, mod[:nb], router_bias[0], p)
    y_sample = _encoder(x_sample, mod[nb:], router_bias[0], p)
    return (y_prompt, y_sample)
```

```python
import functools

import jax
import jax.numpy as jnp
from jax import lax
from jax.experimental import pallas as pl
from jax.experimental.pallas import tpu as pltpu
from jax.experimental.pallas import tpu_sc as plsc

D_MODEL = 1024
POOL_WIDTH = 512
POOL_WINDOWS = (2, 4, 8, 16)
POOL_GROUP = 128
N_HEADS = 8
V_HEAD_DIM = 64
QK_NOPE_DIM = 64
QK_ROPE_DIM = 32
QK_HEAD_DIM = 96
Q_LORA_RANK = 256
KV_LORA_RANK = 128
ROPE_THETA = 10000.0
N_EXPERTS = 64
TOP_K = 6
N_GROUPS = 8
TOPK_GROUPS = 4
EXPERT_FF = 256
SHARED_FF = 256
ROUTED_SCALE = 2.5
EPS = 1e-6

LANES = 128
HEAD_PAD = 128
HALO = 16
ROW_BLOCK = 512
TOPK_PAD = 8
V_AUG = 80
SHIFT_LANE = QK_HEAD_DIM
MAX_FIXED_SHIFT = 40.0
LOG2E = 1.4426950408889634
PACK_W = D_MODEL // 2
SC_CORES = 2
SC_WORKERS = 32
SC_CHUNK = 128
VMEM_LIMIT = 48 * 1024 * 1024

F32 = jnp.float32
BF16 = jnp.bfloat16


def _cparams(sem):
    return pltpu.CompilerParams(dimension_semantics=sem, vmem_limit_bytes=VMEM_LIMIT)


def _silu(x):
    return x * (1.0 / (1.0 + jnp.exp(-x)))


def _adaln_kernel(c_ref, w_ref, b_ref, o_ref):
    c = c_ref[...]
    o_ref[...] = jnp.dot(_silu(c), w_ref[...], preferred_element_type=F32,
                         precision=lax.Precision.HIGHEST) + b_ref[...]


def _adaln(c, w_ada, b_ada):
    nb, d = c.shape
    n = w_ada.shape[1]
    tn = 1536
    return pl.pallas_call(
        _adaln_kernel,
        grid=(n // tn,),
        in_specs=[pl.BlockSpec((nb, d), lambda j: (0, 0)),
                  pl.BlockSpec((d, tn), lambda j: (0, j)),
                  pl.BlockSpec((1, tn), lambda j: (0, j))],
        out_specs=pl.BlockSpec((nb, tn), lambda j: (0, j)),
        out_shape=jax.ShapeDtypeStruct((nb, n), F32),
        compiler_params=_cparams(("arbitrary",)),
        name="adaln",
    )(c, w_ada, b_ada.reshape(1, n))


def _inproj_kernel(x_ref, mod_ref, n1w_ref, win_ref, qan_ref, wqt_ref, kvan_ref, wk_ref, wvt_ref,
                   qg_ref, kg_ref, ksp_ref, cos_ref, sa_ref, sb_ref, cost_ref, sint_ref,
                   u_ref, qt_ref, k_ref, vt_ref):
    ts = x_ref.shape[1]
    half = QK_ROPE_DIM // 2
    x = x_ref[0]
    shift1 = mod_ref[0, 0:1, :]
    scale1 = mod_ref[0, 1:2, :]
    r = lax.rsqrt(jnp.mean(x * x, axis=-1, keepdims=True) + EPS)
    h = x * r * n1w_ref[...] * (1.0 + scale1) + shift1
    z = jnp.dot(h.astype(BF16), win_ref[...], preferred_element_type=F32)
    u_ref[0] = z[:, :POOL_WIDTH].astype(BF16)

    cq = z[:, POOL_WIDTH:POOL_WIDTH + Q_LORA_RANK]
    cqn = cq * lax.rsqrt(jnp.mean(cq * cq, axis=-1, keepdims=True) + EPS) * qan_ref[...]
    qt = jnp.dot(wqt_ref[...], cqn.T.astype(BF16), preferred_element_type=F32)
    reps = ts // LANES
    qg = jnp.concatenate([qg_ref[...]] * reps, axis=1)
    cost = cost_ref[...]
    sint = sint_ref[...]
    spare = jnp.where(lax.broadcasted_iota(jnp.int32, (HEAD_PAD - QK_HEAD_DIM, ts), 0) == 0, 1.0, 0.0)
    for hd in range(N_HEADS):
        t = qt[hd * HEAD_PAD:(hd + 1) * HEAD_PAD]
        rn = lax.rsqrt(jnp.sum(t * t, axis=0, keepdims=True) * (1.0 / QK_HEAD_DIM) + EPS)
        tn = t * rn * qg
        t1 = tn[QK_NOPE_DIM:QK_NOPE_DIM + half]
        t2 = tn[QK_NOPE_DIM + half:QK_HEAD_DIM]
        out = jnp.concatenate([tn[:QK_NOPE_DIM], t1 * cost - t2 * sint, t1 * sint + t2 * cost, spare],
                              axis=0)
        qt_ref[0, hd * HEAD_PAD:(hd + 1) * HEAD_PAD, :] = out.astype(BF16)

    c0 = POOL_WIDTH + Q_LORA_RANK
    ckv = z[:, c0:c0 + KV_LORA_RANK]
    ckvn = ckv * lax.rsqrt(jnp.mean(ckv * ckv, axis=-1, keepdims=True) + EPS) * kvan_ref[...]

    vt = jnp.dot(wvt_ref[...], ckvn.T.astype(BF16), preferred_element_type=F32)
    ones = jnp.ones((V_AUG - V_HEAD_DIM, ts), BF16)
    for hd in range(N_HEADS):
        vt_ref[0, hd * V_AUG:hd * V_AUG + V_HEAD_DIM, :] = (
            vt[hd * V_HEAD_DIM:(hd + 1) * V_HEAD_DIM].astype(BF16))
        vt_ref[0, hd * V_AUG + V_HEAD_DIM:(hd + 1) * V_AUG, :] = ones

    kk = jnp.dot(ckvn.astype(BF16), wk_ref[...], preferred_element_type=F32)
    kpe = z[:, c0 + KV_LORA_RANK:]
    kg = kg_ref[...]
    ksp = ksp_ref[...]
    pe_ssq = jnp.sum(kpe * kpe, axis=-1, keepdims=True)
    pg = kpe * kg
    pe_rot = (pg * cos_ref[...] + pltpu.roll(pg, HEAD_PAD - half, 1) * sa_ref[...]
              + pltpu.roll(pg, half, 1) * sb_ref[...])
    for hd in range(N_HEADS):
        t = kk[:, hd * HEAD_PAD:(hd + 1) * HEAD_PAD]
        ssq = jnp.sum(t * t, axis=-1, keepdims=True) + pe_ssq
        rn = lax.rsqrt(ssq * (1.0 / QK_HEAD_DIM) + EPS)
        k_ref[0, :, hd * HEAD_PAD:(hd + 1) * HEAD_PAD] = ((t * kg + pe_rot) * rn + ksp).astype(BF16)


def _inproj(x, mod, p, ts):
    b, s, d = x.shape
    qk_w = N_HEADS * HEAD_PAD
    half = QK_ROPE_DIM // 2
    const = lambda shape: pl.BlockSpec(shape, lambda bi, i: (0,) * len(shape))
    row_tab = pl.BlockSpec((ts, HEAD_PAD), lambda bi, i: (i, 0))
    col_tab = pl.BlockSpec((half, ts), lambda bi, i: (0, i))
    return pl.pallas_call(
        _inproj_kernel,
        grid=(b, s // ts),
        in_specs=[pl.BlockSpec((1, ts, d), lambda bi, i: (bi, i, 0)),
                  pl.BlockSpec((1, 6, d), lambda bi, i: (bi, 0, 0)),
                  const((1, d)), const((d, d)), const((1, Q_LORA_RANK)),
                  const((qk_w, Q_LORA_RANK)), const((1, KV_LORA_RANK)),
                  const((KV_LORA_RANK, qk_w)), const((N_HEADS * V_HEAD_DIM, KV_LORA_RANK)),
                  const((HEAD_PAD, LANES)), const((1, HEAD_PAD)), const((1, HEAD_PAD)),
                  row_tab, row_tab, row_tab, col_tab, col_tab],
        out_specs=[pl.BlockSpec((1, ts, POOL_WIDTH), lambda bi, i: (bi, i, 0)),
                   pl.BlockSpec((1, qk_w, ts), lambda bi, i: (bi, 0, i)),
                   pl.BlockSpec((1, ts, qk_w), lambda bi, i: (bi, i, 0)),
                   pl.BlockSpec((1, N_HEADS * V_AUG, ts), lambda bi, i: (bi, 0, i))],
        out_shape=[jax.ShapeDtypeStruct((b, s, POOL_WIDTH), BF16),
                   jax.ShapeDtypeStruct((b, qk_w, s), BF16),
                   jax.ShapeDtypeStruct((b, s, qk_w), BF16),
                   jax.ShapeDtypeStruct((b, N_HEADS * V_AUG, s), BF16)],
        compiler_params=_cparams(("parallel", "parallel")),
        name="inproj",
    )(x, mod, p["norm1_w"], p["w_in"], p["q_a_norm_w"], p["w_q_t"], p["kv_a_norm_w"], p["w_k"],
      p["w_v_t"], p["q_gain_col"], p["k_norm_w"], p["k_spare"],
      p["rope_cos"], p["rope_sa"], p["rope_sb"], p["rope_cos_t"], p["rope_sin_t"])


def _attn_kernel(online_ref, qt_ref, k_ref, vt_ref, o_ref, acc_a, acc_b, m_ref, *, tk, tk_online):
    s_len = k_ref.shape[1]
    acc_a[...] = jnp.zeros_like(acc_a)
    acc_b[...] = jnp.zeros_like(acc_b)
    qa = qt_ref[0, :HEAD_PAD, :]
    qb = qt_ref[0, HEAD_PAD:, :]

    @pl.when(online_ref[0] == 0)
    def _():
        def body(c, _):
            off = pl.multiple_of(c * tk, tk)
            ks = k_ref[0, pl.ds(off, tk), :]
            vts = vt_ref[0, :, pl.ds(off, tk)]
            pa = jnp.exp2(jnp.dot(ks[:, :HEAD_PAD], qa, preferred_element_type=F32)).astype(BF16)
            pb = jnp.exp2(jnp.dot(ks[:, HEAD_PAD:], qb, preferred_element_type=F32)).astype(BF16)
            acc_a[...] += jnp.dot(vts[:V_AUG], pa, preferred_element_type=F32)
            acc_b[...] += jnp.dot(vts[V_AUG:], pb, preferred_element_type=F32)
            return 0

        lax.fori_loop(0, s_len // tk, body, 0)

    @pl.when(online_ref[0] != 0)
    def _():
        m_ref[...] = jnp.full_like(m_ref, -jnp.inf)

        def one_head(kh, q, vth, acc, row):
            s = jnp.dot(kh, q, preferred_element_type=F32)
            m_old = m_ref[row:row + 1, :]
            m_new = jnp.maximum(m_old, jnp.max(s, axis=0, keepdims=True))
            p = jnp.exp2(s - m_new).astype(BF16)
            acc[...] = acc[...] * jnp.exp2(m_old - m_new) + jnp.dot(vth, p, preferred_element_type=F32)
            m_ref[row:row + 1, :] = m_new

        def body(c, _):
            off = pl.multiple_of(c * tk_online, tk_online)
            ks = k_ref[0, pl.ds(off, tk_online), :]
            vts = vt_ref[0, :, pl.ds(off, tk_online)]
            one_head(ks[:, :HEAD_PAD], qa, vts[:V_AUG], acc_a, 0)
            one_head(ks[:, HEAD_PAD:], qb, vts[V_AUG:], acc_b, 1)
            return 0

        lax.fori_loop(0, s_len // tk_online, body, 0)

    oa = acc_a[:V_HEAD_DIM] / acc_a[V_HEAD_DIM:V_HEAD_DIM + 1]
    ob = acc_b[:V_HEAD_DIM] / acc_b[V_HEAD_DIM:V_HEAD_DIM + 1]
    o_ref[0] = jnp.concatenate([oa, ob], axis=0).T.astype(BF16)


def _attention(online, qt, k, vt, tq, tk, tk_online):
    b, _, s = qt.shape
    return pl.pallas_call(
        functools.partial(_attn_kernel, tk=tk, tk_online=tk_online),
        grid_spec=pltpu.PrefetchScalarGridSpec(
            num_scalar_prefetch=1,
            grid=(b, N_HEADS // 2, s // tq),
            in_specs=[pl.BlockSpec((1, 2 * HEAD_PAD, tq), lambda bi, j, i, on: (bi, j, i)),
                      pl.BlockSpec((1, s, 2 * HEAD_PAD), lambda bi, j, i, on: (bi, 0, j)),
                      pl.BlockSpec((1, 2 * V_AUG, s), lambda bi, j, i, on: (bi, j, 0))],
            out_specs=pl.BlockSpec((1, tq, 2 * V_HEAD_DIM), lambda bi, j, i, on: (bi, i, j)),
            scratch_shapes=[pltpu.VMEM((V_AUG, tq), F32), pltpu.VMEM((V_AUG, tq), F32),
                            pltpu.VMEM((8, tq), F32)]),
        out_shape=jax.ShapeDtypeStruct((b, s, N_HEADS * V_HEAD_DIM), BF16),
        compiler_params=_cparams(("parallel", "parallel", "arbitrary")),
        name="attn",
    )(online, qt, k, vt)


def _postmix_kernel(u_ref, up_ref, un_ref, a_ref, x_ref, mod_ref, pw_ref, ps_ref, wo_ref, n2w_ref,
                    wrh_ref, wrl_ref, x1_ref, h2_ref, lg_ref, *, seq):
    i = pl.program_id(1)
    ts = u_ref.shape[1]
    ext_rows = ts + 2 * HALO
    ext = jnp.concatenate([up_ref[0], u_ref[0], un_ref[0]], axis=0).astype(F32)
    pos = i * ts - HALO + lax.broadcasted_iota(jnp.int32, (ext_rows, 1), 0)
    ext = jnp.where((pos >= 0) & (pos < seq), ext, 0.0)
    p = i * ts + lax.broadcasted_iota(jnp.int32, (ts, 1), 0)

    outs = []
    for g, w in enumerate(POOL_WINDOWS):
        left = w // 2
        right = w - 1 - left
        t = ext[:, g * POOL_GROUP:(g + 1) * POOL_GROUP]
        step = 1
        while step < w:
            t = t + pltpu.roll(t, ext_rows - step, 0)
            step *= 2
        win = pltpu.roll(t, left, 0)[HALO:HALO + ts]
        cnt = (jnp.minimum(p + right + 1, seq) - jnp.maximum(p - left, 0)).astype(F32)
        d = win / cnt - ext[HALO:HALO + ts, g * POOL_GROUP:(g + 1) * POOL_GROUP]
        outs.append(jnp.dot(d.astype(BF16), pw_ref[g], preferred_element_type=F32))
    pool = (jnp.concatenate(outs, axis=-1) * ps_ref[...]).astype(BF16)

    mix = (jnp.dot(pool, wo_ref[:POOL_WIDTH, :], preferred_element_type=F32)
           + jnp.dot(a_ref[0], wo_ref[POOL_WIDTH:, :], preferred_element_type=F32))
    x1 = x_ref[0] + mod_ref[0, 2:3, :] * mix
    x1_ref[0] = x1
    r = lax.rsqrt(jnp.mean(x1 * x1, axis=-1, keepdims=True) + EPS)
    h2 = x1 * r * n2w_ref[...] * (1.0 + mod_ref[0, 4:5, :]) + mod_ref[0, 3:4, :]
    h2_ref[0] = _pack_rows(h2)
    hi = h2.astype(BF16)
    lo = (h2 - hi.astype(F32)).astype(BF16)
    nt = (((1,), (1,)), ((), ()))
    lg_ref[...] = (lax.dot_general(wrh_ref[...], hi, nt, preferred_element_type=F32)
                   + lax.dot_general(wrl_ref[...], hi, nt, preferred_element_type=F32)
                   + lax.dot_general(wrh_ref[...], lo, nt, preferred_element_type=F32))


def _postmix(u, attn, x, mod, p, ts):
    b, s, d = x.shape
    nt = s // ts
    hb = ts // HALO
    const = lambda shape: pl.BlockSpec(shape, lambda bi, i: (0,) * len(shape))
    return pl.pallas_call(
        functools.partial(_postmix_kernel, seq=s),
        grid=(b, nt),
        in_specs=[pl.BlockSpec((1, ts, POOL_WIDTH), lambda bi, i: (bi, i, 0)),
                  pl.BlockSpec((1, HALO, POOL_WIDTH), lambda bi, i: (bi, jnp.maximum(i * hb - 1, 0), 0)),
                  pl.BlockSpec((1, HALO, POOL_WIDTH),
                               lambda bi, i: (bi, jnp.minimum((i + 1) * hb, s // HALO - 1), 0)),
                  pl.BlockSpec((1, ts, POOL_WIDTH), lambda bi, i: (bi, i, 0)),
                  pl.BlockSpec((1, ts, d), lambda bi, i: (bi, i, 0)),
                  pl.BlockSpec((1, 6, d), lambda bi, i: (bi, 0, 0)),
                  const((len(POOL_WINDOWS), POOL_GROUP, POOL_GROUP)), const((1, POOL_WIDTH)),
                  const((d, d)), const((1, d)), const((N_EXPERTS, d)), const((N_EXPERTS, d))],
        out_specs=[pl.BlockSpec((1, ts, d), lambda bi, i: (bi, i, 0)),
                   pl.BlockSpec((1, ts, PACK_W), lambda bi, i: (bi, i, 0)),
                   pl.BlockSpec((N_EXPERTS, ts), lambda bi, i: (0, bi * nt + i))],
        out_shape=[jax.ShapeDtypeStruct((b, s, d), F32),
                   jax.ShapeDtypeStruct((b, s, PACK_W), jnp.int32),
                   jax.ShapeDtypeStruct((N_EXPERTS, b * s), F32)],
        compiler_params=_cparams(("parallel", "parallel")),
        name="postmix",
    )(u, u, u, attn, x, mod, p["pool_w"], p["pool_scale"], p["w_o"], p["norm2_w"],
      p["w_router_hi"], p["w_router_lo"])


def _router_kernel(lg_ref, bias_ref, tri_ref, idx_ref, wts_ref, pos_ref, cnt_ref, carry_ref):
    @pl.when(pl.program_id(0) == 0)
    def _():
        carry_ref[...] = jnp.zeros_like(carry_ref)

    ts = lg_ref.shape[1]
    gsz = N_EXPERTS // N_GROUPS
    ninf = -jnp.inf
    scores = 1.0 / (1.0 + jnp.exp(-lg_ref[...]))
    choice = scores + bias_ref[...]
    sub = lax.broadcasted_iota(jnp.int32, (gsz, ts), 0)

    gs_rows = []
    for g in range(N_GROUPS):
        grp = choice[g * gsz:(g + 1) * gsz]
        m1 = jnp.max(grp, axis=0, keepdims=True)
        i1 = jnp.min(jnp.where(grp == m1, sub, gsz), axis=0, keepdims=True)
        m2 = jnp.max(jnp.where(sub == i1, ninf, grp), axis=0, keepdims=True)
        gs_rows.append(m1 + m2)
    gs = jnp.concatenate(gs_rows, axis=0)

    rank = jnp.zeros((N_GROUPS, ts), jnp.int32)
    for g in range(N_GROUPS):
        row = gs[g:g + 1]
        beats = (row > gs) | ((row == gs) & (sub > g))
        rank = rank + beats.astype(jnp.int32)
    gsel = rank < TOPK_GROUPS

    masked = jnp.concatenate(
        [jnp.where(gsel[g:g + 1], choice[g * gsz:(g + 1) * gsz], ninf) for g in range(N_GROUPS)],
        axis=0)
    eio = lax.broadcasted_iota(jnp.int32, (N_EXPERTS, ts), 0)
    idx_rows, w_rows, hits = [], [], []
    for _ in range(TOP_K):
        m = jnp.max(masked, axis=0, keepdims=True)
        i = jnp.min(jnp.where(masked == m, eio, N_EXPERTS), axis=0, keepdims=True)
        hit = eio == i
        w_rows.append(jnp.sum(jnp.where(hit, scores, 0.0), axis=0, keepdims=True))
        masked = jnp.where(hit, ninf, masked)
        idx_rows.append(i)
        hits.append(hit)

    wsum = functools.reduce(lambda a, c: a + c, w_rows)
    pad_i = [jnp.zeros((1, ts), jnp.int32)] * (TOPK_PAD - TOP_K)
    pad_f = [jnp.zeros((1, ts), F32)] * (TOPK_PAD - TOP_K)
    idx_ref[...] = jnp.concatenate(idx_rows + pad_i, axis=0)
    wts_ref[...] = jnp.concatenate([w / wsum * ROUTED_SCALE for w in w_rows] + pad_f, axis=0)

    sel = functools.reduce(lambda a, c: a | c, hits)
    onehot = jnp.where(sel, 1.0, 0.0).astype(BF16)
    run = jnp.dot(onehot, tri_ref[...], preferred_element_type=F32) + carry_ref[:, 0:1]
    pos_rows = [jnp.sum(jnp.where(h, run - 1.0, 0.0), axis=0, keepdims=True).astype(jnp.int32)
                for h in hits]
    pos_ref[...] = jnp.concatenate(pos_rows + pad_i, axis=0)
    total = run[:, ts - 1:ts]
    carry_ref[...] = jnp.broadcast_to(total, carry_ref.shape)
    cnt_ref[...] = jnp.broadcast_to(total, cnt_ref.shape)


def _router(logits_t, router_bias, ts):
    t = logits_t.shape[1]
    tri = jnp.triu(jnp.ones((ts, ts), BF16))
    tok = pl.BlockSpec((TOPK_PAD, ts), lambda i: (0, i))
    return pl.pallas_call(
        _router_kernel,
        grid=(t // ts,),
        in_specs=[pl.BlockSpec((N_EXPERTS, ts), lambda i: (0, i)),
                  pl.BlockSpec((N_EXPERTS, 1), lambda i: (0, 0)),
                  pl.BlockSpec((ts, ts), lambda i: (0, 0))],
        out_specs=[tok, tok, tok, pl.BlockSpec((N_EXPERTS, LANES), lambda i: (0, 0))],
        out_shape=[jax.ShapeDtypeStruct((TOPK_PAD, t), jnp.int32),
                   jax.ShapeDtypeStruct((TOPK_PAD, t), F32),
                   jax.ShapeDtypeStruct((TOPK_PAD, t), jnp.int32),
                   jax.ShapeDtypeStruct((N_EXPERTS, LANES), F32)],
        scratch_shapes=[pltpu.VMEM((N_EXPERTS, LANES), F32)],
        compiler_params=_cparams(("arbitrary",)),
        name="router",
    )(logits_t, router_bias.reshape(N_EXPERTS, 1), tri)


def _dest_kernel(idx_ref, pos_ref, start_ref, dest_ref):
    ts = idx_ref.shape[1]
    eio = lax.broadcasted_iota(jnp.int32, (N_EXPERTS, ts), 0)
    start = start_ref[...]
    rows = [pos_ref[k:k + 1, :]
            + jnp.sum(jnp.where(eio == idx_ref[k:k + 1, :], start, 0), axis=0, keepdims=True)
            for k in range(TOPK_PAD)]
    dest_ref[...] = jnp.concatenate(rows, axis=0)


def _dest_rows(idx, pos, pad_start, ts):
    t = idx.shape[1]
    tok = pl.BlockSpec((TOPK_PAD, ts), lambda i: (0, i))
    return pl.pallas_call(
        _dest_kernel,
        grid=(t // ts,),
        in_specs=[tok, tok, pl.BlockSpec((N_EXPERTS, 1), lambda i: (0, 0))],
        out_specs=tok,
        out_shape=jax.ShapeDtypeStruct((TOPK_PAD, t), jnp.int32),
        compiler_params=_cparams(("parallel",)),
        name="dest_rows",
    )(idx, pos, pad_start.reshape(N_EXPERTS, 1))


def _pack_rows(x):
    bits = lax.bitcast_convert_type(x.astype(BF16).astype(F32), jnp.int32)
    return bits[:, :PACK_W] | lax.shift_right_logical(bits[:, PACK_W:], 16)


def _unpack_rows(words):
    hi = lax.bitcast_convert_type(words & jnp.int32(-65536), F32)
    lo = lax.bitcast_convert_type(lax.shift_left(words, 16), F32)
    return jnp.concatenate([hi, lo], axis=1)


def _sc_mesh():
    return plsc.VectorSubcoreMesh(core_axis_name="core", subcore_axis_name="subcore")


def _sc_worker_base(rows_per_worker):
    return (lax.axis_index("subcore") * SC_CORES + lax.axis_index("core")) * rows_per_worker


def _sc_scatter_rows(x, dest, n_rows):
    t, w = x.shape
    per_worker = t // SC_WORKERS
    assert per_worker * SC_WORKERS == t and per_worker % SC_CHUNK == 0

    @functools.partial(
        pl.kernel, out_type=jax.ShapeDtypeStruct((n_rows, w), x.dtype), mesh=_sc_mesh(),
        scratch_types=[pltpu.VMEM((TOPK_PAD, SC_CHUNK), jnp.int32), pltpu.VMEM((SC_CHUNK, w), x.dtype),
                       pltpu.SemaphoreType.DMA])
    def scatter(x_hbm, dest_hbm, out_hbm, idx_v, rows_v, sem):
        base = _sc_worker_base(per_worker)

        @pl.loop(0, per_worker // SC_CHUNK)
        def _(c):
            off = pl.multiple_of(base + c * SC_CHUNK, SC_CHUNK)
            pltpu.sync_copy(dest_hbm.at[:, pl.ds(off, SC_CHUNK)], idx_v)
            pltpu.sync_copy(x_hbm.at[pl.ds(off, SC_CHUNK)], rows_v)
            copies = [pltpu.async_copy(rows_v, out_hbm.at[idx_v.at[k]], sem) for k in range(TOP_K)]
            for cp in copies:
                cp.wait()

    return scatter(x, dest)


def _sc_gather_rows(table, idx):
    m = idx.shape[0]
    w = table.shape[1]
    per_worker = m // SC_WORKERS
    assert per_worker * SC_WORKERS == m and per_worker % SC_CHUNK == 0

    @functools.partial(
        pl.kernel, out_type=jax.ShapeDtypeStruct((m, w), table.dtype), mesh=_sc_mesh(),
        scratch_types=[pltpu.VMEM((SC_CHUNK,), jnp.int32), pltpu.VMEM((SC_CHUNK, w), table.dtype),
                       pltpu.SemaphoreType.DMA])
    def gather(table_hbm, idx_hbm, out_hbm, idx_v, rows_v, sem):
        base = _sc_worker_base(per_worker)

        @pl.loop(0, per_worker // SC_CHUNK)
        def _(c):
            off = pl.multiple_of(base + c * SC_CHUNK, SC_CHUNK)
            pltpu.sync_copy(idx_hbm.at[pl.ds(off, SC_CHUNK)], idx_v)
            pltpu.async_copy(table_hbm.at[idx_v], rows_v, sem).wait()
            pltpu.sync_copy(rows_v, out_hbm.at[pl.ds(off, SC_CHUNK)])

    return gather(table, idx)


def _experts_kernel(blk_exp_ref, n_used_ref, xs_ref, wg_ref, wu_ref, wd_ref, ys_ref):
    @pl.when(pl.program_id(0) < n_used_ref[0])
    def _():
        xb = _unpack_rows(xs_ref[...]).astype(BF16)
        g = jnp.dot(xb, wg_ref[0].astype(BF16), preferred_element_type=F32)
        u = jnp.dot(xb, wu_ref[0].astype(BF16), preferred_element_type=F32)
        hb = (_silu(g) * u).astype(BF16)
        ys_ref[...] = _pack_rows(jnp.dot(hb, wd_ref[0].astype(BF16), preferred_element_type=F32))


def _experts(blk_exp, n_used, xs, w_gate, w_up, w_down):
    n_rows, w = xs.shape
    d = D_MODEL
    n_blocks = n_rows // ROW_BLOCK
    row = lambda i, be, nu: (jnp.minimum(i, nu[0] - 1), 0)
    wsel = lambda i, be, nu: (be[i], 0, 0)
    return pl.pallas_call(
        _experts_kernel,
        grid_spec=pltpu.PrefetchScalarGridSpec(
            num_scalar_prefetch=2,
            grid=(n_blocks,),
            in_specs=[pl.BlockSpec((ROW_BLOCK, w), row),
                      pl.BlockSpec((1, d, EXPERT_FF), wsel),
                      pl.BlockSpec((1, d, EXPERT_FF), wsel),
                      pl.BlockSpec((1, EXPERT_FF, d), wsel)],
            out_specs=pl.BlockSpec((ROW_BLOCK, w), row)),
        out_shape=jax.ShapeDtypeStruct((n_rows, w), jnp.int32),
        compiler_params=_cparams(("arbitrary",)),
        name="experts",
    )(blk_exp, n_used, xs, w_gate, w_up, w_down)


def _combine_kernel(g_ref, w_ref, h2_ref, x1_ref, mod_ref, wsg_ref, wsu_ref, wsd_ref, out_ref):
    hb = _unpack_rows(h2_ref[...]).astype(BF16)
    g = jnp.dot(hb, wsg_ref[...], preferred_element_type=F32)
    u = jnp.dot(hb, wsu_ref[...], preferred_element_type=F32)
    acc = jnp.dot((_silu(g) * u).astype(BF16), wsd_ref[...], preferred_element_type=F32)
    w = w_ref[...]
    for k in range(TOP_K):
        acc = acc + _unpack_rows(g_ref[k]) * w[:, k:k + 1]
    out_ref[...] = x1_ref[...] + mod_ref[0, 5:6, :] * acc


def _combine(gathered, wts_t, h2p, x1, mod, p, tt, seq):
    t, d = x1.shape
    per_seq = seq // tt
    tok = pl.BlockSpec((tt, d), lambda i: (i, 0))
    const = lambda shape: pl.BlockSpec(shape, lambda i: (0,) * len(shape))
    return pl.pallas_call(
        _combine_kernel,
        grid=(t // tt,),
        in_specs=[pl.BlockSpec((TOP_K, tt, PACK_W), lambda i: (0, i, 0)),
                  pl.BlockSpec((tt, TOPK_PAD), lambda i: (i, 0)),
                  pl.BlockSpec((tt, PACK_W), lambda i: (i, 0)),
                  tok,
                  pl.BlockSpec((1, 6, d), lambda i: (i // per_seq, 0, 0)),
                  const((d, SHARED_FF)), const((d, SHARED_FF)), const((SHARED_FF, d))],
        out_specs=tok,
        out_shape=jax.ShapeDtypeStruct((t, d), F32),
        compiler_params=_cparams(("parallel",)),
        name="combine",
    )(gathered, wts_t, h2p, x1, mod, p["ws_gate"], p["ws_up"], p["ws_down"])


def _rope_tables(s):
    half = QK_ROPE_DIM // 2
    inv_freq = ROPE_THETA ** (-jnp.arange(half, dtype=F32) / half)
    ang = jnp.arange(s, dtype=F32)[:, None] * inv_freq[None, :]
    cos, sin = jnp.cos(ang), jnp.sin(ang)
    z = lambda n: jnp.zeros((s, n), F32)
    tab_cos = jnp.concatenate([jnp.ones((s, QK_NOPE_DIM), F32), cos, cos, z(HEAD_PAD - QK_HEAD_DIM)], 1)
    tab_sa = jnp.concatenate([z(QK_NOPE_DIM), -sin, z(HEAD_PAD - QK_NOPE_DIM - half)], 1)
    tab_sb = jnp.concatenate([z(QK_NOPE_DIM + half), sin, z(HEAD_PAD - QK_HEAD_DIM)], 1)
    return dict(rope_cos=tab_cos, rope_sa=tab_sa, rope_sb=tab_sb, rope_cos_t=cos.T, rope_sin_t=sin.T)


def _prep_weights(norm1_w, w_in, pool_w, pool_scale, q_a_norm_w, w_q_b, kv_a_norm_w, w_kv_b,
                  q_norm_w, k_norm_w, w_o, norm2_w, w_router, w_gate, w_up, w_down,
                  ws_gate, ws_up, ws_down):
    d = D_MODEL
    c0 = POOL_WIDTH + Q_LORA_RANK + KV_LORA_RANK
    pad_h = HEAD_PAD - QK_HEAD_DIM
    w_in_p = jnp.concatenate(
        [w_in[:, :c0], jnp.zeros((d, QK_NOPE_DIM), F32), w_in[:, c0:], jnp.zeros((d, pad_h), F32)], 1)
    w_q = jnp.pad(w_q_b.reshape(Q_LORA_RANK, N_HEADS, QK_HEAD_DIM), ((0, 0), (0, 0), (0, pad_h)))
    kv = w_kv_b.reshape(KV_LORA_RANK, N_HEADS, QK_NOPE_DIM + V_HEAD_DIM)
    w_k = jnp.pad(kv[:, :, :QK_NOPE_DIM], ((0, 0), (0, 0), (0, HEAD_PAD - QK_NOPE_DIM)))
    w_v = kv[:, :, QK_NOPE_DIM:]
    w_r_t = w_router.T
    w_r_hi = w_r_t.astype(BF16)
    w_r_lo = (w_r_t - w_r_hi.astype(F32)).astype(BF16)
    q_gain = q_norm_w * (QK_HEAD_DIM ** -0.5 * LOG2E)
    bound = QK_HEAD_DIM * jnp.max(jnp.abs(q_gain)) * jnp.max(jnp.abs(k_norm_w)) * 1.02 + 0.25
    bound = bound.astype(BF16).astype(F32)
    lane = jnp.arange(HEAD_PAD) == SHIFT_LANE
    return dict(
        score_bound=bound,
        k_spare=jnp.where(lane, -bound, 0.0).astype(F32).reshape(1, HEAD_PAD),
        norm1_w=norm1_w.reshape(1, d), w_in=w_in_p.astype(BF16),
        q_a_norm_w=q_a_norm_w.reshape(1, -1), w_q_t=w_q.reshape(Q_LORA_RANK, -1).T.astype(BF16),
        kv_a_norm_w=kv_a_norm_w.reshape(1, -1), w_k=w_k.reshape(KV_LORA_RANK, -1).astype(BF16),
        w_v_t=w_v.reshape(KV_LORA_RANK, -1).T.astype(BF16),
        q_gain_col=jnp.broadcast_to(jnp.pad(q_gain, (0, pad_h))[:, None], (HEAD_PAD, LANES)),
        k_norm_w=jnp.pad(k_norm_w, (0, pad_h)).reshape(1, HEAD_PAD),
        pool_w=pool_w.astype(BF16), pool_scale=pool_scale.reshape(1, -1), w_o=w_o.astype(BF16),
        norm2_w=norm2_w.reshape(1, d), w_router_hi=w_r_hi, w_router_lo=w_r_lo,
        w_gate=w_gate, w_up=w_up, w_down=w_down,
        ws_gate=ws_gate.astype(BF16), ws_up=ws_up.astype(BF16), ws_down=ws_down.astype(BF16))


def _tile(n, pref):
    return pref if n % pref == 0 else n


def _mixer_and_routing(x, mod, router_bias, p):
    b, s, d = x.shape
    t = b * s
    ts = _tile(s, 512)
    p = dict(p, **_rope_tables(s))

    u, qt, k, vt = _inproj(x, mod, p, ts)
    online = (p["score_bound"] > MAX_FIXED_SHIFT).astype(jnp.int32).reshape(1)
    attn = _attention(online, qt, k, vt, _tile(s, 1024), _tile(s, 1024), _tile(s, 512))
    x1, h2p, logits_t = _postmix(u, attn, x, mod, p, ts)
    tr = _tile(t, 512)
    idx, wts, pos, cnt = _router(logits_t, router_bias, tr)

    counts = cnt[:, 0].astype(jnp.int32)
    padded = (counts + ROW_BLOCK - 1) // ROW_BLOCK * ROW_BLOCK
    pad_end = jnp.cumsum(padded)
    dest = _dest_rows(idx, pos, pad_end - padded, tr)
    n_blocks = -(-t * TOP_K // ROW_BLOCK) + N_EXPERTS
    blk_row = jnp.arange(n_blocks, dtype=jnp.int32)[:, None] * ROW_BLOCK
    blk_exp = jnp.minimum(jnp.sum((pad_end[None, :] <= blk_row).astype(jnp.int32), axis=1),
                          N_EXPERTS - 1)
    n_used = (pad_end[-1:] // ROW_BLOCK).astype(jnp.int32)
    h2f = h2p.reshape(t, PACK_W)
    xs = _sc_scatter_rows(h2f, dest, n_blocks * ROW_BLOCK)
    return dict(xs=xs, dest=dest, blk_exp=blk_exp, n_used=n_used, wts_t=wts.T, h2f=h2f,
                x1=x1.reshape(t, d), mod=mod, shape=(b, s, d))


def _gather_and_combine(st, ys, p):
    b, s, d = st["shape"]
    t = b * s
    gathered = _sc_gather_rows(ys, st["dest"][:TOP_K].reshape(TOP_K * t)).reshape(TOP_K, t, PACK_W)
    out = _combine(gathered, st["wts_t"], st["h2f"], st["x1"], st["mod"], p, _tile(s, 256), s)
    return out.reshape(b, s, d)


def kernel(x_prompt, x_sample, c_prompt, c_sample, w_ada, b_ada, norm1_w, w_in, pool_w, pool_scale,
           q_a_norm_w, w_q_b, kv_a_norm_w, w_kv_b, q_norm_w, k_norm_w, w_o, norm2_w, w_router,
           router_bias, w_gate, w_up, w_down, ws_gate, ws_up, ws_down):
    assert w_ada.shape[0] == 1, "single-layer encoder"
    p = _prep_weights(norm1_w[0], w_in[0], pool_w[0], pool_scale[0], q_a_norm_w[0], w_q_b[0],
                      kv_a_norm_w[0], w_kv_b[0], q_norm_w[0], k_norm_w[0], w_o[0], norm2_w[0],
                      w_router[0], w_gate[0], w_up[0], w_down[0], ws_gate[0], ws_up[0], ws_down[0])
    nb = x_prompt.shape[0]
    c = jnp.concatenate([c_prompt, c_sample], axis=0).astype(F32)
    mod = _adaln(c, w_ada[0], b_ada[0]).reshape(c.shape[0], 6, D_MODEL)
    experts = lambda st, xs: _experts(st["blk_exp"], st["n_used"], xs, p["w_gate"], p["w_up"], p["w_down"])

    sp = _mixer_and_routing(x_prompt, mod[:nb], router_bias[0], p)
    dest_p, x_sample = lax.optimization_barrier((sp["dest"], x_sample))
    sp["dest"] = dest_p
    ss = _mixer_and_routing(x_sample, mod[nb:], router_bias[0], p)
    ys_p = experts(sp, sp["xs"])
    ys_p, xs_s = lax.optimization_barrier((ys_p, ss["xs"]))
    ys_s = experts(ss, xs_s)
    y_prompt = _gather_and_combine(sp, ys_p, p)
    y_sample = _gather_and_combine(ss, ys_s, p)
    return (y_prompt, y_sample)
```

```python
import functools

import jax
import jax.numpy as jnp
from jax import lax
from jax.experimental import pallas as pl
from jax.experimental.pallas import tpu as pltpu
from jax.experimental.pallas import tpu_sc as plsc

D_MODEL = 1024
POOL_WIDTH = 512
POOL_WINDOWS = (2, 4, 8, 16)
POOL_GROUP = 128
N_HEADS = 8
V_HEAD_DIM = 64
QK_NOPE_DIM = 64
QK_ROPE_DIM = 32
QK_HEAD_DIM = 96
Q_LORA_RANK = 256
KV_LORA_RANK = 128
ROPE_THETA = 10000.0
N_EXPERTS = 64
TOP_K = 6
N_GROUPS = 8
TOPK_GROUPS = 4
EXPERT_FF = 256
SHARED_FF = 256
ROUTED_SCALE = 2.5
EPS = 1e-6

LANES = 128
HEAD_PAD = 128
HALO = 16
ROW_BLOCK = 512
TOPK_PAD = 8
V_AUG = 80
SHIFT_LANE = QK_HEAD_DIM
MAX_FIXED_SHIFT = 40.0
LOG2E = 1.4426950408889634
PACK_W = D_MODEL // 2
SC_CORES = 2
SC_WORKERS = 32
SC_CHUNK = 128
VMEM_LIMIT = 48 * 1024 * 1024

F32 = jnp.float32
BF16 = jnp.bfloat16
QK_DTYPE = jnp.bfloat16


def _cparams(sem):
    return pltpu.CompilerParams(dimension_semantics=sem, vmem_limit_bytes=VMEM_LIMIT)


def _silu(x):
    return x * (1.0 / (1.0 + jnp.exp(-x)))


def _adaln_kernel(c_ref, w_ref, b_ref, o_ref):
    c = c_ref[...]
    o_ref[...] = jnp.dot(_silu(c), w_ref[...], preferred_element_type=F32,
                         precision=lax.Precision.HIGHEST) + b_ref[...]


def _adaln(c, w_ada, b_ada):
    nb, d = c.shape
    n = w_ada.shape[1]
    tn = 1536
    return pl.pallas_call(
        _adaln_kernel,
        grid=(n // tn,),
        in_specs=[pl.BlockSpec((nb, d), lambda j: (0, 0)),
                  pl.BlockSpec((d, tn), lambda j: (0, j)),
                  pl.BlockSpec((1, tn), lambda j: (0, j))],
        out_specs=pl.BlockSpec((nb, tn), lambda j: (0, j)),
        out_shape=jax.ShapeDtypeStruct((nb, n), F32),
        compiler_params=_cparams(("arbitrary",)),
        name="adaln",
    )(c, w_ada, b_ada.reshape(1, n))


def _inproj_kernel(x_ref, mod_ref, n1w_ref, win_ref, qan_ref, wqt_ref, kvan_ref, wk_ref, wvt_ref,
                   qg_ref, kg_ref, ksp_ref, cos_ref, sa_ref, sb_ref, cost_ref, sint_ref,
                   u_ref, qt_ref, k_ref, vt_ref):
    ts = x_ref.shape[1]
    half = QK_ROPE_DIM // 2
    x = x_ref[0]
    shift1 = mod_ref[0, 0:1, :]
    scale1 = mod_ref[0, 1:2, :]
    r = lax.rsqrt(jnp.mean(x * x, axis=-1, keepdims=True) + EPS)
    h = x * r * n1w_ref[...] * (1.0 + scale1) + shift1
    z = jnp.dot(h.astype(BF16), win_ref[...], preferred_element_type=F32)
    u_ref[0] = z[:, :POOL_WIDTH].astype(BF16)

    cq = z[:, POOL_WIDTH:POOL_WIDTH + Q_LORA_RANK]
    cqn = cq * lax.rsqrt(jnp.mean(cq * cq, axis=-1, keepdims=True) + EPS) * qan_ref[...]
    qt = jnp.dot(wqt_ref[...], cqn.T.astype(BF16), preferred_element_type=F32)
    reps = ts // LANES
    qg = jnp.concatenate([qg_ref[...]] * reps, axis=1)
    cost = cost_ref[...]
    sint = sint_ref[...]
    spare = jnp.where(lax.broadcasted_iota(jnp.int32, (HEAD_PAD - QK_HEAD_DIM, ts), 0) == 0, 1.0, 0.0)
    for hd in range(N_HEADS):
        t = qt[hd * HEAD_PAD:(hd + 1) * HEAD_PAD]
        rn = lax.rsqrt(jnp.sum(t * t, axis=0, keepdims=True) * (1.0 / QK_HEAD_DIM) + EPS)
        tn = t * rn * qg
        t1 = tn[QK_NOPE_DIM:QK_NOPE_DIM + half]
        t2 = tn[QK_NOPE_DIM + half:QK_HEAD_DIM]
        out = jnp.concatenate([tn[:QK_NOPE_DIM], t1 * cost - t2 * sint, t1 * sint + t2 * cost, spare],
                              axis=0)
        qt_ref[0, hd * HEAD_PAD:(hd + 1) * HEAD_PAD, :] = out.astype(QK_DTYPE)

    c0 = POOL_WIDTH + Q_LORA_RANK
    ckv = z[:, c0:c0 + KV_LORA_RANK]
    ckvn = ckv * lax.rsqrt(jnp.mean(ckv * ckv, axis=-1, keepdims=True) + EPS) * kvan_ref[...]

    vt = jnp.dot(wvt_ref[...], ckvn.T.astype(BF16), preferred_element_type=F32)
    ones = jnp.ones((V_AUG - V_HEAD_DIM, ts), BF16)
    for hd in range(N_HEADS):
        vt_ref[0, hd * V_AUG:hd * V_AUG + V_HEAD_DIM, :] = (
            vt[hd * V_HEAD_DIM:(hd + 1) * V_HEAD_DIM].astype(BF16))
        vt_ref[0, hd * V_AUG + V_HEAD_DIM:(hd + 1) * V_AUG, :] = ones

    kk = jnp.dot(ckvn.astype(BF16), wk_ref[...], preferred_element_type=F32)
    kpe = z[:, c0 + KV_LORA_RANK:]
    kg = kg_ref[...]
    ksp = ksp_ref[...]
    pe_ssq = jnp.sum(kpe * kpe, axis=-1, keepdims=True)
    pg = kpe * kg
    pe_rot = (pg * cos_ref[...] + pltpu.roll(pg, HEAD_PAD - half, 1) * sa_ref[...]
              + pltpu.roll(pg, half, 1) * sb_ref[...])
    for hd in range(N_HEADS):
        t = kk[:, hd * HEAD_PAD:(hd + 1) * HEAD_PAD]
        ssq = jnp.sum(t * t, axis=-1, keepdims=True) + pe_ssq
        rn = lax.rsqrt(ssq * (1.0 / QK_HEAD_DIM) + EPS)
        k_ref[0, :, hd * HEAD_PAD:(hd + 1) * HEAD_PAD] = ((t * kg + pe_rot) * rn + ksp).astype(QK_DTYPE)


def _inproj(x, mod, p, ts):
    b, s, d = x.shape
    qk_w = N_HEADS * HEAD_PAD
    half = QK_ROPE_DIM // 2
    const = lambda shape: pl.BlockSpec(shape, lambda bi, i: (0,) * len(shape))
    row_tab = pl.BlockSpec((ts, HEAD_PAD), lambda bi, i: (i, 0))
    col_tab = pl.BlockSpec((half, ts), lambda bi, i: (0, i))
    return pl.pallas_call(
        _inproj_kernel,
        grid=(b, s // ts),
        in_specs=[pl.BlockSpec((1, ts, d), lambda bi, i: (bi, i, 0)),
                  pl.BlockSpec((1, 6, d), lambda bi, i: (bi, 0, 0)),
                  const((1, d)), const((d, d)), const((1, Q_LORA_RANK)),
                  const((qk_w, Q_LORA_RANK)), const((1, KV_LORA_RANK)),
                  const((KV_LORA_RANK, qk_w)), const((N_HEADS * V_HEAD_DIM, KV_LORA_RANK)),
                  const((HEAD_PAD, LANES)), const((1, HEAD_PAD)), const((1, HEAD_PAD)),
                  row_tab, row_tab, row_tab, col_tab, col_tab],
        out_specs=[pl.BlockSpec((1, ts, POOL_WIDTH), lambda bi, i: (bi, i, 0)),
                   pl.BlockSpec((1, qk_w, ts), lambda bi, i: (bi, 0, i)),
                   pl.BlockSpec((1, ts, qk_w), lambda bi, i: (bi, i, 0)),
                   pl.BlockSpec((1, N_HEADS * V_AUG, ts), lambda bi, i: (bi, 0, i))],
        out_shape=[jax.ShapeDtypeStruct((b, s, POOL_WIDTH), BF16),
                   jax.ShapeDtypeStruct((b, qk_w, s), QK_DTYPE),
                   jax.ShapeDtypeStruct((b, s, qk_w), QK_DTYPE),
                   jax.ShapeDtypeStruct((b, N_HEADS * V_AUG, s), BF16)],
        compiler_params=_cparams(("parallel", "parallel")),
        name="inproj",
    )(x, mod, p["norm1_w"], p["w_in"], p["q_a_norm_w"], p["w_q_t"], p["kv_a_norm_w"], p["w_k"],
      p["w_v_t"], p["q_gain_col"], p["k_norm_w"], p["k_spare"],
      p["rope_cos"], p["rope_sa"], p["rope_sb"], p["rope_cos_t"], p["rope_sin_t"])


def _attn_kernel(online_ref, qt_ref, k_ref, vt_ref, o_ref, acc_a, acc_b, m_ref, *, tk, tk_online):
    s_len = k_ref.shape[1]
    acc_a[...] = jnp.zeros_like(acc_a)
    acc_b[...] = jnp.zeros_like(acc_b)
    qa = qt_ref[0, :HEAD_PAD, :]
    qb = qt_ref[0, HEAD_PAD:, :]

    @pl.when(online_ref[0] == 0)
    def _():
        def body(c, _):
            off = pl.multiple_of(c * tk, tk)
            ks = k_ref[0, pl.ds(off, tk), :]
            vts = vt_ref[0, :, pl.ds(off, tk)]
            pa = jnp.exp2(jnp.dot(ks[:, :HEAD_PAD], qa, preferred_element_type=F32)).astype(BF16)
            pb = jnp.exp2(jnp.dot(ks[:, HEAD_PAD:], qb, preferred_element_type=F32)).astype(BF16)
            acc_a[...] += jnp.dot(vts[:V_AUG], pa, preferred_element_type=F32)
            acc_b[...] += jnp.dot(vts[V_AUG:], pb, preferred_element_type=F32)
            return 0

        lax.fori_loop(0, s_len // tk, body, 0)

    @pl.when(online_ref[0] != 0)
    def _():
        m_ref[...] = jnp.full_like(m_ref, -jnp.inf)

        def one_head(kh, q, vth, acc, row):
            s = jnp.dot(kh, q, preferred_element_type=F32)
            m_old = m_ref[row:row + 1, :]
            m_new = jnp.maximum(m_old, jnp.max(s, axis=0, keepdims=True))
            p = jnp.exp2(s - m_new).astype(BF16)
            acc[...] = acc[...] * jnp.exp2(m_old - m_new) + jnp.dot(vth, p, preferred_element_type=F32)
            m_ref[row:row + 1, :] = m_new

        def body(c, _):
            off = pl.multiple_of(c * tk_online, tk_online)
            ks = k_ref[0, pl.ds(off, tk_online), :]
            vts = vt_ref[0, :, pl.ds(off, tk_online)]
            one_head(ks[:, :HEAD_PAD], qa, vts[:V_AUG], acc_a, 0)
            one_head(ks[:, HEAD_PAD:], qb, vts[V_AUG:], acc_b, 1)
            return 0

        lax.fori_loop(0, s_len // tk_online, body, 0)

    oa = acc_a[:V_HEAD_DIM] / acc_a[V_HEAD_DIM:V_HEAD_DIM + 1]
    ob = acc_b[:V_HEAD_DIM] / acc_b[V_HEAD_DIM:V_HEAD_DIM + 1]
    o_ref[0] = jnp.concatenate([oa, ob], axis=0).T.astype(BF16)


def _attention(online, qt, k, vt, tq, tk, tk_online):
    b, _, s = qt.shape
    return pl.pallas_call(
        functools.partial(_attn_kernel, tk=tk, tk_online=tk_online),
        grid_spec=pltpu.PrefetchScalarGridSpec(
            num_scalar_prefetch=1,
            grid=(b, N_HEADS // 2, s // tq),
            in_specs=[pl.BlockSpec((1, 2 * HEAD_PAD, tq), lambda bi, j, i, on: (bi, j, i)),
                      pl.BlockSpec((1, s, 2 * HEAD_PAD), lambda bi, j, i, on: (bi, 0, j)),
                      pl.BlockSpec((1, 2 * V_AUG, s), lambda bi, j, i, on: (bi, j, 0))],
            out_specs=pl.BlockSpec((1, tq, 2 * V_HEAD_DIM), lambda bi, j, i, on: (bi, i, j)),
            scratch_shapes=[pltpu.VMEM((V_AUG, tq), F32), pltpu.VMEM((V_AUG, tq), F32),
                            pltpu.VMEM((8, tq), F32)]),
        out_shape=jax.ShapeDtypeStruct((b, s, N_HEADS * V_HEAD_DIM), BF16),
        compiler_params=_cparams(("parallel", "parallel", "arbitrary")),
        name="attn",
    )(online, qt, k, vt)


def _postmix_kernel(u_ref, up_ref, un_ref, a_ref, x_ref, mod_ref, pw_ref, ps_ref, wo_ref, n2w_ref,
                    wrh_ref, wrl_ref, x1_ref, h2_ref, lg_ref, *, seq):
    i = pl.program_id(1)
    ts = u_ref.shape[1]
    ext_rows = ts + 2 * HALO
    ext = jnp.concatenate([up_ref[0], u_ref[0], un_ref[0]], axis=0).astype(F32)
    pos = i * ts - HALO + lax.broadcasted_iota(jnp.int32, (ext_rows, 1), 0)
    ext = jnp.where((pos >= 0) & (pos < seq), ext, 0.0)
    p = i * ts + lax.broadcasted_iota(jnp.int32, (ts, 1), 0)

    outs = []
    for g, w in enumerate(POOL_WINDOWS):
        left = w // 2
        right = w - 1 - left
        t = ext[:, g * POOL_GROUP:(g + 1) * POOL_GROUP]
        step = 1
        while step < w:
            t = t + pltpu.roll(t, ext_rows - step, 0)
            step *= 2
        win = pltpu.roll(t, left, 0)[HALO:HALO + ts]
        cnt = (jnp.minimum(p + right + 1, seq) - jnp.maximum(p - left, 0)).astype(F32)
        d = win / cnt - ext[HALO:HALO + ts, g * POOL_GROUP:(g + 1) * POOL_GROUP]
        outs.append(jnp.dot(d.astype(BF16), pw_ref[g], preferred_element_type=F32))
    pool = (jnp.concatenate(outs, axis=-1) * ps_ref[...]).astype(BF16)

    mix = (jnp.dot(pool, wo_ref[:POOL_WIDTH, :], preferred_element_type=F32)
           + jnp.dot(a_ref[0], wo_ref[POOL_WIDTH:, :], preferred_element_type=F32))
    x1 = x_ref[0] + mod_ref[0, 2:3, :] * mix
    x1_ref[0] = x1
    r = lax.rsqrt(jnp.mean(x1 * x1, axis=-1, keepdims=True) + EPS)
    h2 = x1 * r * n2w_ref[...] * (1.0 + mod_ref[0, 4:5, :]) + mod_ref[0, 3:4, :]
    h2_ref[0] = _pack_rows(h2)
    hi = h2.astype(BF16)
    lo = (h2 - hi.astype(F32)).astype(BF16)
    nt = (((1,), (1,)), ((), ()))
    lg_ref[...] = (lax.dot_general(wrh_ref[...], hi, nt, preferred_element_type=F32)
                   + lax.dot_general(wrl_ref[...], hi, nt, preferred_element_type=F32)
                   + lax.dot_general(wrh_ref[...], lo, nt, preferred_element_type=F32))


def _postmix(u, attn, x, mod, p, ts):
    b, s, d = x.shape
    nt = s // ts
    hb = ts // HALO
    const = lambda shape: pl.BlockSpec(shape, lambda bi, i: (0,) * len(shape))
    return pl.pallas_call(
        functools.partial(_postmix_kernel, seq=s),
        grid=(b, nt),
        in_specs=[pl.BlockSpec((1, ts, POOL_WIDTH), lambda bi, i: (bi, i, 0)),
                  pl.BlockSpec((1, HALO, POOL_WIDTH), lambda bi, i: (bi, jnp.maximum(i * hb - 1, 0), 0)),
                  pl.BlockSpec((1, HALO, POOL_WIDTH),
                               lambda bi, i: (bi, jnp.minimum((i + 1) * hb, s // HALO - 1), 0)),
                  pl.BlockSpec((1, ts, POOL_WIDTH), lambda bi, i: (bi, i, 0)),
                  pl.BlockSpec((1, ts, d), lambda bi, i: (bi, i, 0)),
                  pl.BlockSpec((1, 6, d), lambda bi, i: (bi, 0, 0)),
                  const((len(POOL_WINDOWS), POOL_GROUP, POOL_GROUP)), const((1, POOL_WIDTH)),
                  const((d, d)), const((1, d)), const((N_EXPERTS, d)), const((N_EXPERTS, d))],
        out_specs=[pl.BlockSpec((1, ts, d), lambda bi, i: (bi, i, 0)),
                   pl.BlockSpec((1, ts, PACK_W), lambda bi, i: (bi, i, 0)),
                   pl.BlockSpec((N_EXPERTS, ts), lambda bi, i: (0, bi * nt + i))],
        out_shape=[jax.ShapeDtypeStruct((b, s, d), F32),
                   jax.ShapeDtypeStruct((b, s, PACK_W), jnp.int32),
                   jax.ShapeDtypeStruct((N_EXPERTS, b * s), F32)],
        compiler_params=_cparams(("parallel", "parallel")),
        name="postmix",
    )(u, u, u, attn, x, mod, p["pool_w"], p["pool_scale"], p["w_o"], p["norm2_w"],
      p["w_router_hi"], p["w_router_lo"])


def _router_kernel(lg_ref, bias_ref, tri_ref, idx_ref, wts_ref, pos_ref, cnt_ref, carry_ref):
    @pl.when(pl.program_id(0) == 0)
    def _():
        carry_ref[...] = jnp.zeros_like(carry_ref)

    ts = lg_ref.shape[1]
    gsz = N_EXPERTS // N_GROUPS
    ninf = -jnp.inf
    scores = 1.0 / (1.0 + jnp.exp(-lg_ref[...]))
    choice = scores + bias_ref[...]
    sub = lax.broadcasted_iota(jnp.int32, (gsz, ts), 0)

    gs_rows = []
    for g in range(N_GROUPS):
        grp = choice[g * gsz:(g + 1) * gsz]
        m1 = jnp.max(grp, axis=0, keepdims=True)
        i1 = jnp.min(jnp.where(grp == m1, sub, gsz), axis=0, keepdims=True)
        m2 = jnp.max(jnp.where(sub == i1, ninf, grp), axis=0, keepdims=True)
        gs_rows.append(m1 + m2)
    gs = jnp.concatenate(gs_rows, axis=0)

    rank = jnp.zeros((N_GROUPS, ts), jnp.int32)
    for g in range(N_GROUPS):
        row = gs[g:g + 1]
        beats = (row > gs) | ((row == gs) & (sub > g))
        rank = rank + beats.astype(jnp.int32)
    gsel = rank < TOPK_GROUPS

    masked = jnp.concatenate(
        [jnp.where(gsel[g:g + 1], choice[g * gsz:(g + 1) * gsz], ninf) for g in range(N_GROUPS)],
        axis=0)
    eio = lax.broadcasted_iota(jnp.int32, (N_EXPERTS, ts), 0)
    idx_rows, w_rows, hits = [], [], []
    for _ in range(TOP_K):
        m = jnp.max(masked, axis=0, keepdims=True)
        i = jnp.min(jnp.where(masked == m, eio, N_EXPERTS), axis=0, keepdims=True)
        hit = eio == i
        w_rows.append(jnp.sum(jnp.where(hit, scores, 0.0), axis=0, keepdims=True))
        masked = jnp.where(hit, ninf, masked)
        idx_rows.append(i)
        hits.append(hit)

    wsum = functools.reduce(lambda a, c: a + c, w_rows)
    pad_i = [jnp.zeros((1, ts), jnp.int32)] * (TOPK_PAD - TOP_K)
    pad_f = [jnp.zeros((1, ts), F32)] * (TOPK_PAD - TOP_K)
    idx_ref[...] = jnp.concatenate(idx_rows + pad_i, axis=0)
    wts_ref[...] = jnp.concatenate([w / wsum * ROUTED_SCALE for w in w_rows] + pad_f, axis=0)

    sel = functools.reduce(lambda a, c: a | c, hits)
    onehot = jnp.where(sel, 1.0, 0.0).astype(BF16)
    run = jnp.dot(onehot, tri_ref[...], preferred_element_type=F32) + carry_ref[:, 0:1]
    pos_rows = [jnp.sum(jnp.where(h, run - 1.0, 0.0), axis=0, keepdims=True).astype(jnp.int32)
                for h in hits]
    pos_ref[...] = jnp.concatenate(pos_rows + pad_i, axis=0)
    total = run[:, ts - 1:ts]
    carry_ref[...] = jnp.broadcast_to(total, carry_ref.shape)
    cnt_ref[...] = jnp.broadcast_to(total, cnt_ref.shape)


def _router(logits_t, router_bias, ts):
    t = logits_t.shape[1]
    tri = jnp.triu(jnp.ones((ts, ts), BF16))
    tok = pl.BlockSpec((TOPK_PAD, ts), lambda i: (0, i))
    return pl.pallas_call(
        _router_kernel,
        grid=(t // ts,),
        in_specs=[pl.BlockSpec((N_EXPERTS, ts), lambda i: (0, i)),
                  pl.BlockSpec((N_EXPERTS, 1), lambda i: (0, 0)),
                  pl.BlockSpec((ts, ts), lambda i: (0, 0))],
        out_specs=[tok, tok, tok, pl.BlockSpec((N_EXPERTS, LANES), lambda i: (0, 0))],
        out_shape=[jax.ShapeDtypeStruct((TOPK_PAD, t), jnp.int32),
                   jax.ShapeDtypeStruct((TOPK_PAD, t), F32),
                   jax.ShapeDtypeStruct((TOPK_PAD, t), jnp.int32),
                   jax.ShapeDtypeStruct((N_EXPERTS, LANES), F32)],
        scratch_shapes=[pltpu.VMEM((N_EXPERTS, LANES), F32)],
        compiler_params=_cparams(("arbitrary",)),
        name="router",
    )(logits_t, router_bias.reshape(N_EXPERTS, 1), tri)


def _dest_kernel(idx_ref, pos_ref, start_ref, dest_ref):
    ts = idx_ref.shape[1]
    eio = lax.broadcasted_iota(jnp.int32, (N_EXPERTS, ts), 0)
    start = start_ref[...]
    rows = [pos_ref[k:k + 1, :]
            + jnp.sum(jnp.where(eio == idx_ref[k:k + 1, :], start, 0), axis=0, keepdims=True)
            for k in range(TOPK_PAD)]
    dest_ref[...] = jnp.concatenate(rows, axis=0)


def _dest_rows(idx, pos, pad_start, ts):
    t = idx.shape[1]
    tok = pl.BlockSpec((TOPK_PAD, ts), lambda i: (0, i))
    return pl.pallas_call(
        _dest_kernel,
        grid=(t // ts,),
        in_specs=[tok, tok, pl.BlockSpec((N_EXPERTS, 1), lambda i: (0, 0))],
        out_specs=tok,
        out_shape=jax.ShapeDtypeStruct((TOPK_PAD, t), jnp.int32),
        compiler_params=_cparams(("parallel",)),
        name="dest_rows",
    )(idx, pos, pad_start.reshape(N_EXPERTS, 1))


def _pack_rows(x):
    bits = lax.bitcast_convert_type(x.astype(BF16).astype(F32), jnp.int32)
    return bits[:, :PACK_W] | lax.shift_right_logical(bits[:, PACK_W:], 16)


def _unpack_rows(words):
    hi = lax.bitcast_convert_type(words & jnp.int32(-65536), F32)
    lo = lax.bitcast_convert_type(lax.shift_left(words, 16), F32)
    return jnp.concatenate([hi, lo], axis=1)


def _sc_mesh():
    return plsc.VectorSubcoreMesh(core_axis_name="core", subcore_axis_name="subcore")


def _sc_worker_base(rows_per_worker):
    return (lax.axis_index("subcore") * SC_CORES + lax.axis_index("core")) * rows_per_worker


def _sc_scatter_rows(x, dest, n_rows):
    t, w = x.shape
    per_worker = t // SC_WORKERS
    assert per_worker * SC_WORKERS == t and per_worker % SC_CHUNK == 0

    @functools.partial(
        pl.kernel, out_type=jax.ShapeDtypeStruct((n_rows, w), x.dtype), mesh=_sc_mesh(),
        scratch_types=[pltpu.VMEM((TOPK_PAD, SC_CHUNK), jnp.int32), pltpu.VMEM((SC_CHUNK, w), x.dtype),
                       pltpu.SemaphoreType.DMA])
    def scatter(x_hbm, dest_hbm, out_hbm, idx_v, rows_v, sem):
        base = _sc_worker_base(per_worker)

        @pl.loop(0, per_worker // SC_CHUNK)
        def _(c):
            off = pl.multiple_of(base + c * SC_CHUNK, SC_CHUNK)
            pltpu.sync_copy(dest_hbm.at[:, pl.ds(off, SC_CHUNK)], idx_v)
            pltpu.sync_copy(x_hbm.at[pl.ds(off, SC_CHUNK)], rows_v)
            copies = [pltpu.async_copy(rows_v, out_hbm.at[idx_v.at[k]], sem) for k in range(TOP_K)]
            for cp in copies:
                cp.wait()

    return scatter(x, dest)


def _sc_gather_rows(table, idx):
    m = idx.shape[0]
    w = table.shape[1]
    per_worker = m // SC_WORKERS
    assert per_worker * SC_WORKERS == m and per_worker % SC_CHUNK == 0

    @functools.partial(
        pl.kernel, out_type=jax.ShapeDtypeStruct((m, w), table.dtype), mesh=_sc_mesh(),
        scratch_types=[pltpu.VMEM((SC_CHUNK,), jnp.int32), pltpu.VMEM((SC_CHUNK, w), table.dtype),
                       pltpu.SemaphoreType.DMA])
    def gather(table_hbm, idx_hbm, out_hbm, idx_v, rows_v, sem):
        base = _sc_worker_base(per_worker)

        @pl.loop(0, per_worker // SC_CHUNK)
        def _(c):
            off = pl.multiple_of(base + c * SC_CHUNK, SC_CHUNK)
            pltpu.sync_copy(idx_hbm.at[pl.ds(off, SC_CHUNK)], idx_v)
            pltpu.async_copy(table_hbm.at[idx_v], rows_v, sem).wait()
            pltpu.sync_copy(rows_v, out_hbm.at[pl.ds(off, SC_CHUNK)])

    return gather(table, idx)


def _experts_kernel(blk_exp_ref, n_used_ref, xs_ref, wg_ref, wu_ref, wd_ref, ys_ref):
    @pl.when(pl.program_id(0) < n_used_ref[0])
    def _():
        xb = _unpack_rows(xs_ref[...]).astype(BF16)
        g = jnp.dot(xb, wg_ref[0].astype(BF16), preferred_element_type=F32)
        u = jnp.dot(xb, wu_ref[0].astype(BF16), preferred_element_type=F32)
        hb = (_silu(g) * u).astype(BF16)
        ys_ref[...] = _pack_rows(jnp.dot(hb, wd_ref[0].astype(BF16), preferred_element_type=F32))


def _experts(blk_exp, n_used, xs, w_gate, w_up, w_down):
    n_rows, w = xs.shape
    d = D_MODEL
    n_blocks = n_rows // ROW_BLOCK
    row = lambda i, be, nu: (jnp.minimum(i, nu[0] - 1), 0)
    wsel = lambda i, be, nu: (be[i], 0, 0)
    return pl.pallas_call(
        _experts_kernel,
        grid_spec=pltpu.PrefetchScalarGridSpec(
            num_scalar_prefetch=2,
            grid=(n_blocks,),
            in_specs=[pl.BlockSpec((ROW_BLOCK, w), row),
                      pl.BlockSpec((1, d, EXPERT_FF), wsel),
                      pl.BlockSpec((1, d, EXPERT_FF), wsel),
                      pl.BlockSpec((1, EXPERT_FF, d), wsel)],
            out_specs=pl.BlockSpec((ROW_BLOCK, w), row)),
        out_shape=jax.ShapeDtypeStruct((n_rows, w), jnp.int32),
        compiler_params=_cparams(("arbitrary",)),
        name="experts",
    )(blk_exp, n_used, xs, w_gate, w_up, w_down)


def _combine_kernel(g_ref, w_ref, h2_ref, x1_ref, mod_ref, wsg_ref, wsu_ref, wsd_ref, out_ref):
    hb = _unpack_rows(h2_ref[...]).astype(BF16)
    g = jnp.dot(hb, wsg_ref[...], preferred_element_type=F32)
    u = jnp.dot(hb, wsu_ref[...], preferred_element_type=F32)
    acc = jnp.dot((_silu(g) * u).astype(BF16), wsd_ref[...], preferred_element_type=F32)
    w = w_ref[...]
    for k in range(TOP_K):
        acc = acc + _unpack_rows(g_ref[k]) * w[:, k:k + 1]
    out_ref[...] = x1_ref[...] + mod_ref[0, 5:6, :] * acc


def _combine(gathered, wts_t, h2p, x1, mod, p, tt, seq):
    t, d = x1.shape
    per_seq = seq // tt
    tok = pl.BlockSpec((tt, d), lambda i: (i, 0))
    const = lambda shape: pl.BlockSpec(shape, lambda i: (0,) * len(shape))
    return pl.pallas_call(
        _combine_kernel,
        grid=(t // tt,),
        in_specs=[pl.BlockSpec((TOP_K, tt, PACK_W), lambda i: (0, i, 0)),
                  pl.BlockSpec((tt, TOPK_PAD), lambda i: (i, 0)),
                  pl.BlockSpec((tt, PACK_W), lambda i: (i, 0)),
                  tok,
                  pl.BlockSpec((1, 6, d), lambda i: (i // per_seq, 0, 0)),
                  const((d, SHARED_FF)), const((d, SHARED_FF)), const((SHARED_FF, d))],
        out_specs=tok,
        out_shape=jax.ShapeDtypeStruct((t, d), F32),
        compiler_params=_cparams(("parallel",)),
        name="combine",
    )(gathered, wts_t, h2p, x1, mod, p["ws_gate"], p["ws_up"], p["ws_down"])


def _rope_tables(s):
    half = QK_ROPE_DIM // 2
    inv_freq = ROPE_THETA ** (-jnp.arange(half, dtype=F32) / half)
    ang = jnp.arange(s, dtype=F32)[:, None] * inv_freq[None, :]
    cos, sin = jnp.cos(ang), jnp.sin(ang)
    z = lambda n: jnp.zeros((s, n), F32)
    tab_cos = jnp.concatenate([jnp.ones((s, QK_NOPE_DIM), F32), cos, cos, z(HEAD_PAD - QK_HEAD_DIM)], 1)
    tab_sa = jnp.concatenate([z(QK_NOPE_DIM), -sin, z(HEAD_PAD - QK_NOPE_DIM - half)], 1)
    tab_sb = jnp.concatenate([z(QK_NOPE_DIM + half), sin, z(HEAD_PAD - QK_HEAD_DIM)], 1)
    return dict(rope_cos=tab_cos, rope_sa=tab_sa, rope_sb=tab_sb, rope_cos_t=cos.T, rope_sin_t=sin.T)


def _prep_weights(norm1_w, w_in, pool_w, pool_scale, q_a_norm_w, w_q_b, kv_a_norm_w, w_kv_b,
                  q_norm_w, k_norm_w, w_o, norm2_w, w_router, w_gate, w_up, w_down,
                  ws_gate, ws_up, ws_down):
    d = D_MODEL
    c0 = POOL_WIDTH + Q_LORA_RANK + KV_LORA_RANK
    pad_h = HEAD_PAD - QK_HEAD_DIM
    w_in_p = jnp.concatenate(
        [w_in[:, :c0], jnp.zeros((d, QK_NOPE_DIM), F32), w_in[:, c0:], jnp.zeros((d, pad_h), F32)], 1)
    w_q = jnp.pad(w_q_b.reshape(Q_LORA_RANK, N_HEADS, QK_HEAD_DIM), ((0, 0), (0, 0), (0, pad_h)))
    kv = w_kv_b.reshape(KV_LORA_RANK, N_HEADS, QK_NOPE_DIM + V_HEAD_DIM)
    w_k = jnp.pad(kv[:, :, :QK_NOPE_DIM], ((0, 0), (0, 0), (0, HEAD_PAD - QK_NOPE_DIM)))
    w_v = kv[:, :, QK_NOPE_DIM:]
    w_r_t = w_router.T
    w_r_hi = w_r_t.astype(BF16)
    w_r_lo = (w_r_t - w_r_hi.astype(F32)).astype(BF16)
    q_gain = q_norm_w * (QK_HEAD_DIM ** -0.5 * LOG2E)
    bound = QK_HEAD_DIM * jnp.max(jnp.abs(q_gain)) * jnp.max(jnp.abs(k_norm_w)) * 1.02 + 0.25
    bound = bound.astype(QK_DTYPE).astype(F32)
    lane = jnp.arange(HEAD_PAD) == SHIFT_LANE
    return dict(
        score_bound=bound,
        k_spare=jnp.where(lane, -bound, 0.0).astype(F32).reshape(1, HEAD_PAD),
        norm1_w=norm1_w.reshape(1, d), w_in=w_in_p.astype(BF16),
        q_a_norm_w=q_a_norm_w.reshape(1, -1), w_q_t=w_q.reshape(Q_LORA_RANK, -1).T.astype(BF16),
        kv_a_norm_w=kv_a_norm_w.reshape(1, -1), w_k=w_k.reshape(KV_LORA_RANK, -1).astype(BF16),
        w_v_t=w_v.reshape(KV_LORA_RANK, -1).T.astype(BF16),
        q_gain_col=jnp.broadcast_to(jnp.pad(q_gain, (0, pad_h))[:, None], (HEAD_PAD, LANES)),
        k_norm_w=jnp.pad(k_norm_w, (0, pad_h)).reshape(1, HEAD_PAD),
        pool_w=pool_w.astype(BF16), pool_scale=pool_scale.reshape(1, -1), w_o=w_o.astype(BF16),
        norm2_w=norm2_w.reshape(1, d), w_router_hi=w_r_hi, w_router_lo=w_r_lo,
        w_gate=w_gate, w_up=w_up, w_down=w_down,
        ws_gate=ws_gate.astype(BF16), ws_up=ws_up.astype(BF16), ws_down=ws_down.astype(BF16))


def _tile(n, pref):
    return pref if n % pref == 0 else n


def _mixer_and_routing(x, mod, router_bias, p):
    b, s, d = x.shape
    t = b * s
    ts = _tile(s, 512)
    p = dict(p, **_rope_tables(s))

    u, qt, k, vt = _inproj(x, mod, p, ts)
    online = (p["score_bound"] > MAX_FIXED_SHIFT).astype(jnp.int32).reshape(1)
    attn = _attention(online, qt, k, vt, _tile(s, 1024), _tile(s, 2048), _tile(s, 512))
    x1, h2p, logits_t = _postmix(u, attn, x, mod, p, ts)
    tr = _tile(t, 512)
    idx, wts, pos, cnt = _router(logits_t, router_bias, tr)

    counts = cnt[:, 0].astype(jnp.int32)
    padded = (counts + ROW_BLOCK - 1) // ROW_BLOCK * ROW_BLOCK
    pad_end = jnp.cumsum(padded)
    dest = _dest_rows(idx, pos, pad_end - padded, tr)
    n_blocks = -(-t * TOP_K // ROW_BLOCK) + N_EXPERTS
    blk_row = jnp.arange(n_blocks, dtype=jnp.int32)[:, None] * ROW_BLOCK
    blk_exp = jnp.minimum(jnp.sum((pad_end[None, :] <= blk_row).astype(jnp.int32), axis=1),
                          N_EXPERTS - 1)
    n_used = (pad_end[-1:] // ROW_BLOCK).astype(jnp.int32)
    h2f = h2p.reshape(t, PACK_W)
    xs = _sc_scatter_rows(h2f, dest, n_blocks * ROW_BLOCK)
    return dict(xs=xs, dest=dest, blk_exp=blk_exp, n_used=n_used, wts_t=wts.T, h2f=h2f,
                x1=x1.reshape(t, d), mod=mod, shape=(b, s, d))


def _gather_and_combine(st, ys, p):
    b, s, d = st["shape"]
    t = b * s
    gathered = _sc_gather_rows(ys, st["dest"][:TOP_K].reshape(TOP_K * t)).reshape(TOP_K, t, PACK_W)
    out = _combine(gathered, st["wts_t"], st["h2f"], st["x1"], st["mod"], p, _tile(s, 512), s)
    return out.reshape(b, s, d)


def kernel(x_prompt, x_sample, c_prompt, c_sample, w_ada, b_ada, norm1_w, w_in, pool_w, pool_scale,
           q_a_norm_w, w_q_b, kv_a_norm_w, w_kv_b, q_norm_w, k_norm_w, w_o, norm2_w, w_router,
           router_bias, w_gate, w_up, w_down, ws_gate, ws_up, ws_down):
    assert w_ada.shape[0] == 1, "single-layer encoder"
    p = _prep_weights(norm1_w[0], w_in[0], pool_w[0], pool_scale[0], q_a_norm_w[0], w_q_b[0],
                      kv_a_norm_w[0], w_kv_b[0], q_norm_w[0], k_norm_w[0], w_o[0], norm2_w[0],
                      w_router[0], w_gate[0], w_up[0], w_down[0], ws_gate[0], ws_up[0], ws_down[0])
    nb = x_prompt.shape[0]
    c = jnp.concatenate([c_prompt, c_sample], axis=0).astype(F32)
    mod = _adaln(c, w_ada[0], b_ada[0]).reshape(c.shape[0], 6, D_MODEL)
    experts = lambda st, xs: _experts(st["blk_exp"], st["n_used"], xs, p["w_gate"], p["w_up"], p["w_down"])

    sp = _mixer_and_routing(x_prompt, mod[:nb], router_bias[0], p)
    dest_p, x_sample = lax.optimization_barrier((sp["dest"], x_sample))
    sp["dest"] = dest_p
    ss = _mixer_and_routing(x_sample, mod[nb:], router_bias[0], p)
    ys_p = experts(sp, sp["xs"])
    ys_p, xs_s = lax.optimization_barrier((ys_p, ss["xs"]))
    ys_s = experts(ss, xs_s)
    y_prompt = _gather_and_combine(sp, ys_p, p)
    y_sample = _gather_and_combine(ss, ys_s, p)
    return (y_prompt, y_sample)
```

```python
import functools

import jax
import jax.numpy as jnp
from jax import lax
from jax.experimental import pallas as pl
from jax.experimental.pallas import tpu as pltpu
from jax.experimental.pallas import tpu_sc as plsc

D_MODEL = 1024
POOL_WIDTH = 512
POOL_WINDOWS = (2, 4, 8, 16)
POOL_GROUP = 128
N_HEADS = 8
V_HEAD_DIM = 64
QK_NOPE_DIM = 64
QK_ROPE_DIM = 32
QK_HEAD_DIM = 96
Q_LORA_RANK = 256
KV_LORA_RANK = 128
ROPE_THETA = 10000.0
N_EXPERTS = 64
TOP_K = 6
N_GROUPS = 8
TOPK_GROUPS = 4
EXPERT_FF = 256
SHARED_FF = 256
ROUTED_SCALE = 2.5
EPS = 1e-6

LANES = 128
HEAD_PAD = 128
HALO = 16
TOPK_PAD = 8
V_AUG = 80
SHIFT_LANE = QK_HEAD_DIM
MAX_FIXED_SHIFT = 40.0
LOG2E = 1.4426950408889634
PACK_W = D_MODEL // 2
SC_CORES = 2
SC_WORKERS = 32
SC_CHUNK = 128
VMEM_LIMIT = 48 * 1024 * 1024

F32 = jnp.float32
BF16 = jnp.bfloat16
QK_DTYPE = jnp.bfloat16


def _cparams(sem):
    return pltpu.CompilerParams(dimension_semantics=sem, vmem_limit_bytes=VMEM_LIMIT)


def _silu(x):
    return x * (1.0 / (1.0 + jnp.exp(-x)))


def _adaln_kernel(c_ref, w_ref, b_ref, o_ref):
    c = c_ref[...]
    o_ref[...] = jnp.dot(_silu(c), w_ref[...], preferred_element_type=F32,
                         precision=lax.Precision.HIGHEST) + b_ref[...]


def _adaln(c, w_ada, b_ada):
    nb, d = c.shape
    n = w_ada.shape[1]
    tn = 1536
    return pl.pallas_call(
        _adaln_kernel,
        grid=(n // tn,),
        in_specs=[pl.BlockSpec((nb, d), lambda j: (0, 0)),
                  pl.BlockSpec((d, tn), lambda j: (0, j)),
                  pl.BlockSpec((1, tn), lambda j: (0, j))],
        out_specs=pl.BlockSpec((nb, tn), lambda j: (0, j)),
        out_shape=jax.ShapeDtypeStruct((nb, n), F32),
        compiler_params=_cparams(("arbitrary",)),
        name="adaln",
    )(c, w_ada, b_ada.reshape(1, n))


def _inproj_kernel(x_ref, mod_ref, n1w_ref, win_ref, qan_ref, wqt_ref, kvan_ref, wk_ref, wvt_ref,
                   qg_ref, kg_ref, ksp_ref, cos_ref, sa_ref, sb_ref, cost_ref, sint_ref,
                   u_ref, qt_ref, k_ref, vt_ref):
    ts = x_ref.shape[1]
    half = QK_ROPE_DIM // 2
    x = x_ref[0]
    shift1 = mod_ref[0, 0:1, :]
    scale1 = mod_ref[0, 1:2, :]
    r = lax.rsqrt(jnp.mean(x * x, axis=-1, keepdims=True) + EPS)
    h = x * r * n1w_ref[...] * (1.0 + scale1) + shift1
    z = jnp.dot(h.astype(BF16), win_ref[...], preferred_element_type=F32)
    u_ref[0] = z[:, :POOL_WIDTH].astype(BF16)

    cq = z[:, POOL_WIDTH:POOL_WIDTH + Q_LORA_RANK]
    cqn = cq * lax.rsqrt(jnp.mean(cq * cq, axis=-1, keepdims=True) + EPS) * qan_ref[...]
    qt = jnp.dot(wqt_ref[...], cqn.T.astype(BF16), preferred_element_type=F32)
    reps = ts // LANES
    qg = jnp.concatenate([qg_ref[...]] * reps, axis=1)
    cost = cost_ref[...]
    sint = sint_ref[...]
    spare = jnp.where(lax.broadcasted_iota(jnp.int32, (HEAD_PAD - QK_HEAD_DIM, ts), 0) == 0, 1.0, 0.0)
    for hd in range(N_HEADS):
        t = qt[hd * HEAD_PAD:(hd + 1) * HEAD_PAD]
        rn = lax.rsqrt(jnp.sum(t * t, axis=0, keepdims=True) * (1.0 / QK_HEAD_DIM) + EPS)
        tn = t * rn * qg
        t1 = tn[QK_NOPE_DIM:QK_NOPE_DIM + half]
        t2 = tn[QK_NOPE_DIM + half:QK_HEAD_DIM]
        out = jnp.concatenate([tn[:QK_NOPE_DIM], t1 * cost - t2 * sint, t1 * sint + t2 * cost, spare],
                              axis=0)
        qt_ref[0, hd * HEAD_PAD:(hd + 1) * HEAD_PAD, :] = out.astype(QK_DTYPE)

    c0 = POOL_WIDTH + Q_LORA_RANK
    ckv = z[:, c0:c0 + KV_LORA_RANK]
    ckvn = ckv * lax.rsqrt(jnp.mean(ckv * ckv, axis=-1, keepdims=True) + EPS) * kvan_ref[...]

    vt = jnp.dot(wvt_ref[...], ckvn.T.astype(BF16), preferred_element_type=F32)
    ones = jnp.ones((V_AUG - V_HEAD_DIM, ts), BF16)
    for hd in range(N_HEADS):
        vt_ref[0, hd * V_AUG:hd * V_AUG + V_HEAD_DIM, :] = (
            vt[hd * V_HEAD_DIM:(hd + 1) * V_HEAD_DIM].astype(BF16))
        vt_ref[0, hd * V_AUG + V_HEAD_DIM:(hd + 1) * V_AUG, :] = ones

    kk = jnp.dot(ckvn.astype(BF16), wk_ref[...], preferred_element_type=F32)
    kpe = z[:, c0 + KV_LORA_RANK:]
    kg = kg_ref[...]
    ksp = ksp_ref[...]
    pe_ssq = jnp.sum(kpe * kpe, axis=-1, keepdims=True)
    pg = kpe * kg
    pe_rot = (pg * cos_ref[...] + pltpu.roll(pg, HEAD_PAD - half, 1) * sa_ref[...]
              + pltpu.roll(pg, half, 1) * sb_ref[...])
    for hd in range(N_HEADS):
        t = kk[:, hd * HEAD_PAD:(hd + 1) * HEAD_PAD]
        ssq = jnp.sum(t * t, axis=-1, keepdims=True) + pe_ssq
        rn = lax.rsqrt(ssq * (1.0 / QK_HEAD_DIM) + EPS)
        k_ref[0, :, hd * HEAD_PAD:(hd + 1) * HEAD_PAD] = ((t * kg + pe_rot) * rn + ksp).astype(QK_DTYPE)


def _inproj(x, mod, p, ts):
    b, s, d = x.shape
    qk_w = N_HEADS * HEAD_PAD
    half = QK_ROPE_DIM // 2
    const = lambda shape: pl.BlockSpec(shape, lambda bi, i: (0,) * len(shape))
    row_tab = pl.BlockSpec((ts, HEAD_PAD), lambda bi, i: (i, 0))
    col_tab = pl.BlockSpec((half, ts), lambda bi, i: (0, i))
    return pl.pallas_call(
        _inproj_kernel,
        grid=(b, s // ts),
        in_specs=[pl.BlockSpec((1, ts, d), lambda bi, i: (bi, i, 0)),
                  pl.BlockSpec((1, 6, d), lambda bi, i: (bi, 0, 0)),
                  const((1, d)), const((d, d)), const((1, Q_LORA_RANK)),
                  const((qk_w, Q_LORA_RANK)), const((1, KV_LORA_RANK)),
                  const((KV_LORA_RANK, qk_w)), const((N_HEADS * V_HEAD_DIM, KV_LORA_RANK)),
                  const((HEAD_PAD, LANES)), const((1, HEAD_PAD)), const((1, HEAD_PAD)),
                  row_tab, row_tab, row_tab, col_tab, col_tab],
        out_specs=[pl.BlockSpec((1, ts, POOL_WIDTH), lambda bi, i: (bi, i, 0)),
                   pl.BlockSpec((1, qk_w, ts), lambda bi, i: (bi, 0, i)),
                   pl.BlockSpec((1, ts, qk_w), lambda bi, i: (bi, i, 0)),
                   pl.BlockSpec((1, N_HEADS * V_AUG, ts), lambda bi, i: (bi, 0, i))],
        out_shape=[jax.ShapeDtypeStruct((b, s, POOL_WIDTH), BF16),
                   jax.ShapeDtypeStruct((b, qk_w, s), QK_DTYPE),
                   jax.ShapeDtypeStruct((b, s, qk_w), QK_DTYPE),
                   jax.ShapeDtypeStruct((b, N_HEADS * V_AUG, s), BF16)],
        compiler_params=_cparams(("parallel", "parallel")),
        name="inproj",
    )(x, mod, p["norm1_w"], p["w_in"], p["q_a_norm_w"], p["w_q_t"], p["kv_a_norm_w"], p["w_k"],
      p["w_v_t"], p["q_gain_col"], p["k_norm_w"], p["k_spare"],
      p["rope_cos"], p["rope_sa"], p["rope_sb"], p["rope_cos_t"], p["rope_sin_t"])


def _attn_kernel(online_ref, qt_ref, k_ref, vt_ref, o_ref, acc_a, acc_b, m_ref, *, tk, tk_online):
    s_len = k_ref.shape[1]
    acc_a[...] = jnp.zeros_like(acc_a)
    acc_b[...] = jnp.zeros_like(acc_b)
    qa = qt_ref[0, :HEAD_PAD, :]
    qb = qt_ref[0, HEAD_PAD:, :]

    @pl.when(online_ref[0] == 0)
    def _():
        def body(c, _):
            off = pl.multiple_of(c * tk, tk)
            ks = k_ref[0, pl.ds(off, tk), :]
            vts = vt_ref[0, :, pl.ds(off, tk)]
            pa = jnp.exp2(jnp.dot(ks[:, :HEAD_PAD], qa, preferred_element_type=F32)).astype(BF16)
            pb = jnp.exp2(jnp.dot(ks[:, HEAD_PAD:], qb, preferred_element_type=F32)).astype(BF16)
            acc_a[...] += jnp.dot(vts[:V_AUG], pa, preferred_element_type=F32)
            acc_b[...] += jnp.dot(vts[V_AUG:], pb, preferred_element_type=F32)
            return 0

        lax.fori_loop(0, s_len // tk, body, 0)

    @pl.when(online_ref[0] != 0)
    def _():
        m_ref[...] = jnp.full_like(m_ref, -jnp.inf)

        def one_head(kh, q, vth, acc, row):
            s = jnp.dot(kh, q, preferred_element_type=F32)
            m_old = m_ref[row:row + 1, :]
            m_new = jnp.maximum(m_old, jnp.max(s, axis=0, keepdims=True))
            p = jnp.exp2(s - m_new).astype(BF16)
            acc[...] = acc[...] * jnp.exp2(m_old - m_new) + jnp.dot(vth, p, preferred_element_type=F32)
            m_ref[row:row + 1, :] = m_new

        def body(c, _):
            off = pl.multiple_of(c * tk_online, tk_online)
            ks = k_ref[0, pl.ds(off, tk_online), :]
            vts = vt_ref[0, :, pl.ds(off, tk_online)]
            one_head(ks[:, :HEAD_PAD], qa, vts[:V_AUG], acc_a, 0)
            one_head(ks[:, HEAD_PAD:], qb, vts[V_AUG:], acc_b, 1)
            return 0

        lax.fori_loop(0, s_len // tk_online, body, 0)

    oa = acc_a[:V_HEAD_DIM] / acc_a[V_HEAD_DIM:V_HEAD_DIM + 1]
    ob = acc_b[:V_HEAD_DIM] / acc_b[V_HEAD_DIM:V_HEAD_DIM + 1]
    o_ref[0] = jnp.concatenate([oa, ob], axis=0).T.astype(BF16)


def _attention(online, qt, k, vt, tq, tk, tk_online):
    b, _, s = qt.shape
    return pl.pallas_call(
        functools.partial(_attn_kernel, tk=tk, tk_online=tk_online),
        grid_spec=pltpu.PrefetchScalarGridSpec(
            num_scalar_prefetch=1,
            grid=(b, N_HEADS // 2, s // tq),
            in_specs=[pl.BlockSpec((1, 2 * HEAD_PAD, tq), lambda bi, j, i, on: (bi, j, i)),
                      pl.BlockSpec((1, s, 2 * HEAD_PAD), lambda bi, j, i, on: (bi, 0, j)),
                      pl.BlockSpec((1, 2 * V_AUG, s), lambda bi, j, i, on: (bi, j, 0))],
            out_specs=pl.BlockSpec((1, tq, 2 * V_HEAD_DIM), lambda bi, j, i, on: (bi, i, j)),
            scratch_shapes=[pltpu.VMEM((V_AUG, tq), F32), pltpu.VMEM((V_AUG, tq), F32),
                            pltpu.VMEM((8, tq), F32)]),
        out_shape=jax.ShapeDtypeStruct((b, s, N_HEADS * V_HEAD_DIM), BF16),
        compiler_params=_cparams(("parallel", "parallel", "arbitrary")),
        name="attn",
    )(online, qt, k, vt)


def _postmix_kernel(u_ref, up_ref, un_ref, a_ref, x_ref, mod_ref, pw_ref, ps_ref, wo_ref, n2w_ref,
                    wrh_ref, wrl_ref, x1_ref, h2_ref, lg_ref, *, seq):
    i = pl.program_id(1)
    ts = u_ref.shape[1]
    ext_rows = ts + 2 * HALO
    ext = jnp.concatenate([up_ref[0], u_ref[0], un_ref[0]], axis=0).astype(F32)
    pos = i * ts - HALO + lax.broadcasted_iota(jnp.int32, (ext_rows, 1), 0)
    ext = jnp.where((pos >= 0) & (pos < seq), ext, 0.0)
    p = i * ts + lax.broadcasted_iota(jnp.int32, (ts, 1), 0)

    outs = []
    for g, w in enumerate(POOL_WINDOWS):
        left = w // 2
        right = w - 1 - left
        t = ext[:, g * POOL_GROUP:(g + 1) * POOL_GROUP]
        step = 1
        while step < w:
            t = t + pltpu.roll(t, ext_rows - step, 0)
            step *= 2
        win = pltpu.roll(t, left, 0)[HALO:HALO + ts]
        cnt = (jnp.minimum(p + right + 1, seq) - jnp.maximum(p - left, 0)).astype(F32)
        d = win * (1.0 / cnt) - ext[HALO:HALO + ts, g * POOL_GROUP:(g + 1) * POOL_GROUP]
        outs.append(jnp.dot(d.astype(BF16), pw_ref[g], preferred_element_type=F32))
    pool = (jnp.concatenate(outs, axis=-1) * ps_ref[...]).astype(BF16)

    mix = (jnp.dot(pool, wo_ref[:POOL_WIDTH, :], preferred_element_type=F32)
           + jnp.dot(a_ref[0], wo_ref[POOL_WIDTH:, :], preferred_element_type=F32))
    x1 = x_ref[0] + mod_ref[0, 2:3, :] * mix
    x1_ref[0] = x1
    r = lax.rsqrt(jnp.mean(x1 * x1, axis=-1, keepdims=True) + EPS)
    h2 = x1 * r * n2w_ref[...] * (1.0 + mod_ref[0, 4:5, :]) + mod_ref[0, 3:4, :]
    h2_ref[0] = _pack_rows(h2)
    hi = h2.astype(BF16)
    lo = (h2 - hi.astype(F32)).astype(BF16)
    nt = (((1,), (1,)), ((), ()))
    lg_ref[...] = (lax.dot_general(wrh_ref[...], hi, nt, preferred_element_type=F32)
                   + lax.dot_general(wrl_ref[...], hi, nt, preferred_element_type=F32)
                   + lax.dot_general(wrh_ref[...], lo, nt, preferred_element_type=F32))


def _postmix(u, attn, x, mod, p, ts):
    b, s, d = x.shape
    nt = s // ts
    hb = ts // HALO
    const = lambda shape: pl.BlockSpec(shape, lambda bi, i: (0,) * len(shape))
    return pl.pallas_call(
        functools.partial(_postmix_kernel, seq=s),
        grid=(b, nt),
        in_specs=[pl.BlockSpec((1, ts, POOL_WIDTH), lambda bi, i: (bi, i, 0)),
                  pl.BlockSpec((1, HALO, POOL_WIDTH), lambda bi, i: (bi, jnp.maximum(i * hb - 1, 0), 0)),
                  pl.BlockSpec((1, HALO, POOL_WIDTH),
                               lambda bi, i: (bi, jnp.minimum((i + 1) * hb, s // HALO - 1), 0)),
                  pl.BlockSpec((1, ts, POOL_WIDTH), lambda bi, i: (bi, i, 0)),
                  pl.BlockSpec((1, ts, d), lambda bi, i: (bi, i, 0)),
                  pl.BlockSpec((1, 6, d), lambda bi, i: (bi, 0, 0)),
                  const((len(POOL_WINDOWS), POOL_GROUP, POOL_GROUP)), const((1, POOL_WIDTH)),
                  const((d, d)), const((1, d)), const((N_EXPERTS, d)), const((N_EXPERTS, d))],
        out_specs=[pl.BlockSpec((1, ts, d), lambda bi, i: (bi, i, 0)),
                   pl.BlockSpec((1, ts, PACK_W), lambda bi, i: (bi, i, 0)),
                   pl.BlockSpec((N_EXPERTS, ts), lambda bi, i: (0, bi * nt + i))],
        out_shape=[jax.ShapeDtypeStruct((b, s, d), F32),
                   jax.ShapeDtypeStruct((b, s, PACK_W), jnp.int32),
                   jax.ShapeDtypeStruct((N_EXPERTS, b * s), F32)],
        compiler_params=_cparams(("parallel", "parallel")),
        name="postmix",
    )(u, u, u, attn, x, mod, p["pool_w"], p["pool_scale"], p["w_o"], p["norm2_w"],
      p["w_router_hi"], p["w_router_lo"])


def _router_kernel(lg_ref, bias_ref, tri_ref, idx_ref, wts_ref, pos_ref, cnt_ref, carry_ref):
    @pl.when(pl.program_id(0) == 0)
    def _():
        carry_ref[...] = jnp.zeros_like(carry_ref)

    ts = lg_ref.shape[1]
    gsz = N_EXPERTS // N_GROUPS
    ninf = -jnp.inf
    scores = 1.0 / (1.0 + jnp.exp(-lg_ref[...]))
    choice = scores + bias_ref[...]
    sub = lax.broadcasted_iota(jnp.int32, (gsz, ts), 0)

    gs_rows = []
    for g in range(N_GROUPS):
        grp = choice[g * gsz:(g + 1) * gsz]
        m1 = jnp.max(grp, axis=0, keepdims=True)
        i1 = jnp.min(jnp.where(grp == m1, sub, gsz), axis=0, keepdims=True)
        m2 = jnp.max(jnp.where(sub == i1, ninf, grp), axis=0, keepdims=True)
        gs_rows.append(m1 + m2)
    gs = jnp.concatenate(gs_rows, axis=0)

    rank = jnp.zeros((N_GROUPS, ts), jnp.int32)
    for g in range(N_GROUPS):
        row = gs[g:g + 1]
        beats = (row > gs) | ((row == gs) & (sub > g))
        rank = rank + beats.astype(jnp.int32)
    gsel = rank < TOPK_GROUPS

    masked = jnp.concatenate(
        [jnp.where(gsel[g:g + 1], choice[g * gsz:(g + 1) * gsz], ninf) for g in range(N_GROUPS)],
        axis=0)
    eio = lax.broadcasted_iota(jnp.int32, (N_EXPERTS, ts), 0)
    idx_rows, w_rows, hits = [], [], []
    for _ in range(TOP_K):
        m = jnp.max(masked, axis=0, keepdims=True)
        i = jnp.min(jnp.where(masked == m, eio, N_EXPERTS), axis=0, keepdims=True)
        hit = eio == i
        w_rows.append(jnp.sum(jnp.where(hit, scores, 0.0), axis=0, keepdims=True))
        masked = jnp.where(hit, ninf, masked)
        idx_rows.append(i)
        hits.append(hit)

    wsum = functools.reduce(lambda a, c: a + c, w_rows)
    pad_i = [jnp.zeros((1, ts), jnp.int32)] * (TOPK_PAD - TOP_K)
    pad_f = [jnp.zeros((1, ts), F32)] * (TOPK_PAD - TOP_K)
    idx_ref[...] = jnp.concatenate(idx_rows + pad_i, axis=0)
    wts_ref[...] = jnp.concatenate([w / wsum * ROUTED_SCALE for w in w_rows] + pad_f, axis=0)

    sel = functools.reduce(lambda a, c: a | c, hits)
    onehot = jnp.where(sel, 1.0, 0.0).astype(BF16)
    run = jnp.dot(onehot, tri_ref[...], preferred_element_type=F32) + carry_ref[:, 0:1]
    pos_rows = [jnp.sum(jnp.where(h, run - 1.0, 0.0), axis=0, keepdims=True).astype(jnp.int32)
                for h in hits]
    pos_ref[...] = jnp.concatenate(pos_rows + pad_i, axis=0)
    total = run[:, ts - 1:ts]
    carry_ref[...] = jnp.broadcast_to(total, carry_ref.shape)
    cnt_ref[...] = jnp.broadcast_to(total, cnt_ref.shape)


def _router(logits_t, router_bias, ts):
    t = logits_t.shape[1]
    tri = jnp.triu(jnp.ones((ts, ts), BF16))
    tok = pl.BlockSpec((TOPK_PAD, ts), lambda i: (0, i))
    return pl.pallas_call(
        _router_kernel,
        grid=(t // ts,),
        in_specs=[pl.BlockSpec((N_EXPERTS, ts), lambda i: (0, i)),
                  pl.BlockSpec((N_EXPERTS, 1), lambda i: (0, 0)),
                  pl.BlockSpec((ts, ts), lambda i: (0, 0))],
        out_specs=[tok, tok, tok, pl.BlockSpec((N_EXPERTS, LANES), lambda i: (0, 0))],
        out_shape=[jax.ShapeDtypeStruct((TOPK_PAD, t), jnp.int32),
                   jax.ShapeDtypeStruct((TOPK_PAD, t), F32),
                   jax.ShapeDtypeStruct((TOPK_PAD, t), jnp.int32),
                   jax.ShapeDtypeStruct((N_EXPERTS, LANES), F32)],
        scratch_shapes=[pltpu.VMEM((N_EXPERTS, LANES), F32)],
        compiler_params=_cparams(("arbitrary",)),
        name="router",
    )(logits_t, router_bias.reshape(N_EXPERTS, 1), tri)


def _dest_kernel(idx_ref, pos_ref, start_ref, dest_ref):
    ts = idx_ref.shape[1]
    eio = lax.broadcasted_iota(jnp.int32, (N_EXPERTS, ts), 0)
    start = start_ref[...]
    rows = [pos_ref[k:k + 1, :]
            + jnp.sum(jnp.where(eio == idx_ref[k:k + 1, :], start, 0), axis=0, keepdims=True)
            for k in range(TOPK_PAD)]
    dest_ref[...] = jnp.concatenate(rows, axis=0)


def _dest_rows(idx, pos, pad_start, ts):
    t = idx.shape[1]
    tok = pl.BlockSpec((TOPK_PAD, ts), lambda i: (0, i))
    return pl.pallas_call(
        _dest_kernel,
        grid=(t // ts,),
        in_specs=[tok, tok, pl.BlockSpec((N_EXPERTS, 1), lambda i: (0, 0))],
        out_specs=tok,
        out_shape=jax.ShapeDtypeStruct((TOPK_PAD, t), jnp.int32),
        compiler_params=_cparams(("parallel",)),
        name="dest_rows",
    )(idx, pos, pad_start.reshape(N_EXPERTS, 1))


def _pack_rows(x):
    bits = lax.bitcast_convert_type(x.astype(BF16).astype(F32), jnp.int32)
    return bits[:, :PACK_W] | lax.shift_right_logical(bits[:, PACK_W:], 16)


def _unpack_rows(words):
    hi = lax.bitcast_convert_type(words & jnp.int32(-65536), F32)
    lo = lax.bitcast_convert_type(lax.shift_left(words, 16), F32)
    return jnp.concatenate([hi, lo], axis=1)


def _sc_mesh():
    return plsc.VectorSubcoreMesh(core_axis_name="core", subcore_axis_name="subcore")


def _sc_worker_base(rows_per_worker):
    return (lax.axis_index("subcore") * SC_CORES + lax.axis_index("core")) * rows_per_worker


def _sc_scatter_rows(x, dest, n_rows):
    t, w = x.shape
    per_worker = t // SC_WORKERS
    assert per_worker * SC_WORKERS == t and per_worker % SC_CHUNK == 0

    @functools.partial(
        pl.kernel, out_type=jax.ShapeDtypeStruct((n_rows, w), x.dtype), mesh=_sc_mesh(),
        scratch_types=[pltpu.VMEM((TOPK_PAD, SC_CHUNK), jnp.int32), pltpu.VMEM((SC_CHUNK, w), x.dtype),
                       pltpu.SemaphoreType.DMA])
    def scatter(x_hbm, dest_hbm, out_hbm, idx_v, rows_v, sem):
        base = _sc_worker_base(per_worker)

        @pl.loop(0, per_worker // SC_CHUNK)
        def _(c):
            off = pl.multiple_of(base + c * SC_CHUNK, SC_CHUNK)
            pltpu.sync_copy(dest_hbm.at[:, pl.ds(off, SC_CHUNK)], idx_v)
            pltpu.sync_copy(x_hbm.at[pl.ds(off, SC_CHUNK)], rows_v)
            copies = [pltpu.async_copy(rows_v, out_hbm.at[idx_v.at[k]], sem) for k in range(TOP_K)]
            for cp in copies:
                cp.wait()

    return scatter(x, dest)


def _sc_gather_rows(table, idx):
    m = idx.shape[0]
    w = table.shape[1]
    per_worker = m // SC_WORKERS
    assert per_worker * SC_WORKERS == m and per_worker % SC_CHUNK == 0

    @functools.partial(
        pl.kernel, out_type=jax.ShapeDtypeStruct((m, w), table.dtype), mesh=_sc_mesh(),
        scratch_types=[pltpu.VMEM((SC_CHUNK,), jnp.int32), pltpu.VMEM((SC_CHUNK, w), table.dtype),
                       pltpu.SemaphoreType.DMA])
    def gather(table_hbm, idx_hbm, out_hbm, idx_v, rows_v, sem):
        base = _sc_worker_base(per_worker)

        @pl.loop(0, per_worker // SC_CHUNK)
        def _(c):
            off = pl.multiple_of(base + c * SC_CHUNK, SC_CHUNK)
            pltpu.sync_copy(idx_hbm.at[pl.ds(off, SC_CHUNK)], idx_v)
            pltpu.async_copy(table_hbm.at[idx_v], rows_v, sem).wait()
            pltpu.sync_copy(rows_v, out_hbm.at[pl.ds(off, SC_CHUNK)])

    return gather(table, idx)


def _experts_kernel(blk_exp_ref, n_used_ref, xs_ref, wg_ref, wu_ref, wd_ref, ys_ref):
    @pl.when(pl.program_id(0) < n_used_ref[0])
    def _():
        xb = _unpack_rows(xs_ref[...]).astype(BF16)
        g = jnp.dot(xb, wg_ref[0].astype(BF16), preferred_element_type=F32)
        u = jnp.dot(xb, wu_ref[0].astype(BF16), preferred_element_type=F32)
        hb = (_silu(g) * u).astype(BF16)
        ys_ref[...] = _pack_rows(jnp.dot(hb, wd_ref[0].astype(BF16), preferred_element_type=F32))


def _row_block(t):
    return 512 if t * TOP_K // N_EXPERTS >= 8 * 512 else 256


def _experts(blk_exp, n_used, xs, w_gate, w_up, w_down, row_block):
    n_rows, w = xs.shape
    d = D_MODEL
    n_blocks = n_rows // row_block
    row = lambda i, be, nu: (jnp.minimum(i, nu[0] - 1), 0)
    wsel = lambda i, be, nu: (be[i], 0, 0)
    return pl.pallas_call(
        _experts_kernel,
        grid_spec=pltpu.PrefetchScalarGridSpec(
            num_scalar_prefetch=2,
            grid=(n_blocks,),
            in_specs=[pl.BlockSpec((row_block, w), row),
                      pl.BlockSpec((1, d, EXPERT_FF), wsel),
                      pl.BlockSpec((1, d, EXPERT_FF), wsel),
                      pl.BlockSpec((1, EXPERT_FF, d), wsel)],
            out_specs=pl.BlockSpec((row_block, w), row)),
        out_shape=jax.ShapeDtypeStruct((n_rows, w), jnp.int32),
        compiler_params=_cparams(("arbitrary",)),
        name="experts",
    )(blk_exp, n_used, xs, w_gate, w_up, w_down)


def _combine_kernel(g_ref, w_ref, h2_ref, x1_ref, mod_ref, wsg_ref, wsu_ref, wsd_ref, out_ref):
    hb = _unpack_rows(h2_ref[...]).astype(BF16)
    g = jnp.dot(hb, wsg_ref[...], preferred_element_type=F32)
    u = jnp.dot(hb, wsu_ref[...], preferred_element_type=F32)
    acc = jnp.dot((_silu(g) * u).astype(BF16), wsd_ref[...], preferred_element_type=F32)
    w = w_ref[...]
    for k in range(TOP_K):
        acc = acc + _unpack_rows(g_ref[k]) * w[:, k:k + 1]
    out_ref[...] = x1_ref[...] + mod_ref[0, 5:6, :] * acc


def _combine(gathered, wts_t, h2p, x1, mod, p, tt, seq):
    t, d = x1.shape
    per_seq = seq // tt
    tok = pl.BlockSpec((tt, d), lambda i: (i, 0))
    const = lambda shape: pl.BlockSpec(shape, lambda i: (0,) * len(shape))
    return pl.pallas_call(
        _combine_kernel,
        grid=(t // tt,),
        in_specs=[pl.BlockSpec((TOP_K, tt, PACK_W), lambda i: (0, i, 0)),
                  pl.BlockSpec((tt, TOPK_PAD), lambda i: (i, 0)),
                  pl.BlockSpec((tt, PACK_W), lambda i: (i, 0)),
                  tok,
                  pl.BlockSpec((1, 6, d), lambda i: (i // per_seq, 0, 0)),
                  const((d, SHARED_FF)), const((d, SHARED_FF)), const((SHARED_FF, d))],
        out_specs=tok,
        out_shape=jax.ShapeDtypeStruct((t, d), F32),
        compiler_params=_cparams(("parallel",)),
        name="combine",
    )(gathered, wts_t, h2p, x1, mod, p["ws_gate"], p["ws_up"], p["ws_down"])


def _rope_tables(s):
    half = QK_ROPE_DIM // 2
    inv_freq = ROPE_THETA ** (-jnp.arange(half, dtype=F32) / half)
    ang = jnp.arange(s, dtype=F32)[:, None] * inv_freq[None, :]
    cos, sin = jnp.cos(ang), jnp.sin(ang)
    z = lambda n: jnp.zeros((s, n), F32)
    tab_cos = jnp.concatenate([jnp.ones((s, QK_NOPE_DIM), F32), cos, cos, z(HEAD_PAD - QK_HEAD_DIM)], 1)
    tab_sa = jnp.concatenate([z(QK_NOPE_DIM), -sin, z(HEAD_PAD - QK_NOPE_DIM - half)], 1)
    tab_sb = jnp.concatenate([z(QK_NOPE_DIM + half), sin, z(HEAD_PAD - QK_HEAD_DIM)], 1)
    return dict(rope_cos=tab_cos, rope_sa=tab_sa, rope_sb=tab_sb, rope_cos_t=cos.T, rope_sin_t=sin.T)


def _prep_weights(norm1_w, w_in, pool_w, pool_scale, q_a_norm_w, w_q_b, kv_a_norm_w, w_kv_b,
                  q_norm_w, k_norm_w, w_o, norm2_w, w_router, w_gate, w_up, w_down,
                  ws_gate, ws_up, ws_down):
    d = D_MODEL
    c0 = POOL_WIDTH + Q_LORA_RANK + KV_LORA_RANK
    pad_h = HEAD_PAD - QK_HEAD_DIM
    w_in_p = jnp.concatenate(
        [w_in[:, :c0], jnp.zeros((d, QK_NOPE_DIM), F32), w_in[:, c0:], jnp.zeros((d, pad_h), F32)], 1)
    w_q = jnp.pad(w_q_b.reshape(Q_LORA_RANK, N_HEADS, QK_HEAD_DIM), ((0, 0), (0, 0), (0, pad_h)))
    kv = w_kv_b.reshape(KV_LORA_RANK, N_HEADS, QK_NOPE_DIM + V_HEAD_DIM)
    w_k = jnp.pad(kv[:, :, :QK_NOPE_DIM], ((0, 0), (0, 0), (0, HEAD_PAD - QK_NOPE_DIM)))
    w_v = kv[:, :, QK_NOPE_DIM:]
    w_r_t = w_router.T
    w_r_hi = w_r_t.astype(BF16)
    w_r_lo = (w_r_t - w_r_hi.astype(F32)).astype(BF16)
    q_gain = q_norm_w * (QK_HEAD_DIM ** -0.5 * LOG2E)
    bound = QK_HEAD_DIM * jnp.max(jnp.abs(q_gain)) * jnp.max(jnp.abs(k_norm_w)) * 1.02 + 0.25
    bound = bound.astype(QK_DTYPE).astype(F32)
    lane = jnp.arange(HEAD_PAD) == SHIFT_LANE
    return dict(
        score_bound=bound,
        k_spare=jnp.where(lane, -bound, 0.0).astype(F32).reshape(1, HEAD_PAD),
        norm1_w=norm1_w.reshape(1, d), w_in=w_in_p.astype(BF16),
        q_a_norm_w=q_a_norm_w.reshape(1, -1), w_q_t=w_q.reshape(Q_LORA_RANK, -1).T.astype(BF16),
        kv_a_norm_w=kv_a_norm_w.reshape(1, -1), w_k=w_k.reshape(KV_LORA_RANK, -1).astype(BF16),
        w_v_t=w_v.reshape(KV_LORA_RANK, -1).T.astype(BF16),
        q_gain_col=jnp.broadcast_to(jnp.pad(q_gain, (0, pad_h))[:, None], (HEAD_PAD, LANES)),
        k_norm_w=jnp.pad(k_norm_w, (0, pad_h)).reshape(1, HEAD_PAD),
        pool_w=pool_w.astype(BF16), pool_scale=pool_scale.reshape(1, -1), w_o=w_o.astype(BF16),
        norm2_w=norm2_w.reshape(1, d), w_router_hi=w_r_hi, w_router_lo=w_r_lo,
        w_gate=w_gate, w_up=w_up, w_down=w_down,
        ws_gate=ws_gate.astype(BF16), ws_up=ws_up.astype(BF16), ws_down=ws_down.astype(BF16))


def _tile(n, pref):
    return pref if n % pref == 0 else n


def _mixer_and_routing(x, mod, router_bias, p):
    b, s, d = x.shape
    t = b * s
    ts = _tile(s, 512)
    p = dict(p, **_rope_tables(s))

    u, qt, k, vt = _inproj(x, mod, p, ts)
    online = (p["score_bound"] > MAX_FIXED_SHIFT).astype(jnp.int32).reshape(1)
    attn = _attention(online, qt, k, vt, _tile(s, 1024), _tile(s, 2048), _tile(s, 512))
    x1, h2p, logits_t = _postmix(u, attn, x, mod, p, ts)
    tr = _tile(t, 512)
    idx, wts, pos, cnt = _router(logits_t, router_bias, tr)

    rb = _row_block(t)
    counts = cnt[:, 0].astype(jnp.int32)
    padded = (counts + rb - 1) // rb * rb
    pad_end = jnp.cumsum(padded)
    dest = _dest_rows(idx, pos, pad_end - padded, tr)
    n_blocks = -(-t * TOP_K // rb) + N_EXPERTS
    blk_row = jnp.arange(n_blocks, dtype=jnp.int32)[:, None] * rb
    blk_exp = jnp.minimum(jnp.sum((pad_end[None, :] <= blk_row).astype(jnp.int32), axis=1),
                          N_EXPERTS - 1)
    n_used = (pad_end[-1:] // rb).astype(jnp.int32)
    h2f = h2p.reshape(t, PACK_W)
    xs = _sc_scatter_rows(h2f, dest, n_blocks * rb)
    return dict(xs=xs, dest=dest, blk_exp=blk_exp, n_used=n_used, wts_t=wts.T, h2f=h2f,
                x1=x1.reshape(t, d), mod=mod, shape=(b, s, d), row_block=rb)


def _gather_and_combine(st, ys, p):
    b, s, d = st["shape"]
    t = b * s
    gathered = _sc_gather_rows(ys, st["dest"][:TOP_K].reshape(TOP_K * t)).reshape(TOP_K, t, PACK_W)
    out = _combine(gathered, st["wts_t"], st["h2f"], st["x1"], st["mod"], p, _tile(s, 512), s)
    return out.reshape(b, s, d)


def kernel(x_prompt, x_sample, c_prompt, c_sample, w_ada, b_ada, norm1_w, w_in, pool_w, pool_scale,
           q_a_norm_w, w_q_b, kv_a_norm_w, w_kv_b, q_norm_w, k_norm_w, w_o, norm2_w, w_router,
           router_bias, w_gate, w_up, w_down, ws_gate, ws_up, ws_down):
    assert w_ada.shape[0] == 1, "single-layer encoder"
    p = _prep_weights(norm1_w[0], w_in[0], pool_w[0], pool_scale[0], q_a_norm_w[0], w_q_b[0],
                      kv_a_norm_w[0], w_kv_b[0], q_norm_w[0], k_norm_w[0], w_o[0], norm2_w[0],
                      w_router[0], w_gate[0], w_up[0], w_down[0], ws_gate[0], ws_up[0], ws_down[0])
    nb = x_prompt.shape[0]
    c = jnp.concatenate([c_prompt, c_sample], axis=0).astype(F32)
    mod = _adaln(c, w_ada[0], b_ada[0]).reshape(c.shape[0], 6, D_MODEL)
    experts = lambda st, xs: _experts(st["blk_exp"], st["n_used"], xs, p["w_gate"], p["w_up"],
                                      p["w_down"], st["row_block"])

    sp = _mixer_and_routing(x_prompt, mod[:nb], router_bias[0], p)
    dest_p, x_sample = lax.optimization_barrier((sp["dest"], x_sample))
    sp["dest"] = dest_p
    ss = _mixer_and_routing(x_sample, mod[nb:], router_bias[0], p)
    ys_p = experts(sp, sp["xs"])
    ys_p, xs_s = lax.optimization_barrier((ys_p, ss["xs"]))
    ys_s = experts(ss, xs_s)
    y_prompt = _gather_and_combine(sp, ys_p, p)
    y_sample = _gather_and_combine(ss, ys_s, p)
    return (y_prompt, y_sample)
```

```python
import functools

import jax
import jax.numpy as jnp
from jax import lax
from jax.experimental import pallas as pl
from jax.experimental.pallas import tpu as pltpu
from jax.experimental.pallas import tpu_sc as plsc

D_MODEL = 1024
POOL_WIDTH = 512
POOL_WINDOWS = (2, 4, 8, 16)
POOL_GROUP = 128
N_HEADS = 8
V_HEAD_DIM = 64
QK_NOPE_DIM = 64
QK_ROPE_DIM = 32
QK_HEAD_DIM = 96
Q_LORA_RANK = 256
KV_LORA_RANK = 128
ROPE_THETA = 10000.0
N_EXPERTS = 64
TOP_K = 6
N_GROUPS = 8
TOPK_GROUPS = 4
EXPERT_FF = 256
SHARED_FF = 256
ROUTED_SCALE = 2.5
EPS = 1e-6

LANES = 128
HEAD_PAD = 128
HALO = 16
TOPK_PAD = 8
V_AUG = 80
SHIFT_LANE = QK_HEAD_DIM
MAX_FIXED_SHIFT = 40.0
LOG2E = 1.4426950408889634
PACK_W = D_MODEL // 2
SC_CORES = 2
SC_WORKERS = 32
SC_CHUNK = 128
VMEM_LIMIT = 48 * 1024 * 1024

F32 = jnp.float32
BF16 = jnp.bfloat16
QK_DTYPE = jnp.bfloat16


def _cparams(sem):
    return pltpu.CompilerParams(dimension_semantics=sem, vmem_limit_bytes=VMEM_LIMIT)


def _silu(x):
    return x * (1.0 / (1.0 + jnp.exp(-x)))


def _adaln_kernel(c_ref, w_ref, b_ref, o_ref):
    c = c_ref[...]
    o_ref[...] = jnp.dot(_silu(c), w_ref[...], preferred_element_type=F32,
                         precision=lax.Precision.HIGHEST) + b_ref[...]


def _adaln(c, w_ada, b_ada):
    nb, d = c.shape
    n = w_ada.shape[1]
    tn = 1536
    return pl.pallas_call(
        _adaln_kernel,
        grid=(n // tn,),
        in_specs=[pl.BlockSpec((nb, d), lambda j: (0, 0)),
                  pl.BlockSpec((d, tn), lambda j: (0, j)),
                  pl.BlockSpec((1, tn), lambda j: (0, j))],
        out_specs=pl.BlockSpec((nb, tn), lambda j: (0, j)),
        out_shape=jax.ShapeDtypeStruct((nb, n), F32),
        compiler_params=_cparams(("arbitrary",)),
        name="adaln",
    )(c, w_ada, b_ada.reshape(1, n))


def _inproj_kernel(x_ref, mod_ref, n1w_ref, win_ref, qan_ref, wqt_ref, kvan_ref, wk_ref, wvt_ref,
                   qg_ref, kg_ref, ksp_ref, cos_ref, sa_ref, sb_ref, cost_ref, sint_ref,
                   u_ref, qt_ref, k_ref, vt_ref):
    ts = x_ref.shape[1]
    half = QK_ROPE_DIM // 2
    x = x_ref[0]
    shift1 = mod_ref[0, 0:1, :]
    scale1 = mod_ref[0, 1:2, :]
    r = lax.rsqrt(jnp.mean(x * x, axis=-1, keepdims=True) + EPS)
    h = x * r * n1w_ref[...] * (1.0 + scale1) + shift1
    z = jnp.dot(h.astype(BF16), win_ref[...], preferred_element_type=F32)
    u_ref[0] = z[:, :POOL_WIDTH].astype(BF16)

    cq = z[:, POOL_WIDTH:POOL_WIDTH + Q_LORA_RANK]
    cqn = cq * lax.rsqrt(jnp.mean(cq * cq, axis=-1, keepdims=True) + EPS) * qan_ref[...]
    qt = jnp.dot(wqt_ref[...], cqn.T.astype(BF16), preferred_element_type=F32)
    reps = ts // LANES
    qg = jnp.concatenate([qg_ref[...]] * reps, axis=1)
    cost = cost_ref[...]
    sint = sint_ref[...]
    spare = jnp.where(lax.broadcasted_iota(jnp.int32, (HEAD_PAD - QK_HEAD_DIM, ts), 0) == 0, 1.0, 0.0)
    for hd in range(N_HEADS):
        t = qt[hd * HEAD_PAD:(hd + 1) * HEAD_PAD]
        rn = lax.rsqrt(jnp.sum(t * t, axis=0, keepdims=True) * (1.0 / QK_HEAD_DIM) + EPS)
        tn = t * rn * qg
        t1 = tn[QK_NOPE_DIM:QK_NOPE_DIM + half]
        t2 = tn[QK_NOPE_DIM + half:QK_HEAD_DIM]
        out = jnp.concatenate([tn[:QK_NOPE_DIM], t1 * cost - t2 * sint, t1 * sint + t2 * cost, spare],
                              axis=0)
        qt_ref[0, hd * HEAD_PAD:(hd + 1) * HEAD_PAD, :] = out.astype(QK_DTYPE)

    c0 = POOL_WIDTH + Q_LORA_RANK
    ckv = z[:, c0:c0 + KV_LORA_RANK]
    ckvn = ckv * lax.rsqrt(jnp.mean(ckv * ckv, axis=-1, keepdims=True) + EPS) * kvan_ref[...]

    vt = jnp.dot(wvt_ref[...], ckvn.T.astype(BF16), preferred_element_type=F32)
    ones = jnp.ones((V_AUG - V_HEAD_DIM, ts), BF16)
    for hd in range(N_HEADS):
        vt_ref[0, hd * V_AUG:hd * V_AUG + V_HEAD_DIM, :] = (
            vt[hd * V_HEAD_DIM:(hd + 1) * V_HEAD_DIM].astype(BF16))
        vt_ref[0, hd * V_AUG + V_HEAD_DIM:(hd + 1) * V_AUG, :] = ones

    kk = jnp.dot(ckvn.astype(BF16), wk_ref[...], preferred_element_type=F32)
    kpe = z[:, c0 + KV_LORA_RANK:]
    kg = kg_ref[...]
    ksp = ksp_ref[...]
    pe_ssq = jnp.sum(kpe * kpe, axis=-1, keepdims=True)
    pg = kpe * kg
    pe_rot = (pg * cos_ref[...] + pltpu.roll(pg, HEAD_PAD - half, 1) * sa_ref[...]
              + pltpu.roll(pg, half, 1) * sb_ref[...])
    for hd in range(N_HEADS):
        t = kk[:, hd * HEAD_PAD:(hd + 1) * HEAD_PAD]
        ssq = jnp.sum(t * t, axis=-1, keepdims=True) + pe_ssq
        rn = lax.rsqrt(ssq * (1.0 / QK_HEAD_DIM) + EPS)
        k_ref[0, :, hd * HEAD_PAD:(hd + 1) * HEAD_PAD] = ((t * kg + pe_rot) * rn + ksp).astype(QK_DTYPE)


def _inproj(x, mod, p, ts):
    b, s, d = x.shape
    qk_w = N_HEADS * HEAD_PAD
    half = QK_ROPE_DIM // 2
    const = lambda shape: pl.BlockSpec(shape, lambda bi, i: (0,) * len(shape))
    row_tab = pl.BlockSpec((ts, HEAD_PAD), lambda bi, i: (i, 0))
    col_tab = pl.BlockSpec((half, ts), lambda bi, i: (0, i))
    return pl.pallas_call(
        _inproj_kernel,
        grid=(b, s // ts),
        in_specs=[pl.BlockSpec((1, ts, d), lambda bi, i: (bi, i, 0)),
                  pl.BlockSpec((1, 6, d), lambda bi, i: (bi, 0, 0)),
                  const((1, d)), const((d, d)), const((1, Q_LORA_RANK)),
                  const((qk_w, Q_LORA_RANK)), const((1, KV_LORA_RANK)),
                  const((KV_LORA_RANK, qk_w)), const((N_HEADS * V_HEAD_DIM, KV_LORA_RANK)),
                  const((HEAD_PAD, LANES)), const((1, HEAD_PAD)), const((1, HEAD_PAD)),
                  row_tab, row_tab, row_tab, col_tab, col_tab],
        out_specs=[pl.BlockSpec((1, ts, POOL_WIDTH), lambda bi, i: (bi, i, 0)),
                   pl.BlockSpec((1, qk_w, ts), lambda bi, i: (bi, 0, i)),
                   pl.BlockSpec((1, ts, qk_w), lambda bi, i: (bi, i, 0)),
                   pl.BlockSpec((1, N_HEADS * V_AUG, ts), lambda bi, i: (bi, 0, i))],
        out_shape=[jax.ShapeDtypeStruct((b, s, POOL_WIDTH), BF16),
                   jax.ShapeDtypeStruct((b, qk_w, s), QK_DTYPE),
                   jax.ShapeDtypeStruct((b, s, qk_w), QK_DTYPE),
                   jax.ShapeDtypeStruct((b, N_HEADS * V_AUG, s), BF16)],
        compiler_params=_cparams(("parallel", "parallel")),
        name="inproj",
    )(x, mod, p["norm1_w"], p["w_in"], p["q_a_norm_w"], p["w_q_t"], p["kv_a_norm_w"], p["w_k"],
      p["w_v_t"], p["q_gain_col"], p["k_norm_w"], p["k_spare"],
      p["rope_cos"], p["rope_sa"], p["rope_sb"], p["rope_cos_t"], p["rope_sin_t"])


def _attn_kernel(online_ref, qt_ref, k_ref, vt_ref, o_ref, acc_a, acc_b, m_ref, *, tk, tk_online):
    s_len = k_ref.shape[1]
    acc_a[...] = jnp.zeros_like(acc_a)
    acc_b[...] = jnp.zeros_like(acc_b)
    qa = qt_ref[0, :HEAD_PAD, :]
    qb = qt_ref[0, HEAD_PAD:, :]

    @pl.when(online_ref[0] == 0)
    def _():
        def body(c, _):
            off = pl.multiple_of(c * tk, tk)
            ks = k_ref[0, pl.ds(off, tk), :]
            vts = vt_ref[0, :, pl.ds(off, tk)]
            pa = jnp.exp2(jnp.dot(ks[:, :HEAD_PAD], qa, preferred_element_type=F32)).astype(BF16)
            pb = jnp.exp2(jnp.dot(ks[:, HEAD_PAD:], qb, preferred_element_type=F32)).astype(BF16)
            acc_a[...] += jnp.dot(vts[:V_AUG], pa, preferred_element_type=F32)
            acc_b[...] += jnp.dot(vts[V_AUG:], pb, preferred_element_type=F32)
            return 0

        lax.fori_loop(0, s_len // tk, body, 0)

    @pl.when(online_ref[0] != 0)
    def _():
        m_ref[...] = jnp.full_like(m_ref, -jnp.inf)

        def one_head(kh, q, vth, acc, row):
            s = jnp.dot(kh, q, preferred_element_type=F32)
            m_old = m_ref[row:row + 1, :]
            m_new = jnp.maximum(m_old, jnp.max(s, axis=0, keepdims=True))
            p = jnp.exp2(s - m_new).astype(BF16)
            acc[...] = acc[...] * jnp.exp2(m_old - m_new) + jnp.dot(vth, p, preferred_element_type=F32)
            m_ref[row:row + 1, :] = m_new

        def body(c, _):
            off = pl.multiple_of(c * tk_online, tk_online)
            ks = k_ref[0, pl.ds(off, tk_online), :]
            vts = vt_ref[0, :, pl.ds(off, tk_online)]
            one_head(ks[:, :HEAD_PAD], qa, vts[:V_AUG], acc_a, 0)
            one_head(ks[:, HEAD_PAD:], qb, vts[V_AUG:], acc_b, 1)
            return 0

        lax.fori_loop(0, s_len // tk_online, body, 0)

    oa = acc_a[:V_HEAD_DIM] / acc_a[V_HEAD_DIM:V_HEAD_DIM + 1]
    ob = acc_b[:V_HEAD_DIM] / acc_b[V_HEAD_DIM:V_HEAD_DIM + 1]
    o_ref[0] = jnp.concatenate([oa, ob], axis=0).T.astype(BF16)


def _attention(online, qt, k, vt, tq, tk, tk_online):
    b, _, s = qt.shape
    return pl.pallas_call(
        functools.partial(_attn_kernel, tk=tk, tk_online=tk_online),
        grid_spec=pltpu.PrefetchScalarGridSpec(
            num_scalar_prefetch=1,
            grid=(b, N_HEADS // 2, s // tq),
            in_specs=[pl.BlockSpec((1, 2 * HEAD_PAD, tq), lambda bi, j, i, on: (bi, j, i)),
                      pl.BlockSpec((1, s, 2 * HEAD_PAD), lambda bi, j, i, on: (bi, 0, j)),
                      pl.BlockSpec((1, 2 * V_AUG, s), lambda bi, j, i, on: (bi, j, 0))],
            out_specs=pl.BlockSpec((1, tq, 2 * V_HEAD_DIM), lambda bi, j, i, on: (bi, i, j)),
            scratch_shapes=[pltpu.VMEM((V_AUG, tq), F32), pltpu.VMEM((V_AUG, tq), F32),
                            pltpu.VMEM((8, tq), F32)]),
        out_shape=jax.ShapeDtypeStruct((b, s, N_HEADS * V_HEAD_DIM), BF16),
        compiler_params=_cparams(("parallel", "parallel", "arbitrary")),
        name="attn",
    )(online, qt, k, vt)


def _postmix_kernel(u_ref, up_ref, un_ref, a_ref, x_ref, mod_ref, pw_ref, ps_ref, wo_ref, n2w_ref,
                    wrh_ref, wrl_ref, x1_ref, h2_ref, lg_ref, *, seq):
    i = pl.program_id(1)
    ts = u_ref.shape[1]
    ext_rows = ts + 2 * HALO
    ext = jnp.concatenate([up_ref[0], u_ref[0], un_ref[0]], axis=0).astype(F32)
    pos = i * ts - HALO + lax.broadcasted_iota(jnp.int32, (ext_rows, 1), 0)
    ext = jnp.where((pos >= 0) & (pos < seq), ext, 0.0)
    p = i * ts + lax.broadcasted_iota(jnp.int32, (ts, 1), 0)

    outs = []
    for g, w in enumerate(POOL_WINDOWS):
        left = w // 2
        right = w - 1 - left
        t = ext[:, g * POOL_GROUP:(g + 1) * POOL_GROUP]
        step = 1
        while step < w:
            t = t + pltpu.roll(t, ext_rows - step, 0)
            step *= 2
        win = pltpu.roll(t, left, 0)[HALO:HALO + ts]
        cnt = (jnp.minimum(p + right + 1, seq) - jnp.maximum(p - left, 0)).astype(F32)
        d = win * (1.0 / cnt) - ext[HALO:HALO + ts, g * POOL_GROUP:(g + 1) * POOL_GROUP]
        outs.append(jnp.dot(d.astype(BF16), pw_ref[g], preferred_element_type=F32))
    pool = (jnp.concatenate(outs, axis=-1) * ps_ref[...]).astype(BF16)

    mix = (jnp.dot(pool, wo_ref[:POOL_WIDTH, :], preferred_element_type=F32)
           + jnp.dot(a_ref[0], wo_ref[POOL_WIDTH:, :], preferred_element_type=F32))
    x1 = x_ref[0] + mod_ref[0, 2:3, :] * mix
    x1_ref[0] = x1
    r = lax.rsqrt(jnp.mean(x1 * x1, axis=-1, keepdims=True) + EPS)
    h2 = x1 * r * n2w_ref[...] * (1.0 + mod_ref[0, 4:5, :]) + mod_ref[0, 3:4, :]
    h2_ref[0] = _pack_rows(h2)
    hi = h2.astype(BF16)
    lo = (h2 - hi.astype(F32)).astype(BF16)
    nt = (((1,), (1,)), ((), ()))
    lg_ref[...] = (lax.dot_general(wrh_ref[...], hi, nt, preferred_element_type=F32)
                   + lax.dot_general(wrl_ref[...], hi, nt, preferred_element_type=F32)
                   + lax.dot_general(wrh_ref[...], lo, nt, preferred_element_type=F32))


def _postmix(u, attn, x, mod, p, ts):
    b, s, d = x.shape
    nt = s // ts
    hb = ts // HALO
    const = lambda shape: pl.BlockSpec(shape, lambda bi, i: (0,) * len(shape))
    return pl.pallas_call(
        functools.partial(_postmix_kernel, seq=s),
        grid=(b, nt),
        in_specs=[pl.BlockSpec((1, ts, POOL_WIDTH), lambda bi, i: (bi, i, 0)),
                  pl.BlockSpec((1, HALO, POOL_WIDTH), lambda bi, i: (bi, jnp.maximum(i * hb - 1, 0), 0)),
                  pl.BlockSpec((1, HALO, POOL_WIDTH),
                               lambda bi, i: (bi, jnp.minimum((i + 1) * hb, s // HALO - 1), 0)),
                  pl.BlockSpec((1, ts, POOL_WIDTH), lambda bi, i: (bi, i, 0)),
                  pl.BlockSpec((1, ts, d), lambda bi, i: (bi, i, 0)),
                  pl.BlockSpec((1, 6, d), lambda bi, i: (bi, 0, 0)),
                  const((len(POOL_WINDOWS), POOL_GROUP, POOL_GROUP)), const((1, POOL_WIDTH)),
                  const((d, d)), const((1, d)), const((N_EXPERTS, d)), const((N_EXPERTS, d))],
        out_specs=[pl.BlockSpec((1, ts, d), lambda bi, i: (bi, i, 0)),
                   pl.BlockSpec((1, ts, PACK_W), lambda bi, i: (bi, i, 0)),
                   pl.BlockSpec((N_EXPERTS, ts), lambda bi, i: (0, bi * nt + i))],
        out_shape=[jax.ShapeDtypeStruct((b, s, d), F32),
                   jax.ShapeDtypeStruct((b, s, PACK_W), jnp.int32),
                   jax.ShapeDtypeStruct((N_EXPERTS, b * s), F32)],
        compiler_params=_cparams(("parallel", "parallel")),
        name="postmix",
    )(u, u, u, attn, x, mod, p["pool_w"], p["pool_scale"], p["w_o"], p["norm2_w"],
      p["w_router_hi"], p["w_router_lo"])


def _router_kernel(lg_ref, bias_ref, tri_ref, idx_ref, wts_ref, pos_ref, cnt_ref, carry_ref):
    @pl.when(pl.program_id(0) == 0)
    def _():
        carry_ref[...] = jnp.zeros_like(carry_ref)

    ts = lg_ref.shape[1]
    gsz = N_EXPERTS // N_GROUPS
    ninf = -jnp.inf
    scores = 1.0 / (1.0 + jnp.exp(-lg_ref[...]))
    choice = scores + bias_ref[...]
    sub = lax.broadcasted_iota(jnp.int32, (gsz, ts), 0)

    gs_rows = []
    for g in range(N_GROUPS):
        grp = choice[g * gsz:(g + 1) * gsz]
        m1 = jnp.max(grp, axis=0, keepdims=True)
        i1 = jnp.min(jnp.where(grp == m1, sub, gsz), axis=0, keepdims=True)
        m2 = jnp.max(jnp.where(sub == i1, ninf, grp), axis=0, keepdims=True)
        gs_rows.append(m1 + m2)
    gs = jnp.concatenate(gs_rows, axis=0)

    rank = jnp.zeros((N_GROUPS, ts), jnp.int32)
    for g in range(N_GROUPS):
        row = gs[g:g + 1]
        beats = (row > gs) | ((row == gs) & (sub > g))
        rank = rank + beats.astype(jnp.int32)
    gsel = rank < TOPK_GROUPS

    masked = jnp.concatenate(
        [jnp.where(gsel[g:g + 1], choice[g * gsz:(g + 1) * gsz], ninf) for g in range(N_GROUPS)],
        axis=0)
    eio = lax.broadcasted_iota(jnp.int32, (N_EXPERTS, ts), 0)
    idx_rows, w_rows, hits = [], [], []
    for _ in range(TOP_K):
        m = jnp.max(masked, axis=0, keepdims=True)
        i = jnp.min(jnp.where(masked == m, eio, N_EXPERTS), axis=0, keepdims=True)
        hit = eio == i
        w_rows.append(jnp.sum(jnp.where(hit, scores, 0.0), axis=0, keepdims=True))
        masked = jnp.where(hit, ninf, masked)
        idx_rows.append(i)
        hits.append(hit)

    wsum = functools.reduce(lambda a, c: a + c, w_rows)
    pad_i = [jnp.zeros((1, ts), jnp.int32)] * (TOPK_PAD - TOP_K)
    pad_f = [jnp.zeros((1, ts), F32)] * (TOPK_PAD - TOP_K)
    idx_ref[...] = jnp.concatenate(idx_rows + pad_i, axis=0)
    wts_ref[...] = jnp.concatenate([w / wsum * ROUTED_SCALE for w in w_rows] + pad_f, axis=0)

    sel = functools.reduce(lambda a, c: a | c, hits)
    onehot = jnp.where(sel, 1.0, 0.0).astype(BF16)
    run = jnp.dot(onehot, tri_ref[...], preferred_element_type=F32) + carry_ref[:, 0:1]
    pos_rows = [jnp.sum(jnp.where(h, run - 1.0, 0.0), axis=0, keepdims=True).astype(jnp.int32)
                for h in hits]
    pos_ref[...] = jnp.concatenate(pos_rows + pad_i, axis=0)
    total = run[:, ts - 1:ts]
    carry_ref[...] = jnp.broadcast_to(total, carry_ref.shape)
    cnt_ref[...] = jnp.broadcast_to(total, cnt_ref.shape)


def _router(logits_t, router_bias, ts):
    t = logits_t.shape[1]
    tri = jnp.triu(jnp.ones((ts, ts), BF16))
    tok = pl.BlockSpec((TOPK_PAD, ts), lambda i: (0, i))
    return pl.pallas_call(
        _router_kernel,
        grid=(t // ts,),
        in_specs=[pl.BlockSpec((N_EXPERTS, ts), lambda i: (0, i)),
                  pl.BlockSpec((N_EXPERTS, 1), lambda i: (0, 0)),
                  pl.BlockSpec((ts, ts), lambda i: (0, 0))],
        out_specs=[tok, tok, tok, pl.BlockSpec((N_EXPERTS, LANES), lambda i: (0, 0))],
        out_shape=[jax.ShapeDtypeStruct((TOPK_PAD, t), jnp.int32),
                   jax.ShapeDtypeStruct((TOPK_PAD, t), F32),
                   jax.ShapeDtypeStruct((TOPK_PAD, t), jnp.int32),
                   jax.ShapeDtypeStruct((N_EXPERTS, LANES), F32)],
        scratch_shapes=[pltpu.VMEM((N_EXPERTS, LANES), F32)],
        compiler_params=_cparams(("arbitrary",)),
        name="router",
    )(logits_t, router_bias.reshape(N_EXPERTS, 1), tri)


def _dest_kernel(idx_ref, pos_ref, start_ref, dest_ref):
    ts = idx_ref.shape[1]
    eio = lax.broadcasted_iota(jnp.int32, (N_EXPERTS, ts), 0)
    start = start_ref[...]
    rows = [pos_ref[k:k + 1, :]
            + jnp.sum(jnp.where(eio == idx_ref[k:k + 1, :], start, 0), axis=0, keepdims=True)
            for k in range(TOPK_PAD)]
    dest_ref[...] = jnp.concatenate(rows, axis=0)


def _dest_rows(idx, pos, pad_start, ts):
    t = idx.shape[1]
    tok = pl.BlockSpec((TOPK_PAD, ts), lambda i: (0, i))
    return pl.pallas_call(
        _dest_kernel,
        grid=(t // ts,),
        in_specs=[tok, tok, pl.BlockSpec((N_EXPERTS, 1), lambda i: (0, 0))],
        out_specs=tok,
        out_shape=jax.ShapeDtypeStruct((TOPK_PAD, t), jnp.int32),
        compiler_params=_cparams(("parallel",)),
        name="dest_rows",
    )(idx, pos, pad_start.reshape(N_EXPERTS, 1))


def _pack_rows(x):
    bits = lax.bitcast_convert_type(x.astype(BF16).astype(F32), jnp.int32)
    return bits[:, :PACK_W] | lax.shift_right_logical(bits[:, PACK_W:], 16)


def _unpack_rows(words):
    hi = lax.bitcast_convert_type(words & jnp.int32(-65536), F32)
    lo = lax.bitcast_convert_type(lax.shift_left(words, 16), F32)
    return jnp.concatenate([hi, lo], axis=1)


def _sc_mesh():
    return plsc.VectorSubcoreMesh(core_axis_name="core", subcore_axis_name="subcore")


def _sc_worker_base(rows_per_worker):
    return (lax.axis_index("subcore") * SC_CORES + lax.axis_index("core")) * rows_per_worker


def _sc_scatter_rows(x, dest, n_rows):
    t, w = x.shape
    per_worker = t // SC_WORKERS
    assert per_worker * SC_WORKERS == t and per_worker % SC_CHUNK == 0

    @functools.partial(
        pl.kernel, out_type=jax.ShapeDtypeStruct((n_rows, w), x.dtype), mesh=_sc_mesh(),
        scratch_types=[pltpu.VMEM((TOPK_PAD, SC_CHUNK), jnp.int32), pltpu.VMEM((SC_CHUNK, w), x.dtype),
                       pltpu.SemaphoreType.DMA])
    def scatter(x_hbm, dest_hbm, out_hbm, idx_v, rows_v, sem):
        base = _sc_worker_base(per_worker)

        @pl.loop(0, per_worker // SC_CHUNK)
        def _(c):
            off = pl.multiple_of(base + c * SC_CHUNK, SC_CHUNK)
            pltpu.sync_copy(dest_hbm.at[:, pl.ds(off, SC_CHUNK)], idx_v)
            pltpu.sync_copy(x_hbm.at[pl.ds(off, SC_CHUNK)], rows_v)
            copies = [pltpu.async_copy(rows_v, out_hbm.at[idx_v.at[k]], sem) for k in range(TOP_K)]
            for cp in copies:
                cp.wait()

    return scatter(x, dest)


def _sc_gather_rows(table, idx):
    m = idx.shape[0]
    w = table.shape[1]
    per_worker = m // SC_WORKERS
    assert per_worker * SC_WORKERS == m and per_worker % SC_CHUNK == 0

    @functools.partial(
        pl.kernel, out_type=jax.ShapeDtypeStruct((m, w), table.dtype), mesh=_sc_mesh(),
        scratch_types=[pltpu.VMEM((SC_CHUNK,), jnp.int32), pltpu.VMEM((SC_CHUNK, w), table.dtype),
                       pltpu.SemaphoreType.DMA])
    def gather(table_hbm, idx_hbm, out_hbm, idx_v, rows_v, sem):
        base = _sc_worker_base(per_worker)

        @pl.loop(0, per_worker // SC_CHUNK)
        def _(c):
            off = pl.multiple_of(base + c * SC_CHUNK, SC_CHUNK)
            pltpu.sync_copy(idx_hbm.at[pl.ds(off, SC_CHUNK)], idx_v)
            pltpu.async_copy(table_hbm.at[idx_v], rows_v, sem).wait()
            pltpu.sync_copy(rows_v, out_hbm.at[pl.ds(off, SC_CHUNK)])

    return gather(table, idx)


def _experts_kernel(blk_exp_ref, n_used_ref, xs_ref, wg_ref, wu_ref, wd_ref, ys_ref):
    @pl.when(pl.program_id(0) < n_used_ref[0])
    def _():
        xb = _unpack_rows(xs_ref[...]).astype(BF16)
        g = jnp.dot(xb, wg_ref[0].astype(BF16), preferred_element_type=F32)
        u = jnp.dot(xb, wu_ref[0].astype(BF16), preferred_element_type=F32)
        hb = (_silu(g) * u).astype(BF16)
        ys_ref[...] = _pack_rows(jnp.dot(hb, wd_ref[0].astype(BF16), preferred_element_type=F32))


def _row_block(t):
    per_expert = t * TOP_K // N_EXPERTS
    return 1024 if per_expert >= 4 * 1024 else (512 if per_expert >= 4 * 512 else 256)


def _experts(blk_exp, n_used, xs, w_gate, w_up, w_down, row_block):
    n_rows, w = xs.shape
    d = D_MODEL
    n_blocks = n_rows // row_block
    row = lambda i, be, nu: (jnp.minimum(i, nu[0] - 1), 0)
    wsel = lambda i, be, nu: (be[i], 0, 0)
    return pl.pallas_call(
        _experts_kernel,
        grid_spec=pltpu.PrefetchScalarGridSpec(
            num_scalar_prefetch=2,
            grid=(n_blocks,),
            in_specs=[pl.BlockSpec((row_block, w), row),
                      pl.BlockSpec((1, d, EXPERT_FF), wsel),
                      pl.BlockSpec((1, d, EXPERT_FF), wsel),
                      pl.BlockSpec((1, EXPERT_FF, d), wsel)],
            out_specs=pl.BlockSpec((row_block, w), row)),
        out_shape=jax.ShapeDtypeStruct((n_rows, w), jnp.int32),
        compiler_params=_cparams(("arbitrary",)),
        name="experts",
    )(blk_exp, n_used, xs, w_gate, w_up, w_down)


def _combine_kernel(g_ref, w_ref, h2_ref, x1_ref, mod_ref, wsg_ref, wsu_ref, wsd_ref, out_ref):
    hb = _unpack_rows(h2_ref[...]).astype(BF16)
    g = jnp.dot(hb, wsg_ref[...], preferred_element_type=F32)
    u = jnp.dot(hb, wsu_ref[...], preferred_element_type=F32)
    acc = jnp.dot((_silu(g) * u).astype(BF16), wsd_ref[...], preferred_element_type=F32)
    w = w_ref[...]
    for k in range(TOP_K):
        acc = acc + _unpack_rows(g_ref[k]) * w[:, k:k + 1]
    out_ref[...] = x1_ref[...] + mod_ref[0, 5:6, :] * acc


def _combine(gathered, wts_t, h2p, x1, mod, p, tt, seq):
    t, d = x1.shape
    per_seq = seq // tt
    tok = pl.BlockSpec((tt, d), lambda i: (i, 0))
    const = lambda shape: pl.BlockSpec(shape, lambda i: (0,) * len(shape))
    return pl.pallas_call(
        _combine_kernel,
        grid=(t // tt,),
        in_specs=[pl.BlockSpec((TOP_K, tt, PACK_W), lambda i: (0, i, 0)),
                  pl.BlockSpec((tt, TOPK_PAD), lambda i: (i, 0)),
                  pl.BlockSpec((tt, PACK_W), lambda i: (i, 0)),
                  tok,
                  pl.BlockSpec((1, 6, d), lambda i: (i // per_seq, 0, 0)),
                  const((d, SHARED_FF)), const((d, SHARED_FF)), const((SHARED_FF, d))],
        out_specs=tok,
        out_shape=jax.ShapeDtypeStruct((t, d), F32),
        compiler_params=_cparams(("parallel",)),
        name="combine",
    )(gathered, wts_t, h2p, x1, mod, p["ws_gate"], p["ws_up"], p["ws_down"])


def _rope_tables(s):
    half = QK_ROPE_DIM // 2
    inv_freq = ROPE_THETA ** (-jnp.arange(half, dtype=F32) / half)
    ang = jnp.arange(s, dtype=F32)[:, None] * inv_freq[None, :]
    cos, sin = jnp.cos(ang), jnp.sin(ang)
    z = lambda n: jnp.zeros((s, n), F32)
    tab_cos = jnp.concatenate([jnp.ones((s, QK_NOPE_DIM), F32), cos, cos, z(HEAD_PAD - QK_HEAD_DIM)], 1)
    tab_sa = jnp.concatenate([z(QK_NOPE_DIM), -sin, z(HEAD_PAD - QK_NOPE_DIM - half)], 1)
    tab_sb = jnp.concatenate([z(QK_NOPE_DIM + half), sin, z(HEAD_PAD - QK_HEAD_DIM)], 1)
    return dict(rope_cos=tab_cos, rope_sa=tab_sa, rope_sb=tab_sb, rope_cos_t=cos.T, rope_sin_t=sin.T)


def _prep_weights(norm1_w, w_in, pool_w, pool_scale, q_a_norm_w, w_q_b, kv_a_norm_w, w_kv_b,
                  q_norm_w, k_norm_w, w_o, norm2_w, w_router, w_gate, w_up, w_down,
                  ws_gate, ws_up, ws_down):
    d = D_MODEL
    c0 = POOL_WIDTH + Q_LORA_RANK + KV_LORA_RANK
    pad_h = HEAD_PAD - QK_HEAD_DIM
    w_in_p = jnp.concatenate(
        [w_in[:, :c0], jnp.zeros((d, QK_NOPE_DIM), F32), w_in[:, c0:], jnp.zeros((d, pad_h), F32)], 1)
    w_q = jnp.pad(w_q_b.reshape(Q_LORA_RANK, N_HEADS, QK_HEAD_DIM), ((0, 0), (0, 0), (0, pad_h)))
    kv = w_kv_b.reshape(KV_LORA_RANK, N_HEADS, QK_NOPE_DIM + V_HEAD_DIM)
    w_k = jnp.pad(kv[:, :, :QK_NOPE_DIM], ((0, 0), (0, 0), (0, HEAD_PAD - QK_NOPE_DIM)))
    w_v = kv[:, :, QK_NOPE_DIM:]
    w_r_t = w_router.T
    w_r_hi = w_r_t.astype(BF16)
    w_r_lo = (w_r_t - w_r_hi.astype(F32)).astype(BF16)
    q_gain = q_norm_w * (QK_HEAD_DIM ** -0.5 * LOG2E)
    bound = QK_HEAD_DIM * jnp.max(jnp.abs(q_gain)) * jnp.max(jnp.abs(k_norm_w)) * 1.02 + 0.25
    bound = bound.astype(QK_DTYPE).astype(F32)
    lane = jnp.arange(HEAD_PAD) == SHIFT_LANE
    return dict(
        score_bound=bound,
        k_spare=jnp.where(lane, -bound, 0.0).astype(F32).reshape(1, HEAD_PAD),
        norm1_w=norm1_w.reshape(1, d), w_in=w_in_p.astype(BF16),
        q_a_norm_w=q_a_norm_w.reshape(1, -1), w_q_t=w_q.reshape(Q_LORA_RANK, -1).T.astype(BF16),
        kv_a_norm_w=kv_a_norm_w.reshape(1, -1), w_k=w_k.reshape(KV_LORA_RANK, -1).astype(BF16),
        w_v_t=w_v.reshape(KV_LORA_RANK, -1).T.astype(BF16),
        q_gain_col=jnp.broadcast_to(jnp.pad(q_gain, (0, pad_h))[:, None], (HEAD_PAD, LANES)),
        k_norm_w=jnp.pad(k_norm_w, (0, pad_h)).reshape(1, HEAD_PAD),
        pool_w=pool_w.astype(BF16), pool_scale=pool_scale.reshape(1, -1), w_o=w_o.astype(BF16),
        norm2_w=norm2_w.reshape(1, d), w_router_hi=w_r_hi, w_router_lo=w_r_lo,
        w_gate=w_gate, w_up=w_up, w_down=w_down,
        ws_gate=ws_gate.astype(BF16), ws_up=ws_up.astype(BF16), ws_down=ws_down.astype(BF16))


def _tile(n, pref):
    return pref if n % pref == 0 else n


def _mixer_and_routing(x, mod, router_bias, p):
    b, s, d = x.shape
    t = b * s
    ts = _tile(s, 512)
    p = dict(p, **_rope_tables(s))

    u, qt, k, vt = _inproj(x, mod, p, ts)
    online = (p["score_bound"] > MAX_FIXED_SHIFT).astype(jnp.int32).reshape(1)
    attn = _attention(online, qt, k, vt, _tile(s, 1024), _tile(s, 2048), _tile(s, 512))
    x1, h2p, logits_t = _postmix(u, attn, x, mod, p, ts)
    tr = _tile(t, 512)
    idx, wts, pos, cnt = _router(logits_t, router_bias, tr)

    rb = _row_block(t)
    counts = cnt[:, 0].astype(jnp.int32)
    padded = (counts + rb - 1) // rb * rb
    pad_end = jnp.cumsum(padded)
    dest = _dest_rows(idx, pos, pad_end - padded, _tile(t, 2048))
    n_blocks = -(-t * TOP_K // rb) + N_EXPERTS
    blk_row = jnp.arange(n_blocks, dtype=jnp.int32)[:, None] * rb
    blk_exp = jnp.minimum(jnp.sum((pad_end[None, :] <= blk_row).astype(jnp.int32), axis=1),
                          N_EXPERTS - 1)
    n_used = (pad_end[-1:] // rb).astype(jnp.int32)
    h2f = h2p.reshape(t, PACK_W)
    xs = _sc_scatter_rows(h2f, dest, n_blocks * rb)
    return dict(xs=xs, dest=dest, blk_exp=blk_exp, n_used=n_used, wts_t=wts.T, h2f=h2f,
                x1=x1.reshape(t, d), mod=mod, shape=(b, s, d), row_block=rb)


def _gather_and_combine(st, ys, p):
    b, s, d = st["shape"]
    t = b * s
    gathered = _sc_gather_rows(ys, st["dest"][:TOP_K].reshape(TOP_K * t)).reshape(TOP_K, t, PACK_W)
    out = _combine(gathered, st["wts_t"], st["h2f"], st["x1"], st["mod"], p, _tile(s, 512), s)
    return out.reshape(b, s, d)


def kernel(x_prompt, x_sample, c_prompt, c_sample, w_ada, b_ada, norm1_w, w_in, pool_w, pool_scale,
           q_a_norm_w, w_q_b, kv_a_norm_w, w_kv_b, q_norm_w, k_norm_w, w_o, norm2_w, w_router,
           router_bias, w_gate, w_up, w_down, ws_gate, ws_up, ws_down):
    assert w_ada.shape[0] == 1, "single-layer encoder"
    p = _prep_weights(norm1_w[0], w_in[0], pool_w[0], pool_scale[0], q_a_norm_w[0], w_q_b[0],
                      kv_a_norm_w[0], w_kv_b[0], q_norm_w[0], k_norm_w[0], w_o[0], norm2_w[0],
                      w_router[0], w_gate[0], w_up[0], w_down[0], ws_gate[0], ws_up[0], ws_down[0])
    nb = x_prompt.shape[0]
    c = jnp.concatenate([c_prompt, c_sample], axis=0).astype(F32)
    mod = _adaln(c, w_ada[0], b_ada[0]).reshape(c.shape[0], 6, D_MODEL)
    experts = lambda st, xs: _experts(st["blk_exp"], st["n_used"], xs, p["w_gate"], p["w_up"],
                                      p["w_down"], st["row_block"])

    sp = _mixer_and_routing(x_prompt, mod[:nb], router_bias[0], p)
    dest_p, x_sample = lax.optimization_barrier((sp["dest"], x_sample))
    sp["dest"] = dest_p
    ss = _mixer_and_routing(x_sample, mod[nb:], router_bias[0], p)
    ys_p = experts(sp, sp["xs"])
    ys_p, xs_s = lax.optimization_barrier((ys_p, ss["xs"]))
    ys_s = experts(ss, xs_s)
    y_prompt = _gather_and_combine(sp, ys_p, p)
    y_sample = _gather_and_combine(ss, ys_s, p)
    return (y_prompt, y_sample)
```

```python
import functools

import jax
import jax.numpy as jnp
from jax import lax
from jax.experimental import pallas as pl
from jax.experimental.pallas import tpu as pltpu
from jax.experimental.pallas import tpu_sc as plsc

D_MODEL = 1024
POOL_WIDTH = 512
POOL_WINDOWS = (2, 4, 8, 16)
POOL_GROUP = 128
N_HEADS = 8
V_HEAD_DIM = 64
QK_NOPE_DIM = 64
QK_ROPE_DIM = 32
QK_HEAD_DIM = 96
Q_LORA_RANK = 256
KV_LORA_RANK = 128
ROPE_THETA = 10000.0
N_EXPERTS = 64
TOP_K = 6
N_GROUPS = 8
TOPK_GROUPS = 4
EXPERT_FF = 256
SHARED_FF = 256
ROUTED_SCALE = 2.5
EPS = 1e-6

LANES = 128
HEAD_PAD = 128
HALO = 16
TOPK_PAD = 8
V_AUG = 80
MAX_UNSHIFTED_SCORE = 40.0
LOG2E = 1.4426950408889634
PACK_W = D_MODEL // 2
SC_CORES = 2
SC_WORKERS = 32
SC_CHUNK = 128
VMEM_LIMIT = 48 * 1024 * 1024

F32 = jnp.float32
BF16 = jnp.bfloat16
QK_DTYPE = jnp.bfloat16


def _cparams(sem):
    return pltpu.CompilerParams(dimension_semantics=sem, vmem_limit_bytes=VMEM_LIMIT)


def _silu(x):
    return x * (1.0 / (1.0 + jnp.exp(-x)))


def _adaln_kernel(c_ref, w_ref, b_ref, o_ref):
    c = c_ref[...]
    o_ref[...] = jnp.dot(_silu(c), w_ref[...], preferred_element_type=F32,
                         precision=lax.Precision.HIGHEST) + b_ref[...]


def _adaln(c, w_ada, b_ada):
    nb, d = c.shape
    n = w_ada.shape[1]
    tn = 1536
    return pl.pallas_call(
        _adaln_kernel,
        grid=(n // tn,),
        in_specs=[pl.BlockSpec((nb, d), lambda j: (0, 0)),
                  pl.BlockSpec((d, tn), lambda j: (0, j)),
                  pl.BlockSpec((1, tn), lambda j: (0, j))],
        out_specs=pl.BlockSpec((nb, tn), lambda j: (0, j)),
        out_shape=jax.ShapeDtypeStruct((nb, n), F32),
        compiler_params=_cparams(("arbitrary",)),
        name="adaln",
    )(c, w_ada, b_ada.reshape(1, n))


def _inproj_kernel(x_ref, mod_ref, n1w_ref, win_ref, qan_ref, wqt_ref, kvan_ref, wk_ref, wvt_ref,
                   qg_ref, kg_ref, cos_ref, sa_ref, sb_ref, cost_ref, sint_ref,
                   u_ref, qt_ref, k_ref, vt_ref):
    ts = x_ref.shape[1]
    half = QK_ROPE_DIM // 2
    x = x_ref[0]
    shift1 = mod_ref[0, 0:1, :]
    gain1 = n1w_ref[...] * (1.0 + mod_ref[0, 1:2, :])
    r = lax.rsqrt(jnp.mean(x * x, axis=-1, keepdims=True) + EPS)
    h = x * r * gain1 + shift1
    z = jnp.dot(h.astype(BF16), win_ref[...], preferred_element_type=F32)
    u_ref[0] = z[:, :POOL_WIDTH].astype(BF16)

    cq = z[:, POOL_WIDTH:POOL_WIDTH + Q_LORA_RANK]
    cqn = cq * lax.rsqrt(jnp.mean(cq * cq, axis=-1, keepdims=True) + EPS) * qan_ref[...]
    qt = jnp.dot(wqt_ref[...], cqn.T.astype(BF16), preferred_element_type=F32)
    reps = ts // LANES
    qg = jnp.concatenate([qg_ref[...]] * reps, axis=1)
    cost = cost_ref[...]
    sint = sint_ref[...]
    spare = jnp.zeros((HEAD_PAD - QK_HEAD_DIM, ts), F32)
    for hd in range(N_HEADS):
        t = qt[hd * HEAD_PAD:(hd + 1) * HEAD_PAD]
        rn = lax.rsqrt(jnp.sum(t * t, axis=0, keepdims=True) * (1.0 / QK_HEAD_DIM) + EPS)
        tn = t * rn * qg
        t1 = tn[QK_NOPE_DIM:QK_NOPE_DIM + half]
        t2 = tn[QK_NOPE_DIM + half:QK_HEAD_DIM]
        out = jnp.concatenate([tn[:QK_NOPE_DIM], t1 * cost - t2 * sint, t1 * sint + t2 * cost, spare],
                              axis=0)
        qt_ref[0, hd * HEAD_PAD:(hd + 1) * HEAD_PAD, :] = out.astype(QK_DTYPE)

    c0 = POOL_WIDTH + Q_LORA_RANK
    ckv = z[:, c0:c0 + KV_LORA_RANK]
    ckvn = ckv * lax.rsqrt(jnp.mean(ckv * ckv, axis=-1, keepdims=True) + EPS) * kvan_ref[...]

    vt = jnp.dot(wvt_ref[...], ckvn.T.astype(BF16), preferred_element_type=F32)
    ones = jnp.ones((V_AUG - V_HEAD_DIM, ts), BF16)
    for hd in range(N_HEADS):
        vt_ref[0, hd * V_AUG:hd * V_AUG + V_HEAD_DIM, :] = (
            vt[hd * V_HEAD_DIM:(hd + 1) * V_HEAD_DIM].astype(BF16))
        vt_ref[0, hd * V_AUG + V_HEAD_DIM:(hd + 1) * V_AUG, :] = ones

    kk = jnp.dot(ckvn.astype(BF16), wk_ref[...], preferred_element_type=F32)
    kpe = z[:, c0 + KV_LORA_RANK:]
    kg = kg_ref[...]
    pe_ssq = jnp.sum(kpe * kpe, axis=-1, keepdims=True)
    pg = kpe * kg
    pe_rot = (pg * cos_ref[...] + pltpu.roll(pg, HEAD_PAD - half, 1) * sa_ref[...]
              + pltpu.roll(pg, half, 1) * sb_ref[...])
    for hd in range(N_HEADS):
        t = kk[:, hd * HEAD_PAD:(hd + 1) * HEAD_PAD]
        ssq = jnp.sum(t * t, axis=-1, keepdims=True) + pe_ssq
        rn = lax.rsqrt(ssq * (1.0 / QK_HEAD_DIM) + EPS)
        k_ref[0, :, hd * HEAD_PAD:(hd + 1) * HEAD_PAD] = ((t * kg + pe_rot) * rn).astype(QK_DTYPE)


def _inproj(x, mod, p, ts):
    b, s, d = x.shape
    qk_w = N_HEADS * HEAD_PAD
    half = QK_ROPE_DIM // 2
    const = lambda shape: pl.BlockSpec(shape, lambda bi, i: (0,) * len(shape))
    row_tab = pl.BlockSpec((ts, HEAD_PAD), lambda bi, i: (i, 0))
    col_tab = pl.BlockSpec((half, ts), lambda bi, i: (0, i))
    return pl.pallas_call(
        _inproj_kernel,
        grid=(b, s // ts),
        in_specs=[pl.BlockSpec((1, ts, d), lambda bi, i: (bi, i, 0)),
                  pl.BlockSpec((1, 6, d), lambda bi, i: (bi, 0, 0)),
                  const((1, d)), const((d, d)), const((1, Q_LORA_RANK)),
                  const((qk_w, Q_LORA_RANK)), const((1, KV_LORA_RANK)),
                  const((KV_LORA_RANK, qk_w)), const((N_HEADS * V_HEAD_DIM, KV_LORA_RANK)),
                  const((HEAD_PAD, LANES)), const((1, HEAD_PAD)),
                  row_tab, row_tab, row_tab, col_tab, col_tab],
        out_specs=[pl.BlockSpec((1, ts, POOL_WIDTH), lambda bi, i: (bi, i, 0)),
                   pl.BlockSpec((1, qk_w, ts), lambda bi, i: (bi, 0, i)),
                   pl.BlockSpec((1, ts, qk_w), lambda bi, i: (bi, i, 0)),
                   pl.BlockSpec((1, N_HEADS * V_AUG, ts), lambda bi, i: (bi, 0, i))],
        out_shape=[jax.ShapeDtypeStruct((b, s, POOL_WIDTH), BF16),
                   jax.ShapeDtypeStruct((b, qk_w, s), QK_DTYPE),
                   jax.ShapeDtypeStruct((b, s, qk_w), QK_DTYPE),
                   jax.ShapeDtypeStruct((b, N_HEADS * V_AUG, s), BF16)],
        compiler_params=_cparams(("parallel", "parallel")),
        name="inproj",
    )(x, mod, p["norm1_w"], p["w_in"], p["q_a_norm_w"], p["w_q_t"], p["kv_a_norm_w"], p["w_k"],
      p["w_v_t"], p["q_gain_col"], p["k_norm_w"],
      p["rope_cos"], p["rope_sa"], p["rope_sb"], p["rope_cos_t"], p["rope_sin_t"])


def _attn_kernel(online_ref, qt_ref, k_ref, vt_ref, o_ref, acc_a, acc_b, m_ref, *, tk, tk_online):
    s_len = k_ref.shape[1]
    acc_a[...] = jnp.zeros_like(acc_a)
    acc_b[...] = jnp.zeros_like(acc_b)
    qa = qt_ref[0, :HEAD_PAD, :]
    qb = qt_ref[0, HEAD_PAD:, :]

    @pl.when(online_ref[0] == 0)
    def _():
        def body(c, _):
            off = pl.multiple_of(c * tk, tk)
            ks = k_ref[0, pl.ds(off, tk), :]
            vts = vt_ref[0, :, pl.ds(off, tk)]
            pa = jnp.exp2(jnp.dot(ks[:, :HEAD_PAD], qa, preferred_element_type=F32)).astype(BF16)
            pb = jnp.exp2(jnp.dot(ks[:, HEAD_PAD:], qb, preferred_element_type=F32)).astype(BF16)
            acc_a[...] += jnp.dot(vts[:V_AUG], pa, preferred_element_type=F32)
            acc_b[...] += jnp.dot(vts[V_AUG:], pb, preferred_element_type=F32)
            return 0

        lax.fori_loop(0, s_len // tk, body, 0)

    @pl.when(online_ref[0] != 0)
    def _():
        m_ref[...] = jnp.full_like(m_ref, -jnp.inf)

        def one_head(kh, q, vth, acc, row):
            s = jnp.dot(kh, q, preferred_element_type=F32)
            m_old = m_ref[row:row + 1, :]
            m_new = jnp.maximum(m_old, jnp.max(s, axis=0, keepdims=True))
            p = jnp.exp2(s - m_new).astype(BF16)
            acc[...] = acc[...] * jnp.exp2(m_old - m_new) + jnp.dot(vth, p, preferred_element_type=F32)
            m_ref[row:row + 1, :] = m_new

        def body(c, _):
            off = pl.multiple_of(c * tk_online, tk_online)
            ks = k_ref[0, pl.ds(off, tk_online), :]
            vts = vt_ref[0, :, pl.ds(off, tk_online)]
            one_head(ks[:, :HEAD_PAD], qa, vts[:V_AUG], acc_a, 0)
            one_head(ks[:, HEAD_PAD:], qb, vts[V_AUG:], acc_b, 1)
            return 0

        lax.fori_loop(0, s_len // tk_online, body, 0)

    oa = acc_a[:V_HEAD_DIM] / acc_a[V_HEAD_DIM:V_HEAD_DIM + 1]
    ob = acc_b[:V_HEAD_DIM] / acc_b[V_HEAD_DIM:V_HEAD_DIM + 1]
    o_ref[0] = jnp.concatenate([oa, ob], axis=0).T.astype(BF16)


def _attention(online, qt, k, vt, tq, tk, tk_online):
    b, _, s = qt.shape
    return pl.pallas_call(
        functools.partial(_attn_kernel, tk=tk, tk_online=tk_online),
        grid_spec=pltpu.PrefetchScalarGridSpec(
            num_scalar_prefetch=1,
            grid=(b, N_HEADS // 2, s // tq),
            in_specs=[pl.BlockSpec((1, 2 * HEAD_PAD, tq), lambda bi, j, i, on: (bi, j, i)),
                      pl.BlockSpec((1, s, 2 * HEAD_PAD), lambda bi, j, i, on: (bi, 0, j)),
                      pl.BlockSpec((1, 2 * V_AUG, s), lambda bi, j, i, on: (bi, j, 0))],
            out_specs=pl.BlockSpec((1, tq, 2 * V_HEAD_DIM), lambda bi, j, i, on: (bi, i, j)),
            scratch_shapes=[pltpu.VMEM((V_AUG, tq), F32), pltpu.VMEM((V_AUG, tq), F32),
                            pltpu.VMEM((8, tq), F32)]),
        out_shape=jax.ShapeDtypeStruct((b, s, N_HEADS * V_HEAD_DIM), BF16),
        compiler_params=_cparams(("parallel", "parallel", "arbitrary")),
        name="attn",
    )(online, qt, k, vt)


def _postmix_kernel(u_ref, up_ref, un_ref, a_ref, x_ref, mod_ref, pw_ref, ps_ref, wo_ref, n2w_ref,
                    wrh_ref, wrl_ref, x1_ref, h2_ref, lg_ref, *, seq):
    i = pl.program_id(1)
    ts = u_ref.shape[1]
    ext_rows = ts + 2 * HALO
    ext = jnp.concatenate([up_ref[0], u_ref[0], un_ref[0]], axis=0).astype(F32)
    pos = i * ts - HALO + lax.broadcasted_iota(jnp.int32, (ext_rows, 1), 0)
    ext = jnp.where((pos >= 0) & (pos < seq), ext, 0.0)
    p = i * ts + lax.broadcasted_iota(jnp.int32, (ts, 1), 0)

    outs = []
    for g, w in enumerate(POOL_WINDOWS):
        left = w // 2
        right = w - 1 - left
        t = ext[:, g * POOL_GROUP:(g + 1) * POOL_GROUP]
        step = 1
        while step < w:
            t = t + pltpu.roll(t, ext_rows - step, 0)
            step *= 2
        win = pltpu.roll(t, left, 0)[HALO:HALO + ts]
        cnt = (jnp.minimum(p + right + 1, seq) - jnp.maximum(p - left, 0)).astype(F32)
        d = win * (1.0 / cnt) - ext[HALO:HALO + ts, g * POOL_GROUP:(g + 1) * POOL_GROUP]
        outs.append(jnp.dot(d.astype(BF16), pw_ref[g], preferred_element_type=F32))
    pool = (jnp.concatenate(outs, axis=-1) * ps_ref[...]).astype(BF16)

    mix = (jnp.dot(pool, wo_ref[:POOL_WIDTH, :], preferred_element_type=F32)
           + jnp.dot(a_ref[0], wo_ref[POOL_WIDTH:, :], preferred_element_type=F32))
    x1 = x_ref[0] + mod_ref[0, 2:3, :] * mix
    x1_ref[0] = x1
    r = lax.rsqrt(jnp.mean(x1 * x1, axis=-1, keepdims=True) + EPS)
    gain2 = n2w_ref[...] * (1.0 + mod_ref[0, 4:5, :])
    h2 = x1 * r * gain2 + mod_ref[0, 3:4, :]
    h2_ref[0] = _pack_rows(h2)
    hi = h2.astype(BF16)
    lo = (h2 - hi.astype(F32)).astype(BF16)
    nt = (((1,), (1,)), ((), ()))
    lg_ref[...] = (lax.dot_general(wrh_ref[...], hi, nt, preferred_element_type=F32)
                   + lax.dot_general(wrl_ref[...], hi, nt, preferred_element_type=F32)
                   + lax.dot_general(wrh_ref[...], lo, nt, preferred_element_type=F32))


def _postmix(u, attn, x, mod, p, ts):
    b, s, d = x.shape
    nt = s // ts
    hb = ts // HALO
    const = lambda shape: pl.BlockSpec(shape, lambda bi, i: (0,) * len(shape))
    return pl.pallas_call(
        functools.partial(_postmix_kernel, seq=s),
        grid=(b, nt),
        in_specs=[pl.BlockSpec((1, ts, POOL_WIDTH), lambda bi, i: (bi, i, 0)),
                  pl.BlockSpec((1, HALO, POOL_WIDTH), lambda bi, i: (bi, jnp.maximum(i * hb - 1, 0), 0)),
                  pl.BlockSpec((1, HALO, POOL_WIDTH),
                               lambda bi, i: (bi, jnp.minimum((i + 1) * hb, s // HALO - 1), 0)),
                  pl.BlockSpec((1, ts, POOL_WIDTH), lambda bi, i: (bi, i, 0)),
                  pl.BlockSpec((1, ts, d), lambda bi, i: (bi, i, 0)),
                  pl.BlockSpec((1, 6, d), lambda bi, i: (bi, 0, 0)),
                  const((len(POOL_WINDOWS), POOL_GROUP, POOL_GROUP)), const((1, POOL_WIDTH)),
                  const((d, d)), const((1, d)), const((N_EXPERTS, d)), const((N_EXPERTS, d))],
        out_specs=[pl.BlockSpec((1, ts, d), lambda bi, i: (bi, i, 0)),
                   pl.BlockSpec((1, ts, PACK_W), lambda bi, i: (bi, i, 0)),
                   pl.BlockSpec((N_EXPERTS, ts), lambda bi, i: (0, bi * nt + i))],
        out_shape=[jax.ShapeDtypeStruct((b, s, d), F32),
                   jax.ShapeDtypeStruct((b, s, PACK_W), jnp.int32),
                   jax.ShapeDtypeStruct((N_EXPERTS, b * s), F32)],
        compiler_params=_cparams(("parallel", "parallel")),
        name="postmix",
    )(u, u, u, attn, x, mod, p["pool_w"], p["pool_scale"], p["w_o"], p["norm2_w"],
      p["w_router_hi"], p["w_router_lo"])


def _router_kernel(lg_ref, bias_ref, tri_ref, idx_ref, wts_ref, pos_ref, cnt_ref, carry_ref):
    @pl.when(pl.program_id(0) == 0)
    def _():
        carry_ref[...] = jnp.zeros_like(carry_ref)

    ts = lg_ref.shape[1]
    gsz = N_EXPERTS // N_GROUPS
    ninf = -jnp.inf
    scores = 1.0 / (1.0 + jnp.exp(-lg_ref[...]))
    choice = scores + bias_ref[...]
    sub = lax.broadcasted_iota(jnp.int32, (gsz, ts), 0)

    gs_rows = []
    for g in range(N_GROUPS):
        grp = choice[g * gsz:(g + 1) * gsz]
        m1 = jnp.max(grp, axis=0, keepdims=True)
        i1 = jnp.min(jnp.where(grp == m1, sub, gsz), axis=0, keepdims=True)
        m2 = jnp.max(jnp.where(sub == i1, ninf, grp), axis=0, keepdims=True)
        gs_rows.append(m1 + m2)
    gs = jnp.concatenate(gs_rows, axis=0)

    rank = jnp.zeros((N_GROUPS, ts), jnp.int32)
    for g in range(N_GROUPS):
        row = gs[g:g + 1]
        beats = (row > gs) | ((row == gs) & (sub > g))
        rank = rank + beats.astype(jnp.int32)
    gsel = rank < TOPK_GROUPS

    masked = jnp.concatenate(
        [jnp.where(gsel[g:g + 1], choice[g * gsz:(g + 1) * gsz], ninf) for g in range(N_GROUPS)],
        axis=0)
    eio = lax.broadcasted_iota(jnp.int32, (N_EXPERTS, ts), 0)
    idx_rows, w_rows, hits = [], [], []
    for _ in range(TOP_K):
        m = jnp.max(masked, axis=0, keepdims=True)
        i = jnp.min(jnp.where(masked == m, eio, N_EXPERTS), axis=0, keepdims=True)
        hit = eio == i
        w_rows.append(jnp.sum(jnp.where(hit, scores, 0.0), axis=0, keepdims=True))
        masked = jnp.where(hit, ninf, masked)
        idx_rows.append(i)
        hits.append(hit)

    wsum = functools.reduce(lambda a, c: a + c, w_rows)
    pad_i = [jnp.zeros((1, ts), jnp.int32)] * (TOPK_PAD - TOP_K)
    pad_f = [jnp.zeros((1, ts), F32)] * (TOPK_PAD - TOP_K)
    idx_ref[...] = jnp.concatenate(idx_rows + pad_i, axis=0)
    wts_ref[...] = jnp.concatenate([w / wsum * ROUTED_SCALE for w in w_rows] + pad_f, axis=0)

    sel = functools.reduce(lambda a, c: a | c, hits)
    onehot = jnp.where(sel, 1.0, 0.0).astype(BF16)
    run = jnp.dot(onehot, tri_ref[...], preferred_element_type=F32) + carry_ref[:, 0:1]
    pos_rows = [jnp.sum(jnp.where(h, run - 1.0, 0.0), axis=0, keepdims=True).astype(jnp.int32)
                for h in hits]
    pos_ref[...] = jnp.concatenate(pos_rows + pad_i, axis=0)
    total = run[:, ts - 1:ts]
    carry_ref[...] = jnp.broadcast_to(total, carry_ref.shape)
    cnt_ref[...] = jnp.broadcast_to(total, cnt_ref.shape)


def _router(logits_t, router_bias, ts):
    t = logits_t.shape[1]
    tri = jnp.triu(jnp.ones((ts, ts), BF16))
    tok = pl.BlockSpec((TOPK_PAD, ts), lambda i: (0, i))
    return pl.pallas_call(
        _router_kernel,
        grid=(t // ts,),
        in_specs=[pl.BlockSpec((N_EXPERTS, ts), lambda i: (0, i)),
                  pl.BlockSpec((N_EXPERTS, 1), lambda i: (0, 0)),
                  pl.BlockSpec((ts, ts), lambda i: (0, 0))],
        out_specs=[tok, tok, tok, pl.BlockSpec((N_EXPERTS, LANES), lambda i: (0, 0))],
        out_shape=[jax.ShapeDtypeStruct((TOPK_PAD, t), jnp.int32),
                   jax.ShapeDtypeStruct((TOPK_PAD, t), F32),
                   jax.ShapeDtypeStruct((TOPK_PAD, t), jnp.int32),
                   jax.ShapeDtypeStruct((N_EXPERTS, LANES), F32)],
        scratch_shapes=[pltpu.VMEM((N_EXPERTS, LANES), F32)],
        compiler_params=_cparams(("arbitrary",)),
        name="router",
    )(logits_t, router_bias.reshape(N_EXPERTS, 1), tri)


def _dest_kernel(idx_ref, pos_ref, start_ref, dest_ref):
    ts = idx_ref.shape[1]
    eio = lax.broadcasted_iota(jnp.int32, (N_EXPERTS, ts), 0)
    start = start_ref[...]
    rows = [pos_ref[k:k + 1, :]
            + jnp.sum(jnp.where(eio == idx_ref[k:k + 1, :], start, 0), axis=0, keepdims=True)
            for k in range(TOPK_PAD)]
    dest_ref[...] = jnp.concatenate(rows, axis=0)


def _dest_rows(idx, pos, pad_start, ts):
    t = idx.shape[1]
    tok = pl.BlockSpec((TOPK_PAD, ts), lambda i: (0, i))
    return pl.pallas_call(
        _dest_kernel,
        grid=(t // ts,),
        in_specs=[tok, tok, pl.BlockSpec((N_EXPERTS, 1), lambda i: (0, 0))],
        out_specs=tok,
        out_shape=jax.ShapeDtypeStruct((TOPK_PAD, t), jnp.int32),
        compiler_params=_cparams(("parallel",)),
        name="dest_rows",
    )(idx, pos, pad_start.reshape(N_EXPERTS, 1))


def _pack_rows(x):
    bits = lax.bitcast_convert_type(x.astype(BF16).astype(F32), jnp.int32)
    return bits[:, :PACK_W] | lax.shift_right_logical(bits[:, PACK_W:], 16)


def _unpack_rows(words):
    hi = lax.bitcast_convert_type(words & jnp.int32(-65536), F32)
    lo = lax.bitcast_convert_type(lax.shift_left(words, 16), F32)
    return jnp.concatenate([hi, lo], axis=1)


def _sc_mesh():
    return plsc.VectorSubcoreMesh(core_axis_name="core", subcore_axis_name="subcore")


def _sc_worker_base(rows_per_worker):
    return (lax.axis_index("subcore") * SC_CORES + lax.axis_index("core")) * rows_per_worker


def _sc_scatter_rows(x, dest, n_rows):
    t, w = x.shape
    per_worker = t // SC_WORKERS
    assert per_worker * SC_WORKERS == t and per_worker % SC_CHUNK == 0

    @functools.partial(
        pl.kernel, out_type=jax.ShapeDtypeStruct((n_rows, w), x.dtype), mesh=_sc_mesh(),
        scratch_types=[pltpu.VMEM((TOPK_PAD, SC_CHUNK), jnp.int32), pltpu.VMEM((SC_CHUNK, w), x.dtype),
                       pltpu.SemaphoreType.DMA])
    def scatter(x_hbm, dest_hbm, out_hbm, idx_v, rows_v, sem):
        base = _sc_worker_base(per_worker)

        @pl.loop(0, per_worker // SC_CHUNK)
        def _(c):
            off = pl.multiple_of(base + c * SC_CHUNK, SC_CHUNK)
            pltpu.sync_copy(dest_hbm.at[:, pl.ds(off, SC_CHUNK)], idx_v)
            pltpu.sync_copy(x_hbm.at[pl.ds(off, SC_CHUNK)], rows_v)
            copies = [pltpu.async_copy(rows_v, out_hbm.at[idx_v.at[k]], sem) for k in range(TOP_K)]
            for cp in copies:
                cp.wait()

    return scatter(x, dest)


def _sc_gather_rows(table, idx):
    m = idx.shape[0]
    w = table.shape[1]
    per_worker = m // SC_WORKERS
    assert per_worker * SC_WORKERS == m and per_worker % SC_CHUNK == 0

    @functools.partial(
        pl.kernel, out_type=jax.ShapeDtypeStruct((m, w), table.dtype), mesh=_sc_mesh(),
        scratch_types=[pltpu.VMEM((SC_CHUNK,), jnp.int32), pltpu.VMEM((SC_CHUNK, w), table.dtype),
                       pltpu.SemaphoreType.DMA])
    def gather(table_hbm, idx_hbm, out_hbm, idx_v, rows_v, sem):
        base = _sc_worker_base(per_worker)

        @pl.loop(0, per_worker // SC_CHUNK)
        def _(c):
            off = pl.multiple_of(base + c * SC_CHUNK, SC_CHUNK)
            pltpu.sync_copy(idx_hbm.at[pl.ds(off, SC_CHUNK)], idx_v)
            pltpu.async_copy(table_hbm.at[idx_v], rows_v, sem).wait()
            pltpu.sync_copy(rows_v, out_hbm.at[pl.ds(off, SC_CHUNK)])

    return gather(table, idx)


def _experts_kernel(blk_exp_ref, n_used_ref, xs_ref, wg_ref, wu_ref, wd_ref, ys_ref):
    @pl.when(pl.program_id(0) < n_used_ref[0])
    def _():
        xb = _unpack_rows(xs_ref[...]).astype(BF16)
        g = jnp.dot(xb, wg_ref[0].astype(BF16), preferred_element_type=F32)
        u = jnp.dot(xb, wu_ref[0].astype(BF16), preferred_element_type=F32)
        hb = (_silu(g) * u).astype(BF16)
        ys_ref[...] = _pack_rows(jnp.dot(hb, wd_ref[0].astype(BF16), preferred_element_type=F32))


def _row_block(t):
    per_expert = t * TOP_K // N_EXPERTS
    return 1024 if per_expert >= 4 * 1024 else (512 if per_expert >= 4 * 512 else 256)


def _experts(blk_exp, n_used, xs, w_gate, w_up, w_down, row_block):
    n_rows, w = xs.shape
    d = D_MODEL
    n_blocks = n_rows // row_block
    row = lambda i, be, nu: (jnp.minimum(i, nu[0] - 1), 0)
    wsel = lambda i, be, nu: (be[i], 0, 0)
    return pl.pallas_call(
        _experts_kernel,
        grid_spec=pltpu.PrefetchScalarGridSpec(
            num_scalar_prefetch=2,
            grid=(n_blocks,),
            in_specs=[pl.BlockSpec((row_block, w), row),
                      pl.BlockSpec((1, d, EXPERT_FF), wsel),
                      pl.BlockSpec((1, d, EXPERT_FF), wsel),
                      pl.BlockSpec((1, EXPERT_FF, d), wsel)],
            out_specs=pl.BlockSpec((row_block, w), row)),
        out_shape=jax.ShapeDtypeStruct((n_rows, w), jnp.int32),
        compiler_params=_cparams(("arbitrary",)),
        name="experts",
    )(blk_exp, n_used, xs, w_gate, w_up, w_down)


def _combine_kernel(g_ref, w_ref, h2_ref, x1_ref, mod_ref, wsg_ref, wsu_ref, wsd_ref, out_ref):
    hb = _unpack_rows(h2_ref[...]).astype(BF16)
    g = jnp.dot(hb, wsg_ref[...], preferred_element_type=F32)
    u = jnp.dot(hb, wsu_ref[...], preferred_element_type=F32)
    acc = jnp.dot((_silu(g) * u).astype(BF16), wsd_ref[...], preferred_element_type=F32)
    w = w_ref[...]
    for k in range(TOP_K):
        acc = acc + _unpack_rows(g_ref[k]) * w[:, k:k + 1]
    out_ref[...] = x1_ref[...] + mod_ref[0, 5:6, :] * acc


def _combine(gathered, wts_t, h2p, x1, mod, p, tt, seq):
    t, d = x1.shape
    per_seq = seq // tt
    tok = pl.BlockSpec((tt, d), lambda i: (i, 0))
    const = lambda shape: pl.BlockSpec(shape, lambda i: (0,) * len(shape))
    return pl.pallas_call(
        _combine_kernel,
        grid=(t // tt,),
        in_specs=[pl.BlockSpec((TOP_K, tt, PACK_W), lambda i: (0, i, 0)),
                  pl.BlockSpec((tt, TOPK_PAD), lambda i: (i, 0)),
                  pl.BlockSpec((tt, PACK_W), lambda i: (i, 0)),
                  tok,
                  pl.BlockSpec((1, 6, d), lambda i: (i // per_seq, 0, 0)),
                  const((d, SHARED_FF)), const((d, SHARED_FF)), const((SHARED_FF, d))],
        out_specs=tok,
        out_shape=jax.ShapeDtypeStruct((t, d), F32),
        compiler_params=_cparams(("parallel",)),
        name="combine",
    )(gathered, wts_t, h2p, x1, mod, p["ws_gate"], p["ws_up"], p["ws_down"])


def _rope_tables(s):
    half = QK_ROPE_DIM // 2
    inv_freq = ROPE_THETA ** (-jnp.arange(half, dtype=F32) / half)
    ang = jnp.arange(s, dtype=F32)[:, None] * inv_freq[None, :]
    cos, sin = jnp.cos(ang), jnp.sin(ang)
    z = lambda n: jnp.zeros((s, n), F32)
    tab_cos = jnp.concatenate([jnp.ones((s, QK_NOPE_DIM), F32), cos, cos, z(HEAD_PAD - QK_HEAD_DIM)], 1)
    tab_sa = jnp.concatenate([z(QK_NOPE_DIM), -sin, z(HEAD_PAD - QK_NOPE_DIM - half)], 1)
    tab_sb = jnp.concatenate([z(QK_NOPE_DIM + half), sin, z(HEAD_PAD - QK_HEAD_DIM)], 1)
    return dict(rope_cos=tab_cos, rope_sa=tab_sa, rope_sb=tab_sb, rope_cos_t=cos.T, rope_sin_t=sin.T)


def _prep_weights(norm1_w, w_in, pool_w, pool_scale, q_a_norm_w, w_q_b, kv_a_norm_w, w_kv_b,
                  q_norm_w, k_norm_w, w_o, norm2_w, w_router, w_gate, w_up, w_down,
                  ws_gate, ws_up, ws_down):
    d = D_MODEL
    c0 = POOL_WIDTH + Q_LORA_RANK + KV_LORA_RANK
    pad_h = HEAD_PAD - QK_HEAD_DIM
    w_in_p = jnp.concatenate(
        [w_in[:, :c0], jnp.zeros((d, QK_NOPE_DIM), F32), w_in[:, c0:], jnp.zeros((d, pad_h), F32)], 1)
    w_q = jnp.pad(w_q_b.reshape(Q_LORA_RANK, N_HEADS, QK_HEAD_DIM), ((0, 0), (0, 0), (0, pad_h)))
    kv = w_kv_b.reshape(KV_LORA_RANK, N_HEADS, QK_NOPE_DIM + V_HEAD_DIM)
    w_k = jnp.pad(kv[:, :, :QK_NOPE_DIM], ((0, 0), (0, 0), (0, HEAD_PAD - QK_NOPE_DIM)))
    w_v = kv[:, :, QK_NOPE_DIM:]
    w_r_t = w_router.T
    w_r_hi = w_r_t.astype(BF16)
    w_r_lo = (w_r_t - w_r_hi.astype(F32)).astype(BF16)
    q_gain = q_norm_w * (QK_HEAD_DIM ** -0.5 * LOG2E)
    bound = QK_HEAD_DIM * jnp.max(jnp.abs(q_gain)) * jnp.max(jnp.abs(k_norm_w)) * 1.02 + 0.25
    return dict(
        score_bound=bound,
        norm1_w=norm1_w.reshape(1, d), w_in=w_in_p.astype(BF16),
        q_a_norm_w=q_a_norm_w.reshape(1, -1), w_q_t=w_q.reshape(Q_LORA_RANK, -1).T.astype(BF16),
        kv_a_norm_w=kv_a_norm_w.reshape(1, -1), w_k=w_k.reshape(KV_LORA_RANK, -1).astype(BF16),
        w_v_t=w_v.reshape(KV_LORA_RANK, -1).T.astype(BF16),
        q_gain_col=jnp.broadcast_to(jnp.pad(q_gain, (0, pad_h))[:, None], (HEAD_PAD, LANES)),
        k_norm_w=jnp.pad(k_norm_w, (0, pad_h)).reshape(1, HEAD_PAD),
        pool_w=pool_w.astype(BF16), pool_scale=pool_scale.reshape(1, -1), w_o=w_o.astype(BF16),
        norm2_w=norm2_w.reshape(1, d), w_router_hi=w_r_hi, w_router_lo=w_r_lo,
        w_gate=w_gate, w_up=w_up, w_down=w_down,
        ws_gate=ws_gate.astype(BF16), ws_up=ws_up.astype(BF16), ws_down=ws_down.astype(BF16))


def _tile(n, pref):
    return pref if n % pref == 0 else n


def _mixer_and_routing(x, mod, router_bias, p):
    b, s, d = x.shape
    t = b * s
    ts = _tile(s, 512)
    p = dict(p, **_rope_tables(s))

    u, qt, k, vt = _inproj(x, mod, p, ts)
    online = (p["score_bound"] > MAX_UNSHIFTED_SCORE).astype(jnp.int32).reshape(1)
    attn = _attention(online, qt, k, vt, _tile(s, 1024), _tile(s, 2048), _tile(s, 512))
    x1, h2p, logits_t = _postmix(u, attn, x, mod, p, ts)
    tr = _tile(t, 512)
    idx, wts, pos, cnt = _router(logits_t, router_bias, tr)

    rb = _row_block(t)
    counts = cnt[:, 0].astype(jnp.int32)
    padded = (counts + rb - 1) // rb * rb
    pad_end = jnp.cumsum(padded)
    dest = _dest_rows(idx, pos, pad_end - padded, _tile(t, 2048))
    n_blocks = -(-t * TOP_K // rb) + N_EXPERTS
    blk_row = jnp.arange(n_blocks, dtype=jnp.int32)[:, None] * rb
    blk_exp = jnp.minimum(jnp.sum((pad_end[None, :] <= blk_row).astype(jnp.int32), axis=1),
                          N_EXPERTS - 1)
    n_used = (pad_end[-1:] // rb).astype(jnp.int32)
    h2f = h2p.reshape(t, PACK_W)
    xs = _sc_scatter_rows(h2f, dest, n_blocks * rb)
    return dict(xs=xs, dest=dest, blk_exp=blk_exp, n_used=n_used, wts_t=wts.T, h2f=h2f,
                x1=x1.reshape(t, d), mod=mod, shape=(b, s, d), row_block=rb)


def _gather_and_combine(st, ys, p):
    b, s, d = st["shape"]
    t = b * s
    gathered = _sc_gather_rows(ys, st["dest"][:TOP_K].reshape(TOP_K * t)).reshape(TOP_K, t, PACK_W)
    out = _combine(gathered, st["wts_t"], st["h2f"], st["x1"], st["mod"], p, _tile(s, 512), s)
    return out.reshape(b, s, d)


def kernel(x_prompt, x_sample, c_prompt, c_sample, w_ada, b_ada, norm1_w, w_in, pool_w, pool_scale,
           q_a_norm_w, w_q_b, kv_a_norm_w, w_kv_b, q_norm_w, k_norm_w, w_o, norm2_w, w_router,
           router_bias, w_gate, w_up, w_down, ws_gate, ws_up, ws_down):
    assert w_ada.shape[0] == 1, "single-layer encoder"
    p = _prep_weights(norm1_w[0], w_in[0], pool_w[0], pool_scale[0], q_a_norm_w[0], w_q_b[0],
                      kv_a_norm_w[0], w_kv_b[0], q_norm_w[0], k_norm_w[0], w_o[0], norm2_w[0],
                      w_router[0], w_gate[0], w_up[0], w_down[0], ws_gate[0], ws_up[0], ws_down[0])
    nb = x_prompt.shape[0]
    c = jnp.concatenate([c_prompt, c_sample], axis=0).astype(F32)
    mod = _adaln(c, w_ada[0], b_ada[0]).reshape(c.shape[0], 6, D_MODEL)
    experts = lambda st, xs: _experts(st["blk_exp"], st["n_used"], xs, p["w_gate"], p["w_up"],
                                      p["w_down"], st["row_block"])

    sp = _mixer_and_routing(x_prompt, mod[:nb], router_bias[0], p)
    dest_p, x_sample = lax.optimization_barrier((sp["dest"], x_sample))
    sp["dest"] = dest_p
    ss = _mixer_and_routing(x_sample, mod[nb:], router_bias[0], p)
    ys_p = experts(sp, sp["xs"])
    ys_p, xs_s = lax.optimization_barrier((ys_p, ss["xs"]))
    ys_s = experts(ss, xs_s)
    y_prompt = _gather_and_combine(sp, ys_p, p)
    y_sample = _gather_and_combine(ss, ys_s, p)
    return (y_prompt, y_sample)
```

```python
import functools

import jax
import jax.numpy as jnp
from jax import lax
from jax.experimental import pallas as pl
from jax.experimental.pallas import tpu as pltpu
from jax.experimental.pallas import tpu_sc as plsc

D_MODEL = 1024
POOL_WIDTH = 512
POOL_WINDOWS = (2, 4, 8, 16)
POOL_GROUP = 128
N_HEADS = 8
V_HEAD_DIM = 64
QK_NOPE_DIM = 64
QK_ROPE_DIM = 32
QK_HEAD_DIM = 96
Q_LORA_RANK = 256
KV_LORA_RANK = 128
ROPE_THETA = 10000.0
N_EXPERTS = 64
TOP_K = 6
N_GROUPS = 8
TOPK_GROUPS = 4
EXPERT_FF = 256
SHARED_FF = 256
ROUTED_SCALE = 2.5
EPS = 1e-6

LANES = 128
HEAD_PAD = 128
HALO = 16
TOPK_PAD = 8
V_AUG = 80
MAX_UNSHIFTED_SCORE = 40.0
LOG2E = 1.4426950408889634
PACK_W = D_MODEL // 2
SC_CORES = 2
SC_WORKERS = 32
SC_CHUNK = 128
VMEM_LIMIT = 48 * 1024 * 1024

F32 = jnp.float32
BF16 = jnp.bfloat16
QK_DTYPE = jnp.bfloat16


def _cparams(sem):
    return pltpu.CompilerParams(dimension_semantics=sem, vmem_limit_bytes=VMEM_LIMIT)


def _silu(x):
    return x * (1.0 / (1.0 + jnp.exp(-x)))


def _adaln_kernel(c_ref, w_ref, b_ref, o_ref):
    c = c_ref[...]
    o_ref[...] = jnp.dot(_silu(c), w_ref[...], preferred_element_type=F32,
                         precision=lax.Precision.HIGHEST) + b_ref[...]


def _adaln(c, w_ada, b_ada):
    nb, d = c.shape
    n = w_ada.shape[1]
    tn = 1536
    return pl.pallas_call(
        _adaln_kernel,
        grid=(n // tn,),
        in_specs=[pl.BlockSpec((nb, d), lambda j: (0, 0)),
                  pl.BlockSpec((d, tn), lambda j: (0, j)),
                  pl.BlockSpec((1, tn), lambda j: (0, j))],
        out_specs=pl.BlockSpec((nb, tn), lambda j: (0, j)),
        out_shape=jax.ShapeDtypeStruct((nb, n), F32),
        compiler_params=_cparams(("arbitrary",)),
        name="adaln",
    )(c, w_ada, b_ada.reshape(1, n))


def _inproj_kernel(x_ref, mod_ref, n1w_ref, win_ref, qan_ref, wqt_ref, kvan_ref, wk_ref, wvt_ref,
                   qg_ref, kg_ref, cos_ref, sa_ref, sb_ref, cost_ref, sint_ref,
                   u_ref, qt_ref, k_ref, vt_ref):
    ts = x_ref.shape[1]
    half = QK_ROPE_DIM // 2
    x = x_ref[0]
    shift1 = mod_ref[0, 0:1, :]
    gain1 = n1w_ref[...] * (1.0 + mod_ref[0, 1:2, :])
    r = lax.rsqrt(jnp.mean(x * x, axis=-1, keepdims=True) + EPS)
    h = x * r * gain1 + shift1
    z = jnp.dot(h.astype(BF16), win_ref[...], preferred_element_type=F32)
    u_ref[0] = z[:, :POOL_WIDTH].astype(BF16)

    cq = z[:, POOL_WIDTH:POOL_WIDTH + Q_LORA_RANK]
    cqn = cq * lax.rsqrt(jnp.mean(cq * cq, axis=-1, keepdims=True) + EPS) * qan_ref[...]
    qt = jnp.dot(wqt_ref[...], cqn.T.astype(BF16), preferred_element_type=F32)
    reps = ts // LANES
    qg = jnp.concatenate([qg_ref[...]] * reps, axis=1)
    cost = cost_ref[...]
    sint = sint_ref[...]
    spare = jnp.zeros((HEAD_PAD - QK_HEAD_DIM, ts), F32)
    for hd in range(N_HEADS):
        t = qt[hd * HEAD_PAD:(hd + 1) * HEAD_PAD]
        rn = lax.rsqrt(jnp.sum(t * t, axis=0, keepdims=True) * (1.0 / QK_HEAD_DIM) + EPS)
        tn = t * rn * qg
        t1 = tn[QK_NOPE_DIM:QK_NOPE_DIM + half]
        t2 = tn[QK_NOPE_DIM + half:QK_HEAD_DIM]
        out = jnp.concatenate([tn[:QK_NOPE_DIM], t1 * cost - t2 * sint, t1 * sint + t2 * cost, spare],
                              axis=0)
        qt_ref[0, hd * HEAD_PAD:(hd + 1) * HEAD_PAD, :] = out.astype(QK_DTYPE)

    c0 = POOL_WIDTH + Q_LORA_RANK
    ckv = z[:, c0:c0 + KV_LORA_RANK]
    ckvn = ckv * lax.rsqrt(jnp.mean(ckv * ckv, axis=-1, keepdims=True) + EPS) * kvan_ref[...]

    vt = jnp.dot(wvt_ref[...], ckvn.T.astype(BF16), preferred_element_type=F32)
    ones = jnp.ones((V_AUG - V_HEAD_DIM, ts), BF16)
    for hd in range(N_HEADS):
        vt_ref[0, hd * V_AUG:hd * V_AUG + V_HEAD_DIM, :] = (
            vt[hd * V_HEAD_DIM:(hd + 1) * V_HEAD_DIM].astype(BF16))
        vt_ref[0, hd * V_AUG + V_HEAD_DIM:(hd + 1) * V_AUG, :] = ones

    kk = jnp.dot(ckvn.astype(BF16), wk_ref[...], preferred_element_type=F32)
    kpe = z[:, c0 + KV_LORA_RANK:]
    kg = kg_ref[...]
    pe_ssq = jnp.sum(kpe * kpe, axis=-1, keepdims=True)
    pg = kpe * kg
    pe_rot = (pg * cos_ref[...] + pltpu.roll(pg, HEAD_PAD - half, 1) * sa_ref[...]
              + pltpu.roll(pg, half, 1) * sb_ref[...])
    for hd in range(N_HEADS):
        t = kk[:, hd * HEAD_PAD:(hd + 1) * HEAD_PAD]
        ssq = jnp.sum(t * t, axis=-1, keepdims=True) + pe_ssq
        rn = lax.rsqrt(ssq * (1.0 / QK_HEAD_DIM) + EPS)
        k_ref[0, :, hd * HEAD_PAD:(hd + 1) * HEAD_PAD] = ((t * kg + pe_rot) * rn).astype(QK_DTYPE)


def _inproj(x, b_off, b, mod, p, ts):
    _, s, d = x.shape
    qk_w = N_HEADS * HEAD_PAD
    half = QK_ROPE_DIM // 2
    const = lambda shape: pl.BlockSpec(shape, lambda bi, i: (0,) * len(shape))
    row_tab = pl.BlockSpec((ts, HEAD_PAD), lambda bi, i: (i, 0))
    col_tab = pl.BlockSpec((half, ts), lambda bi, i: (0, i))
    return pl.pallas_call(
        _inproj_kernel,
        grid=(b, s // ts),
        in_specs=[pl.BlockSpec((1, ts, d), lambda bi, i: (bi + b_off, i, 0)),
                  pl.BlockSpec((1, 6, d), lambda bi, i: (bi, 0, 0)),
                  const((1, d)), const((d, d)), const((1, Q_LORA_RANK)),
                  const((qk_w, Q_LORA_RANK)), const((1, KV_LORA_RANK)),
                  const((KV_LORA_RANK, qk_w)), const((N_HEADS * V_HEAD_DIM, KV_LORA_RANK)),
                  const((HEAD_PAD, LANES)), const((1, HEAD_PAD)),
                  row_tab, row_tab, row_tab, col_tab, col_tab],
        out_specs=[pl.BlockSpec((1, ts, POOL_WIDTH), lambda bi, i: (bi, i, 0)),
                   pl.BlockSpec((1, qk_w, ts), lambda bi, i: (bi, 0, i)),
                   pl.BlockSpec((1, ts, qk_w), lambda bi, i: (bi, i, 0)),
                   pl.BlockSpec((1, N_HEADS * V_AUG, ts), lambda bi, i: (bi, 0, i))],
        out_shape=[jax.ShapeDtypeStruct((b, s, POOL_WIDTH), BF16),
                   jax.ShapeDtypeStruct((b, qk_w, s), QK_DTYPE),
                   jax.ShapeDtypeStruct((b, s, qk_w), QK_DTYPE),
                   jax.ShapeDtypeStruct((b, N_HEADS * V_AUG, s), BF16)],
        compiler_params=_cparams(("parallel", "parallel")),
        name="inproj",
    )(x, mod, p["norm1_w"], p["w_in"], p["q_a_norm_w"], p["w_q_t"], p["kv_a_norm_w"], p["w_k"],
      p["w_v_t"], p["q_gain_col"], p["k_norm_w"],
      p["rope_cos"], p["rope_sa"], p["rope_sb"], p["rope_cos_t"], p["rope_sin_t"])


def _attn_kernel(online_ref, qt_ref, k_ref, vt_ref, o_ref, acc_a, acc_b, m_ref, *, tk, tk_online):
    s_len = k_ref.shape[1]
    acc_a[...] = jnp.zeros_like(acc_a)
    acc_b[...] = jnp.zeros_like(acc_b)
    qa = qt_ref[0, :HEAD_PAD, :]
    qb = qt_ref[0, HEAD_PAD:, :]

    @pl.when(online_ref[0] == 0)
    def _():
        def body(c, _):
            off = pl.multiple_of(c * tk, tk)
            ks = k_ref[0, pl.ds(off, tk), :]
            vts = vt_ref[0, :, pl.ds(off, tk)]
            pa = jnp.exp2(jnp.dot(ks[:, :HEAD_PAD], qa, preferred_element_type=F32)).astype(BF16)
            pb = jnp.exp2(jnp.dot(ks[:, HEAD_PAD:], qb, preferred_element_type=F32)).astype(BF16)
            acc_a[...] += jnp.dot(vts[:V_AUG], pa, preferred_element_type=F32)
            acc_b[...] += jnp.dot(vts[V_AUG:], pb, preferred_element_type=F32)
            return 0

        lax.fori_loop(0, s_len // tk, body, 0)

    @pl.when(online_ref[0] != 0)
    def _():
        m_ref[...] = jnp.full_like(m_ref, -jnp.inf)

        def one_head(kh, q, vth, acc, row):
            s = jnp.dot(kh, q, preferred_element_type=F32)
            m_old = m_ref[row:row + 1, :]
            m_new = jnp.maximum(m_old, jnp.max(s, axis=0, keepdims=True))
            p = jnp.exp2(s - m_new).astype(BF16)
            acc[...] = acc[...] * jnp.exp2(m_old - m_new) + jnp.dot(vth, p, preferred_element_type=F32)
            m_ref[row:row + 1, :] = m_new

        def body(c, _):
            off = pl.multiple_of(c * tk_online, tk_online)
            ks = k_ref[0, pl.ds(off, tk_online), :]
            vts = vt_ref[0, :, pl.ds(off, tk_online)]
            one_head(ks[:, :HEAD_PAD], qa, vts[:V_AUG], acc_a, 0)
            one_head(ks[:, HEAD_PAD:], qb, vts[V_AUG:], acc_b, 1)
            return 0

        lax.fori_loop(0, s_len // tk_online, body, 0)

    oa = acc_a[:V_HEAD_DIM] / acc_a[V_HEAD_DIM:V_HEAD_DIM + 1]
    ob = acc_b[:V_HEAD_DIM] / acc_b[V_HEAD_DIM:V_HEAD_DIM + 1]
    o_ref[0] = jnp.concatenate([oa, ob], axis=0).T.astype(BF16)


def _attention(online, qt, k, vt, tq, tk, tk_online):
    b, _, s = qt.shape
    return pl.pallas_call(
        functools.partial(_attn_kernel, tk=tk, tk_online=tk_online),
        grid_spec=pltpu.PrefetchScalarGridSpec(
            num_scalar_prefetch=1,
            grid=(b, N_HEADS // 2, s // tq),
            in_specs=[pl.BlockSpec((1, 2 * HEAD_PAD, tq), lambda bi, j, i, on: (bi, j, i)),
                      pl.BlockSpec((1, s, 2 * HEAD_PAD), lambda bi, j, i, on: (bi, 0, j)),
                      pl.BlockSpec((1, 2 * V_AUG, s), lambda bi, j, i, on: (bi, j, 0))],
            out_specs=pl.BlockSpec((1, tq, 2 * V_HEAD_DIM), lambda bi, j, i, on: (bi, i, j)),
            scratch_shapes=[pltpu.VMEM((V_AUG, tq), F32), pltpu.VMEM((V_AUG, tq), F32),
                            pltpu.VMEM((8, tq), F32)]),
        out_shape=jax.ShapeDtypeStruct((b, s, N_HEADS * V_HEAD_DIM), BF16),
        compiler_params=_cparams(("parallel", "parallel", "arbitrary")),
        name="attn",
    )(online, qt, k, vt)


def _postmix_kernel(u_ref, up_ref, un_ref, a_ref, x_ref, mod_ref, pw_ref, ps_ref, wo_ref, n2w_ref,
                    wrh_ref, wrl_ref, x1_ref, h2_ref, lg_ref, *, seq):
    i = pl.program_id(1)
    ts = u_ref.shape[1]
    ext_rows = ts + 2 * HALO
    ext = jnp.concatenate([up_ref[0], u_ref[0], un_ref[0]], axis=0).astype(F32)
    pos = i * ts - HALO + lax.broadcasted_iota(jnp.int32, (ext_rows, 1), 0)
    ext = jnp.where((pos >= 0) & (pos < seq), ext, 0.0)
    p = i * ts + lax.broadcasted_iota(jnp.int32, (ts, 1), 0)

    outs = []
    for g, w in enumerate(POOL_WINDOWS):
        left = w // 2
        right = w - 1 - left
        t = ext[:, g * POOL_GROUP:(g + 1) * POOL_GROUP]
        step = 1
        while step < w:
            t = t + pltpu.roll(t, ext_rows - step, 0)
            step *= 2
        win = pltpu.roll(t, left, 0)[HALO:HALO + ts]
        cnt = (jnp.minimum(p + right + 1, seq) - jnp.maximum(p - left, 0)).astype(F32)
        d = win * (1.0 / cnt) - ext[HALO:HALO + ts, g * POOL_GROUP:(g + 1) * POOL_GROUP]
        outs.append(jnp.dot(d.astype(BF16), pw_ref[g], preferred_element_type=F32))
    pool = (jnp.concatenate(outs, axis=-1) * ps_ref[...]).astype(BF16)

    mix = (jnp.dot(pool, wo_ref[:POOL_WIDTH, :], preferred_element_type=F32)
           + jnp.dot(a_ref[0], wo_ref[POOL_WIDTH:, :], preferred_element_type=F32))
    x1 = x_ref[0] + mod_ref[0, 2:3, :] * mix
    x1_ref[0] = x1
    r = lax.rsqrt(jnp.mean(x1 * x1, axis=-1, keepdims=True) + EPS)
    gain2 = n2w_ref[...] * (1.0 + mod_ref[0, 4:5, :])
    h2 = x1 * r * gain2 + mod_ref[0, 3:4, :]
    h2_ref[0] = _pack_rows(h2)
    hi = h2.astype(BF16)
    lo = (h2 - hi.astype(F32)).astype(BF16)
    nt = (((1,), (1,)), ((), ()))
    lg_ref[...] = (lax.dot_general(wrh_ref[...], hi, nt, preferred_element_type=F32)
                   + lax.dot_general(wrl_ref[...], hi, nt, preferred_element_type=F32)
                   + lax.dot_general(wrh_ref[...], lo, nt, preferred_element_type=F32))


def _postmix(u, attn, x, b_off, mod, p, ts):
    _, s, d = x.shape
    b = u.shape[0]
    nt = s // ts
    hb = ts // HALO
    const = lambda shape: pl.BlockSpec(shape, lambda bi, i: (0,) * len(shape))
    return pl.pallas_call(
        functools.partial(_postmix_kernel, seq=s),
        grid=(b, nt),
        in_specs=[pl.BlockSpec((1, ts, POOL_WIDTH), lambda bi, i: (bi, i, 0)),
                  pl.BlockSpec((1, HALO, POOL_WIDTH), lambda bi, i: (bi, jnp.maximum(i * hb - 1, 0), 0)),
                  pl.BlockSpec((1, HALO, POOL_WIDTH),
                               lambda bi, i: (bi, jnp.minimum((i + 1) * hb, s // HALO - 1), 0)),
                  pl.BlockSpec((1, ts, POOL_WIDTH), lambda bi, i: (bi, i, 0)),
                  pl.BlockSpec((1, ts, d), lambda bi, i: (bi + b_off, i, 0)),
                  pl.BlockSpec((1, 6, d), lambda bi, i: (bi, 0, 0)),
                  const((len(POOL_WINDOWS), POOL_GROUP, POOL_GROUP)), const((1, POOL_WIDTH)),
                  const((d, d)), const((1, d)), const((N_EXPERTS, d)), const((N_EXPERTS, d))],
        out_specs=[pl.BlockSpec((1, ts, d), lambda bi, i: (bi, i, 0)),
                   pl.BlockSpec((1, ts, PACK_W), lambda bi, i: (bi, i, 0)),
                   pl.BlockSpec((N_EXPERTS, ts), lambda bi, i: (0, bi * nt + i))],
        out_shape=[jax.ShapeDtypeStruct((b, s, d), F32),
                   jax.ShapeDtypeStruct((b, s, PACK_W), jnp.int32),
                   jax.ShapeDtypeStruct((N_EXPERTS, b * s), F32)],
        compiler_params=_cparams(("parallel", "parallel")),
        name="postmix",
    )(u, u, u, attn, x, mod, p["pool_w"], p["pool_scale"], p["w_o"], p["norm2_w"],
      p["w_router_hi"], p["w_router_lo"])


def _router_kernel(lg_ref, bias_ref, tri_ref, idx_ref, wts_ref, pos_ref, cnt_ref, carry_ref):
    @pl.when(pl.program_id(0) == 0)
    def _():
        carry_ref[...] = jnp.zeros_like(carry_ref)

    ts = lg_ref.shape[1]
    gsz = N_EXPERTS // N_GROUPS
    ninf = -jnp.inf
    scores = 1.0 / (1.0 + jnp.exp(-lg_ref[...]))
    choice = scores + bias_ref[...]
    sub = lax.broadcasted_iota(jnp.int32, (gsz, ts), 0)

    gs_rows = []
    for g in range(N_GROUPS):
        grp = choice[g * gsz:(g + 1) * gsz]
        m1 = jnp.max(grp, axis=0, keepdims=True)
        i1 = jnp.min(jnp.where(grp == m1, sub, gsz), axis=0, keepdims=True)
        m2 = jnp.max(jnp.where(sub == i1, ninf, grp), axis=0, keepdims=True)
        gs_rows.append(m1 + m2)
    gs = jnp.concatenate(gs_rows, axis=0)

    rank = jnp.zeros((N_GROUPS, ts), jnp.int32)
    for g in range(N_GROUPS):
        row = gs[g:g + 1]
        beats = (row > gs) | ((row == gs) & (sub > g))
        rank = rank + beats.astype(jnp.int32)
    gsel = rank < TOPK_GROUPS

    masked = jnp.concatenate(
        [jnp.where(gsel[g:g + 1], choice[g * gsz:(g + 1) * gsz], ninf) for g in range(N_GROUPS)],
        axis=0)
    eio = lax.broadcasted_iota(jnp.int32, (N_EXPERTS, ts), 0)
    idx_rows, w_rows, hits = [], [], []
    for _ in range(TOP_K):
        m = jnp.max(masked, axis=0, keepdims=True)
        i = jnp.min(jnp.where(masked == m, eio, N_EXPERTS), axis=0, keepdims=True)
        hit = eio == i
        w_rows.append(jnp.sum(jnp.where(hit, scores, 0.0), axis=0, keepdims=True))
        masked = jnp.where(hit, ninf, masked)
        idx_rows.append(i)
        hits.append(hit)

    wsum = functools.reduce(lambda a, c: a + c, w_rows)
    pad_i = [jnp.zeros((1, ts), jnp.int32)] * (TOPK_PAD - TOP_K)
    pad_f = [jnp.zeros((1, ts), F32)] * (TOPK_PAD - TOP_K)
    idx_ref[...] = jnp.concatenate(idx_rows + pad_i, axis=0)
    wts_ref[...] = jnp.concatenate([w / wsum * ROUTED_SCALE for w in w_rows] + pad_f, axis=0)

    sel = functools.reduce(lambda a, c: a | c, hits)
    onehot = jnp.where(sel, 1.0, 0.0).astype(BF16)
    run = jnp.dot(onehot, tri_ref[...], preferred_element_type=F32) + carry_ref[:, 0:1]
    pos_rows = [jnp.sum(jnp.where(h, run - 1.0, 0.0), axis=0, keepdims=True).astype(jnp.int32)
                for h in hits]
    pos_ref[...] = jnp.concatenate(pos_rows + pad_i, axis=0)
    total = run[:, ts - 1:ts]
    carry_ref[...] = jnp.broadcast_to(total, carry_ref.shape)
    cnt_ref[...] = jnp.broadcast_to(total, cnt_ref.shape)


def _router(logits_t, router_bias, ts):
    t = logits_t.shape[1]
    tri = jnp.triu(jnp.ones((ts, ts), BF16))
    tok = pl.BlockSpec((TOPK_PAD, ts), lambda i: (0, i))
    return pl.pallas_call(
        _router_kernel,
        grid=(t // ts,),
        in_specs=[pl.BlockSpec((N_EXPERTS, ts), lambda i: (0, i)),
                  pl.BlockSpec((N_EXPERTS, 1), lambda i: (0, 0)),
                  pl.BlockSpec((ts, ts), lambda i: (0, 0))],
        out_specs=[tok, tok, tok, pl.BlockSpec((N_EXPERTS, LANES), lambda i: (0, 0))],
        out_shape=[jax.ShapeDtypeStruct((TOPK_PAD, t), jnp.int32),
                   jax.ShapeDtypeStruct((TOPK_PAD, t), F32),
                   jax.ShapeDtypeStruct((TOPK_PAD, t), jnp.int32),
                   jax.ShapeDtypeStruct((N_EXPERTS, LANES), F32)],
        scratch_shapes=[pltpu.VMEM((N_EXPERTS, LANES), F32)],
        compiler_params=_cparams(("arbitrary",)),
        name="router",
    )(logits_t, router_bias.reshape(N_EXPERTS, 1), tri)


def _dest_kernel(idx_ref, pos_ref, start_ref, dest_ref):
    ts = idx_ref.shape[1]
    eio = lax.broadcasted_iota(jnp.int32, (N_EXPERTS, ts), 0)
    start = start_ref[...]
    rows = [pos_ref[k:k + 1, :]
            + jnp.sum(jnp.where(eio == idx_ref[k:k + 1, :], start, 0), axis=0, keepdims=True)
            for k in range(TOPK_PAD)]
    dest_ref[...] = jnp.concatenate(rows, axis=0)


def _dest_rows(idx, pos, pad_start, ts):
    t = idx.shape[1]
    tok = pl.BlockSpec((TOPK_PAD, ts), lambda i: (0, i))
    return pl.pallas_call(
        _dest_kernel,
        grid=(t // ts,),
        in_specs=[tok, tok, pl.BlockSpec((N_EXPERTS, 1), lambda i: (0, 0))],
        out_specs=tok,
        out_shape=jax.ShapeDtypeStruct((TOPK_PAD, t), jnp.int32),
        compiler_params=_cparams(("parallel",)),
        name="dest_rows",
    )(idx, pos, pad_start.reshape(N_EXPERTS, 1))


def _pack_rows(x):
    bits = lax.bitcast_convert_type(x.astype(BF16).astype(F32), jnp.int32)
    return bits[:, :PACK_W] | lax.shift_right_logical(bits[:, PACK_W:], 16)


def _unpack_rows(words):
    hi = lax.bitcast_convert_type(words & jnp.int32(-65536), F32)
    lo = lax.bitcast_convert_type(lax.shift_left(words, 16), F32)
    return jnp.concatenate([hi, lo], axis=1)


def _sc_mesh():
    return plsc.VectorSubcoreMesh(core_axis_name="core", subcore_axis_name="subcore")


def _sc_worker_base(rows_per_worker):
    return (lax.axis_index("subcore") * SC_CORES + lax.axis_index("core")) * rows_per_worker


def _sc_scatter_rows(x, dest, n_rows):
    t, w = x.shape
    per_worker = t // SC_WORKERS
    assert per_worker * SC_WORKERS == t and per_worker % SC_CHUNK == 0

    @functools.partial(
        pl.kernel, out_type=jax.ShapeDtypeStruct((n_rows, w), x.dtype), mesh=_sc_mesh(),
        scratch_types=[pltpu.VMEM((TOPK_PAD, SC_CHUNK), jnp.int32), pltpu.VMEM((SC_CHUNK, w), x.dtype),
                       pltpu.SemaphoreType.DMA])
    def scatter(x_hbm, dest_hbm, out_hbm, idx_v, rows_v, sem):
        base = _sc_worker_base(per_worker)

        @pl.loop(0, per_worker // SC_CHUNK)
        def _(c):
            off = pl.multiple_of(base + c * SC_CHUNK, SC_CHUNK)
            pltpu.sync_copy(dest_hbm.at[:, pl.ds(off, SC_CHUNK)], idx_v)
            pltpu.sync_copy(x_hbm.at[pl.ds(off, SC_CHUNK)], rows_v)
            copies = [pltpu.async_copy(rows_v, out_hbm.at[idx_v.at[k]], sem) for k in range(TOP_K)]
            for cp in copies:
                cp.wait()

    return scatter(x, dest)


def _sc_gather_rows(table, idx):
    m = idx.shape[0]
    w = table.shape[1]
    per_worker = m // SC_WORKERS
    assert per_worker * SC_WORKERS == m and per_worker % SC_CHUNK == 0

    @functools.partial(
        pl.kernel, out_type=jax.ShapeDtypeStruct((m, w), table.dtype), mesh=_sc_mesh(),
        scratch_types=[pltpu.VMEM((SC_CHUNK,), jnp.int32), pltpu.VMEM((SC_CHUNK, w), table.dtype),
                       pltpu.SemaphoreType.DMA])
    def gather(table_hbm, idx_hbm, out_hbm, idx_v, rows_v, sem):
        base = _sc_worker_base(per_worker)

        @pl.loop(0, per_worker // SC_CHUNK)
        def _(c):
            off = pl.multiple_of(base + c * SC_CHUNK, SC_CHUNK)
            pltpu.sync_copy(idx_hbm.at[pl.ds(off, SC_CHUNK)], idx_v)
            pltpu.async_copy(table_hbm.at[idx_v], rows_v, sem).wait()
            pltpu.sync_copy(rows_v, out_hbm.at[pl.ds(off, SC_CHUNK)])

    return gather(table, idx)


def _experts_kernel(blk_exp_ref, n_used_ref, xs_ref, wg_ref, wu_ref, wd_ref, ys_ref):
    @pl.when(pl.program_id(0) < n_used_ref[0])
    def _():
        xb = _unpack_rows(xs_ref[...]).astype(BF16)
        g = jnp.dot(xb, wg_ref[0].astype(BF16), preferred_element_type=F32)
        u = jnp.dot(xb, wu_ref[0].astype(BF16), preferred_element_type=F32)
        hb = (_silu(g) * u).astype(BF16)
        ys_ref[...] = _pack_rows(jnp.dot(hb, wd_ref[0].astype(BF16), preferred_element_type=F32))


def _row_block(t):
    per_expert = t * TOP_K // N_EXPERTS
    return 1024 if per_expert >= 4 * 1024 else (512 if per_expert >= 4 * 512 else 256)


def _experts(blk_exp, n_used, xs, w_gate, w_up, w_down, row_block):
    n_rows, w = xs.shape
    d = D_MODEL
    n_blocks = n_rows // row_block
    row = lambda i, be, nu: (jnp.minimum(i, nu[0] - 1), 0)
    wsel = lambda i, be, nu: (be[i], 0, 0)
    return pl.pallas_call(
        _experts_kernel,
        grid_spec=pltpu.PrefetchScalarGridSpec(
            num_scalar_prefetch=2,
            grid=(n_blocks,),
            in_specs=[pl.BlockSpec((row_block, w), row),
                      pl.BlockSpec((1, d, EXPERT_FF), wsel),
                      pl.BlockSpec((1, d, EXPERT_FF), wsel),
                      pl.BlockSpec((1, EXPERT_FF, d), wsel)],
            out_specs=pl.BlockSpec((row_block, w), row)),
        out_shape=jax.ShapeDtypeStruct((n_rows, w), jnp.int32),
        compiler_params=_cparams(("arbitrary",)),
        name="experts",
    )(blk_exp, n_used, xs, w_gate, w_up, w_down)


def _combine_kernel(g_ref, w_ref, h2_ref, x1_ref, mod_ref, wsg_ref, wsu_ref, wsd_ref, out_ref):
    hb = _unpack_rows(h2_ref[...]).astype(BF16)
    g = jnp.dot(hb, wsg_ref[...], preferred_element_type=F32)
    u = jnp.dot(hb, wsu_ref[...], preferred_element_type=F32)
    acc = jnp.dot((_silu(g) * u).astype(BF16), wsd_ref[...], preferred_element_type=F32)
    w = w_ref[...]
    for k in range(TOP_K):
        acc = acc + _unpack_rows(g_ref[k]) * w[:, k:k + 1]
    out_ref[...] = x1_ref[...] + mod_ref[0, 5:6, :] * acc


def _combine_into_kernel(g_ref, w_ref, h2_ref, x1_ref, mod_ref, wsg_ref, wsu_ref, wsd_ref, prev_ref,
                         out_ref):
    del prev_ref
    _combine_kernel(g_ref, w_ref, h2_ref, x1_ref, mod_ref, wsg_ref, wsu_ref, wsd_ref, out_ref)


def _combine(gathered, wts_t, h2p, x1, mod, p, tt, seq, out_rows, row_off, prev):
    t, d = x1.shape
    per_seq = seq // tt
    off = row_off // tt
    tok = pl.BlockSpec((tt, d), lambda i: (i, 0))
    const = lambda shape: pl.BlockSpec(shape, lambda i: (0,) * len(shape))
    in_specs = [pl.BlockSpec((TOP_K, tt, PACK_W), lambda i: (0, i, 0)),
                pl.BlockSpec((tt, TOPK_PAD), lambda i: (i, 0)),
                pl.BlockSpec((tt, PACK_W), lambda i: (i, 0)),
                tok,
                pl.BlockSpec((1, 6, d), lambda i: (i // per_seq, 0, 0)),
                const((d, SHARED_FF)), const((d, SHARED_FF)), const((SHARED_FF, d))]
    args = [gathered, wts_t, h2p, x1, mod, p["ws_gate"], p["ws_up"], p["ws_down"]]
    aliases = {}
    body = _combine_kernel
    if prev is not None:
        in_specs.append(pl.BlockSpec(memory_space=pl.ANY))
        aliases = {len(args): 0}
        args.append(prev)
        body = _combine_into_kernel
    return pl.pallas_call(
        body,
        grid=(t // tt,),
        in_specs=in_specs,
        out_specs=pl.BlockSpec((tt, d), lambda i: (i + off, 0)),
        out_shape=jax.ShapeDtypeStruct((out_rows, d), F32),
        input_output_aliases=aliases,
        compiler_params=_cparams(("parallel",)),
        name="combine",
    )(*args)


def _rope_tables(s):
    half = QK_ROPE_DIM // 2
    inv_freq = ROPE_THETA ** (-jnp.arange(half, dtype=F32) / half)
    ang = jnp.arange(s, dtype=F32)[:, None] * inv_freq[None, :]
    cos, sin = jnp.cos(ang), jnp.sin(ang)
    z = lambda n: jnp.zeros((s, n), F32)
    tab_cos = jnp.concatenate([jnp.ones((s, QK_NOPE_DIM), F32), cos, cos, z(HEAD_PAD - QK_HEAD_DIM)], 1)
    tab_sa = jnp.concatenate([z(QK_NOPE_DIM), -sin, z(HEAD_PAD - QK_NOPE_DIM - half)], 1)
    tab_sb = jnp.concatenate([z(QK_NOPE_DIM + half), sin, z(HEAD_PAD - QK_HEAD_DIM)], 1)
    return dict(rope_cos=tab_cos, rope_sa=tab_sa, rope_sb=tab_sb, rope_cos_t=cos.T, rope_sin_t=sin.T)


def _prep_weights(norm1_w, w_in, pool_w, pool_scale, q_a_norm_w, w_q_b, kv_a_norm_w, w_kv_b,
                  q_norm_w, k_norm_w, w_o, norm2_w, w_router, w_gate, w_up, w_down,
                  ws_gate, ws_up, ws_down):
    d = D_MODEL
    c0 = POOL_WIDTH + Q_LORA_RANK + KV_LORA_RANK
    pad_h = HEAD_PAD - QK_HEAD_DIM
    w_in_p = jnp.concatenate(
        [w_in[:, :c0], jnp.zeros((d, QK_NOPE_DIM), F32), w_in[:, c0:], jnp.zeros((d, pad_h), F32)], 1)
    w_q = jnp.pad(w_q_b.reshape(Q_LORA_RANK, N_HEADS, QK_HEAD_DIM), ((0, 0), (0, 0), (0, pad_h)))
    kv = w_kv_b.reshape(KV_LORA_RANK, N_HEADS, QK_NOPE_DIM + V_HEAD_DIM)
    w_k = jnp.pad(kv[:, :, :QK_NOPE_DIM], ((0, 0), (0, 0), (0, HEAD_PAD - QK_NOPE_DIM)))
    w_v = kv[:, :, QK_NOPE_DIM:]
    w_r_t = w_router.T
    w_r_hi = w_r_t.astype(BF16)
    w_r_lo = (w_r_t - w_r_hi.astype(F32)).astype(BF16)
    q_gain = q_norm_w * (QK_HEAD_DIM ** -0.5 * LOG2E)
    bound = QK_HEAD_DIM * jnp.max(jnp.abs(q_gain)) * jnp.max(jnp.abs(k_norm_w)) * 1.02 + 0.25
    return dict(
        score_bound=bound,
        norm1_w=norm1_w.reshape(1, d), w_in=w_in_p.astype(BF16),
        q_a_norm_w=q_a_norm_w.reshape(1, -1), w_q_t=w_q.reshape(Q_LORA_RANK, -1).T.astype(BF16),
        kv_a_norm_w=kv_a_norm_w.reshape(1, -1), w_k=w_k.reshape(KV_LORA_RANK, -1).astype(BF16),
        w_v_t=w_v.reshape(KV_LORA_RANK, -1).T.astype(BF16),
        q_gain_col=jnp.broadcast_to(jnp.pad(q_gain, (0, pad_h))[:, None], (HEAD_PAD, LANES)),
        k_norm_w=jnp.pad(k_norm_w, (0, pad_h)).reshape(1, HEAD_PAD),
        pool_w=pool_w.astype(BF16), pool_scale=pool_scale.reshape(1, -1), w_o=w_o.astype(BF16),
        norm2_w=norm2_w.reshape(1, d), w_router_hi=w_r_hi, w_router_lo=w_r_lo,
        w_gate=w_gate, w_up=w_up, w_down=w_down,
        ws_gate=ws_gate.astype(BF16), ws_up=ws_up.astype(BF16), ws_down=ws_down.astype(BF16))


def _tile(n, pref):
    return pref if n % pref == 0 else n


def _mixer_and_routing(x, b_off, mod, router_bias, p):
    _, s, d = x.shape
    b = mod.shape[0]
    t = b * s
    ts = _tile(s, 512)
    p = dict(p, **_rope_tables(s))

    u, qt, k, vt = _inproj(x, b_off, b, mod, p, ts)
    online = (p["score_bound"] > MAX_UNSHIFTED_SCORE).astype(jnp.int32).reshape(1)
    attn = _attention(online, qt, k, vt, _tile(s, 1024), _tile(s, 2048), _tile(s, 512))
    x1, h2p, logits_t = _postmix(u, attn, x, b_off, mod, p, ts)
    tr = _tile(t, 512)
    idx, wts, pos, cnt = _router(logits_t, router_bias, tr)

    rb = _row_block(t)
    counts = cnt[:, 0].astype(jnp.int32)
    padded = (counts + rb - 1) // rb * rb
    pad_end = jnp.cumsum(padded)
    dest = _dest_rows(idx, pos, pad_end - padded, _tile(t, 2048))
    n_blocks = -(-t * TOP_K // rb) + N_EXPERTS
    blk_row = jnp.arange(n_blocks, dtype=jnp.int32)[:, None] * rb
    blk_exp = jnp.minimum(jnp.sum((pad_end[None, :] <= blk_row).astype(jnp.int32), axis=1),
                          N_EXPERTS - 1)
    n_used = (pad_end[-1:] // rb).astype(jnp.int32)
    h2f = h2p.reshape(t, PACK_W)
    xs = _sc_scatter_rows(h2f, dest, n_blocks * rb)
    return dict(xs=xs, dest=dest, blk_exp=blk_exp, n_used=n_used, wts_t=wts.T, h2f=h2f,
                x1=x1.reshape(t, d), mod=mod, shape=(b, s, d), row_block=rb)


def _gather_expert_rows(st, ys):
    b, s, _ = st["shape"]
    t = b * s
    return _sc_gather_rows(ys, st["dest"][:TOP_K].reshape(TOP_K * t)).reshape(TOP_K, t, PACK_W)


def _combine_group(st, gathered, p, out_rows=None, row_off=0, prev=None):
    b, s, _ = st["shape"]
    out_rows = b * s if out_rows is None else out_rows
    return _combine(gathered, st["wts_t"], st["h2f"], st["x1"], st["mod"], p, _tile(s, 512), s,
                    out_rows, row_off, prev)


def kernel(x_prompt, x_sample, c_prompt, c_sample, w_ada, b_ada, norm1_w, w_in, pool_w, pool_scale,
           q_a_norm_w, w_q_b, kv_a_norm_w, w_kv_b, q_norm_w, k_norm_w, w_o, norm2_w, w_router,
           router_bias, w_gate, w_up, w_down, ws_gate, ws_up, ws_down):
    assert w_ada.shape[0] == 1, "single-layer encoder"
    p = _prep_weights(norm1_w[0], w_in[0], pool_w[0], pool_scale[0], q_a_norm_w[0], w_q_b[0],
                      kv_a_norm_w[0], w_kv_b[0], q_norm_w[0], k_norm_w[0], w_o[0], norm2_w[0],
                      w_router[0], w_gate[0], w_up[0], w_down[0], ws_gate[0], ws_up[0], ws_down[0])
    nb = x_prompt.shape[0]
    c = jnp.concatenate([c_prompt, c_sample], axis=0).astype(F32)
    mod = _adaln(c, w_ada[0], b_ada[0]).reshape(c.shape[0], 6, D_MODEL)
    experts = lambda st, xs: _experts(st["blk_exp"], st["n_used"], xs, p["w_gate"], p["w_up"],
                                      p["w_down"], st["row_block"])

    after = lax.optimization_barrier
    h = nb // 2
    t_prompt = nb * x_prompt.shape[1]
    sa = _mixer_and_routing(x_prompt, 0, mod[:h], router_bias[0], p)
    sa["dest"], x_sample = after((sa["dest"], x_sample))
    ss = _mixer_and_routing(x_sample, 0, mod[nb:], router_bias[0], p)
    xs_a, ss["dest"] = after((sa["xs"], ss["dest"]))
    ys_a = experts(sa, xs_a)
    ys_a, x_rest = after((ys_a, x_prompt))
    sb = _mixer_and_routing(x_rest, h, mod[h:nb], router_bias[0], p)
    g_a = _gather_expert_rows(sa, ys_a)
    xs_s, sb["dest"] = after((ss["xs"], sb["dest"]))
    ys_s = experts(ss, xs_s)
    g_a, ys_s = after((g_a, ys_s))
    y_prompt = _combine_group(sa, g_a, p, out_rows=t_prompt)
    g_s = _gather_expert_rows(ss, ys_s)
    xs_b, y_prompt = after((sb["xs"], y_prompt))
    ys_b = experts(sb, xs_b)
    g_s, ys_b = after((g_s, ys_b))
    y_sample = _combine_group(ss, g_s, p)
    g_b = _gather_expert_rows(sb, ys_b)
    g_b, y_sample = after((g_b, y_sample))
    y_prompt = _combine_group(sb, g_b, p, out_rows=t_prompt, row_off=h * x_prompt.shape[1], prev=y_prompt)
    return (y_prompt.reshape(x_prompt.shape), y_sample.reshape(x_sample.shape))
```

```python
import functools

import jax
import jax.numpy as jnp
from jax import lax
from jax.experimental import pallas as pl
from jax.experimental.pallas import tpu as pltpu
from jax.experimental.pallas import tpu_sc as plsc

D_MODEL = 1024
POOL_WIDTH = 512
POOL_WINDOWS = (2, 4, 8, 16)
POOL_GROUP = 128
N_HEADS = 8
V_HEAD_DIM = 64
QK_NOPE_DIM = 64
QK_ROPE_DIM = 32
QK_HEAD_DIM = 96
Q_LORA_RANK = 256
KV_LORA_RANK = 128
ROPE_THETA = 10000.0
N_EXPERTS = 64
TOP_K = 6
N_GROUPS = 8
TOPK_GROUPS = 4
EXPERT_FF = 256
SHARED_FF = 256
ROUTED_SCALE = 2.5
EPS = 1e-6

LANES = 128
HEAD_PAD = 128
HALO = 16
TOPK_PAD = 8
V_AUG = 80
MAX_UNSHIFTED_SCORE = 40.0
LOG2E = 1.4426950408889634
PACK_W = D_MODEL // 2
SC_CORES = 2
SC_WORKERS = 32
SC_CHUNK = 128
VMEM_LIMIT = 48 * 1024 * 1024

F32 = jnp.float32
BF16 = jnp.bfloat16
QK_DTYPE = jnp.bfloat16


def _cparams(sem):
    return pltpu.CompilerParams(dimension_semantics=sem, vmem_limit_bytes=VMEM_LIMIT)


def _silu(x):
    return x * (1.0 / (1.0 + jnp.exp(-x)))


def _adaln_kernel(c_ref, w_ref, b_ref, o_ref):
    c = c_ref[...]
    o_ref[...] = jnp.dot(_silu(c), w_ref[...], preferred_element_type=F32,
                         precision=lax.Precision.HIGHEST) + b_ref[...]


def _adaln(c, w_ada, b_ada):
    nb, d = c.shape
    n = w_ada.shape[1]
    tn = 1536
    return pl.pallas_call(
        _adaln_kernel,
        grid=(n // tn,),
        in_specs=[pl.BlockSpec((nb, d), lambda j: (0, 0)),
                  pl.BlockSpec((d, tn), lambda j: (0, j)),
                  pl.BlockSpec((1, tn), lambda j: (0, j))],
        out_specs=pl.BlockSpec((nb, tn), lambda j: (0, j)),
        out_shape=jax.ShapeDtypeStruct((nb, n), F32),
        compiler_params=_cparams(("arbitrary",)),
        name="adaln",
    )(c, w_ada, b_ada.reshape(1, n))


def _inproj_kernel(x_ref, mod_ref, n1w_ref, win_ref, qan_ref, wqt_ref, kvan_ref, wk_ref, wvt_ref,
                   qg_ref, kg_ref, cos_ref, sa_ref, sb_ref, cost_ref, sint_ref,
                   u_ref, qt_ref, k_ref, vt_ref):
    ts = x_ref.shape[1]
    half = QK_ROPE_DIM // 2
    x = x_ref[0]
    shift1 = mod_ref[0, 0:1, :]
    gain1 = n1w_ref[...] * (1.0 + mod_ref[0, 1:2, :])
    r = lax.rsqrt(jnp.mean(x * x, axis=-1, keepdims=True) + EPS)
    h = x * r * gain1 + shift1
    z = jnp.dot(h.astype(BF16), win_ref[...], preferred_element_type=F32)
    u_ref[0] = z[:, :POOL_WIDTH].astype(BF16)

    cq = z[:, POOL_WIDTH:POOL_WIDTH + Q_LORA_RANK]
    cqn = cq * lax.rsqrt(jnp.mean(cq * cq, axis=-1, keepdims=True) + EPS) * qan_ref[...]
    qt = jnp.dot(wqt_ref[...], cqn.T.astype(BF16), preferred_element_type=F32)
    reps = ts // LANES
    qg = jnp.concatenate([qg_ref[...]] * reps, axis=1)
    cost = cost_ref[...]
    sint = sint_ref[...]
    spare = jnp.zeros((HEAD_PAD - QK_HEAD_DIM, ts), F32)
    for hd in range(N_HEADS):
        t = qt[hd * HEAD_PAD:(hd + 1) * HEAD_PAD]
        rn = lax.rsqrt(jnp.sum(t * t, axis=0, keepdims=True) * (1.0 / QK_HEAD_DIM) + EPS)
        tn = t * rn * qg
        t1 = tn[QK_NOPE_DIM:QK_NOPE_DIM + half]
        t2 = tn[QK_NOPE_DIM + half:QK_HEAD_DIM]
        out = jnp.concatenate([tn[:QK_NOPE_DIM], t1 * cost - t2 * sint, t1 * sint + t2 * cost, spare],
                              axis=0)
        qt_ref[0, hd * HEAD_PAD:(hd + 1) * HEAD_PAD, :] = out.astype(QK_DTYPE)

    c0 = POOL_WIDTH + Q_LORA_RANK
    ckv = z[:, c0:c0 + KV_LORA_RANK]
    ckvn = ckv * lax.rsqrt(jnp.mean(ckv * ckv, axis=-1, keepdims=True) + EPS) * kvan_ref[...]

    vt = jnp.dot(wvt_ref[...], ckvn.T.astype(BF16), preferred_element_type=F32)
    ones = jnp.ones((V_AUG - V_HEAD_DIM, ts), BF16)
    for hd in range(N_HEADS):
        vt_ref[0, hd * V_AUG:hd * V_AUG + V_HEAD_DIM, :] = (
            vt[hd * V_HEAD_DIM:(hd + 1) * V_HEAD_DIM].astype(BF16))
        vt_ref[0, hd * V_AUG + V_HEAD_DIM:(hd + 1) * V_AUG, :] = ones

    kk = jnp.dot(ckvn.astype(BF16), wk_ref[...], preferred_element_type=F32)
    kpe = z[:, c0 + KV_LORA_RANK:]
    kg = kg_ref[...]
    pe_ssq = jnp.sum(kpe * kpe, axis=-1, keepdims=True)
    pg = kpe * kg
    pe_rot = (pg * cos_ref[...] + pltpu.roll(pg, HEAD_PAD - half, 1) * sa_ref[...]
              + pltpu.roll(pg, half, 1) * sb_ref[...])
    for hd in range(N_HEADS):
        t = kk[:, hd * HEAD_PAD:(hd + 1) * HEAD_PAD]
        ssq = jnp.sum(t * t, axis=-1, keepdims=True) + pe_ssq
        rn = lax.rsqrt(ssq * (1.0 / QK_HEAD_DIM) + EPS)
        k_ref[0, :, hd * HEAD_PAD:(hd + 1) * HEAD_PAD] = ((t * kg + pe_rot) * rn).astype(QK_DTYPE)


def _inproj(x, mod, p, ts):
    b, s, d = x.shape
    qk_w = N_HEADS * HEAD_PAD
    half = QK_ROPE_DIM // 2
    const = lambda shape: pl.BlockSpec(shape, lambda bi, i: (0,) * len(shape))
    row_tab = pl.BlockSpec((ts, HEAD_PAD), lambda bi, i: (i, 0))
    col_tab = pl.BlockSpec((half, ts), lambda bi, i: (0, i))
    return pl.pallas_call(
        _inproj_kernel,
        grid=(b, s // ts),
        in_specs=[pl.BlockSpec((1, ts, d), lambda bi, i: (bi, i, 0)),
                  pl.BlockSpec((1, 6, d), lambda bi, i: (bi, 0, 0)),
                  const((1, d)), const((d, d)), const((1, Q_LORA_RANK)),
                  const((qk_w, Q_LORA_RANK)), const((1, KV_LORA_RANK)),
                  const((KV_LORA_RANK, qk_w)), const((N_HEADS * V_HEAD_DIM, KV_LORA_RANK)),
                  const((HEAD_PAD, LANES)), const((1, HEAD_PAD)),
                  row_tab, row_tab, row_tab, col_tab, col_tab],
        out_specs=[pl.BlockSpec((1, ts, POOL_WIDTH), lambda bi, i: (bi, i, 0)),
                   pl.BlockSpec((1, qk_w, ts), lambda bi, i: (bi, 0, i)),
                   pl.BlockSpec((1, ts, qk_w), lambda bi, i: (bi, i, 0)),
                   pl.BlockSpec((1, N_HEADS * V_AUG, ts), lambda bi, i: (bi, 0, i))],
        out_shape=[jax.ShapeDtypeStruct((b, s, POOL_WIDTH), BF16),
                   jax.ShapeDtypeStruct((b, qk_w, s), QK_DTYPE),
                   jax.ShapeDtypeStruct((b, s, qk_w), QK_DTYPE),
                   jax.ShapeDtypeStruct((b, N_HEADS * V_AUG, s), BF16)],
        compiler_params=_cparams(("parallel", "parallel")),
        name="inproj",
    )(x, mod, p["norm1_w"], p["w_in"], p["q_a_norm_w"], p["w_q_t"], p["kv_a_norm_w"], p["w_k"],
      p["w_v_t"], p["q_gain_col"], p["k_norm_w"],
      p["rope_cos"], p["rope_sa"], p["rope_sb"], p["rope_cos_t"], p["rope_sin_t"])


def _attn_kernel(online_ref, qt_ref, k_ref, vt_ref, o_ref, acc_a, acc_b, m_ref, *, tk, tk_online):
    s_len = k_ref.shape[1]
    acc_a[...] = jnp.zeros_like(acc_a)
    acc_b[...] = jnp.zeros_like(acc_b)
    qa = qt_ref[0, :HEAD_PAD, :]
    qb = qt_ref[0, HEAD_PAD:, :]

    @pl.when(online_ref[0] == 0)
    def _():
        def body(c, _):
            off = pl.multiple_of(c * tk, tk)
            ks = k_ref[0, pl.ds(off, tk), :]
            vts = vt_ref[0, :, pl.ds(off, tk)]
            pa = jnp.exp2(jnp.dot(ks[:, :HEAD_PAD], qa, preferred_element_type=F32)).astype(BF16)
            pb = jnp.exp2(jnp.dot(ks[:, HEAD_PAD:], qb, preferred_element_type=F32)).astype(BF16)
            acc_a[...] += jnp.dot(vts[:V_AUG], pa, preferred_element_type=F32)
            acc_b[...] += jnp.dot(vts[V_AUG:], pb, preferred_element_type=F32)
            return 0

        lax.fori_loop(0, s_len // tk, body, 0)

    @pl.when(online_ref[0] != 0)
    def _():
        m_ref[...] = jnp.full_like(m_ref, -jnp.inf)

        def one_head(kh, q, vth, acc, row):
            s = jnp.dot(kh, q, preferred_element_type=F32)
            m_old = m_ref[row:row + 1, :]
            m_new = jnp.maximum(m_old, jnp.max(s, axis=0, keepdims=True))
            p = jnp.exp2(s - m_new).astype(BF16)
            acc[...] = acc[...] * jnp.exp2(m_old - m_new) + jnp.dot(vth, p, preferred_element_type=F32)
            m_ref[row:row + 1, :] = m_new

        def body(c, _):
            off = pl.multiple_of(c * tk_online, tk_online)
            ks = k_ref[0, pl.ds(off, tk_online), :]
            vts = vt_ref[0, :, pl.ds(off, tk_online)]
            one_head(ks[:, :HEAD_PAD], qa, vts[:V_AUG], acc_a, 0)
            one_head(ks[:, HEAD_PAD:], qb, vts[V_AUG:], acc_b, 1)
            return 0

        lax.fori_loop(0, s_len // tk_online, body, 0)

    oa = acc_a[:V_HEAD_DIM] / acc_a[V_HEAD_DIM:V_HEAD_DIM + 1]
    ob = acc_b[:V_HEAD_DIM] / acc_b[V_HEAD_DIM:V_HEAD_DIM + 1]
    o_ref[0] = jnp.concatenate([oa, ob], axis=0).T.astype(BF16)


def _attention(online, qt, k, vt, tq, tk, tk_online):
    b, _, s = qt.shape
    return pl.pallas_call(
        functools.partial(_attn_kernel, tk=tk, tk_online=tk_online),
        grid_spec=pltpu.PrefetchScalarGridSpec(
            num_scalar_prefetch=1,
            grid=(b, N_HEADS // 2, s // tq),
            in_specs=[pl.BlockSpec((1, 2 * HEAD_PAD, tq), lambda bi, j, i, on: (bi, j, i)),
                      pl.BlockSpec((1, s, 2 * HEAD_PAD), lambda bi, j, i, on: (bi, 0, j)),
                      pl.BlockSpec((1, 2 * V_AUG, s), lambda bi, j, i, on: (bi, j, 0))],
            out_specs=pl.BlockSpec((1, tq, 2 * V_HEAD_DIM), lambda bi, j, i, on: (bi, i, j)),
            scratch_shapes=[pltpu.VMEM((V_AUG, tq), F32), pltpu.VMEM((V_AUG, tq), F32),
                            pltpu.VMEM((8, tq), F32)]),
        out_shape=jax.ShapeDtypeStruct((b, s, N_HEADS * V_HEAD_DIM), BF16),
        compiler_params=_cparams(("parallel", "parallel", "arbitrary")),
        name="attn",
    )(online, qt, k, vt)


def _postmix_kernel(u_ref, up_ref, un_ref, a_ref, x_ref, mod_ref, pw_ref, ps_ref, wo_ref, n2w_ref,
                    wrh_ref, wrl_ref, x1_ref, h2_ref, lg_ref, *, seq):
    i = pl.program_id(1)
    ts = u_ref.shape[1]
    ext_rows = ts + 2 * HALO
    ext = jnp.concatenate([up_ref[0], u_ref[0], un_ref[0]], axis=0).astype(F32)
    pos = i * ts - HALO + lax.broadcasted_iota(jnp.int32, (ext_rows, 1), 0)
    ext = jnp.where((pos >= 0) & (pos < seq), ext, 0.0)
    p = i * ts + lax.broadcasted_iota(jnp.int32, (ts, 1), 0)

    outs = []
    for g, w in enumerate(POOL_WINDOWS):
        left = w // 2
        right = w - 1 - left
        t = ext[:, g * POOL_GROUP:(g + 1) * POOL_GROUP]
        step = 1
        while step < w:
            t = t + pltpu.roll(t, ext_rows - step, 0)
            step *= 2
        win = pltpu.roll(t, left, 0)[HALO:HALO + ts]
        cnt = (jnp.minimum(p + right + 1, seq) - jnp.maximum(p - left, 0)).astype(F32)
        d = win * (1.0 / cnt) - ext[HALO:HALO + ts, g * POOL_GROUP:(g + 1) * POOL_GROUP]
        outs.append(jnp.dot(d.astype(BF16), pw_ref[g], preferred_element_type=F32))
    pool = (jnp.concatenate(outs, axis=-1) * ps_ref[...]).astype(BF16)

    mix = (jnp.dot(pool, wo_ref[:POOL_WIDTH, :], preferred_element_type=F32)
           + jnp.dot(a_ref[0], wo_ref[POOL_WIDTH:, :], preferred_element_type=F32))
    x1 = x_ref[0] + mod_ref[0, 2:3, :] * mix
    x1_ref[0] = x1
    r = lax.rsqrt(jnp.mean(x1 * x1, axis=-1, keepdims=True) + EPS)
    gain2 = n2w_ref[...] * (1.0 + mod_ref[0, 4:5, :])
    h2 = x1 * r * gain2 + mod_ref[0, 3:4, :]
    h2_ref[0] = _pack_rows(h2)
    hi = h2.astype(BF16)
    lo = (h2 - hi.astype(F32)).astype(BF16)
    nt = (((1,), (1,)), ((), ()))
    lg_ref[...] = (lax.dot_general(wrh_ref[...], hi, nt, preferred_element_type=F32)
                   + lax.dot_general(wrl_ref[...], hi, nt, preferred_element_type=F32)
                   + lax.dot_general(wrh_ref[...], lo, nt, preferred_element_type=F32))


def _postmix(u, attn, x, mod, p, ts):
    b, s, d = x.shape
    nt = s // ts
    hb = ts // HALO
    const = lambda shape: pl.BlockSpec(shape, lambda bi, i: (0,) * len(shape))
    return pl.pallas_call(
        functools.partial(_postmix_kernel, seq=s),
        grid=(b, nt),
        in_specs=[pl.BlockSpec((1, ts, POOL_WIDTH), lambda bi, i: (bi, i, 0)),
                  pl.BlockSpec((1, HALO, POOL_WIDTH), lambda bi, i: (bi, jnp.maximum(i * hb - 1, 0), 0)),
                  pl.BlockSpec((1, HALO, POOL_WIDTH),
                               lambda bi, i: (bi, jnp.minimum((i + 1) * hb, s // HALO - 1), 0)),
                  pl.BlockSpec((1, ts, POOL_WIDTH), lambda bi, i: (bi, i, 0)),
                  pl.BlockSpec((1, ts, d), lambda bi, i: (bi, i, 0)),
                  pl.BlockSpec((1, 6, d), lambda bi, i: (bi, 0, 0)),
                  const((len(POOL_WINDOWS), POOL_GROUP, POOL_GROUP)), const((1, POOL_WIDTH)),
                  const((d, d)), const((1, d)), const((N_EXPERTS, d)), const((N_EXPERTS, d))],
        out_specs=[pl.BlockSpec((1, ts, d), lambda bi, i: (bi, i, 0)),
                   pl.BlockSpec((1, ts, PACK_W), lambda bi, i: (bi, i, 0)),
                   pl.BlockSpec((N_EXPERTS, ts), lambda bi, i: (0, bi * nt + i))],
        out_shape=[jax.ShapeDtypeStruct((b, s, d), F32),
                   jax.ShapeDtypeStruct((b, s, PACK_W), jnp.int32),
                   jax.ShapeDtypeStruct((N_EXPERTS, b * s), F32)],
        compiler_params=_cparams(("parallel", "parallel")),
        name="postmix",
    )(u, u, u, attn, x, mod, p["pool_w"], p["pool_scale"], p["w_o"], p["norm2_w"],
      p["w_router_hi"], p["w_router_lo"])


def _router_kernel(lg_ref, bias_ref, tri_ref, idx_ref, wts_ref, pos_ref, cnt_ref, carry_ref):
    @pl.when(pl.program_id(0) == 0)
    def _():
        carry_ref[...] = jnp.zeros_like(carry_ref)

    ts = lg_ref.shape[1]
    gsz = N_EXPERTS // N_GROUPS
    ninf = -jnp.inf
    scores = 1.0 / (1.0 + jnp.exp(-lg_ref[...]))
    choice = scores + bias_ref[...]
    sub = lax.broadcasted_iota(jnp.int32, (gsz, ts), 0)

    gs_rows = []
    for g in range(N_GROUPS):
        grp = choice[g * gsz:(g + 1) * gsz]
        m1 = jnp.max(grp, axis=0, keepdims=True)
        i1 = jnp.min(jnp.where(grp == m1, sub, gsz), axis=0, keepdims=True)
        m2 = jnp.max(jnp.where(sub == i1, ninf, grp), axis=0, keepdims=True)
        gs_rows.append(m1 + m2)
    gs = jnp.concatenate(gs_rows, axis=0)

    rank = jnp.zeros((N_GROUPS, ts), jnp.int32)
    for g in range(N_GROUPS):
        row = gs[g:g + 1]
        beats = (row > gs) | ((row == gs) & (sub > g))
        rank = rank + beats.astype(jnp.int32)
    gsel = rank < TOPK_GROUPS

    masked = jnp.concatenate(
        [jnp.where(gsel[g:g + 1], choice[g * gsz:(g + 1) * gsz], ninf) for g in range(N_GROUPS)],
        axis=0)
    eio = lax.broadcasted_iota(jnp.int32, (N_EXPERTS, ts), 0)
    idx_rows, w_rows, hits = [], [], []
    for _ in range(TOP_K):
        m = jnp.max(masked, axis=0, keepdims=True)
        i = jnp.min(jnp.where(masked == m, eio, N_EXPERTS), axis=0, keepdims=True)
        hit = eio == i
        w_rows.append(jnp.sum(jnp.where(hit, scores, 0.0), axis=0, keepdims=True))
        masked = jnp.where(hit, ninf, masked)
        idx_rows.append(i)
        hits.append(hit)

    wsum = functools.reduce(lambda a, c: a + c, w_rows)
    pad_i = [jnp.zeros((1, ts), jnp.int32)] * (TOPK_PAD - TOP_K)
    pad_f = [jnp.zeros((1, ts), F32)] * (TOPK_PAD - TOP_K)
    idx_ref[...] = jnp.concatenate(idx_rows + pad_i, axis=0)
    wts_ref[...] = jnp.concatenate([w / wsum * ROUTED_SCALE for w in w_rows] + pad_f, axis=0)

    sel = functools.reduce(lambda a, c: a | c, hits)
    onehot = jnp.where(sel, 1.0, 0.0).astype(BF16)
    run = jnp.dot(onehot, tri_ref[...], preferred_element_type=F32) + carry_ref[:, 0:1]
    pos_rows = [jnp.sum(jnp.where(h, run - 1.0, 0.0), axis=0, keepdims=True).astype(jnp.int32)
                for h in hits]
    pos_ref[...] = jnp.concatenate(pos_rows + pad_i, axis=0)
    total = run[:, ts - 1:ts]
    carry_ref[...] = jnp.broadcast_to(total, carry_ref.shape)
    cnt_ref[...] = jnp.broadcast_to(total, cnt_ref.shape)


def _router(logits_t, router_bias, ts):
    t = logits_t.shape[1]
    tri = jnp.triu(jnp.ones((ts, ts), BF16))
    tok = pl.BlockSpec((TOPK_PAD, ts), lambda i: (0, i))
    return pl.pallas_call(
        _router_kernel,
        grid=(t // ts,),
        in_specs=[pl.BlockSpec((N_EXPERTS, ts), lambda i: (0, i)),
                  pl.BlockSpec((N_EXPERTS, 1), lambda i: (0, 0)),
                  pl.BlockSpec((ts, ts), lambda i: (0, 0))],
        out_specs=[tok, tok, tok, pl.BlockSpec((N_EXPERTS, LANES), lambda i: (0, 0))],
        out_shape=[jax.ShapeDtypeStruct((TOPK_PAD, t), jnp.int32),
                   jax.ShapeDtypeStruct((TOPK_PAD, t), F32),
                   jax.ShapeDtypeStruct((TOPK_PAD, t), jnp.int32),
                   jax.ShapeDtypeStruct((N_EXPERTS, LANES), F32)],
        scratch_shapes=[pltpu.VMEM((N_EXPERTS, LANES), F32)],
        compiler_params=_cparams(("arbitrary",)),
        name="router",
    )(logits_t, router_bias.reshape(N_EXPERTS, 1), tri)


def _dest_kernel(idx_ref, pos_ref, start_ref, dest_ref):
    ts = idx_ref.shape[1]
    eio = lax.broadcasted_iota(jnp.int32, (N_EXPERTS, ts), 0)
    start = start_ref[...]
    rows = [pos_ref[k:k + 1, :]
            + jnp.sum(jnp.where(eio == idx_ref[k:k + 1, :], start, 0), axis=0, keepdims=True)
            for k in range(TOPK_PAD)]
    dest_ref[...] = jnp.concatenate(rows, axis=0)


def _dest_rows(idx, pos, pad_start, ts):
    t = idx.shape[1]
    tok = pl.BlockSpec((TOPK_PAD, ts), lambda i: (0, i))
    return pl.pallas_call(
        _dest_kernel,
        grid=(t // ts,),
        in_specs=[tok, tok, pl.BlockSpec((N_EXPERTS, 1), lambda i: (0, 0))],
        out_specs=tok,
        out_shape=jax.ShapeDtypeStruct((TOPK_PAD, t), jnp.int32),
        compiler_params=_cparams(("parallel",)),
        name="dest_rows",
    )(idx, pos, pad_start.reshape(N_EXPERTS, 1))


def _pack_rows(x):
    bits = lax.bitcast_convert_type(x.astype(BF16).astype(F32), jnp.int32)
    return bits[:, :PACK_W] | lax.shift_right_logical(bits[:, PACK_W:], 16)


def _unpack_rows(words):
    hi = lax.bitcast_convert_type(words & jnp.int32(-65536), F32)
    lo = lax.bitcast_convert_type(lax.shift_left(words, 16), F32)
    return jnp.concatenate([hi, lo], axis=1)


def _sc_mesh():
    return plsc.VectorSubcoreMesh(core_axis_name="core", subcore_axis_name="subcore")


def _sc_worker_base(rows_per_worker):
    return (lax.axis_index("subcore") * SC_CORES + lax.axis_index("core")) * rows_per_worker


def _sc_scatter_rows(x, dest, n_rows):
    t, w = x.shape
    per_worker = t // SC_WORKERS
    assert per_worker * SC_WORKERS == t and per_worker % SC_CHUNK == 0

    @functools.partial(
        pl.kernel, out_type=jax.ShapeDtypeStruct((n_rows, w), x.dtype), mesh=_sc_mesh(),
        scratch_types=[pltpu.VMEM((TOPK_PAD, SC_CHUNK), jnp.int32), pltpu.VMEM((SC_CHUNK, w), x.dtype),
                       pltpu.SemaphoreType.DMA])
    def scatter(x_hbm, dest_hbm, out_hbm, idx_v, rows_v, sem):
        base = _sc_worker_base(per_worker)

        @pl.loop(0, per_worker // SC_CHUNK)
        def _(c):
            off = pl.multiple_of(base + c * SC_CHUNK, SC_CHUNK)
            pltpu.sync_copy(dest_hbm.at[:, pl.ds(off, SC_CHUNK)], idx_v)
            pltpu.sync_copy(x_hbm.at[pl.ds(off, SC_CHUNK)], rows_v)
            copies = [pltpu.async_copy(rows_v, out_hbm.at[idx_v.at[k]], sem) for k in range(TOP_K)]
            for cp in copies:
                cp.wait()

    return scatter(x, dest)


def _sc_gather_rows(table, idx):
    m = idx.shape[0]
    w = table.shape[1]
    per_worker = m // SC_WORKERS
    assert per_worker * SC_WORKERS == m and per_worker % SC_CHUNK == 0

    @functools.partial(
        pl.kernel, out_type=jax.ShapeDtypeStruct((m, w), table.dtype), mesh=_sc_mesh(),
        scratch_types=[pltpu.VMEM((SC_CHUNK,), jnp.int32), pltpu.VMEM((SC_CHUNK, w), table.dtype),
                       pltpu.SemaphoreType.DMA])
    def gather(table_hbm, idx_hbm, out_hbm, idx_v, rows_v, sem):
        base = _sc_worker_base(per_worker)

        @pl.loop(0, per_worker // SC_CHUNK)
        def _(c):
            off = pl.multiple_of(base + c * SC_CHUNK, SC_CHUNK)
            pltpu.sync_copy(idx_hbm.at[pl.ds(off, SC_CHUNK)], idx_v)
            pltpu.async_copy(table_hbm.at[idx_v], rows_v, sem).wait()
            pltpu.sync_copy(rows_v, out_hbm.at[pl.ds(off, SC_CHUNK)])

    return gather(table, idx)


def _experts_kernel(blk_exp_ref, n_used_ref, xs_ref, wg_ref, wu_ref, wd_ref, ys_ref):
    @pl.when(pl.program_id(0) < n_used_ref[0])
    def _():
        xb = _unpack_rows(xs_ref[...]).astype(BF16)
        g = jnp.dot(xb, wg_ref[0].astype(BF16), preferred_element_type=F32)
        u = jnp.dot(xb, wu_ref[0].astype(BF16), preferred_element_type=F32)
        hb = (_silu(g) * u).astype(BF16)
        ys_ref[...] = _pack_rows(jnp.dot(hb, wd_ref[0].astype(BF16), preferred_element_type=F32))


def _row_block(t):
    per_expert = t * TOP_K // N_EXPERTS
    return 1024 if per_expert >= 4 * 1024 else (512 if per_expert >= 4 * 512 else 256)


def _experts(blk_exp, n_used, xs, w_gate, w_up, w_down, row_block):
    n_rows, w = xs.shape
    d = D_MODEL
    n_blocks = n_rows // row_block
    row = lambda i, be, nu: (jnp.minimum(i, nu[0] - 1), 0)
    wsel = lambda i, be, nu: (be[i], 0, 0)
    return pl.pallas_call(
        _experts_kernel,
        grid_spec=pltpu.PrefetchScalarGridSpec(
            num_scalar_prefetch=2,
            grid=(n_blocks,),
            in_specs=[pl.BlockSpec((row_block, w), row),
                      pl.BlockSpec((1, d, EXPERT_FF), wsel),
                      pl.BlockSpec((1, d, EXPERT_FF), wsel),
                      pl.BlockSpec((1, EXPERT_FF, d), wsel)],
            out_specs=pl.BlockSpec((row_block, w), row)),
        out_shape=jax.ShapeDtypeStruct((n_rows, w), jnp.int32),
        compiler_params=_cparams(("arbitrary",)),
        name="experts",
    )(blk_exp, n_used, xs, w_gate, w_up, w_down)


def _combine_kernel(g_ref, w_ref, h2_ref, x1_ref, mod_ref, wsg_ref, wsu_ref, wsd_ref, out_ref):
    hb = _unpack_rows(h2_ref[...]).astype(BF16)
    g = jnp.dot(hb, wsg_ref[...], preferred_element_type=F32)
    u = jnp.dot(hb, wsu_ref[...], preferred_element_type=F32)
    acc = jnp.dot((_silu(g) * u).astype(BF16), wsd_ref[...], preferred_element_type=F32)
    w = w_ref[...]
    for k in range(TOP_K):
        acc = acc + _unpack_rows(g_ref[k]) * w[:, k:k + 1]
    out_ref[...] = x1_ref[...] + mod_ref[0, 5:6, :] * acc


def _combine(gathered, wts_t, h2p, x1, mod, p, tt, seq):
    t, d = x1.shape
    per_seq = seq // tt
    tok = pl.BlockSpec((tt, d), lambda i: (i, 0))
    const = lambda shape: pl.BlockSpec(shape, lambda i: (0,) * len(shape))
    return pl.pallas_call(
        _combine_kernel,
        grid=(t // tt,),
        in_specs=[pl.BlockSpec((TOP_K, tt, PACK_W), lambda i: (0, i, 0)),
                  pl.BlockSpec((tt, TOPK_PAD), lambda i: (i, 0)),
                  pl.BlockSpec((tt, PACK_W), lambda i: (i, 0)),
                  tok,
                  pl.BlockSpec((1, 6, d), lambda i: (i // per_seq, 0, 0)),
                  const((d, SHARED_FF)), const((d, SHARED_FF)), const((SHARED_FF, d))],
        out_specs=tok,
        out_shape=jax.ShapeDtypeStruct((t, d), F32),
        compiler_params=_cparams(("parallel",)),
        name="combine",
    )(gathered, wts_t, h2p, x1, mod, p["ws_gate"], p["ws_up"], p["ws_down"])


def _rope_tables(s):
    half = QK_ROPE_DIM // 2
    inv_freq = ROPE_THETA ** (-jnp.arange(half, dtype=F32) / half)
    ang = jnp.arange(s, dtype=F32)[:, None] * inv_freq[None, :]
    cos, sin = jnp.cos(ang), jnp.sin(ang)
    z = lambda n: jnp.zeros((s, n), F32)
    tab_cos = jnp.concatenate([jnp.ones((s, QK_NOPE_DIM), F32), cos, cos, z(HEAD_PAD - QK_HEAD_DIM)], 1)
    tab_sa = jnp.concatenate([z(QK_NOPE_DIM), -sin, z(HEAD_PAD - QK_NOPE_DIM - half)], 1)
    tab_sb = jnp.concatenate([z(QK_NOPE_DIM + half), sin, z(HEAD_PAD - QK_HEAD_DIM)], 1)
    return dict(rope_cos=tab_cos, rope_sa=tab_sa, rope_sb=tab_sb, rope_cos_t=cos.T, rope_sin_t=sin.T)


def _prep_weights(norm1_w, w_in, pool_w, pool_scale, q_a_norm_w, w_q_b, kv_a_norm_w, w_kv_b,
                  q_norm_w, k_norm_w, w_o, norm2_w, w_router, w_gate, w_up, w_down,
                  ws_gate, ws_up, ws_down):
    d = D_MODEL
    c0 = POOL_WIDTH + Q_LORA_RANK + KV_LORA_RANK
    pad_h = HEAD_PAD - QK_HEAD_DIM
    w_in_p = jnp.concatenate(
        [w_in[:, :c0], jnp.zeros((d, QK_NOPE_DIM), F32), w_in[:, c0:], jnp.zeros((d, pad_h), F32)], 1)
    w_q = jnp.pad(w_q_b.reshape(Q_LORA_RANK, N_HEADS, QK_HEAD_DIM), ((0, 0), (0, 0), (0, pad_h)))
    kv = w_kv_b.reshape(KV_LORA_RANK, N_HEADS, QK_NOPE_DIM + V_HEAD_DIM)
    w_k = jnp.pad(kv[:, :, :QK_NOPE_DIM], ((0, 0), (0, 0), (0, HEAD_PAD - QK_NOPE_DIM)))
    w_v = kv[:, :, QK_NOPE_DIM:]
    w_r_t = w_router.T
    w_r_hi = w_r_t.astype(BF16)
    w_r_lo = (w_r_t - w_r_hi.astype(F32)).astype(BF16)
    q_gain = q_norm_w * (QK_HEAD_DIM ** -0.5 * LOG2E)
    bound = QK_HEAD_DIM * jnp.max(jnp.abs(q_gain)) * jnp.max(jnp.abs(k_norm_w)) * 1.02 + 0.25
    return dict(
        score_bound=bound,
        norm1_w=norm1_w.reshape(1, d), w_in=w_in_p.astype(BF16),
        q_a_norm_w=q_a_norm_w.reshape(1, -1), w_q_t=w_q.reshape(Q_LORA_RANK, -1).T.astype(BF16),
        kv_a_norm_w=kv_a_norm_w.reshape(1, -1), w_k=w_k.reshape(KV_LORA_RANK, -1).astype(BF16),
        w_v_t=w_v.reshape(KV_LORA_RANK, -1).T.astype(BF16),
        q_gain_col=jnp.broadcast_to(jnp.pad(q_gain, (0, pad_h))[:, None], (HEAD_PAD, LANES)),
        k_norm_w=jnp.pad(k_norm_w, (0, pad_h)).reshape(1, HEAD_PAD),
        pool_w=pool_w.astype(BF16), pool_scale=pool_scale.reshape(1, -1), w_o=w_o.astype(BF16),
        norm2_w=norm2_w.reshape(1, d), w_router_hi=w_r_hi, w_router_lo=w_r_lo,
        w_gate=w_gate, w_up=w_up, w_down=w_down,
        ws_gate=ws_gate.astype(BF16), ws_up=ws_up.astype(BF16), ws_down=ws_down.astype(BF16))


def _tile(n, pref):
    return pref if n % pref == 0 else n


def _mixer_and_routing(x, mod, router_bias, p):
    b, s, d = x.shape
    t = b * s
    ts = _tile(s, 512)
    p = dict(p, **_rope_tables(s))

    u, qt, k, vt = _inproj(x, mod, p, ts)
    online = (p["score_bound"] > MAX_UNSHIFTED_SCORE).astype(jnp.int32).reshape(1)
    attn = _attention(online, qt, k, vt, _tile(s, 1024), _tile(s, 2048), _tile(s, 512))
    x1, h2p, logits_t = _postmix(u, attn, x, mod, p, ts)
    tr = _tile(t, 512)
    idx, wts, pos, cnt = _router(logits_t, router_bias, tr)

    rb = _row_block(t)
    counts = cnt[:, 0].astype(jnp.int32)
    padded = (counts + rb - 1) // rb * rb
    pad_end = jnp.cumsum(padded)
    dest = _dest_rows(idx, pos, pad_end - padded, _tile(t, 2048))
    n_blocks = -(-t * TOP_K // rb) + N_EXPERTS
    blk_row = jnp.arange(n_blocks, dtype=jnp.int32)[:, None] * rb
    blk_exp = jnp.minimum(jnp.sum((pad_end[None, :] <= blk_row).astype(jnp.int32), axis=1),
                          N_EXPERTS - 1)
    n_used = (pad_end[-1:] // rb).astype(jnp.int32)
    h2f = h2p.reshape(t, PACK_W)
    xs = _sc_scatter_rows(h2f, dest, n_blocks * rb)
    return dict(xs=xs, dest=dest, blk_exp=blk_exp, n_used=n_used, wts_t=wts.T, h2f=h2f,
                x1=x1.reshape(t, d), mod=mod, shape=(b, s, d), row_block=rb)


def _gather_and_combine(st, ys, p):
    b, s, d = st["shape"]
    t = b * s
    gathered = _sc_gather_rows(ys, st["dest"][:TOP_K].reshape(TOP_K * t)).reshape(TOP_K, t, PACK_W)
    out = _combine(gathered, st["wts_t"], st["h2f"], st["x1"], st["mod"], p, _tile(s, 512), s)
    return out.reshape(b, s, d)


def kernel(x_prompt, x_sample, c_prompt, c_sample, w_ada, b_ada, norm1_w, w_in, pool_w, pool_scale,
           q_a_norm_w, w_q_b, kv_a_norm_w, w_kv_b, q_norm_w, k_norm_w, w_o, norm2_w, w_router,
           router_bias, w_gate, w_up, w_down, ws_gate, ws_up, ws_down):
    assert w_ada.shape[0] == 1, "single-layer encoder"
    p = _prep_weights(norm1_w[0], w_in[0], pool_w[0], pool_scale[0], q_a_norm_w[0], w_q_b[0],
                      kv_a_norm_w[0], w_kv_b[0], q_norm_w[0], k_norm_w[0], w_o[0], norm2_w[0],
                      w_router[0], w_gate[0], w_up[0], w_down[0], ws_gate[0], ws_up[0], ws_down[0])
    nb = x_prompt.shape[0]
    c = jnp.concatenate([c_prompt, c_sample], axis=0).astype(F32)
    mod = _adaln(c, w_ada[0], b_ada[0]).reshape(c.shape[0], 6, D_MODEL)
    experts = lambda st, xs: _experts(st["blk_exp"], st["n_used"], xs, p["w_gate"], p["w_up"],
                                      p["w_down"], st["row_block"])

    sp = _mixer_and_routing(x_prompt, mod[:nb], router_bias[0], p)
    dest_p, x_sample = lax.optimization_barrier((sp["dest"], x_sample))
    sp["dest"] = dest_p
    ss = _mixer_and_routing(x_sample, mod[nb:], router_bias[0], p)
    ys_p = experts(sp, sp["xs"])
    ys_p, xs_s = lax.optimization_barrier((ys_p, ss["xs"]))
    ys_s = experts(ss, xs_s)
    y_prompt = _gather_and_combine(sp, ys_p, p)
    y_sample = _gather_and_combine(ss, ys_s, p)
    return (y_prompt, y_sample)
```

```python
import functools

import jax
import jax.numpy as jnp
from jax import lax
from jax.experimental import pallas as pl
from jax.experimental.pallas import tpu as pltpu
from jax.experimental.pallas import tpu_sc as plsc

D_MODEL = 1024
POOL_WIDTH = 512
POOL_WINDOWS = (2, 4, 8, 16)
POOL_GROUP = 128
N_HEADS = 8
V_HEAD_DIM = 64
QK_NOPE_DIM = 64
QK_ROPE_DIM = 32
QK_HEAD_DIM = 96
Q_LORA_RANK = 256
KV_LORA_RANK = 128
ROPE_THETA = 10000.0
N_EXPERTS = 64
TOP_K = 6
N_GROUPS = 8
TOPK_GROUPS = 4
EXPERT_FF = 256
SHARED_FF = 256
ROUTED_SCALE = 2.5
EPS = 1e-6

LANES = 128
HEAD_PAD = 128
HALO = 16
TOPK_PAD = 8
V_AUG = 80
MAX_UNSHIFTED_SCORE = 40.0
LOG2E = 1.4426950408889634
PACK_W = D_MODEL // 2
SC_CORES = 2
SC_WORKERS = 32
SC_CHUNK = 128
VMEM_LIMIT = 48 * 1024 * 1024

F32 = jnp.float32
BF16 = jnp.bfloat16
QK_DTYPE = jnp.bfloat16


def _cparams(sem):
    return pltpu.CompilerParams(dimension_semantics=sem, vmem_limit_bytes=VMEM_LIMIT)


def _silu(x):
    return x * (1.0 / (1.0 + jnp.exp(-x)))


def _adaln_kernel(c_ref, w_ref, b_ref, o_ref):
    c = c_ref[...]
    o_ref[...] = jnp.dot(_silu(c), w_ref[...], preferred_element_type=F32,
                         precision=lax.Precision.HIGHEST) + b_ref[...]


def _adaln(c, w_ada, b_ada):
    nb, d = c.shape
    n = w_ada.shape[1]
    tn = 1536
    return pl.pallas_call(
        _adaln_kernel,
        grid=(n // tn,),
        in_specs=[pl.BlockSpec((nb, d), lambda j: (0, 0)),
                  pl.BlockSpec((d, tn), lambda j: (0, j)),
                  pl.BlockSpec((1, tn), lambda j: (0, j))],
        out_specs=pl.BlockSpec((nb, tn), lambda j: (0, j)),
        out_shape=jax.ShapeDtypeStruct((nb, n), F32),
        compiler_params=_cparams(("arbitrary",)),
        name="adaln",
    )(c, w_ada, b_ada.reshape(1, n))


def _inproj_kernel(x_ref, mod_ref, n1w_ref, win_ref, qan_ref, wqt_ref, kvan_ref, wk_ref, wvt_ref,
                   qg_ref, kg_ref, cos_ref, sa_ref, sb_ref, cost_ref, sint_ref,
                   u_ref, qt_ref, k_ref, vt_ref):
    ts = x_ref.shape[1]
    half = QK_ROPE_DIM // 2
    x = x_ref[0]
    shift1 = mod_ref[0, 0:1, :]
    gain1 = n1w_ref[...] * (1.0 + mod_ref[0, 1:2, :])
    r = lax.rsqrt(jnp.mean(x * x, axis=-1, keepdims=True) + EPS)
    h = x * r * gain1 + shift1
    z = jnp.dot(h.astype(BF16), win_ref[...], preferred_element_type=F32)
    u_ref[0] = z[:, :POOL_WIDTH].astype(BF16)

    cq = z[:, POOL_WIDTH:POOL_WIDTH + Q_LORA_RANK]
    cqn = cq * lax.rsqrt(jnp.mean(cq * cq, axis=-1, keepdims=True) + EPS) * qan_ref[...]
    qt = jnp.dot(wqt_ref[...], cqn.T.astype(BF16), preferred_element_type=F32)
    reps = ts // LANES
    qg = jnp.concatenate([qg_ref[...]] * reps, axis=1)
    cost = cost_ref[...]
    sint = sint_ref[...]
    spare = jnp.zeros((HEAD_PAD - QK_HEAD_DIM, ts), F32)
    for hd in range(N_HEADS):
        t = qt[hd * HEAD_PAD:(hd + 1) * HEAD_PAD]
        rn = lax.rsqrt(jnp.sum(t * t, axis=0, keepdims=True) * (1.0 / QK_HEAD_DIM) + EPS)
        tn = t * rn * qg
        t1 = tn[QK_NOPE_DIM:QK_NOPE_DIM + half]
        t2 = tn[QK_NOPE_DIM + half:QK_HEAD_DIM]
        out = jnp.concatenate([tn[:QK_NOPE_DIM], t1 * cost - t2 * sint, t1 * sint + t2 * cost, spare],
                              axis=0)
        qt_ref[0, hd * HEAD_PAD:(hd + 1) * HEAD_PAD, :] = out.astype(QK_DTYPE)

    c0 = POOL_WIDTH + Q_LORA_RANK
    ckv = z[:, c0:c0 + KV_LORA_RANK]
    ckvn = ckv * lax.rsqrt(jnp.mean(ckv * ckv, axis=-1, keepdims=True) + EPS) * kvan_ref[...]

    vt = jnp.dot(wvt_ref[...], ckvn.T.astype(BF16), preferred_element_type=F32)
    ones = jnp.ones((V_AUG - V_HEAD_DIM, ts), BF16)
    for hd in range(N_HEADS):
        vt_ref[0, hd * V_AUG:hd * V_AUG + V_HEAD_DIM, :] = (
            vt[hd * V_HEAD_DIM:(hd + 1) * V_HEAD_DIM].astype(BF16))
        vt_ref[0, hd * V_AUG + V_HEAD_DIM:(hd + 1) * V_AUG, :] = ones

    kk = jnp.dot(ckvn.astype(BF16), wk_ref[...], preferred_element_type=F32)
    kpe = z[:, c0 + KV_LORA_RANK:]
    kg = kg_ref[...]
    pe_ssq = jnp.sum(kpe * kpe, axis=-1, keepdims=True)
    pg = kpe * kg
    pe_rot = (pg * cos_ref[...] + pltpu.roll(pg, HEAD_PAD - half, 1) * sa_ref[...]
              + pltpu.roll(pg, half, 1) * sb_ref[...])
    for hd in range(N_HEADS):
        t = kk[:, hd * HEAD_PAD:(hd + 1) * HEAD_PAD]
        ssq = jnp.sum(t * t, axis=-1, keepdims=True) + pe_ssq
        rn = lax.rsqrt(ssq * (1.0 / QK_HEAD_DIM) + EPS)
        k_ref[0, :, hd * HEAD_PAD:(hd + 1) * HEAD_PAD] = ((t * kg + pe_rot) * rn).astype(QK_DTYPE)


def _inproj(x, mod, p, ts):
    b, s, d = x.shape
    qk_w = N_HEADS * HEAD_PAD
    half = QK_ROPE_DIM // 2
    const = lambda shape: pl.BlockSpec(shape, lambda bi, i: (0,) * len(shape))
    row_tab = pl.BlockSpec((ts, HEAD_PAD), lambda bi, i: (i, 0))
    col_tab = pl.BlockSpec((half, ts), lambda bi, i: (0, i))
    return pl.pallas_call(
        _inproj_kernel,
        grid=(b, s // ts),
        in_specs=[pl.BlockSpec((1, ts, d), lambda bi, i: (bi, i, 0)),
                  pl.BlockSpec((1, 6, d), lambda bi, i: (bi, 0, 0)),
                  const((1, d)), const((d, d)), const((1, Q_LORA_RANK)),
                  const((qk_w, Q_LORA_RANK)), const((1, KV_LORA_RANK)),
                  const((KV_LORA_RANK, qk_w)), const((N_HEADS * V_HEAD_DIM, KV_LORA_RANK)),
                  const((HEAD_PAD, LANES)), const((1, HEAD_PAD)),
                  row_tab, row_tab, row_tab, col_tab, col_tab],
        out_specs=[pl.BlockSpec((1, ts, POOL_WIDTH), lambda bi, i: (bi, i, 0)),
                   pl.BlockSpec((1, qk_w, ts), lambda bi, i: (bi, 0, i)),
                   pl.BlockSpec((1, ts, qk_w), lambda bi, i: (bi, i, 0)),
                   pl.BlockSpec((1, N_HEADS * V_AUG, ts), lambda bi, i: (bi, 0, i))],
        out_shape=[jax.ShapeDtypeStruct((b, s, POOL_WIDTH), BF16),
                   jax.ShapeDtypeStruct((b, qk_w, s), QK_DTYPE),
                   jax.ShapeDtypeStruct((b, s, qk_w), QK_DTYPE),
                   jax.ShapeDtypeStruct((b, N_HEADS * V_AUG, s), BF16)],
        compiler_params=_cparams(("parallel", "parallel")),
        name="inproj",
    )(x, mod, p["norm1_w"], p["w_in"], p["q_a_norm_w"], p["w_q_t"], p["kv_a_norm_w"], p["w_k"],
      p["w_v_t"], p["q_gain_col"], p["k_norm_w"],
      p["rope_cos"], p["rope_sa"], p["rope_sb"], p["rope_cos_t"], p["rope_sin_t"])


def _attn_kernel(online_ref, qt_ref, k_ref, vt_ref, o_ref, acc_a, acc_b, m_ref, *, tk, tk_online):
    s_len = k_ref.shape[1]
    acc_a[...] = jnp.zeros_like(acc_a)
    acc_b[...] = jnp.zeros_like(acc_b)
    qa = qt_ref[0, :HEAD_PAD, :]
    qb = qt_ref[0, HEAD_PAD:, :]

    @pl.when(online_ref[0] == 0)
    def _():
        def body(c, _):
            off = pl.multiple_of(c * tk, tk)
            ks = k_ref[0, pl.ds(off, tk), :]
            vts = vt_ref[0, :, pl.ds(off, tk)]
            pa = jnp.exp2(jnp.dot(ks[:, :HEAD_PAD], qa, preferred_element_type=F32)).astype(BF16)
            pb = jnp.exp2(jnp.dot(ks[:, HEAD_PAD:], qb, preferred_element_type=F32)).astype(BF16)
            acc_a[...] += jnp.dot(vts[:V_AUG], pa, preferred_element_type=F32)
            acc_b[...] += jnp.dot(vts[V_AUG:], pb, preferred_element_type=F32)
            return 0

        lax.fori_loop(0, s_len // tk, body, 0)

    @pl.when(online_ref[0] != 0)
    def _():
        m_ref[...] = jnp.full_like(m_ref, -jnp.inf)

        def one_head(kh, q, vth, acc, row):
            s = jnp.dot(kh, q, preferred_element_type=F32)
            m_old = m_ref[row:row + 1, :]
            m_new = jnp.maximum(m_old, jnp.max(s, axis=0, keepdims=True))
            p = jnp.exp2(s - m_new).astype(BF16)
            acc[...] = acc[...] * jnp.exp2(m_old - m_new) + jnp.dot(vth, p, preferred_element_type=F32)
            m_ref[row:row + 1, :] = m_new

        def body(c, _):
            off = pl.multiple_of(c * tk_online, tk_online)
            ks = k_ref[0, pl.ds(off, tk_online), :]
            vts = vt_ref[0, :, pl.ds(off, tk_online)]
            one_head(ks[:, :HEAD_PAD], qa, vts[:V_AUG], acc_a, 0)
            one_head(ks[:, HEAD_PAD:], qb, vts[V_AUG:], acc_b, 1)
            return 0

        lax.fori_loop(0, s_len // tk_online, body, 0)

    oa = acc_a[:V_HEAD_DIM] / acc_a[V_HEAD_DIM:V_HEAD_DIM + 1]
    ob = acc_b[:V_HEAD_DIM] / acc_b[V_HEAD_DIM:V_HEAD_DIM + 1]
    o_ref[0] = jnp.concatenate([oa, ob], axis=0).T.astype(BF16)


def _attention(online, qt, k, vt, tq, tk, tk_online):
    b, _, s = qt.shape
    return pl.pallas_call(
        functools.partial(_attn_kernel, tk=tk, tk_online=tk_online),
        grid_spec=pltpu.PrefetchScalarGridSpec(
            num_scalar_prefetch=1,
            grid=(b, N_HEADS // 2, s // tq),
            in_specs=[pl.BlockSpec((1, 2 * HEAD_PAD, tq), lambda bi, j, i, on: (bi, j, i)),
                      pl.BlockSpec((1, s, 2 * HEAD_PAD), lambda bi, j, i, on: (bi, 0, j)),
                      pl.BlockSpec((1, 2 * V_AUG, s), lambda bi, j, i, on: (bi, j, 0))],
            out_specs=pl.BlockSpec((1, tq, 2 * V_HEAD_DIM), lambda bi, j, i, on: (bi, i, j)),
            scratch_shapes=[pltpu.VMEM((V_AUG, tq), F32), pltpu.VMEM((V_AUG, tq), F32),
                            pltpu.VMEM((8, tq), F32)]),
        out_shape=jax.ShapeDtypeStruct((b, s, N_HEADS * V_HEAD_DIM), BF16),
        compiler_params=_cparams(("parallel", "parallel", "arbitrary")),
        name="attn",
    )(online, qt, k, vt)


def _postmix_kernel(u_ref, up_ref, un_ref, a_ref, x_ref, mod_ref, pw_ref, ps_ref, wo_ref, n2w_ref,
                    wrh_ref, wrl_ref, x1_ref, h2_ref, lg_ref, *, seq):
    i = pl.program_id(1)
    ts = u_ref.shape[1]
    ext_rows = ts + 2 * HALO
    ext = jnp.concatenate([up_ref[0], u_ref[0], un_ref[0]], axis=0).astype(F32)
    pos = i * ts - HALO + lax.broadcasted_iota(jnp.int32, (ext_rows, 1), 0)
    ext = jnp.where((pos >= 0) & (pos < seq), ext, 0.0)
    p = i * ts + lax.broadcasted_iota(jnp.int32, (ts, 1), 0)

    outs = []
    for g, w in enumerate(POOL_WINDOWS):
        left = w // 2
        right = w - 1 - left
        t = ext[:, g * POOL_GROUP:(g + 1) * POOL_GROUP]
        step = 1
        while step < w:
            t = t + pltpu.roll(t, ext_rows - step, 0)
            step *= 2
        win = pltpu.roll(t, left, 0)[HALO:HALO + ts]
        cnt = (jnp.minimum(p + right + 1, seq) - jnp.maximum(p - left, 0)).astype(F32)
        d = win * (1.0 / cnt) - ext[HALO:HALO + ts, g * POOL_GROUP:(g + 1) * POOL_GROUP]
        outs.append(jnp.dot(d.astype(BF16), pw_ref[g], preferred_element_type=F32))
    pool = (jnp.concatenate(outs, axis=-1) * ps_ref[...]).astype(BF16)

    mix = (jnp.dot(pool, wo_ref[:POOL_WIDTH, :], preferred_element_type=F32)
           + jnp.dot(a_ref[0], wo_ref[POOL_WIDTH:, :], preferred_element_type=F32))
    x1 = x_ref[0] + mod_ref[0, 2:3, :] * mix
    x1_ref[0] = x1
    r = lax.rsqrt(jnp.mean(x1 * x1, axis=-1, keepdims=True) + EPS)
    gain2 = n2w_ref[...] * (1.0 + mod_ref[0, 4:5, :])
    h2 = x1 * r * gain2 + mod_ref[0, 3:4, :]
    h2_ref[0] = _pack_rows(h2)
    hi = h2.astype(BF16)
    lo = (h2 - hi.astype(F32)).astype(BF16)
    nt = (((1,), (1,)), ((), ()))
    lg_ref[...] = (lax.dot_general(wrh_ref[...], hi, nt, preferred_element_type=F32)
                   + lax.dot_general(wrl_ref[...], hi, nt, preferred_element_type=F32)
                   + lax.dot_general(wrh_ref[...], lo, nt, preferred_element_type=F32))


def _postmix(u, attn, x, mod, p, ts):
    b, s, d = x.shape
    nt = s // ts
    hb = ts // HALO
    const = lambda shape: pl.BlockSpec(shape, lambda bi, i: (0,) * len(shape))
    return pl.pallas_call(
        functools.partial(_postmix_kernel, seq=s),
        grid=(b, nt),
        in_specs=[pl.BlockSpec((1, ts, POOL_WIDTH), lambda bi, i: (bi, i, 0)),
                  pl.BlockSpec((1, HALO, POOL_WIDTH), lambda bi, i: (bi, jnp.maximum(i * hb - 1, 0), 0)),
                  pl.BlockSpec((1, HALO, POOL_WIDTH),
                               lambda bi, i: (bi, jnp.minimum((i + 1) * hb, s // HALO - 1), 0)),
                  pl.BlockSpec((1, ts, POOL_WIDTH), lambda bi, i: (bi, i, 0)),
                  pl.BlockSpec((1, ts, d), lambda bi, i: (bi, i, 0)),
                  pl.BlockSpec((1, 6, d), lambda bi, i: (bi, 0, 0)),
                  const((len(POOL_WINDOWS), POOL_GROUP, POOL_GROUP)), const((1, POOL_WIDTH)),
                  const((d, d)), const((1, d)), const((N_EXPERTS, d)), const((N_EXPERTS, d))],
        out_specs=[pl.BlockSpec((1, ts, d), lambda bi, i: (bi, i, 0)),
                   pl.BlockSpec((1, ts, PACK_W), lambda bi, i: (bi, i, 0)),
                   pl.BlockSpec((N_EXPERTS, ts), lambda bi, i: (0, bi * nt + i))],
        out_shape=[jax.ShapeDtypeStruct((b, s, d), F32),
                   jax.ShapeDtypeStruct((b, s, PACK_W), jnp.int32),
                   jax.ShapeDtypeStruct((N_EXPERTS, b * s), F32)],
        compiler_params=_cparams(("parallel", "parallel")),
        name="postmix",
    )(u, u, u, attn, x, mod, p["pool_w"], p["pool_scale"], p["w_o"], p["norm2_w"],
      p["w_router_hi"], p["w_router_lo"])


def _router_kernel(lg_ref, bias_ref, tri_ref, idx_ref, wts_ref, pos_ref, cnt_ref, carry_ref):
    @pl.when(pl.program_id(0) == 0)
    def _():
        carry_ref[...] = jnp.zeros_like(carry_ref)

    ts = lg_ref.shape[1]
    gsz = N_EXPERTS // N_GROUPS
    ninf = -jnp.inf
    scores = 1.0 / (1.0 + jnp.exp(-lg_ref[...]))
    choice = scores + bias_ref[...]
    sub = lax.broadcasted_iota(jnp.int32, (gsz, ts), 0)

    gs_rows = []
    for g in range(N_GROUPS):
        grp = choice[g * gsz:(g + 1) * gsz]
        m1 = jnp.max(grp, axis=0, keepdims=True)
        i1 = jnp.min(jnp.where(grp == m1, sub, gsz), axis=0, keepdims=True)
        m2 = jnp.max(jnp.where(sub == i1, ninf, grp), axis=0, keepdims=True)
        gs_rows.append(m1 + m2)
    gs = jnp.concatenate(gs_rows, axis=0)

    rank = jnp.zeros((N_GROUPS, ts), jnp.int32)
    for g in range(N_GROUPS):
        row = gs[g:g + 1]
        beats = (row > gs) | ((row == gs) & (sub > g))
        rank = rank + beats.astype(jnp.int32)
    gsel = rank < TOPK_GROUPS

    masked = jnp.concatenate(
        [jnp.where(gsel[g:g + 1], choice[g * gsz:(g + 1) * gsz], ninf) for g in range(N_GROUPS)],
        axis=0)
    eio = lax.broadcasted_iota(jnp.int32, (N_EXPERTS, ts), 0)
    idx_rows, w_rows, hits = [], [], []
    for _ in range(TOP_K):
        m = jnp.max(masked, axis=0, keepdims=True)
        i = jnp.min(jnp.where(masked == m, eio, N_EXPERTS), axis=0, keepdims=True)
        hit = eio == i
        w_rows.append(jnp.sum(jnp.where(hit, scores, 0.0), axis=0, keepdims=True))
        masked = jnp.where(hit, ninf, masked)
        idx_rows.append(i)
        hits.append(hit)

    wsum = functools.reduce(lambda a, c: a + c, w_rows)
    pad_i = [jnp.zeros((1, ts), jnp.int32)] * (TOPK_PAD - TOP_K)
    pad_f = [jnp.zeros((1, ts), F32)] * (TOPK_PAD - TOP_K)
    idx_ref[...] = jnp.concatenate(idx_rows + pad_i, axis=0)
    wts_ref[...] = jnp.concatenate([w / wsum * ROUTED_SCALE for w in w_rows] + pad_f, axis=0)

    sel = functools.reduce(lambda a, c: a | c, hits)
    onehot = jnp.where(sel, 1.0, 0.0).astype(BF16)
    run = jnp.dot(onehot, tri_ref[...], preferred_element_type=F32) + carry_ref[:, 0:1]
    pos_rows = [jnp.sum(jnp.where(h, run - 1.0, 0.0), axis=0, keepdims=True).astype(jnp.int32)
                for h in hits]
    pos_ref[...] = jnp.concatenate(pos_rows + pad_i, axis=0)
    total = run[:, ts - 1:ts]
    carry_ref[...] = jnp.broadcast_to(total, carry_ref.shape)
    cnt_ref[...] = jnp.broadcast_to(total, cnt_ref.shape)


def _router(logits_t, router_bias, ts):
    t = logits_t.shape[1]
    tri = jnp.triu(jnp.ones((ts, ts), BF16))
    tok = pl.BlockSpec((TOPK_PAD, ts), lambda i: (0, i))
    return pl.pallas_call(
        _router_kernel,
        grid=(t // ts,),
        in_specs=[pl.BlockSpec((N_EXPERTS, ts), lambda i: (0, i)),
                  pl.BlockSpec((N_EXPERTS, 1), lambda i: (0, 0)),
                  pl.BlockSpec((ts, ts), lambda i: (0, 0))],
        out_specs=[tok, tok, tok, pl.BlockSpec((N_EXPERTS, LANES), lambda i: (0, 0))],
        out_shape=[jax.ShapeDtypeStruct((TOPK_PAD, t), jnp.int32),
                   jax.ShapeDtypeStruct((TOPK_PAD, t), F32),
                   jax.ShapeDtypeStruct((TOPK_PAD, t), jnp.int32),
                   jax.ShapeDtypeStruct((N_EXPERTS, LANES), F32)],
        scratch_shapes=[pltpu.VMEM((N_EXPERTS, LANES), F32)],
        compiler_params=_cparams(("arbitrary",)),
        name="router",
    )(logits_t, router_bias.reshape(N_EXPERTS, 1), tri)


def _dest_kernel(idx_ref, pos_ref, start_ref, dest_ref):
    ts = idx_ref.shape[1]
    eio = lax.broadcasted_iota(jnp.int32, (N_EXPERTS, ts), 0)
    start = start_ref[...]
    rows = [pos_ref[k:k + 1, :]
            + jnp.sum(jnp.where(eio == idx_ref[k:k + 1, :], start, 0), axis=0, keepdims=True)
            for k in range(TOPK_PAD)]
    dest_ref[...] = jnp.concatenate(rows, axis=0)


def _dest_rows(idx, pos, pad_start, ts):
    t = idx.shape[1]
    tok = pl.BlockSpec((TOPK_PAD, ts), lambda i: (0, i))
    return pl.pallas_call(
        _dest_kernel,
        grid=(t // ts,),
        in_specs=[tok, tok, pl.BlockSpec((N_EXPERTS, 1), lambda i: (0, 0))],
        out_specs=tok,
        out_shape=jax.ShapeDtypeStruct((TOPK_PAD, t), jnp.int32),
        compiler_params=_cparams(("parallel",)),
        name="dest_rows",
    )(idx, pos, pad_start.reshape(N_EXPERTS, 1))


def _pack_rows(x):
    bits = lax.bitcast_convert_type(x.astype(BF16).astype(F32), jnp.int32)
    return bits[:, :PACK_W] | lax.shift_right_logical(bits[:, PACK_W:], 16)


def _unpack_rows(words):
    hi = lax.bitcast_convert_type(words & jnp.int32(-65536), F32)
    lo = lax.bitcast_convert_type(lax.shift_left(words, 16), F32)
    return jnp.concatenate([hi, lo], axis=1)


def _sc_mesh():
    return plsc.VectorSubcoreMesh(core_axis_name="core", subcore_axis_name="subcore")


def _sc_worker_base(rows_per_worker):
    return (lax.axis_index("subcore") * SC_CORES + lax.axis_index("core")) * rows_per_worker


def _sc_scatter_rows(x, dest, n_rows):
    t, w = x.shape
    per_worker = t // SC_WORKERS
    assert per_worker * SC_WORKERS == t and per_worker % SC_CHUNK == 0

    @functools.partial(
        pl.kernel, out_type=jax.ShapeDtypeStruct((n_rows, w), x.dtype), mesh=_sc_mesh(),
        scratch_types=[pltpu.VMEM((TOPK_PAD, SC_CHUNK), jnp.int32), pltpu.VMEM((SC_CHUNK, w), x.dtype),
                       pltpu.SemaphoreType.DMA])
    def scatter(x_hbm, dest_hbm, out_hbm, idx_v, rows_v, sem):
        base = _sc_worker_base(per_worker)

        @pl.loop(0, per_worker // SC_CHUNK)
        def _(c):
            off = pl.multiple_of(base + c * SC_CHUNK, SC_CHUNK)
            pltpu.sync_copy(dest_hbm.at[:, pl.ds(off, SC_CHUNK)], idx_v)
            pltpu.sync_copy(x_hbm.at[pl.ds(off, SC_CHUNK)], rows_v)
            copies = [pltpu.async_copy(rows_v, out_hbm.at[idx_v.at[k]], sem) for k in range(TOP_K)]
            for cp in copies:
                cp.wait()

    return scatter(x, dest)


def _sc_gather_rows(table, idx):
    m = idx.shape[0]
    w = table.shape[1]
    per_worker = m // SC_WORKERS
    assert per_worker * SC_WORKERS == m and per_worker % SC_CHUNK == 0

    @functools.partial(
        pl.kernel, out_type=jax.ShapeDtypeStruct((m, w), table.dtype), mesh=_sc_mesh(),
        scratch_types=[pltpu.VMEM((SC_CHUNK,), jnp.int32), pltpu.VMEM((SC_CHUNK, w), table.dtype),
                       pltpu.SemaphoreType.DMA])
    def gather(table_hbm, idx_hbm, out_hbm, idx_v, rows_v, sem):
        base = _sc_worker_base(per_worker)

        @pl.loop(0, per_worker // SC_CHUNK)
        def _(c):
            off = pl.multiple_of(base + c * SC_CHUNK, SC_CHUNK)
            pltpu.sync_copy(idx_hbm.at[pl.ds(off, SC_CHUNK)], idx_v)
            pltpu.async_copy(table_hbm.at[idx_v], rows_v, sem).wait()
            pltpu.sync_copy(rows_v, out_hbm.at[pl.ds(off, SC_CHUNK)])

    return gather(table, idx)


def _experts_kernel(blk_exp_ref, n_used_ref, xs_ref, wg_ref, wu_ref, wd_ref, ys_ref):
    @pl.when(pl.program_id(0) < n_used_ref[0])
    def _():
        xb = _unpack_rows(xs_ref[...]).astype(BF16)
        g = jnp.dot(xb, wg_ref[0].astype(BF16), preferred_element_type=F32)
        u = jnp.dot(xb, wu_ref[0].astype(BF16), preferred_element_type=F32)
        hb = (_silu(g) * u).astype(BF16)
        ys_ref[...] = _pack_rows(jnp.dot(hb, wd_ref[0].astype(BF16), preferred_element_type=F32))


def _row_block(t):
    per_expert = t * TOP_K // N_EXPERTS
    return 1024 if per_expert >= 4 * 1024 else (512 if per_expert >= 4 * 512 else 256)


def _experts(blk_exp, n_used, xs, w_gate, w_up, w_down, row_block):
    n_rows, w = xs.shape
    d = D_MODEL
    n_blocks = n_rows // row_block
    row = lambda i, be, nu: (jnp.minimum(i, nu[0] - 1), 0)
    wsel = lambda i, be, nu: (be[i], 0, 0)
    return pl.pallas_call(
        _experts_kernel,
        grid_spec=pltpu.PrefetchScalarGridSpec(
            num_scalar_prefetch=2,
            grid=(n_blocks,),
            in_specs=[pl.BlockSpec((row_block, w), row),
                      pl.BlockSpec((1, d, EXPERT_FF), wsel),
                      pl.BlockSpec((1, d, EXPERT_FF), wsel),
                      pl.BlockSpec((1, EXPERT_FF, d), wsel)],
            out_specs=pl.BlockSpec((row_block, w), row)),
        out_shape=jax.ShapeDtypeStruct((n_rows, w), jnp.int32),
        compiler_params=_cparams(("arbitrary",)),
        name="experts",
    )(blk_exp, n_used, xs, w_gate, w_up, w_down)


def _combine_kernel(g_ref, w_ref, h2_ref, x1_ref, mod_ref, wsg_ref, wsu_ref, wsd_ref, out_ref):
    hb = _unpack_rows(h2_ref[...]).astype(BF16)
    g = jnp.dot(hb, wsg_ref[...], preferred_element_type=F32)
    u = jnp.dot(hb, wsu_ref[...], preferred_element_type=F32)
    acc = jnp.dot((_silu(g) * u).astype(BF16), wsd_ref[...], preferred_element_type=F32)
    w = w_ref[...]
    for k in range(TOP_K):
        acc = acc + _unpack_rows(g_ref[k]) * w[:, k:k + 1]
    out_ref[...] = x1_ref[...] + mod_ref[0, 5:6, :] * acc


def _combine(gathered, wts_t, h2p, x1, mod, p, tt, seq):
    t, d = x1.shape
    per_seq = seq // tt
    tok = pl.BlockSpec((tt, d), lambda i: (i, 0))
    const = lambda shape: pl.BlockSpec(shape, lambda i: (0,) * len(shape))
    return pl.pallas_call(
        _combine_kernel,
        grid=(t // tt,),
        in_specs=[pl.BlockSpec((TOP_K, tt, PACK_W), lambda i: (0, i, 0)),
                  pl.BlockSpec((tt, TOPK_PAD), lambda i: (i, 0)),
                  pl.BlockSpec((tt, PACK_W), lambda i: (i, 0)),
                  tok,
                  pl.BlockSpec((1, 6, d), lambda i: (i // per_seq, 0, 0)),
                  const((d, SHARED_FF)), const((d, SHARED_FF)), const((SHARED_FF, d))],
        out_specs=tok,
        out_shape=jax.ShapeDtypeStruct((t, d), F32),
        compiler_params=_cparams(("parallel",)),
        name="combine",
    )(gathered, wts_t, h2p, x1, mod, p["ws_gate"], p["ws_up"], p["ws_down"])


def _rope_tables(s):
    half = QK_ROPE_DIM // 2
    inv_freq = ROPE_THETA ** (-jnp.arange(half, dtype=F32) / half)
    ang = jnp.arange(s, dtype=F32)[:, None] * inv_freq[None, :]
    cos, sin = jnp.cos(ang), jnp.sin(ang)
    z = lambda n: jnp.zeros((s, n), F32)
    tab_cos = jnp.concatenate([jnp.ones((s, QK_NOPE_DIM), F32), cos, cos, z(HEAD_PAD - QK_HEAD_DIM)], 1)
    tab_sa = jnp.concatenate([z(QK_NOPE_DIM), -sin, z(HEAD_PAD - QK_NOPE_DIM - half)], 1)
    tab_sb = jnp.concatenate([z(QK_NOPE_DIM + half), sin, z(HEAD_PAD - QK_HEAD_DIM)], 1)
    return dict(rope_cos=tab_cos, rope_sa=tab_sa, rope_sb=tab_sb, rope_cos_t=cos.T, rope_sin_t=sin.T)


def _prep_weights(norm1_w, w_in, pool_w, pool_scale, q_a_norm_w, w_q_b, kv_a_norm_w, w_kv_b,
                  q_norm_w, k_norm_w, w_o, norm2_w, w_router, w_gate, w_up, w_down,
                  ws_gate, ws_up, ws_down):
    d = D_MODEL
    c0 = POOL_WIDTH + Q_LORA_RANK + KV_LORA_RANK
    pad_h = HEAD_PAD - QK_HEAD_DIM
    w_in_p = jnp.concatenate(
        [w_in[:, :c0], jnp.zeros((d, QK_NOPE_DIM), F32), w_in[:, c0:], jnp.zeros((d, pad_h), F32)], 1)
    w_q = jnp.pad(w_q_b.reshape(Q_LORA_RANK, N_HEADS, QK_HEAD_DIM), ((0, 0), (0, 0), (0, pad_h)))
    kv = w_kv_b.reshape(KV_LORA_RANK, N_HEADS, QK_NOPE_DIM + V_HEAD_DIM)
    w_k = jnp.pad(kv[:, :, :QK_NOPE_DIM], ((0, 0), (0, 0), (0, HEAD_PAD - QK_NOPE_DIM)))
    w_v = kv[:, :, QK_NOPE_DIM:]
    w_r_t = w_router.T
    w_r_hi = w_r_t.astype(BF16)
    w_r_lo = (w_r_t - w_r_hi.astype(F32)).astype(BF16)
    q_gain = q_norm_w * (QK_HEAD_DIM ** -0.5 * LOG2E)
    bound = QK_HEAD_DIM * jnp.max(jnp.abs(q_gain)) * jnp.max(jnp.abs(k_norm_w)) * 1.02 + 0.25
    return dict(
        score_bound=bound,
        norm1_w=norm1_w.reshape(1, d), w_in=w_in_p.astype(BF16),
        q_a_norm_w=q_a_norm_w.reshape(1, -1), w_q_t=w_q.reshape(Q_LORA_RANK, -1).T.astype(BF16),
        kv_a_norm_w=kv_a_norm_w.reshape(1, -1), w_k=w_k.reshape(KV_LORA_RANK, -1).astype(BF16),
        w_v_t=w_v.reshape(KV_LORA_RANK, -1).T.astype(BF16),
        q_gain_col=jnp.broadcast_to(jnp.pad(q_gain, (0, pad_h))[:, None], (HEAD_PAD, LANES)),
        k_norm_w=jnp.pad(k_norm_w, (0, pad_h)).reshape(1, HEAD_PAD),
        pool_w=pool_w.astype(BF16), pool_scale=pool_scale.reshape(1, -1), w_o=w_o.astype(BF16),
        norm2_w=norm2_w.reshape(1, d), w_router_hi=w_r_hi, w_router_lo=w_r_lo,
        w_gate=w_gate, w_up=w_up, w_down=w_down,
        ws_gate=ws_gate.astype(BF16), ws_up=ws_up.astype(BF16), ws_down=ws_down.astype(BF16))


def _tile(n, pref):
    return pref if n % pref == 0 else n


def _mixer_and_routing(x, mod, router_bias, p):
    b, s, d = x.shape
    t = b * s
    ts = _tile(s, 512)
    p = dict(p, **_rope_tables(s))

    u, qt, k, vt = _inproj(x, mod, p, ts)
    online = (p["score_bound"] > MAX_UNSHIFTED_SCORE).astype(jnp.int32).reshape(1)
    attn = _attention(online, qt, k, vt, _tile(s, 1024), _tile(s, 2048), _tile(s, 512))
    x1, h2p, logits_t = _postmix(u, attn, x, mod, p, _tile(s, 1024))
    tr = _tile(t, 1024)
    idx, wts, pos, cnt = _router(logits_t, router_bias, tr)

    rb = _row_block(t)
    counts = cnt[:, 0].astype(jnp.int32)
    padded = (counts + rb - 1) // rb * rb
    pad_end = jnp.cumsum(padded)
    dest = _dest_rows(idx, pos, pad_end - padded, _tile(t, 2048))
    n_blocks = -(-t * TOP_K // rb) + N_EXPERTS
    blk_row = jnp.arange(n_blocks, dtype=jnp.int32)[:, None] * rb
    blk_exp = jnp.minimum(jnp.sum((pad_end[None, :] <= blk_row).astype(jnp.int32), axis=1),
                          N_EXPERTS - 1)
    n_used = (pad_end[-1:] // rb).astype(jnp.int32)
    h2f = h2p.reshape(t, PACK_W)
    xs = _sc_scatter_rows(h2f, dest, n_blocks * rb)
    return dict(xs=xs, dest=dest, blk_exp=blk_exp, n_used=n_used, wts_t=wts.T, h2f=h2f,
                x1=x1.reshape(t, d), mod=mod, shape=(b, s, d), row_block=rb)


def _gather_and_combine(st, ys, p):
    b, s, d = st["shape"]
    t = b * s
    gathered = _sc_gather_rows(ys, st["dest"][:TOP_K].reshape(TOP_K * t)).reshape(TOP_K, t, PACK_W)
    out = _combine(gathered, st["wts_t"], st["h2f"], st["x1"], st["mod"], p, _tile(s, 512), s)
    return out.reshape(b, s, d)


def kernel(x_prompt, x_sample, c_prompt, c_sample, w_ada, b_ada, norm1_w, w_in, pool_w, pool_scale,
           q_a_norm_w, w_q_b, kv_a_norm_w, w_kv_b, q_norm_w, k_norm_w, w_o, norm2_w, w_router,
           router_bias, w_gate, w_up, w_down, ws_gate, ws_up, ws_down):
    assert w_ada.shape[0] == 1, "single-layer encoder"
    p = _prep_weights(norm1_w[0], w_in[0], pool_w[0], pool_scale[0], q_a_norm_w[0], w_q_b[0],
                      kv_a_norm_w[0], w_kv_b[0], q_norm_w[0], k_norm_w[0], w_o[0], norm2_w[0],
                      w_router[0], w_gate[0], w_up[0], w_down[0], ws_gate[0], ws_up[0], ws_down[0])
    nb = x_prompt.shape[0]
    c = jnp.concatenate([c_prompt, c_sample], axis=0).astype(F32)
    mod = _adaln(c, w_ada[0], b_ada[0]).reshape(c.shape[0], 6, D_MODEL)
    experts = lambda st, xs: _experts(st["blk_exp"], st["n_used"], xs, p["w_gate"], p["w_up"],
                                      p["w_down"], st["row_block"])

    sp = _mixer_and_routing(x_prompt, mod[:nb], router_bias[0], p)
    dest_p, x_sample = lax.optimization_barrier((sp["dest"], x_sample))
    sp["dest"] = dest_p
    ss = _mixer_and_routing(x_sample, mod[nb:], router_bias[0], p)
    ys_p = experts(sp, sp["xs"])
    ys_p, xs_s = lax.optimization_barrier((ys_p, ss["xs"]))
    ys_s = experts(ss, xs_s)
    y_prompt = _gather_and_combine(sp, ys_p, p)
    y_sample = _gather_and_combine(ss, ys_s, p)
    return (y_prompt, y_sample)
```

```python
import functools

import jax
import jax.numpy as jnp
from jax import lax
from jax.experimental import pallas as pl
from jax.experimental.pallas import tpu as pltpu
from jax.experimental.pallas import tpu_sc as plsc

D_MODEL = 1024
POOL_WIDTH = 512
POOL_WINDOWS = (2, 4, 8, 16)
POOL_GROUP = 128
N_HEADS = 8
V_HEAD_DIM = 64
QK_NOPE_DIM = 64
QK_ROPE_DIM = 32
QK_HEAD_DIM = 96
Q_LORA_RANK = 256
KV_LORA_RANK = 128
ROPE_THETA = 10000.0
N_EXPERTS = 64
TOP_K = 6
N_GROUPS = 8
TOPK_GROUPS = 4
EXPERT_FF = 256
SHARED_FF = 256
ROUTED_SCALE = 2.5
EPS = 1e-6

LANES = 128
HEAD_PAD = 128
HALO = 16
TOPK_PAD = 8
V_AUG = 80
ATTN_HEADS = 4
MAX_UNSHIFTED_SCORE = 40.0
LOG2E = 1.4426950408889634
PACK_W = D_MODEL // 2
SC_CORES = 2
SC_WORKERS = 32
SC_CHUNK = 128
VMEM_LIMIT = 48 * 1024 * 1024

F32 = jnp.float32
BF16 = jnp.bfloat16
QK_DTYPE = jnp.bfloat16


def _cparams(sem):
    return pltpu.CompilerParams(dimension_semantics=sem, vmem_limit_bytes=VMEM_LIMIT)


def _silu(x):
    return x * (1.0 / (1.0 + jnp.exp(-x)))


def _adaln_kernel(c_ref, w_ref, b_ref, o_ref):
    c = c_ref[...]
    o_ref[...] = jnp.dot(_silu(c), w_ref[...], preferred_element_type=F32,
                         precision=lax.Precision.HIGHEST) + b_ref[...]


def _adaln(c, w_ada, b_ada):
    nb, d = c.shape
    n = w_ada.shape[1]
    tn = 1536
    return pl.pallas_call(
        _adaln_kernel,
        grid=(n // tn,),
        in_specs=[pl.BlockSpec((nb, d), lambda j: (0, 0)),
                  pl.BlockSpec((d, tn), lambda j: (0, j)),
                  pl.BlockSpec((1, tn), lambda j: (0, j))],
        out_specs=pl.BlockSpec((nb, tn), lambda j: (0, j)),
        out_shape=jax.ShapeDtypeStruct((nb, n), F32),
        compiler_params=_cparams(("arbitrary",)),
        name="adaln",
    )(c, w_ada, b_ada.reshape(1, n))


def _inproj_kernel(x_ref, mod_ref, n1w_ref, win_ref, qan_ref, wqt_ref, kvan_ref, wk_ref, wvt_ref,
                   qg_ref, kg_ref, cos_ref, sa_ref, sb_ref, cost_ref, sint_ref,
                   u_ref, qt_ref, k_ref, vt_ref):
    ts = x_ref.shape[1]
    half = QK_ROPE_DIM // 2
    x = x_ref[0]
    shift1 = mod_ref[0, 0:1, :]
    gain1 = n1w_ref[...] * (1.0 + mod_ref[0, 1:2, :])
    r = lax.rsqrt(jnp.mean(x * x, axis=-1, keepdims=True) + EPS)
    h = x * r * gain1 + shift1
    z = jnp.dot(h.astype(BF16), win_ref[...], preferred_element_type=F32)
    u_ref[0] = z[:, :POOL_WIDTH].astype(BF16)

    cq = z[:, POOL_WIDTH:POOL_WIDTH + Q_LORA_RANK]
    cqn = cq * lax.rsqrt(jnp.mean(cq * cq, axis=-1, keepdims=True) + EPS) * qan_ref[...]
    qt = jnp.dot(wqt_ref[...], cqn.T.astype(BF16), preferred_element_type=F32)
    reps = ts // LANES
    qg = jnp.concatenate([qg_ref[...]] * reps, axis=1)
    cost = cost_ref[...]
    sint = sint_ref[...]
    spare = jnp.zeros((HEAD_PAD - QK_HEAD_DIM, ts), F32)
    for hd in range(N_HEADS):
        t = qt[hd * HEAD_PAD:(hd + 1) * HEAD_PAD]
        rn = lax.rsqrt(jnp.sum(t * t, axis=0, keepdims=True) * (1.0 / QK_HEAD_DIM) + EPS)
        tn = t * rn * qg
        t1 = tn[QK_NOPE_DIM:QK_NOPE_DIM + half]
        t2 = tn[QK_NOPE_DIM + half:QK_HEAD_DIM]
        out = jnp.concatenate([tn[:QK_NOPE_DIM], t1 * cost - t2 * sint, t1 * sint + t2 * cost, spare],
                              axis=0)
        qt_ref[0, hd * HEAD_PAD:(hd + 1) * HEAD_PAD, :] = out.astype(QK_DTYPE)

    c0 = POOL_WIDTH + Q_LORA_RANK
    ckv = z[:, c0:c0 + KV_LORA_RANK]
    ckvn = ckv * lax.rsqrt(jnp.mean(ckv * ckv, axis=-1, keepdims=True) + EPS) * kvan_ref[...]

    vt = jnp.dot(wvt_ref[...], ckvn.T.astype(BF16), preferred_element_type=F32)
    ones = jnp.ones((V_AUG - V_HEAD_DIM, ts), BF16)
    for hd in range(N_HEADS):
        vt_ref[0, hd * V_AUG:hd * V_AUG + V_HEAD_DIM, :] = (
            vt[hd * V_HEAD_DIM:(hd + 1) * V_HEAD_DIM].astype(BF16))
        vt_ref[0, hd * V_AUG + V_HEAD_DIM:(hd + 1) * V_AUG, :] = ones

    kk = jnp.dot(ckvn.astype(BF16), wk_ref[...], preferred_element_type=F32)
    kpe = z[:, c0 + KV_LORA_RANK:]
    kg = kg_ref[...]
    pe_ssq = jnp.sum(kpe * kpe, axis=-1, keepdims=True)
    pg = kpe * kg
    pe_rot = (pg * cos_ref[...] + pltpu.roll(pg, HEAD_PAD - half, 1) * sa_ref[...]
              + pltpu.roll(pg, half, 1) * sb_ref[...])
    for hd in range(N_HEADS):
        t = kk[:, hd * HEAD_PAD:(hd + 1) * HEAD_PAD]
        ssq = jnp.sum(t * t, axis=-1, keepdims=True) + pe_ssq
        rn = lax.rsqrt(ssq * (1.0 / QK_HEAD_DIM) + EPS)
        k_ref[0, :, hd * HEAD_PAD:(hd + 1) * HEAD_PAD] = ((t * kg + pe_rot) * rn).astype(QK_DTYPE)


def _inproj(x, mod, p, ts):
    b, s, d = x.shape
    qk_w = N_HEADS * HEAD_PAD
    half = QK_ROPE_DIM // 2
    const = lambda shape: pl.BlockSpec(shape, lambda bi, i: (0,) * len(shape))
    row_tab = pl.BlockSpec((ts, HEAD_PAD), lambda bi, i: (i, 0))
    col_tab = pl.BlockSpec((half, ts), lambda bi, i: (0, i))
    return pl.pallas_call(
        _inproj_kernel,
        grid=(b, s // ts),
        in_specs=[pl.BlockSpec((1, ts, d), lambda bi, i: (bi, i, 0)),
                  pl.BlockSpec((1, 6, d), lambda bi, i: (bi, 0, 0)),
                  const((1, d)), const((d, d)), const((1, Q_LORA_RANK)),
                  const((qk_w, Q_LORA_RANK)), const((1, KV_LORA_RANK)),
                  const((KV_LORA_RANK, qk_w)), const((N_HEADS * V_HEAD_DIM, KV_LORA_RANK)),
                  const((HEAD_PAD, LANES)), const((1, HEAD_PAD)),
                  row_tab, row_tab, row_tab, col_tab, col_tab],
        out_specs=[pl.BlockSpec((1, ts, POOL_WIDTH), lambda bi, i: (bi, i, 0)),
                   pl.BlockSpec((1, qk_w, ts), lambda bi, i: (bi, 0, i)),
                   pl.BlockSpec((1, ts, qk_w), lambda bi, i: (bi, i, 0)),
                   pl.BlockSpec((1, N_HEADS * V_AUG, ts), lambda bi, i: (bi, 0, i))],
        out_shape=[jax.ShapeDtypeStruct((b, s, POOL_WIDTH), BF16),
                   jax.ShapeDtypeStruct((b, qk_w, s), QK_DTYPE),
                   jax.ShapeDtypeStruct((b, s, qk_w), QK_DTYPE),
                   jax.ShapeDtypeStruct((b, N_HEADS * V_AUG, s), BF16)],
        compiler_params=_cparams(("parallel", "parallel")),
        name="inproj",
    )(x, mod, p["norm1_w"], p["w_in"], p["q_a_norm_w"], p["w_q_t"], p["kv_a_norm_w"], p["w_k"],
      p["w_v_t"], p["q_gain_col"], p["k_norm_w"],
      p["rope_cos"], p["rope_sa"], p["rope_sb"], p["rope_cos_t"], p["rope_sin_t"])


def _attn_kernel(online_ref, qt_ref, k_ref, vt_ref, o_ref, *scratch, tk, tk_online):
    accs, m_ref = scratch[:ATTN_HEADS], scratch[ATTN_HEADS]
    s_len = k_ref.shape[1]
    for acc in accs:
        acc[...] = jnp.zeros_like(acc)
    heads = range(ATTN_HEADS)
    qs = [qt_ref[0, h * HEAD_PAD:(h + 1) * HEAD_PAD, :] for h in heads]

    @pl.when(online_ref[0] == 0)
    def _():
        def body(c, _):
            off = pl.multiple_of(c * tk, tk)
            ks = k_ref[0, pl.ds(off, tk), :]
            vts = vt_ref[0, :, pl.ds(off, tk)]
            for h in heads:
                s = jnp.dot(ks[:, h * HEAD_PAD:(h + 1) * HEAD_PAD], qs[h], preferred_element_type=F32)
                accs[h][...] += jnp.dot(vts[h * V_AUG:(h + 1) * V_AUG], jnp.exp2(s).astype(BF16),
                                        preferred_element_type=F32)
            return 0

        lax.fori_loop(0, s_len // tk, body, 0)

    @pl.when(online_ref[0] != 0)
    def _():
        m_ref[...] = jnp.full_like(m_ref, -jnp.inf)

        def body(c, _):
            off = pl.multiple_of(c * tk_online, tk_online)
            ks = k_ref[0, pl.ds(off, tk_online), :]
            vts = vt_ref[0, :, pl.ds(off, tk_online)]
            for h in heads:
                s = jnp.dot(ks[:, h * HEAD_PAD:(h + 1) * HEAD_PAD], qs[h], preferred_element_type=F32)
                m_old = m_ref[h:h + 1, :]
                m_new = jnp.maximum(m_old, jnp.max(s, axis=0, keepdims=True))
                p = jnp.exp2(s - m_new).astype(BF16)
                accs[h][...] = (accs[h][...] * jnp.exp2(m_old - m_new)
                                + jnp.dot(vts[h * V_AUG:(h + 1) * V_AUG], p, preferred_element_type=F32))
                m_ref[h:h + 1, :] = m_new
            return 0

        lax.fori_loop(0, s_len // tk_online, body, 0)

    outs = [acc[:V_HEAD_DIM] / acc[V_HEAD_DIM:V_HEAD_DIM + 1] for acc in accs]
    o_ref[0] = jnp.concatenate(outs, axis=0).T.astype(BF16)


def _attention(online, qt, k, vt, tq, tk, tk_online):
    b, _, s = qt.shape
    hps = ATTN_HEADS
    return pl.pallas_call(
        functools.partial(_attn_kernel, tk=tk, tk_online=tk_online),
        grid_spec=pltpu.PrefetchScalarGridSpec(
            num_scalar_prefetch=1,
            grid=(b, N_HEADS // hps, s // tq),
            in_specs=[pl.BlockSpec((1, hps * HEAD_PAD, tq), lambda bi, j, i, on: (bi, j, i)),
                      pl.BlockSpec((1, s, hps * HEAD_PAD), lambda bi, j, i, on: (bi, 0, j)),
                      pl.BlockSpec((1, hps * V_AUG, s), lambda bi, j, i, on: (bi, j, 0))],
            out_specs=pl.BlockSpec((1, tq, hps * V_HEAD_DIM), lambda bi, j, i, on: (bi, i, j)),
            scratch_shapes=[pltpu.VMEM((V_AUG, tq), F32)] * hps + [pltpu.VMEM((8, tq), F32)]),
        out_shape=jax.ShapeDtypeStruct((b, s, N_HEADS * V_HEAD_DIM), BF16),
        compiler_params=_cparams(("parallel", "parallel", "arbitrary")),
        name="attn",
    )(online, qt, k, vt)


def _postmix_kernel(u_ref, up_ref, un_ref, a_ref, x_ref, mod_ref, pw_ref, ps_ref, wo_ref, n2w_ref,
                    wrh_ref, wrl_ref, x1_ref, h2_ref, lg_ref, *, seq):
    i = pl.program_id(1)
    ts = u_ref.shape[1]
    ext_rows = ts + 2 * HALO
    ext = jnp.concatenate([up_ref[0], u_ref[0], un_ref[0]], axis=0).astype(F32)
    pos = i * ts - HALO + lax.broadcasted_iota(jnp.int32, (ext_rows, 1), 0)
    ext = jnp.where((pos >= 0) & (pos < seq), ext, 0.0)
    p = i * ts + lax.broadcasted_iota(jnp.int32, (ts, 1), 0)

    outs = []
    for g, w in enumerate(POOL_WINDOWS):
        left = w // 2
        right = w - 1 - left
        t = ext[:, g * POOL_GROUP:(g + 1) * POOL_GROUP]
        step = 1
        while step < w:
            t = t + pltpu.roll(t, ext_rows - step, 0)
            step *= 2
        win = pltpu.roll(t, left, 0)[HALO:HALO + ts]
        cnt = (jnp.minimum(p + right + 1, seq) - jnp.maximum(p - left, 0)).astype(F32)
        d = win * (1.0 / cnt) - ext[HALO:HALO + ts, g * POOL_GROUP:(g + 1) * POOL_GROUP]
        outs.append(jnp.dot(d.astype(BF16), pw_ref[g], preferred_element_type=F32))
    pool = (jnp.concatenate(outs, axis=-1) * ps_ref[...]).astype(BF16)

    mix = (jnp.dot(pool, wo_ref[:POOL_WIDTH, :], preferred_element_type=F32)
           + jnp.dot(a_ref[0], wo_ref[POOL_WIDTH:, :], preferred_element_type=F32))
    x1 = x_ref[0] + mod_ref[0, 2:3, :] * mix
    x1_ref[0] = x1
    r = lax.rsqrt(jnp.mean(x1 * x1, axis=-1, keepdims=True) + EPS)
    gain2 = n2w_ref[...] * (1.0 + mod_ref[0, 4:5, :])
    h2 = x1 * r * gain2 + mod_ref[0, 3:4, :]
    h2_ref[0] = _pack_rows(h2)
    hi = h2.astype(BF16)
    lo = (h2 - hi.astype(F32)).astype(BF16)
    nt = (((1,), (1,)), ((), ()))
    lg_ref[...] = (lax.dot_general(wrh_ref[...], hi, nt, preferred_element_type=F32)
                   + lax.dot_general(wrl_ref[...], hi, nt, preferred_element_type=F32)
                   + lax.dot_general(wrh_ref[...], lo, nt, preferred_element_type=F32))


def _postmix(u, attn, x, mod, p, ts):
    b, s, d = x.shape
    nt = s // ts
    hb = ts // HALO
    const = lambda shape: pl.BlockSpec(shape, lambda bi, i: (0,) * len(shape))
    return pl.pallas_call(
        functools.partial(_postmix_kernel, seq=s),
        grid=(b, nt),
        in_specs=[pl.BlockSpec((1, ts, POOL_WIDTH), lambda bi, i: (bi, i, 0)),
                  pl.BlockSpec((1, HALO, POOL_WIDTH), lambda bi, i: (bi, jnp.maximum(i * hb - 1, 0), 0)),
                  pl.BlockSpec((1, HALO, POOL_WIDTH),
                               lambda bi, i: (bi, jnp.minimum((i + 1) * hb, s // HALO - 1), 0)),
                  pl.BlockSpec((1, ts, POOL_WIDTH), lambda bi, i: (bi, i, 0)),
                  pl.BlockSpec((1, ts, d), lambda bi, i: (bi, i, 0)),
                  pl.BlockSpec((1, 6, d), lambda bi, i: (bi, 0, 0)),
                  const((len(POOL_WINDOWS), POOL_GROUP, POOL_GROUP)), const((1, POOL_WIDTH)),
                  const((d, d)), const((1, d)), const((N_EXPERTS, d)), const((N_EXPERTS, d))],
        out_specs=[pl.BlockSpec((1, ts, d), lambda bi, i: (bi, i, 0)),
                   pl.BlockSpec((1, ts, PACK_W), lambda bi, i: (bi, i, 0)),
                   pl.BlockSpec((N_EXPERTS, ts), lambda bi, i: (0, bi * nt + i))],
        out_shape=[jax.ShapeDtypeStruct((b, s, d), F32),
                   jax.ShapeDtypeStruct((b, s, PACK_W), jnp.int32),
                   jax.ShapeDtypeStruct((N_EXPERTS, b * s), F32)],
        compiler_params=_cparams(("parallel", "parallel")),
        name="postmix",
    )(u, u, u, attn, x, mod, p["pool_w"], p["pool_scale"], p["w_o"], p["norm2_w"],
      p["w_router_hi"], p["w_router_lo"])


def _router_kernel(lg_ref, bias_ref, tri_ref, idx_ref, wts_ref, pos_ref, cnt_ref, carry_ref):
    @pl.when(pl.program_id(0) == 0)
    def _():
        carry_ref[...] = jnp.zeros_like(carry_ref)

    ts = lg_ref.shape[1]
    gsz = N_EXPERTS // N_GROUPS
    ninf = -jnp.inf
    scores = 1.0 / (1.0 + jnp.exp(-lg_ref[...]))
    choice = scores + bias_ref[...]
    sub = lax.broadcasted_iota(jnp.int32, (gsz, ts), 0)

    gs_rows = []
    for g in range(N_GROUPS):
        grp = choice[g * gsz:(g + 1) * gsz]
        m1 = jnp.max(grp, axis=0, keepdims=True)
        i1 = jnp.min(jnp.where(grp == m1, sub, gsz), axis=0, keepdims=True)
        m2 = jnp.max(jnp.where(sub == i1, ninf, grp), axis=0, keepdims=True)
        gs_rows.append(m1 + m2)
    gs = jnp.concatenate(gs_rows, axis=0)

    rank = jnp.zeros((N_GROUPS, ts), jnp.int32)
    for g in range(N_GROUPS):
        row = gs[g:g + 1]
        beats = (row > gs) | ((row == gs) & (sub > g))
        rank = rank + beats.astype(jnp.int32)
    gsel = rank < TOPK_GROUPS

    masked = jnp.concatenate(
        [jnp.where(gsel[g:g + 1], choice[g * gsz:(g + 1) * gsz], ninf) for g in range(N_GROUPS)],
        axis=0)
    eio = lax.broadcasted_iota(jnp.int32, (N_EXPERTS, ts), 0)
    idx_rows, w_rows, hits = [], [], []
    for _ in range(TOP_K):
        m = jnp.max(masked, axis=0, keepdims=True)
        i = jnp.min(jnp.where(masked == m, eio, N_EXPERTS), axis=0, keepdims=True)
        hit = eio == i
        w_rows.append(jnp.sum(jnp.where(hit, scores, 0.0), axis=0, keepdims=True))
        masked = jnp.where(hit, ninf, masked)
        idx_rows.append(i)
        hits.append(hit)

    wsum = functools.reduce(lambda a, c: a + c, w_rows)
    pad_i = [jnp.zeros((1, ts), jnp.int32)] * (TOPK_PAD - TOP_K)
    pad_f = [jnp.zeros((1, ts), F32)] * (TOPK_PAD - TOP_K)
    idx_ref[...] = jnp.concatenate(idx_rows + pad_i, axis=0)
    wts_ref[...] = jnp.concatenate([w / wsum * ROUTED_SCALE for w in w_rows] + pad_f, axis=0)

    sel = functools.reduce(lambda a, c: a | c, hits)
    onehot = jnp.where(sel, 1.0, 0.0).astype(BF16)
    run = jnp.dot(onehot, tri_ref[...], preferred_element_type=F32) + carry_ref[:, 0:1]
    pos_rows = [jnp.sum(jnp.where(h, run - 1.0, 0.0), axis=0, keepdims=True).astype(jnp.int32)
                for h in hits]
    pos_ref[...] = jnp.concatenate(pos_rows + pad_i, axis=0)
    total = run[:, ts - 1:ts]
    carry_ref[...] = jnp.broadcast_to(total, carry_ref.shape)
    cnt_ref[...] = jnp.broadcast_to(total, cnt_ref.shape)


def _router(logits_t, router_bias, ts):
    t = logits_t.shape[1]
    tri = jnp.triu(jnp.ones((ts, ts), BF16))
    tok = pl.BlockSpec((TOPK_PAD, ts), lambda i: (0, i))
    return pl.pallas_call(
        _router_kernel,
        grid=(t // ts,),
        in_specs=[pl.BlockSpec((N_EXPERTS, ts), lambda i: (0, i)),
                  pl.BlockSpec((N_EXPERTS, 1), lambda i: (0, 0)),
                  pl.BlockSpec((ts, ts), lambda i: (0, 0))],
        out_specs=[tok, tok, tok, pl.BlockSpec((N_EXPERTS, LANES), lambda i: (0, 0))],
        out_shape=[jax.ShapeDtypeStruct((TOPK_PAD, t), jnp.int32),
                   jax.ShapeDtypeStruct((TOPK_PAD, t), F32),
                   jax.ShapeDtypeStruct((TOPK_PAD, t), jnp.int32),
                   jax.ShapeDtypeStruct((N_EXPERTS, LANES), F32)],
        scratch_shapes=[pltpu.VMEM((N_EXPERTS, LANES), F32)],
        compiler_params=_cparams(("arbitrary",)),
        name="router",
    )(logits_t, router_bias.reshape(N_EXPERTS, 1), tri)


def _dest_kernel(idx_ref, pos_ref, start_ref, dest_ref):
    ts = idx_ref.shape[1]
    eio = lax.broadcasted_iota(jnp.int32, (N_EXPERTS, ts), 0)
    start = start_ref[...]
    rows = [pos_ref[k:k + 1, :]
            + jnp.sum(jnp.where(eio == idx_ref[k:k + 1, :], start, 0), axis=0, keepdims=True)
            for k in range(TOPK_PAD)]
    dest_ref[...] = jnp.concatenate(rows, axis=0)


def _dest_rows(idx, pos, pad_start, ts):
    t = idx.shape[1]
    tok = pl.BlockSpec((TOPK_PAD, ts), lambda i: (0, i))
    return pl.pallas_call(
        _dest_kernel,
        grid=(t // ts,),
        in_specs=[tok, tok, pl.BlockSpec((N_EXPERTS, 1), lambda i: (0, 0))],
        out_specs=tok,
        out_shape=jax.ShapeDtypeStruct((TOPK_PAD, t), jnp.int32),
        compiler_params=_cparams(("parallel",)),
        name="dest_rows",
    )(idx, pos, pad_start.reshape(N_EXPERTS, 1))


def _pack_rows(x):
    bits = lax.bitcast_convert_type(x.astype(BF16).astype(F32), jnp.int32)
    return bits[:, :PACK_W] | lax.shift_right_logical(bits[:, PACK_W:], 16)


def _unpack_rows(words):
    hi = lax.bitcast_convert_type(words & jnp.int32(-65536), F32)
    lo = lax.bitcast_convert_type(lax.shift_left(words, 16), F32)
    return jnp.concatenate([hi, lo], axis=1)


def _sc_mesh():
    return plsc.VectorSubcoreMesh(core_axis_name="core", subcore_axis_name="subcore")


def _sc_worker_base(rows_per_worker):
    return (lax.axis_index("subcore") * SC_CORES + lax.axis_index("core")) * rows_per_worker


def _sc_scatter_rows(x, dest, n_rows):
    t, w = x.shape
    per_worker = t // SC_WORKERS
    assert per_worker * SC_WORKERS == t and per_worker % SC_CHUNK == 0

    @functools.partial(
        pl.kernel, out_type=jax.ShapeDtypeStruct((n_rows, w), x.dtype), mesh=_sc_mesh(),
        scratch_types=[pltpu.VMEM((TOPK_PAD, SC_CHUNK), jnp.int32), pltpu.VMEM((SC_CHUNK, w), x.dtype),
                       pltpu.SemaphoreType.DMA])
    def scatter(x_hbm, dest_hbm, out_hbm, idx_v, rows_v, sem):
        base = _sc_worker_base(per_worker)

        @pl.loop(0, per_worker // SC_CHUNK)
        def _(c):
            off = pl.multiple_of(base + c * SC_CHUNK, SC_CHUNK)
            pltpu.sync_copy(dest_hbm.at[:, pl.ds(off, SC_CHUNK)], idx_v)
            pltpu.sync_copy(x_hbm.at[pl.ds(off, SC_CHUNK)], rows_v)
            copies = [pltpu.async_copy(rows_v, out_hbm.at[idx_v.at[k]], sem) for k in range(TOP_K)]
            for cp in copies:
                cp.wait()

    return scatter(x, dest)


def _sc_gather_rows(table, idx):
    m = idx.shape[0]
    w = table.shape[1]
    per_worker = m // SC_WORKERS
    assert per_worker * SC_WORKERS == m and per_worker % SC_CHUNK == 0

    @functools.partial(
        pl.kernel, out_type=jax.ShapeDtypeStruct((m, w), table.dtype), mesh=_sc_mesh(),
        scratch_types=[pltpu.VMEM((SC_CHUNK,), jnp.int32), pltpu.VMEM((SC_CHUNK, w), table.dtype),
                       pltpu.SemaphoreType.DMA])
    def gather(table_hbm, idx_hbm, out_hbm, idx_v, rows_v, sem):
        base = _sc_worker_base(per_worker)

        @pl.loop(0, per_worker // SC_CHUNK)
        def _(c):
            off = pl.multiple_of(base + c * SC_CHUNK, SC_CHUNK)
            pltpu.sync_copy(idx_hbm.at[pl.ds(off, SC_CHUNK)], idx_v)
            pltpu.async_copy(table_hbm.at[idx_v], rows_v, sem).wait()
            pltpu.sync_copy(rows_v, out_hbm.at[pl.ds(off, SC_CHUNK)])

    return gather(table, idx)


def _experts_kernel(blk_exp_ref, n_used_ref, xs_ref, wg_ref, wu_ref, wd_ref, ys_ref):
    @pl.when(pl.program_id(0) < n_used_ref[0])
    def _():
        xb = _unpack_rows(xs_ref[...]).astype(BF16)
        g = jnp.dot(xb, wg_ref[0].astype(BF16), preferred_element_type=F32)
        u = jnp.dot(xb, wu_ref[0].astype(BF16), preferred_element_type=F32)
        hb = (_silu(g) * u).astype(BF16)
        ys_ref[...] = _pack_rows(jnp.dot(hb, wd_ref[0].astype(BF16), preferred_element_type=F32))


def _row_block(t):
    per_expert = t * TOP_K // N_EXPERTS
    return 1024 if per_expert >= 4 * 1024 else (512 if per_expert >= 4 * 512 else 256)


def _experts(blk_exp, n_used, xs, w_gate, w_up, w_down, row_block):
    n_rows, w = xs.shape
    d = D_MODEL
    n_blocks = n_rows // row_block
    row = lambda i, be, nu: (jnp.minimum(i, nu[0] - 1), 0)
    wsel = lambda i, be, nu: (be[i], 0, 0)
    return pl.pallas_call(
        _experts_kernel,
        grid_spec=pltpu.PrefetchScalarGridSpec(
            num_scalar_prefetch=2,
            grid=(n_blocks,),
            in_specs=[pl.BlockSpec((row_block, w), row),
                      pl.BlockSpec((1, d, EXPERT_FF), wsel),
                      pl.BlockSpec((1, d, EXPERT_FF), wsel),
                      pl.BlockSpec((1, EXPERT_FF, d), wsel)],
            out_specs=pl.BlockSpec((row_block, w), row)),
        out_shape=jax.ShapeDtypeStruct((n_rows, w), jnp.int32),
        compiler_params=_cparams(("arbitrary",)),
        name="experts",
    )(blk_exp, n_used, xs, w_gate, w_up, w_down)


def _combine_kernel(g_ref, w_ref, h2_ref, x1_ref, mod_ref, wsg_ref, wsu_ref, wsd_ref, out_ref):
    hb = _unpack_rows(h2_ref[...]).astype(BF16)
    g = jnp.dot(hb, wsg_ref[...], preferred_element_type=F32)
    u = jnp.dot(hb, wsu_ref[...], preferred_element_type=F32)
    acc = jnp.dot((_silu(g) * u).astype(BF16), wsd_ref[...], preferred_element_type=F32)
    w = w_ref[...]
    for k in range(TOP_K):
        acc = acc + _unpack_rows(g_ref[k]) * w[:, k:k + 1]
    out_ref[...] = x1_ref[...] + mod_ref[0, 5:6, :] * acc


def _combine(gathered, wts_t, h2p, x1, mod, p, tt, seq):
    t, d = x1.shape
    per_seq = seq // tt
    tok = pl.BlockSpec((tt, d), lambda i: (i, 0))
    const = lambda shape: pl.BlockSpec(shape, lambda i: (0,) * len(shape))
    return pl.pallas_call(
        _combine_kernel,
        grid=(t // tt,),
        in_specs=[pl.BlockSpec((TOP_K, tt, PACK_W), lambda i: (0, i, 0)),
                  pl.BlockSpec((tt, TOPK_PAD), lambda i: (i, 0)),
                  pl.BlockSpec((tt, PACK_W), lambda i: (i, 0)),
                  tok,
                  pl.BlockSpec((1, 6, d), lambda i: (i // per_seq, 0, 0)),
                  const((d, SHARED_FF)), const((d, SHARED_FF)), const((SHARED_FF, d))],
        out_specs=tok,
        out_shape=jax.ShapeDtypeStruct((t, d), F32),
        compiler_params=_cparams(("parallel",)),
        name="combine",
    )(gathered, wts_t, h2p, x1, mod, p["ws_gate"], p["ws_up"], p["ws_down"])


def _rope_tables(s):
    half = QK_ROPE_DIM // 2
    inv_freq = ROPE_THETA ** (-jnp.arange(half, dtype=F32) / half)
    ang = jnp.arange(s, dtype=F32)[:, None] * inv_freq[None, :]
    cos, sin = jnp.cos(ang), jnp.sin(ang)
    z = lambda n: jnp.zeros((s, n), F32)
    tab_cos = jnp.concatenate([jnp.ones((s, QK_NOPE_DIM), F32), cos, cos, z(HEAD_PAD - QK_HEAD_DIM)], 1)
    tab_sa = jnp.concatenate([z(QK_NOPE_DIM), -sin, z(HEAD_PAD - QK_NOPE_DIM - half)], 1)
    tab_sb = jnp.concatenate([z(QK_NOPE_DIM + half), sin, z(HEAD_PAD - QK_HEAD_DIM)], 1)
    return dict(rope_cos=tab_cos, rope_sa=tab_sa, rope_sb=tab_sb, rope_cos_t=cos.T, rope_sin_t=sin.T)


def _prep_weights(norm1_w, w_in, pool_w, pool_scale, q_a_norm_w, w_q_b, kv_a_norm_w, w_kv_b,
                  q_norm_w, k_norm_w, w_o, norm2_w, w_router, w_gate, w_up, w_down,
                  ws_gate, ws_up, ws_down):
    d = D_MODEL
    c0 = POOL_WIDTH + Q_LORA_RANK + KV_LORA_RANK
    pad_h = HEAD_PAD - QK_HEAD_DIM
    w_in_p = jnp.concatenate(
        [w_in[:, :c0], jnp.zeros((d, QK_NOPE_DIM), F32), w_in[:, c0:], jnp.zeros((d, pad_h), F32)], 1)
    w_q = jnp.pad(w_q_b.reshape(Q_LORA_RANK, N_HEADS, QK_HEAD_DIM), ((0, 0), (0, 0), (0, pad_h)))
    kv = w_kv_b.reshape(KV_LORA_RANK, N_HEADS, QK_NOPE_DIM + V_HEAD_DIM)
    w_k = jnp.pad(kv[:, :, :QK_NOPE_DIM], ((0, 0), (0, 0), (0, HEAD_PAD - QK_NOPE_DIM)))
    w_v = kv[:, :, QK_NOPE_DIM:]
    w_r_t = w_router.T
    w_r_hi = w_r_t.astype(BF16)
    w_r_lo = (w_r_t - w_r_hi.astype(F32)).astype(BF16)
    q_gain = q_norm_w * (QK_HEAD_DIM ** -0.5 * LOG2E)
    bound = QK_HEAD_DIM * jnp.max(jnp.abs(q_gain)) * jnp.max(jnp.abs(k_norm_w)) * 1.02 + 0.25
    return dict(
        score_bound=bound,
        norm1_w=norm1_w.reshape(1, d), w_in=w_in_p.astype(BF16),
        q_a_norm_w=q_a_norm_w.reshape(1, -1), w_q_t=w_q.reshape(Q_LORA_RANK, -1).T.astype(BF16),
        kv_a_norm_w=kv_a_norm_w.reshape(1, -1), w_k=w_k.reshape(KV_LORA_RANK, -1).astype(BF16),
        w_v_t=w_v.reshape(KV_LORA_RANK, -1).T.astype(BF16),
        q_gain_col=jnp.broadcast_to(jnp.pad(q_gain, (0, pad_h))[:, None], (HEAD_PAD, LANES)),
        k_norm_w=jnp.pad(k_norm_w, (0, pad_h)).reshape(1, HEAD_PAD),
        pool_w=pool_w.astype(BF16), pool_scale=pool_scale.reshape(1, -1), w_o=w_o.astype(BF16),
        norm2_w=norm2_w.reshape(1, d), w_router_hi=w_r_hi, w_router_lo=w_r_lo,
        w_gate=w_gate, w_up=w_up, w_down=w_down,
        ws_gate=ws_gate.astype(BF16), ws_up=ws_up.astype(BF16), ws_down=ws_down.astype(BF16))


def _tile(n, pref):
    return pref if n % pref == 0 else n


def _mixer_and_routing(x, mod, router_bias, p):
    b, s, d = x.shape
    t = b * s
    ts = _tile(s, 512)
    p = dict(p, **_rope_tables(s))

    u, qt, k, vt = _inproj(x, mod, p, ts)
    online = (p["score_bound"] > MAX_UNSHIFTED_SCORE).astype(jnp.int32).reshape(1)
    attn = _attention(online, qt, k, vt, _tile(s, 1024), _tile(s, 2048), _tile(s, 512))
    x1, h2p, logits_t = _postmix(u, attn, x, mod, p, _tile(s, 1024))
    tr = _tile(t, 1024)
    idx, wts, pos, cnt = _router(logits_t, router_bias, tr)

    rb = _row_block(t)
    counts = cnt[:, 0].astype(jnp.int32)
    padded = (counts + rb - 1) // rb * rb
    pad_end = jnp.cumsum(padded)
    dest = _dest_rows(idx, pos, pad_end - padded, _tile(t, 2048))
    n_blocks = -(-t * TOP_K // rb) + N_EXPERTS
    blk_row = jnp.arange(n_blocks, dtype=jnp.int32)[:, None] * rb
    blk_exp = jnp.minimum(jnp.sum((pad_end[None, :] <= blk_row).astype(jnp.int32), axis=1),
                          N_EXPERTS - 1)
    n_used = (pad_end[-1:] // rb).astype(jnp.int32)
    h2f = h2p.reshape(t, PACK_W)
    xs = _sc_scatter_rows(h2f, dest, n_blocks * rb)
    return dict(xs=xs, dest=dest, blk_exp=blk_exp, n_used=n_used, wts_t=wts.T, h2f=h2f,
                x1=x1.reshape(t, d), mod=mod, shape=(b, s, d), row_block=rb)


def _gather_and_combine(st, ys, p):
    b, s, d = st["shape"]
    t = b * s
    gathered = _sc_gather_rows(ys, st["dest"][:TOP_K].reshape(TOP_K * t)).reshape(TOP_K, t, PACK_W)
    out = _combine(gathered, st["wts_t"], st["h2f"], st["x1"], st["mod"], p, _tile(s, 512), s)
    return out.reshape(b, s, d)


def kernel(x_prompt, x_sample, c_prompt, c_sample, w_ada, b_ada, norm1_w, w_in, pool_w, pool_scale,
           q_a_norm_w, w_q_b, kv_a_norm_w, w_kv_b, q_norm_w, k_norm_w, w_o, norm2_w, w_router,
           router_bias, w_gate, w_up, w_down, ws_gate, ws_up, ws_down):
    assert w_ada.shape[0] == 1, "single-layer encoder"
    p = _prep_weights(norm1_w[0], w_in[0], pool_w[0], pool_scale[0], q_a_norm_w[0], w_q_b[0],
                      kv_a_norm_w[0], w_kv_b[0], q_norm_w[0], k_norm_w[0], w_o[0], norm2_w[0],
                      w_router[0], w_gate[0], w_up[0], w_down[0], ws_gate[0], ws_up[0], ws_down[0])
    nb = x_prompt.shape[0]
    c = jnp.concatenate([c_prompt, c_sample], axis=0).astype(F32)
    mod = _adaln(c, w_ada[0], b_ada[0]).reshape(c.shape[0], 6, D_MODEL)
    experts = lambda st, xs: _experts(st["blk_exp"], st["n_used"], xs, p["w_gate"], p["w_up"],
                                      p["w_down"], st["row_block"])

    sp = _mixer_and_routing(x_prompt, mod[:nb], router_bias[0], p)
    dest_p, x_sample = lax.optimization_barrier((sp["dest"], x_sample))
    sp["dest"] = dest_p
    ss = _mixer_and_routing(x_sample, mod[nb:], router_bias[0], p)
    ys_p = experts(sp, sp["xs"])
    ys_p, xs_s = lax.optimization_barrier((ys_p, ss["xs"]))
    ys_s = experts(ss, xs_s)
    y_prompt = _gather_and_combine(sp, ys_p, p)
    y_sample = _gather_and_combine(ss, ys_s, p)
    return (y_prompt, y_sample)
```

```python
import functools

import jax
import jax.numpy as jnp
from jax import lax
from jax.experimental import pallas as pl
from jax.experimental.pallas import tpu as pltpu
from jax.experimental.pallas import tpu_sc as plsc

D_MODEL = 1024
POOL_WIDTH = 512
POOL_WINDOWS = (2, 4, 8, 16)
POOL_GROUP = 128
N_HEADS = 8
V_HEAD_DIM = 64
QK_NOPE_DIM = 64
QK_ROPE_DIM = 32
QK_HEAD_DIM = 96
Q_LORA_RANK = 256
KV_LORA_RANK = 128
ROPE_THETA = 10000.0
N_EXPERTS = 64
TOP_K = 6
N_GROUPS = 8
TOPK_GROUPS = 4
EXPERT_FF = 256
SHARED_FF = 256
ROUTED_SCALE = 2.5
EPS = 1e-6

LANES = 128
HEAD_PAD = 128
HALO = 16
TOPK_PAD = 8
V_AUG = 80
ATTN_HEADS = 4
XS_RING = 3
MAX_UNSHIFTED_SCORE = 40.0
LOG2E = 1.4426950408889634
PACK_W = D_MODEL // 2
SC_CORES = 2
SC_WORKERS = 32
SC_CHUNK = 128
VMEM_LIMIT = 48 * 1024 * 1024

F32 = jnp.float32
BF16 = jnp.bfloat16
QK_DTYPE = jnp.bfloat16


def _cparams(sem):
    return pltpu.CompilerParams(dimension_semantics=sem, vmem_limit_bytes=VMEM_LIMIT)


def _silu(x):
    return x * (1.0 / (1.0 + jnp.exp(-x)))


def _adaln_kernel(c_ref, w_ref, b_ref, o_ref):
    c = c_ref[...]
    o_ref[...] = jnp.dot(_silu(c), w_ref[...], preferred_element_type=F32,
                         precision=lax.Precision.HIGHEST) + b_ref[...]


def _adaln(c, w_ada, b_ada):
    nb, d = c.shape
    n = w_ada.shape[1]
    tn = 1536
    return pl.pallas_call(
        _adaln_kernel,
        grid=(n // tn,),
        in_specs=[pl.BlockSpec((nb, d), lambda j: (0, 0)),
                  pl.BlockSpec((d, tn), lambda j: (0, j)),
                  pl.BlockSpec((1, tn), lambda j: (0, j))],
        out_specs=pl.BlockSpec((nb, tn), lambda j: (0, j)),
        out_shape=jax.ShapeDtypeStruct((nb, n), F32),
        compiler_params=_cparams(("arbitrary",)),
        name="adaln",
    )(c, w_ada, b_ada.reshape(1, n))


def _inproj_kernel(x_ref, mod_ref, n1w_ref, win_ref, qan_ref, wqt_ref, kvan_ref, wk_ref, wvt_ref,
                   qg_ref, kg_ref, cos_ref, sa_ref, sb_ref, cost_ref, sint_ref,
                   u_ref, qt_ref, k_ref, vt_ref):
    ts = x_ref.shape[1]
    half = QK_ROPE_DIM // 2
    x = x_ref[0]
    shift1 = mod_ref[0, 0:1, :]
    gain1 = n1w_ref[...] * (1.0 + mod_ref[0, 1:2, :])
    r = lax.rsqrt(jnp.mean(x * x, axis=-1, keepdims=True) + EPS)
    h = x * r * gain1 + shift1
    z = jnp.dot(h.astype(BF16), win_ref[...], preferred_element_type=F32)
    u_ref[0] = z[:, :POOL_WIDTH].astype(BF16)

    cq = z[:, POOL_WIDTH:POOL_WIDTH + Q_LORA_RANK]
    cqn = cq * lax.rsqrt(jnp.mean(cq * cq, axis=-1, keepdims=True) + EPS) * qan_ref[...]
    qt = jnp.dot(wqt_ref[...], cqn.T.astype(BF16), preferred_element_type=F32)
    reps = ts // LANES
    qg = jnp.concatenate([qg_ref[...]] * reps, axis=1)
    cost = cost_ref[...]
    sint = sint_ref[...]
    spare = jnp.zeros((HEAD_PAD - QK_HEAD_DIM, ts), F32)
    for hd in range(N_HEADS):
        t = qt[hd * HEAD_PAD:(hd + 1) * HEAD_PAD]
        rn = lax.rsqrt(jnp.sum(t * t, axis=0, keepdims=True) * (1.0 / QK_HEAD_DIM) + EPS)
        tn = t * rn * qg
        t1 = tn[QK_NOPE_DIM:QK_NOPE_DIM + half]
        t2 = tn[QK_NOPE_DIM + half:QK_HEAD_DIM]
        out = jnp.concatenate([tn[:QK_NOPE_DIM], t1 * cost - t2 * sint, t1 * sint + t2 * cost, spare],
                              axis=0)
        qt_ref[0, hd * HEAD_PAD:(hd + 1) * HEAD_PAD, :] = out.astype(QK_DTYPE)

    c0 = POOL_WIDTH + Q_LORA_RANK
    ckv = z[:, c0:c0 + KV_LORA_RANK]
    ckvn = ckv * lax.rsqrt(jnp.mean(ckv * ckv, axis=-1, keepdims=True) + EPS) * kvan_ref[...]

    vt = jnp.dot(wvt_ref[...], ckvn.T.astype(BF16), preferred_element_type=F32)
    ones = jnp.ones((V_AUG - V_HEAD_DIM, ts), BF16)
    for hd in range(N_HEADS):
        vt_ref[0, hd * V_AUG:hd * V_AUG + V_HEAD_DIM, :] = (
            vt[hd * V_HEAD_DIM:(hd + 1) * V_HEAD_DIM].astype(BF16))
        vt_ref[0, hd * V_AUG + V_HEAD_DIM:(hd + 1) * V_AUG, :] = ones

    kk = jnp.dot(ckvn.astype(BF16), wk_ref[...], preferred_element_type=F32)
    kpe = z[:, c0 + KV_LORA_RANK:]
    kg = kg_ref[...]
    pe_ssq = jnp.sum(kpe * kpe, axis=-1, keepdims=True)
    pg = kpe * kg
    pe_rot = (pg * cos_ref[...] + pltpu.roll(pg, HEAD_PAD - half, 1) * sa_ref[...]
              + pltpu.roll(pg, half, 1) * sb_ref[...])
    for hd in range(N_HEADS):
        t = kk[:, hd * HEAD_PAD:(hd + 1) * HEAD_PAD]
        ssq = jnp.sum(t * t, axis=-1, keepdims=True) + pe_ssq
        rn = lax.rsqrt(ssq * (1.0 / QK_HEAD_DIM) + EPS)
        k_ref[0, :, hd * HEAD_PAD:(hd + 1) * HEAD_PAD] = ((t * kg + pe_rot) * rn).astype(QK_DTYPE)


def _inproj(x, mod, p, ts):
    b, s, d = x.shape
    qk_w = N_HEADS * HEAD_PAD
    half = QK_ROPE_DIM // 2
    const = lambda shape: pl.BlockSpec(shape, lambda bi, i: (0,) * len(shape))
    row_tab = pl.BlockSpec((ts, HEAD_PAD), lambda bi, i: (i, 0))
    col_tab = pl.BlockSpec((half, ts), lambda bi, i: (0, i))
    return pl.pallas_call(
        _inproj_kernel,
        grid=(b, s // ts),
        in_specs=[pl.BlockSpec((1, ts, d), lambda bi, i: (bi, i, 0)),
                  pl.BlockSpec((1, 6, d), lambda bi, i: (bi, 0, 0)),
                  const((1, d)), const((d, d)), const((1, Q_LORA_RANK)),
                  const((qk_w, Q_LORA_RANK)), const((1, KV_LORA_RANK)),
                  const((KV_LORA_RANK, qk_w)), const((N_HEADS * V_HEAD_DIM, KV_LORA_RANK)),
                  const((HEAD_PAD, LANES)), const((1, HEAD_PAD)),
                  row_tab, row_tab, row_tab, col_tab, col_tab],
        out_specs=[pl.BlockSpec((1, ts, POOL_WIDTH), lambda bi, i: (bi, i, 0)),
                   pl.BlockSpec((1, qk_w, ts), lambda bi, i: (bi, 0, i)),
                   pl.BlockSpec((1, ts, qk_w), lambda bi, i: (bi, i, 0)),
                   pl.BlockSpec((1, N_HEADS * V_AUG, ts), lambda bi, i: (bi, 0, i))],
        out_shape=[jax.ShapeDtypeStruct((b, s, POOL_WIDTH), BF16),
                   jax.ShapeDtypeStruct((b, qk_w, s), QK_DTYPE),
                   jax.ShapeDtypeStruct((b, s, qk_w), QK_DTYPE),
                   jax.ShapeDtypeStruct((b, N_HEADS * V_AUG, s), BF16)],
        compiler_params=_cparams(("parallel", "parallel")),
        name="inproj",
    )(x, mod, p["norm1_w"], p["w_in"], p["q_a_norm_w"], p["w_q_t"], p["kv_a_norm_w"], p["w_k"],
      p["w_v_t"], p["q_gain_col"], p["k_norm_w"],
      p["rope_cos"], p["rope_sa"], p["rope_sb"], p["rope_cos_t"], p["rope_sin_t"])


def _attn_kernel(online_ref, qt_ref, k_ref, vt_ref, o_ref, *scratch, tk, tk_online):
    accs, m_ref = scratch[:ATTN_HEADS], scratch[ATTN_HEADS]
    s_len = k_ref.shape[1]
    for acc in accs:
        acc[...] = jnp.zeros_like(acc)
    heads = range(ATTN_HEADS)
    qs = [qt_ref[0, h * HEAD_PAD:(h + 1) * HEAD_PAD, :] for h in heads]

    @pl.when(online_ref[0] == 0)
    def _():
        def body(c, _):
            off = pl.multiple_of(c * tk, tk)
            ks = k_ref[0, pl.ds(off, tk), :]
            vts = vt_ref[0, :, pl.ds(off, tk)]
            for h in heads:
                s = jnp.dot(ks[:, h * HEAD_PAD:(h + 1) * HEAD_PAD], qs[h], preferred_element_type=F32)
                accs[h][...] += jnp.dot(vts[h * V_AUG:(h + 1) * V_AUG], jnp.exp2(s).astype(BF16),
                                        preferred_element_type=F32)
            return 0

        lax.fori_loop(0, s_len // tk, body, 0)

    @pl.when(online_ref[0] != 0)
    def _():
        m_ref[...] = jnp.full_like(m_ref, -jnp.inf)

        def body(c, _):
            off = pl.multiple_of(c * tk_online, tk_online)
            ks = k_ref[0, pl.ds(off, tk_online), :]
            vts = vt_ref[0, :, pl.ds(off, tk_online)]
            for h in heads:
                s = jnp.dot(ks[:, h * HEAD_PAD:(h + 1) * HEAD_PAD], qs[h], preferred_element_type=F32)
                m_old = m_ref[h:h + 1, :]
                m_new = jnp.maximum(m_old, jnp.max(s, axis=0, keepdims=True))
                p = jnp.exp2(s - m_new).astype(BF16)
                accs[h][...] = (accs[h][...] * jnp.exp2(m_old - m_new)
                                + jnp.dot(vts[h * V_AUG:(h + 1) * V_AUG], p, preferred_element_type=F32))
                m_ref[h:h + 1, :] = m_new
            return 0

        lax.fori_loop(0, s_len // tk_online, body, 0)

    outs = [acc[:V_HEAD_DIM] / acc[V_HEAD_DIM:V_HEAD_DIM + 1] for acc in accs]
    o_ref[0] = jnp.concatenate(outs, axis=0).T.astype(BF16)


def _attention(online, qt, k, vt, tq, tk, tk_online):
    b, _, s = qt.shape
    hps = ATTN_HEADS
    return pl.pallas_call(
        functools.partial(_attn_kernel, tk=tk, tk_online=tk_online),
        grid_spec=pltpu.PrefetchScalarGridSpec(
            num_scalar_prefetch=1,
            grid=(b, N_HEADS // hps, s // tq),
            in_specs=[pl.BlockSpec((1, hps * HEAD_PAD, tq), lambda bi, j, i, on: (bi, j, i)),
                      pl.BlockSpec((1, s, hps * HEAD_PAD), lambda bi, j, i, on: (bi, 0, j)),
                      pl.BlockSpec((1, hps * V_AUG, s), lambda bi, j, i, on: (bi, j, 0))],
            out_specs=pl.BlockSpec((1, tq, hps * V_HEAD_DIM), lambda bi, j, i, on: (bi, i, j)),
            scratch_shapes=[pltpu.VMEM((V_AUG, tq), F32)] * hps + [pltpu.VMEM((8, tq), F32)]),
        out_shape=jax.ShapeDtypeStruct((b, s, N_HEADS * V_HEAD_DIM), BF16),
        compiler_params=_cparams(("parallel", "parallel", "arbitrary")),
        name="attn",
    )(online, qt, k, vt)


def _postmix_kernel(u_ref, up_ref, un_ref, a_ref, x_ref, mod_ref, pw_ref, ps_ref, wo_ref, n2w_ref,
                    wrh_ref, wrl_ref, x1_ref, h2_ref, lg_ref, *, seq):
    i = pl.program_id(1)
    ts = u_ref.shape[1]
    ext_rows = ts + 2 * HALO
    ext = jnp.concatenate([up_ref[0], u_ref[0], un_ref[0]], axis=0).astype(F32)
    pos = i * ts - HALO + lax.broadcasted_iota(jnp.int32, (ext_rows, 1), 0)
    ext = jnp.where((pos >= 0) & (pos < seq), ext, 0.0)
    p = i * ts + lax.broadcasted_iota(jnp.int32, (ts, 1), 0)

    outs = []
    for g, w in enumerate(POOL_WINDOWS):
        left = w // 2
        right = w - 1 - left
        t = ext[:, g * POOL_GROUP:(g + 1) * POOL_GROUP]
        step = 1
        while step < w:
            t = t + pltpu.roll(t, ext_rows - step, 0)
            step *= 2
        win = pltpu.roll(t, left, 0)[HALO:HALO + ts]
        cnt = (jnp.minimum(p + right + 1, seq) - jnp.maximum(p - left, 0)).astype(F32)
        d = win * (1.0 / cnt) - ext[HALO:HALO + ts, g * POOL_GROUP:(g + 1) * POOL_GROUP]
        outs.append(jnp.dot(d.astype(BF16), pw_ref[g], preferred_element_type=F32))
    pool = (jnp.concatenate(outs, axis=-1) * ps_ref[...]).astype(BF16)

    mix = (jnp.dot(pool, wo_ref[:POOL_WIDTH, :], preferred_element_type=F32)
           + jnp.dot(a_ref[0], wo_ref[POOL_WIDTH:, :], preferred_element_type=F32))
    x1 = x_ref[0] + mod_ref[0, 2:3, :] * mix
    x1_ref[0] = x1
    r = lax.rsqrt(jnp.mean(x1 * x1, axis=-1, keepdims=True) + EPS)
    gain2 = n2w_ref[...] * (1.0 + mod_ref[0, 4:5, :])
    h2 = x1 * r * gain2 + mod_ref[0, 3:4, :]
    h2_ref[0] = _pack_rows(h2)
    hi = h2.astype(BF16)
    lo = (h2 - hi.astype(F32)).astype(BF16)
    nt = (((1,), (1,)), ((), ()))
    lg_ref[...] = (lax.dot_general(wrh_ref[...], hi, nt, preferred_element_type=F32)
                   + lax.dot_general(wrl_ref[...], hi, nt, preferred_element_type=F32)
                   + lax.dot_general(wrh_ref[...], lo, nt, preferred_element_type=F32))


def _postmix(u, attn, x, mod, p, ts):
    b, s, d = x.shape
    nt = s // ts
    hb = ts // HALO
    const = lambda shape: pl.BlockSpec(shape, lambda bi, i: (0,) * len(shape))
    return pl.pallas_call(
        functools.partial(_postmix_kernel, seq=s),
        grid=(b, nt),
        in_specs=[pl.BlockSpec((1, ts, POOL_WIDTH), lambda bi, i: (bi, i, 0)),
                  pl.BlockSpec((1, HALO, POOL_WIDTH), lambda bi, i: (bi, jnp.maximum(i * hb - 1, 0), 0)),
                  pl.BlockSpec((1, HALO, POOL_WIDTH),
                               lambda bi, i: (bi, jnp.minimum((i + 1) * hb, s // HALO - 1), 0)),
                  pl.BlockSpec((1, ts, POOL_WIDTH), lambda bi, i: (bi, i, 0)),
                  pl.BlockSpec((1, ts, d), lambda bi, i: (bi, i, 0)),
                  pl.BlockSpec((1, 6, d), lambda bi, i: (bi, 0, 0)),
                  const((len(POOL_WINDOWS), POOL_GROUP, POOL_GROUP)), const((1, POOL_WIDTH)),
                  const((d, d)), const((1, d)), const((N_EXPERTS, d)), const((N_EXPERTS, d))],
        out_specs=[pl.BlockSpec((1, ts, d), lambda bi, i: (bi, i, 0)),
                   pl.BlockSpec((1, ts, PACK_W), lambda bi, i: (bi, i, 0)),
                   pl.BlockSpec((N_EXPERTS, ts), lambda bi, i: (0, bi * nt + i))],
        out_shape=[jax.ShapeDtypeStruct((b, s, d), F32),
                   jax.ShapeDtypeStruct((b, s, PACK_W), jnp.int32),
                   jax.ShapeDtypeStruct((N_EXPERTS, b * s), F32)],
        compiler_params=_cparams(("parallel", "parallel")),
        name="postmix",
    )(u, u, u, attn, x, mod, p["pool_w"], p["pool_scale"], p["w_o"], p["norm2_w"],
      p["w_router_hi"], p["w_router_lo"])


def _router_kernel(lg_ref, bias_ref, tri_ref, idx_ref, wts_ref, pos_ref, cnt_ref, carry_ref):
    @pl.when(pl.program_id(0) == 0)
    def _():
        carry_ref[...] = jnp.zeros_like(carry_ref)

    ts = lg_ref.shape[1]
    gsz = N_EXPERTS // N_GROUPS
    ninf = -jnp.inf
    scores = 1.0 / (1.0 + jnp.exp(-lg_ref[...]))
    choice = scores + bias_ref[...]
    sub = lax.broadcasted_iota(jnp.int32, (gsz, ts), 0)

    gs_rows = []
    for g in range(N_GROUPS):
        grp = choice[g * gsz:(g + 1) * gsz]
        m1 = jnp.max(grp, axis=0, keepdims=True)
        i1 = jnp.min(jnp.where(grp == m1, sub, gsz), axis=0, keepdims=True)
        m2 = jnp.max(jnp.where(sub == i1, ninf, grp), axis=0, keepdims=True)
        gs_rows.append(m1 + m2)
    gs = jnp.concatenate(gs_rows, axis=0)

    rank = jnp.zeros((N_GROUPS, ts), jnp.int32)
    for g in range(N_GROUPS):
        row = gs[g:g + 1]
        beats = (row > gs) | ((row == gs) & (sub > g))
        rank = rank + beats.astype(jnp.int32)
    gsel = rank < TOPK_GROUPS

    masked = jnp.concatenate(
        [jnp.where(gsel[g:g + 1], choice[g * gsz:(g + 1) * gsz], ninf) for g in range(N_GROUPS)],
        axis=0)
    eio = lax.broadcasted_iota(jnp.int32, (N_EXPERTS, ts), 0)
    idx_rows, w_rows, hits = [], [], []
    for _ in range(TOP_K):
        m = jnp.max(masked, axis=0, keepdims=True)
        i = jnp.min(jnp.where(masked == m, eio, N_EXPERTS), axis=0, keepdims=True)
        hit = eio == i
        w_rows.append(jnp.sum(jnp.where(hit, scores, 0.0), axis=0, keepdims=True))
        masked = jnp.where(hit, ninf, masked)
        idx_rows.append(i)
        hits.append(hit)

    wsum = functools.reduce(lambda a, c: a + c, w_rows)
    pad_i = [jnp.zeros((1, ts), jnp.int32)] * (TOPK_PAD - TOP_K)
    pad_f = [jnp.zeros((1, ts), F32)] * (TOPK_PAD - TOP_K)
    idx_ref[...] = jnp.concatenate(idx_rows + pad_i, axis=0)
    wts_ref[...] = jnp.concatenate([w / wsum * ROUTED_SCALE for w in w_rows] + pad_f, axis=0)

    sel = functools.reduce(lambda a, c: a | c, hits)
    onehot = jnp.where(sel, 1.0, 0.0).astype(BF16)
    run = jnp.dot(onehot, tri_ref[...], preferred_element_type=F32) + carry_ref[:, 0:1]
    pos_rows = [jnp.sum(jnp.where(h, run - 1.0, 0.0), axis=0, keepdims=True).astype(jnp.int32)
                for h in hits]
    pos_ref[...] = jnp.concatenate(pos_rows + pad_i, axis=0)
    total = run[:, ts - 1:ts]
    carry_ref[...] = jnp.broadcast_to(total, carry_ref.shape)
    cnt_ref[...] = jnp.broadcast_to(total, cnt_ref.shape)


def _router(logits_t, router_bias, ts):
    t = logits_t.shape[1]
    tri = jnp.triu(jnp.ones((ts, ts), BF16))
    tok = pl.BlockSpec((TOPK_PAD, ts), lambda i: (0, i))
    return pl.pallas_call(
        _router_kernel,
        grid=(t // ts,),
        in_specs=[pl.BlockSpec((N_EXPERTS, ts), lambda i: (0, i)),
                  pl.BlockSpec((N_EXPERTS, 1), lambda i: (0, 0)),
                  pl.BlockSpec((ts, ts), lambda i: (0, 0))],
        out_specs=[tok, tok, tok, pl.BlockSpec((N_EXPERTS, LANES), lambda i: (0, 0))],
        out_shape=[jax.ShapeDtypeStruct((TOPK_PAD, t), jnp.int32),
                   jax.ShapeDtypeStruct((TOPK_PAD, t), F32),
                   jax.ShapeDtypeStruct((TOPK_PAD, t), jnp.int32),
                   jax.ShapeDtypeStruct((N_EXPERTS, LANES), F32)],
        scratch_shapes=[pltpu.VMEM((N_EXPERTS, LANES), F32)],
        compiler_params=_cparams(("arbitrary",)),
        name="router",
    )(logits_t, router_bias.reshape(N_EXPERTS, 1), tri)


def _dest_kernel(idx_ref, pos_ref, start_ref, dest_ref):
    ts = idx_ref.shape[1]
    eio = lax.broadcasted_iota(jnp.int32, (N_EXPERTS, ts), 0)
    start = start_ref[...]
    rows = [pos_ref[k:k + 1, :]
            + jnp.sum(jnp.where(eio == idx_ref[k:k + 1, :], start, 0), axis=0, keepdims=True)
            for k in range(TOPK_PAD)]
    dest_ref[...] = jnp.concatenate(rows, axis=0)


def _dest_rows(idx, pos, pad_start, ts):
    t = idx.shape[1]
    tok = pl.BlockSpec((TOPK_PAD, ts), lambda i: (0, i))
    return pl.pallas_call(
        _dest_kernel,
        grid=(t // ts,),
        in_specs=[tok, tok, pl.BlockSpec((N_EXPERTS, 1), lambda i: (0, 0))],
        out_specs=tok,
        out_shape=jax.ShapeDtypeStruct((TOPK_PAD, t), jnp.int32),
        compiler_params=_cparams(("parallel",)),
        name="dest_rows",
    )(idx, pos, pad_start.reshape(N_EXPERTS, 1))


def _pack_rows(x):
    bits = lax.bitcast_convert_type(x.astype(BF16).astype(F32), jnp.int32)
    return bits[:, :PACK_W] | lax.shift_right_logical(bits[:, PACK_W:], 16)


def _unpack_rows(words):
    hi = lax.bitcast_convert_type(words & jnp.int32(-65536), F32)
    lo = lax.bitcast_convert_type(lax.shift_left(words, 16), F32)
    return jnp.concatenate([hi, lo], axis=1)


def _sc_mesh():
    return plsc.VectorSubcoreMesh(core_axis_name="core", subcore_axis_name="subcore")


def _sc_worker_base(rows_per_worker):
    return (lax.axis_index("subcore") * SC_CORES + lax.axis_index("core")) * rows_per_worker


def _sc_scatter_rows(x, dest, n_rows):
    t, w = x.shape
    per_worker = t // SC_WORKERS
    assert per_worker * SC_WORKERS == t and per_worker % SC_CHUNK == 0

    @functools.partial(
        pl.kernel, out_type=jax.ShapeDtypeStruct((n_rows, w), x.dtype), mesh=_sc_mesh(),
        scratch_types=[pltpu.VMEM((TOPK_PAD, SC_CHUNK), jnp.int32), pltpu.VMEM((SC_CHUNK, w), x.dtype),
                       pltpu.SemaphoreType.DMA])
    def scatter(x_hbm, dest_hbm, out_hbm, idx_v, rows_v, sem):
        base = _sc_worker_base(per_worker)

        @pl.loop(0, per_worker // SC_CHUNK)
        def _(c):
            off = pl.multiple_of(base + c * SC_CHUNK, SC_CHUNK)
            pltpu.sync_copy(dest_hbm.at[:, pl.ds(off, SC_CHUNK)], idx_v)
            pltpu.sync_copy(x_hbm.at[pl.ds(off, SC_CHUNK)], rows_v)
            copies = [pltpu.async_copy(rows_v, out_hbm.at[idx_v.at[k]], sem) for k in range(TOP_K)]
            for cp in copies:
                cp.wait()

    return scatter(x, dest)


def _sc_gather_rows(table, idx):
    m = idx.shape[0]
    w = table.shape[1]
    per_worker = m // SC_WORKERS
    assert per_worker * SC_WORKERS == m and per_worker % SC_CHUNK == 0

    @functools.partial(
        pl.kernel, out_type=jax.ShapeDtypeStruct((m, w), table.dtype), mesh=_sc_mesh(),
        scratch_types=[pltpu.VMEM((SC_CHUNK,), jnp.int32), pltpu.VMEM((SC_CHUNK, w), table.dtype),
                       pltpu.SemaphoreType.DMA])
    def gather(table_hbm, idx_hbm, out_hbm, idx_v, rows_v, sem):
        base = _sc_worker_base(per_worker)

        @pl.loop(0, per_worker // SC_CHUNK)
        def _(c):
            off = pl.multiple_of(base + c * SC_CHUNK, SC_CHUNK)
            pltpu.sync_copy(idx_hbm.at[pl.ds(off, SC_CHUNK)], idx_v)
            pltpu.async_copy(table_hbm.at[idx_v], rows_v, sem).wait()
            pltpu.sync_copy(rows_v, out_hbm.at[pl.ds(off, SC_CHUNK)])

    return gather(table, idx)


def _experts_kernel(blk_exp_ref, n_used_ref, xs_hbm, wg_ref, wu_ref, wd_ref, ys_ref, ring, sems):
    i = pl.program_id(0)
    n_used = n_used_ref[0]
    rb = ring.shape[1]

    def block_copy(blk):
        slot = blk % XS_RING
        return pltpu.make_async_copy(xs_hbm.at[pl.ds(blk * rb, rb), :], ring.at[slot], sems.at[slot])

    @pl.when(i == 0)
    def _():
        for ahead in range(XS_RING - 1):
            @pl.when(ahead < n_used)
            def _():
                block_copy(ahead).start()

    @pl.when(i + (XS_RING - 1) < n_used)
    def _():
        block_copy(i + (XS_RING - 1)).start()

    @pl.when(i < n_used)
    def _():
        block_copy(i).wait()
        xb = _unpack_rows(ring[i % XS_RING]).astype(BF16)
        g = jnp.dot(xb, wg_ref[0].astype(BF16), preferred_element_type=F32)
        u = jnp.dot(xb, wu_ref[0].astype(BF16), preferred_element_type=F32)
        hb = (_silu(g) * u).astype(BF16)
        ys_ref[...] = _pack_rows(jnp.dot(hb, wd_ref[0].astype(BF16), preferred_element_type=F32))


def _row_block(t):
    per_expert = t * TOP_K // N_EXPERTS
    return 1024 if per_expert >= 4 * 1024 else (512 if per_expert >= 4 * 512 else 256)


def _experts(blk_exp, n_used, xs, w_gate, w_up, w_down, row_block):
    n_rows, w = xs.shape
    d = D_MODEL
    n_blocks = n_rows // row_block
    row = lambda i, be, nu: (jnp.minimum(i, nu[0] - 1), 0)
    wsel = lambda i, be, nu: (be[i], 0, 0)
    return pl.pallas_call(
        _experts_kernel,
        grid_spec=pltpu.PrefetchScalarGridSpec(
            num_scalar_prefetch=2,
            grid=(n_blocks,),
            in_specs=[pl.BlockSpec(memory_space=pl.ANY),
                      pl.BlockSpec((1, d, EXPERT_FF), wsel),
                      pl.BlockSpec((1, d, EXPERT_FF), wsel),
                      pl.BlockSpec((1, EXPERT_FF, d), wsel)],
            out_specs=pl.BlockSpec((row_block, w), row),
            scratch_shapes=[pltpu.VMEM((XS_RING, row_block, w), jnp.int32),
                            pltpu.SemaphoreType.DMA((XS_RING,))]),
        out_shape=jax.ShapeDtypeStruct((n_rows, w), jnp.int32),
        compiler_params=_cparams(("arbitrary",)),
        name="experts",
    )(blk_exp, n_used, xs, w_gate, w_up, w_down)


def _combine_kernel(g_ref, w_ref, h2_ref, x1_ref, mod_ref, wsg_ref, wsu_ref, wsd_ref, out_ref):
    hb = _unpack_rows(h2_ref[...]).astype(BF16)
    g = jnp.dot(hb, wsg_ref[...], preferred_element_type=F32)
    u = jnp.dot(hb, wsu_ref[...], preferred_element_type=F32)
    acc = jnp.dot((_silu(g) * u).astype(BF16), wsd_ref[...], preferred_element_type=F32)
    w = w_ref[...]
    for k in range(TOP_K):
        acc = acc + _unpack_rows(g_ref[k]) * w[:, k:k + 1]
    out_ref[...] = x1_ref[...] + mod_ref[0, 5:6, :] * acc


def _combine(gathered, wts_t, h2p, x1, mod, p, tt, seq):
    t, d = x1.shape
    per_seq = seq // tt
    tok = pl.BlockSpec((tt, d), lambda i: (i, 0))
    const = lambda shape: pl.BlockSpec(shape, lambda i: (0,) * len(shape))
    return pl.pallas_call(
        _combine_kernel,
        grid=(t // tt,),
        in_specs=[pl.BlockSpec((TOP_K, tt, PACK_W), lambda i: (0, i, 0)),
                  pl.BlockSpec((tt, TOPK_PAD), lambda i: (i, 0)),
                  pl.BlockSpec((tt, PACK_W), lambda i: (i, 0)),
                  tok,
                  pl.BlockSpec((1, 6, d), lambda i: (i // per_seq, 0, 0)),
                  const((d, SHARED_FF)), const((d, SHARED_FF)), const((SHARED_FF, d))],
        out_specs=tok,
        out_shape=jax.ShapeDtypeStruct((t, d), F32),
        compiler_params=_cparams(("parallel",)),
        name="combine",
    )(gathered, wts_t, h2p, x1, mod, p["ws_gate"], p["ws_up"], p["ws_down"])


def _rope_tables(s):
    half = QK_ROPE_DIM // 2
    inv_freq = ROPE_THETA ** (-jnp.arange(half, dtype=F32) / half)
    ang = jnp.arange(s, dtype=F32)[:, None] * inv_freq[None, :]
    cos, sin = jnp.cos(ang), jnp.sin(ang)
    z = lambda n: jnp.zeros((s, n), F32)
    tab_cos = jnp.concatenate([jnp.ones((s, QK_NOPE_DIM), F32), cos, cos, z(HEAD_PAD - QK_HEAD_DIM)], 1)
    tab_sa = jnp.concatenate([z(QK_NOPE_DIM), -sin, z(HEAD_PAD - QK_NOPE_DIM - half)], 1)
    tab_sb = jnp.concatenate([z(QK_NOPE_DIM + half), sin, z(HEAD_PAD - QK_HEAD_DIM)], 1)
    return dict(rope_cos=tab_cos, rope_sa=tab_sa, rope_sb=tab_sb, rope_cos_t=cos.T, rope_sin_t=sin.T)


def _prep_weights(norm1_w, w_in, pool_w, pool_scale, q_a_norm_w, w_q_b, kv_a_norm_w, w_kv_b,
                  q_norm_w, k_norm_w, w_o, norm2_w, w_router, w_gate, w_up, w_down,
                  ws_gate, ws_up, ws_down):
    d = D_MODEL
    c0 = POOL_WIDTH + Q_LORA_RANK + KV_LORA_RANK
    pad_h = HEAD_PAD - QK_HEAD_DIM
    w_in_p = jnp.concatenate(
        [w_in[:, :c0], jnp.zeros((d, QK_NOPE_DIM), F32), w_in[:, c0:], jnp.zeros((d, pad_h), F32)], 1)
    w_q = jnp.pad(w_q_b.reshape(Q_LORA_RANK, N_HEADS, QK_HEAD_DIM), ((0, 0), (0, 0), (0, pad_h)))
    kv = w_kv_b.reshape(KV_LORA_RANK, N_HEADS, QK_NOPE_DIM + V_HEAD_DIM)
    w_k = jnp.pad(kv[:, :, :QK_NOPE_DIM], ((0, 0), (0, 0), (0, HEAD_PAD - QK_NOPE_DIM)))
    w_v = kv[:, :, QK_NOPE_DIM:]
    w_r_t = w_router.T
    w_r_hi = w_r_t.astype(BF16)
    w_r_lo = (w_r_t - w_r_hi.astype(F32)).astype(BF16)
    q_gain = q_norm_w * (QK_HEAD_DIM ** -0.5 * LOG2E)
    bound = QK_HEAD_DIM * jnp.max(jnp.abs(q_gain)) * jnp.max(jnp.abs(k_norm_w)) * 1.02 + 0.25
    return dict(
        score_bound=bound,
        norm1_w=norm1_w.reshape(1, d), w_in=w_in_p.astype(BF16),
        q_a_norm_w=q_a_norm_w.reshape(1, -1), w_q_t=w_q.reshape(Q_LORA_RANK, -1).T.astype(BF16),
        kv_a_norm_w=kv_a_norm_w.reshape(1, -1), w_k=w_k.reshape(KV_LORA_RANK, -1).astype(BF16),
        w_v_t=w_v.reshape(KV_LORA_RANK, -1).T.astype(BF16),
        q_gain_col=jnp.broadcast_to(jnp.pad(q_gain, (0, pad_h))[:, None], (HEAD_PAD, LANES)),
        k_norm_w=jnp.pad(k_norm_w, (0, pad_h)).reshape(1, HEAD_PAD),
        pool_w=pool_w.astype(BF16), pool_scale=pool_scale.reshape(1, -1), w_o=w_o.astype(BF16),
        norm2_w=norm2_w.reshape(1, d), w_router_hi=w_r_hi, w_router_lo=w_r_lo,
        w_gate=w_gate, w_up=w_up, w_down=w_down,
        ws_gate=ws_gate.astype(BF16), ws_up=ws_up.astype(BF16), ws_down=ws_down.astype(BF16))


def _tile(n, pref):
    return pref if n % pref == 0 else n


def _mixer_and_routing(x, mod, router_bias, p):
    b, s, d = x.shape
    t = b * s
    ts = _tile(s, 512)
    p = dict(p, **_rope_tables(s))

    u, qt, k, vt = _inproj(x, mod, p, ts)
    online = (p["score_bound"] > MAX_UNSHIFTED_SCORE).astype(jnp.int32).reshape(1)
    attn = _attention(online, qt, k, vt, _tile(s, 1024), _tile(s, 2048), _tile(s, 512))
    x1, h2p, logits_t = _postmix(u, attn, x, mod, p, _tile(s, 1024))
    tr = _tile(t, 1024)
    idx, wts, pos, cnt = _router(logits_t, router_bias, tr)

    rb = _row_block(t)
    counts = cnt[:, 0].astype(jnp.int32)
    padded = (counts + rb - 1) // rb * rb
    pad_end = jnp.cumsum(padded)
    dest = _dest_rows(idx, pos, pad_end - padded, _tile(t, 2048))
    n_blocks = -(-t * TOP_K // rb) + N_EXPERTS
    blk_row = jnp.arange(n_blocks, dtype=jnp.int32)[:, None] * rb
    blk_exp = jnp.minimum(jnp.sum((pad_end[None, :] <= blk_row).astype(jnp.int32), axis=1),
                          N_EXPERTS - 1)
    n_used = (pad_end[-1:] // rb).astype(jnp.int32)
    h2f = h2p.reshape(t, PACK_W)
    xs = _sc_scatter_rows(h2f, dest, n_blocks * rb)
    return dict(xs=xs, dest=dest, blk_exp=blk_exp, n_used=n_used, wts_t=wts.T, h2f=h2f,
                x1=x1.reshape(t, d), mod=mod, shape=(b, s, d), row_block=rb)


def _gather_and_combine(st, ys, p):
    b, s, d = st["shape"]
    t = b * s
    gathered = _sc_gather_rows(ys, st["dest"][:TOP_K].reshape(TOP_K * t)).reshape(TOP_K, t, PACK_W)
    out = _combine(gathered, st["wts_t"], st["h2f"], st["x1"], st["mod"], p, _tile(s, 512), s)
    return out.reshape(b, s, d)


def kernel(x_prompt, x_sample, c_prompt, c_sample, w_ada, b_ada, norm1_w, w_in, pool_w, pool_scale,
           q_a_norm_w, w_q_b, kv_a_norm_w, w_kv_b, q_norm_w, k_norm_w, w_o, norm2_w, w_router,
           router_bias, w_gate, w_up, w_down, ws_gate, ws_up, ws_down):
    assert w_ada.shape[0] == 1, "single-layer encoder"
    p = _prep_weights(norm1_w[0], w_in[0], pool_w[0], pool_scale[0], q_a_norm_w[0], w_q_b[0],
                      kv_a_norm_w[0], w_kv_b[0], q_norm_w[0], k_norm_w[0], w_o[0], norm2_w[0],
                      w_router[0], w_gate[0], w_up[0], w_down[0], ws_gate[0], ws_up[0], ws_down[0])
    nb = x_prompt.shape[0]
    c = jnp.concatenate([c_prompt, c_sample], axis=0).astype(F32)
    mod = _adaln(c, w_ada[0], b_ada[0]).reshape(c.shape[0], 6, D_MODEL)
    experts = lambda st, xs: _experts(st["blk_exp"], st["n_used"], xs, p["w_gate"], p["w_up"],
                                      p["w_down"], st["row_block"])

    sp = _mixer_and_routing(x_prompt, mod[:nb], router_bias[0], p)
    dest_p, x_sample = lax.optimization_barrier((sp["dest"], x_sample))
    sp["dest"] = dest_p
    ss = _mixer_and_routing(x_sample, mod[nb:], router_bias[0], p)
    ys_p = experts(sp, sp["xs"])
    ys_p, xs_s = lax.optimization_barrier((ys_p, ss["xs"]))
    ys_s = experts(ss, xs_s)
    y_prompt = _gather_and_combine(sp, ys_p, p)
    y_sample = _gather_and_combine(ss, ys_s, p)
    return (y_prompt, y_sample)
```

```python
import functools

import jax
import jax.numpy as jnp
from jax import lax
from jax.experimental import pallas as pl
from jax.experimental.pallas import tpu as pltpu
from jax.experimental.pallas import tpu_sc as plsc

D_MODEL = 1024
POOL_WIDTH = 512
POOL_WINDOWS = (2, 4, 8, 16)
POOL_GROUP = 128
N_HEADS = 8
V_HEAD_DIM = 64
QK_NOPE_DIM = 64
QK_ROPE_DIM = 32
QK_HEAD_DIM = 96
Q_LORA_RANK = 256
KV_LORA_RANK = 128
ROPE_THETA = 10000.0
N_EXPERTS = 64
TOP_K = 6
N_GROUPS = 8
TOPK_GROUPS = 4
EXPERT_FF = 256
SHARED_FF = 256
ROUTED_SCALE = 2.5
EPS = 1e-6

LANES = 128
HEAD_PAD = 128
HALO = 16
TOPK_PAD = 8
V_AUG = 80
ATTN_HEADS = 4
XS_RING = 3
MAX_UNSHIFTED_SCORE = 40.0
LOG2E = 1.4426950408889634
PACK_W = D_MODEL // 2
SC_CORES = 2
SC_WORKERS = 32
SC_CHUNK = 128
VMEM_LIMIT = 48 * 1024 * 1024

F32 = jnp.float32
BF16 = jnp.bfloat16
QK_DTYPE = jnp.bfloat16


def _cparams(sem):
    return pltpu.CompilerParams(dimension_semantics=sem, vmem_limit_bytes=VMEM_LIMIT)


def _silu(x):
    return x * (1.0 / (1.0 + jnp.exp(-x)))


def _adaln_kernel(c_ref, w_ref, b_ref, o_ref):
    c = c_ref[...]
    o_ref[...] = jnp.dot(_silu(c), w_ref[...], preferred_element_type=F32,
                         precision=lax.Precision.HIGHEST) + b_ref[...]


def _adaln(c, w_ada, b_ada):
    nb, d = c.shape
    n = w_ada.shape[1]
    tn = 1536
    return pl.pallas_call(
        _adaln_kernel,
        grid=(n // tn,),
        in_specs=[pl.BlockSpec((nb, d), lambda j: (0, 0)),
                  pl.BlockSpec((d, tn), lambda j: (0, j)),
                  pl.BlockSpec((1, tn), lambda j: (0, j))],
        out_specs=pl.BlockSpec((nb, tn), lambda j: (0, j)),
        out_shape=jax.ShapeDtypeStruct((nb, n), F32),
        compiler_params=_cparams(("arbitrary",)),
        name="adaln",
    )(c, w_ada, b_ada.reshape(1, n))


def _inproj_kernel(x_ref, mod_ref, n1w_ref, win_ref, qan_ref, wqt_ref, kvan_ref, wk_ref, wvt_ref,
                   qg_ref, kg_ref, cos_ref, sa_ref, sb_ref, cost_ref, sint_ref,
                   u_ref, qt_ref, k_ref, vt_ref):
    ts = x_ref.shape[1]
    half = QK_ROPE_DIM // 2
    x = x_ref[0]
    shift1 = mod_ref[0, 0:1, :]
    gain1 = n1w_ref[...] * (1.0 + mod_ref[0, 1:2, :])
    r = lax.rsqrt(jnp.mean(x * x, axis=-1, keepdims=True) + EPS)
    h = x * r * gain1 + shift1
    z = jnp.dot(h.astype(BF16), win_ref[...], preferred_element_type=F32)
    u_ref[0] = z[:, :POOL_WIDTH].astype(BF16)

    cq = z[:, POOL_WIDTH:POOL_WIDTH + Q_LORA_RANK]
    cqn = cq * lax.rsqrt(jnp.mean(cq * cq, axis=-1, keepdims=True) + EPS) * qan_ref[...]
    qt = jnp.dot(wqt_ref[...], cqn.T.astype(BF16), preferred_element_type=F32)
    reps = ts // LANES
    qg = jnp.concatenate([qg_ref[...]] * reps, axis=1)
    cost = cost_ref[...]
    sint = sint_ref[...]
    spare = jnp.zeros((HEAD_PAD - QK_HEAD_DIM, ts), F32)
    for hd in range(N_HEADS):
        t = qt[hd * HEAD_PAD:(hd + 1) * HEAD_PAD]
        rn = lax.rsqrt(jnp.sum(t * t, axis=0, keepdims=True) * (1.0 / QK_HEAD_DIM) + EPS)
        tn = t * rn * qg
        t1 = tn[QK_NOPE_DIM:QK_NOPE_DIM + half]
        t2 = tn[QK_NOPE_DIM + half:QK_HEAD_DIM]
        out = jnp.concatenate([tn[:QK_NOPE_DIM], t1 * cost - t2 * sint, t1 * sint + t2 * cost, spare],
                              axis=0)
        qt_ref[0, hd * HEAD_PAD:(hd + 1) * HEAD_PAD, :] = out.astype(QK_DTYPE)

    c0 = POOL_WIDTH + Q_LORA_RANK
    ckv = z[:, c0:c0 + KV_LORA_RANK]
    ckvn = ckv * lax.rsqrt(jnp.mean(ckv * ckv, axis=-1, keepdims=True) + EPS) * kvan_ref[...]

    vt = jnp.dot(wvt_ref[...], ckvn.T.astype(BF16), preferred_element_type=F32)
    ones = jnp.ones((V_AUG - V_HEAD_DIM, ts), BF16)
    for hd in range(N_HEADS):
        vt_ref[0, hd * V_AUG:hd * V_AUG + V_HEAD_DIM, :] = (
            vt[hd * V_HEAD_DIM:(hd + 1) * V_HEAD_DIM].astype(BF16))
        vt_ref[0, hd * V_AUG + V_HEAD_DIM:(hd + 1) * V_AUG, :] = ones

    kk = jnp.dot(ckvn.astype(BF16), wk_ref[...], preferred_element_type=F32)
    kpe = z[:, c0 + KV_LORA_RANK:]
    kg = kg_ref[...]
    pe_ssq = jnp.sum(kpe * kpe, axis=-1, keepdims=True)
    pg = kpe * kg
    pe_rot = (pg * cos_ref[...] + pltpu.roll(pg, HEAD_PAD - half, 1) * sa_ref[...]
              + pltpu.roll(pg, half, 1) * sb_ref[...])
    for hd in range(N_HEADS):
        t = kk[:, hd * HEAD_PAD:(hd + 1) * HEAD_PAD]
        ssq = jnp.sum(t * t, axis=-1, keepdims=True) + pe_ssq
        rn = lax.rsqrt(ssq * (1.0 / QK_HEAD_DIM) + EPS)
        k_ref[0, :, hd * HEAD_PAD:(hd + 1) * HEAD_PAD] = ((t * kg + pe_rot) * rn).astype(QK_DTYPE)


def _inproj(x, mod, p, ts):
    b, s, d = x.shape
    qk_w = N_HEADS * HEAD_PAD
    half = QK_ROPE_DIM // 2
    const = lambda shape: pl.BlockSpec(shape, lambda bi, i: (0,) * len(shape))
    row_tab = pl.BlockSpec((ts, HEAD_PAD), lambda bi, i: (i, 0))
    col_tab = pl.BlockSpec((half, ts), lambda bi, i: (0, i))
    return pl.pallas_call(
        _inproj_kernel,
        grid=(b, s // ts),
        in_specs=[pl.BlockSpec((1, ts, d), lambda bi, i: (bi, i, 0)),
                  pl.BlockSpec((1, 6, d), lambda bi, i: (bi, 0, 0)),
                  const((1, d)), const((d, d)), const((1, Q_LORA_RANK)),
                  const((qk_w, Q_LORA_RANK)), const((1, KV_LORA_RANK)),
                  const((KV_LORA_RANK, qk_w)), const((N_HEADS * V_HEAD_DIM, KV_LORA_RANK)),
                  const((HEAD_PAD, LANES)), const((1, HEAD_PAD)),
                  row_tab, row_tab, row_tab, col_tab, col_tab],
        out_specs=[pl.BlockSpec((1, ts, POOL_WIDTH), lambda bi, i: (bi, i, 0)),
                   pl.BlockSpec((1, qk_w, ts), lambda bi, i: (bi, 0, i)),
                   pl.BlockSpec((1, ts, qk_w), lambda bi, i: (bi, i, 0)),
                   pl.BlockSpec((1, N_HEADS * V_AUG, ts), lambda bi, i: (bi, 0, i))],
        out_shape=[jax.ShapeDtypeStruct((b, s, POOL_WIDTH), BF16),
                   jax.ShapeDtypeStruct((b, qk_w, s), QK_DTYPE),
                   jax.ShapeDtypeStruct((b, s, qk_w), QK_DTYPE),
                   jax.ShapeDtypeStruct((b, N_HEADS * V_AUG, s), BF16)],
        compiler_params=_cparams(("parallel", "parallel")),
        name="inproj",
    )(x, mod, p["norm1_w"], p["w_in"], p["q_a_norm_w"], p["w_q_t"], p["kv_a_norm_w"], p["w_k"],
      p["w_v_t"], p["q_gain_col"], p["k_norm_w"],
      p["rope_cos"], p["rope_sa"], p["rope_sb"], p["rope_cos_t"], p["rope_sin_t"])


def _attn_kernel(online_ref, qt_ref, k_ref, vt_ref, o_ref, *scratch, tk, tk_online):
    accs, m_ref = scratch[:ATTN_HEADS], scratch[ATTN_HEADS]
    s_len = k_ref.shape[1]
    for acc in accs:
        acc[...] = jnp.zeros_like(acc)
    heads = range(ATTN_HEADS)
    qs = [qt_ref[0, h * HEAD_PAD:(h + 1) * HEAD_PAD, :] for h in heads]

    @pl.when(online_ref[0] == 0)
    def _():
        def body(c, _):
            off = pl.multiple_of(c * tk, tk)
            ks = k_ref[0, pl.ds(off, tk), :]
            vts = vt_ref[0, :, pl.ds(off, tk)]
            for h in heads:
                s = jnp.dot(ks[:, h * HEAD_PAD:(h + 1) * HEAD_PAD], qs[h], preferred_element_type=F32)
                accs[h][...] += jnp.dot(vts[h * V_AUG:(h + 1) * V_AUG], jnp.exp2(s).astype(BF16),
                                        preferred_element_type=F32)
            return 0

        lax.fori_loop(0, s_len // tk, body, 0)

    @pl.when(online_ref[0] != 0)
    def _():
        m_ref[...] = jnp.full_like(m_ref, -jnp.inf)

        def body(c, _):
            off = pl.multiple_of(c * tk_online, tk_online)
            ks = k_ref[0, pl.ds(off, tk_online), :]
            vts = vt_ref[0, :, pl.ds(off, tk_online)]
            for h in heads:
                s = jnp.dot(ks[:, h * HEAD_PAD:(h + 1) * HEAD_PAD], qs[h], preferred_element_type=F32)
                m_old = m_ref[h:h + 1, :]
                m_new = jnp.maximum(m_old, jnp.max(s, axis=0, keepdims=True))
                p = jnp.exp2(s - m_new).astype(BF16)
                accs[h][...] = (accs[h][...] * jnp.exp2(m_old - m_new)
                                + jnp.dot(vts[h * V_AUG:(h + 1) * V_AUG], p, preferred_element_type=F32))
                m_ref[h:h + 1, :] = m_new
            return 0

        lax.fori_loop(0, s_len // tk_online, body, 0)

    outs = [acc[:V_HEAD_DIM] / acc[V_HEAD_DIM:V_HEAD_DIM + 1] for acc in accs]
    o_ref[0] = jnp.concatenate(outs, axis=0).T.astype(BF16)


def _attention(online, qt, k, vt, tq, tk, tk_online):
    b, _, s = qt.shape
    hps = ATTN_HEADS
    return pl.pallas_call(
        functools.partial(_attn_kernel, tk=tk, tk_online=tk_online),
        grid_spec=pltpu.PrefetchScalarGridSpec(
            num_scalar_prefetch=1,
            grid=(b, N_HEADS // hps, s // tq),
            in_specs=[pl.BlockSpec((1, hps * HEAD_PAD, tq), lambda bi, j, i, on: (bi, j, i)),
                      pl.BlockSpec((1, s, hps * HEAD_PAD), lambda bi, j, i, on: (bi, 0, j)),
                      pl.BlockSpec((1, hps * V_AUG, s), lambda bi, j, i, on: (bi, j, 0))],
            out_specs=pl.BlockSpec((1, tq, hps * V_HEAD_DIM), lambda bi, j, i, on: (bi, i, j)),
            scratch_shapes=[pltpu.VMEM((V_AUG, tq), F32)] * hps + [pltpu.VMEM((8, tq), F32)]),
        out_shape=jax.ShapeDtypeStruct((b, s, N_HEADS * V_HEAD_DIM), BF16),
        compiler_params=_cparams(("parallel", "parallel", "arbitrary")),
        name="attn",
    )(online, qt, k, vt)


def _postmix_kernel(u_ref, up_ref, un_ref, a_ref, x_ref, mod_ref, pw_ref, ps_ref, wo_ref, n2w_ref,
                    wrh_ref, wrl_ref, x1_ref, h2_ref, lg_ref, *, seq):
    i = pl.program_id(1)
    ts = u_ref.shape[1]
    ext_rows = ts + 2 * HALO
    ext = jnp.concatenate([up_ref[0], u_ref[0], un_ref[0]], axis=0).astype(F32)
    pos = i * ts - HALO + lax.broadcasted_iota(jnp.int32, (ext_rows, 1), 0)
    ext = jnp.where((pos >= 0) & (pos < seq), ext, 0.0)
    p = i * ts + lax.broadcasted_iota(jnp.int32, (ts, 1), 0)

    outs = []
    for g, w in enumerate(POOL_WINDOWS):
        left = w // 2
        right = w - 1 - left
        t = ext[:, g * POOL_GROUP:(g + 1) * POOL_GROUP]
        step = 1
        while step < w:
            t = t + pltpu.roll(t, ext_rows - step, 0)
            step *= 2
        win = pltpu.roll(t, left, 0)[HALO:HALO + ts]
        cnt = (jnp.minimum(p + right + 1, seq) - jnp.maximum(p - left, 0)).astype(F32)
        d = win * (1.0 / cnt) - ext[HALO:HALO + ts, g * POOL_GROUP:(g + 1) * POOL_GROUP]
        outs.append(jnp.dot(d.astype(BF16), pw_ref[g], preferred_element_type=F32))
    pool = (jnp.concatenate(outs, axis=-1) * ps_ref[...]).astype(BF16)

    mix = (jnp.dot(pool, wo_ref[:POOL_WIDTH, :], preferred_element_type=F32)
           + jnp.dot(a_ref[0], wo_ref[POOL_WIDTH:, :], preferred_element_type=F32))
    x1 = x_ref[0] + mod_ref[0, 2:3, :] * mix
    x1_ref[0] = x1
    r = lax.rsqrt(jnp.mean(x1 * x1, axis=-1, keepdims=True) + EPS)
    gain2 = n2w_ref[...] * (1.0 + mod_ref[0, 4:5, :])
    h2 = x1 * r * gain2 + mod_ref[0, 3:4, :]
    h2_ref[0] = _pack_rows(h2)
    hi = h2.astype(BF16)
    lo = (h2 - hi.astype(F32)).astype(BF16)
    nt = (((1,), (1,)), ((), ()))
    lg_ref[...] = (lax.dot_general(wrh_ref[...], hi, nt, preferred_element_type=F32)
                   + lax.dot_general(wrl_ref[...], hi, nt, preferred_element_type=F32)
                   + lax.dot_general(wrh_ref[...], lo, nt, preferred_element_type=F32))


def _postmix(u, attn, x, mod, p, ts):
    b, s, d = x.shape
    nt = s // ts
    hb = ts // HALO
    const = lambda shape: pl.BlockSpec(shape, lambda bi, i: (0,) * len(shape))
    return pl.pallas_call(
        functools.partial(_postmix_kernel, seq=s),
        grid=(b, nt),
        in_specs=[pl.BlockSpec((1, ts, POOL_WIDTH), lambda bi, i: (bi, i, 0)),
                  pl.BlockSpec((1, HALO, POOL_WIDTH), lambda bi, i: (bi, jnp.maximum(i * hb - 1, 0), 0)),
                  pl.BlockSpec((1, HALO, POOL_WIDTH),
                               lambda bi, i: (bi, jnp.minimum((i + 1) * hb, s // HALO - 1), 0)),
                  pl.BlockSpec((1, ts, POOL_WIDTH), lambda bi, i: (bi, i, 0)),
                  pl.BlockSpec((1, ts, d), lambda bi, i: (bi, i, 0)),
                  pl.BlockSpec((1, 6, d), lambda bi, i: (bi, 0, 0)),
                  const((len(POOL_WINDOWS), POOL_GROUP, POOL_GROUP)), const((1, POOL_WIDTH)),
                  const((d, d)), const((1, d)), const((N_EXPERTS, d)), const((N_EXPERTS, d))],
        out_specs=[pl.BlockSpec((1, ts, d), lambda bi, i: (bi, i, 0)),
                   pl.BlockSpec((1, ts, PACK_W), lambda bi, i: (bi, i, 0)),
                   pl.BlockSpec((N_EXPERTS, ts), lambda bi, i: (0, bi * nt + i))],
        out_shape=[jax.ShapeDtypeStruct((b, s, d), F32),
                   jax.ShapeDtypeStruct((b, s, PACK_W), jnp.int32),
                   jax.ShapeDtypeStruct((N_EXPERTS, b * s), F32)],
        compiler_params=_cparams(("parallel", "parallel")),
        name="postmix",
    )(u, u, u, attn, x, mod, p["pool_w"], p["pool_scale"], p["w_o"], p["norm2_w"],
      p["w_router_hi"], p["w_router_lo"])


def _router_kernel(lg_ref, bias_ref, tri_ref, idx_ref, wts_ref, pos_ref, cnt_ref, carry_ref):
    @pl.when(pl.program_id(0) == 0)
    def _():
        carry_ref[...] = jnp.zeros_like(carry_ref)

    ts = lg_ref.shape[1]
    gsz = N_EXPERTS // N_GROUPS
    ninf = -jnp.inf
    scores = 1.0 / (1.0 + jnp.exp(-lg_ref[...]))
    choice = scores + bias_ref[...]
    sub = lax.broadcasted_iota(jnp.int32, (gsz, ts), 0)

    gs_rows = []
    for g in range(N_GROUPS):
        grp = choice[g * gsz:(g + 1) * gsz]
        m1 = jnp.max(grp, axis=0, keepdims=True)
        i1 = jnp.min(jnp.where(grp == m1, sub, gsz), axis=0, keepdims=True)
        m2 = jnp.max(jnp.where(sub == i1, ninf, grp), axis=0, keepdims=True)
        gs_rows.append(m1 + m2)
    gs = jnp.concatenate(gs_rows, axis=0)

    rank = jnp.zeros((N_GROUPS, ts), jnp.int32)
    for g in range(N_GROUPS):
        row = gs[g:g + 1]
        beats = (row > gs) | ((row == gs) & (sub > g))
        rank = rank + beats.astype(jnp.int32)
    gsel = rank < TOPK_GROUPS

    masked = jnp.concatenate(
        [jnp.where(gsel[g:g + 1], choice[g * gsz:(g + 1) * gsz], ninf) for g in range(N_GROUPS)],
        axis=0)
    eio = lax.broadcasted_iota(jnp.int32, (N_EXPERTS, ts), 0)
    idx_rows, w_rows, hits = [], [], []
    for _ in range(TOP_K):
        m = jnp.max(masked, axis=0, keepdims=True)
        i = jnp.min(jnp.where(masked == m, eio, N_EXPERTS), axis=0, keepdims=True)
        hit = eio == i
        w_rows.append(jnp.sum(jnp.where(hit, scores, 0.0), axis=0, keepdims=True))
        masked = jnp.where(hit, ninf, masked)
        idx_rows.append(i)
        hits.append(hit)

    wsum = functools.reduce(lambda a, c: a + c, w_rows)
    pad_i = [jnp.zeros((1, ts), jnp.int32)] * (TOPK_PAD - TOP_K)
    pad_f = [jnp.zeros((1, ts), F32)] * (TOPK_PAD - TOP_K)
    idx_ref[...] = jnp.concatenate(idx_rows + pad_i, axis=0)
    wts_ref[...] = jnp.concatenate([w / wsum * ROUTED_SCALE for w in w_rows] + pad_f, axis=0)

    sel = functools.reduce(lambda a, c: a | c, hits)
    onehot = jnp.where(sel, 1.0, 0.0).astype(BF16)
    run = jnp.dot(onehot, tri_ref[...], preferred_element_type=F32) + carry_ref[:, 0:1]
    pos_rows = [jnp.sum(jnp.where(h, run - 1.0, 0.0), axis=0, keepdims=True).astype(jnp.int32)
                for h in hits]
    pos_ref[...] = jnp.concatenate(pos_rows + pad_i, axis=0)
    total = run[:, ts - 1:ts]
    carry_ref[...] = jnp.broadcast_to(total, carry_ref.shape)
    cnt_ref[...] = jnp.broadcast_to(total, cnt_ref.shape)


def _router(logits_t, router_bias, ts):
    t = logits_t.shape[1]
    tri = jnp.triu(jnp.ones((ts, ts), BF16))
    tok = pl.BlockSpec((TOPK_PAD, ts), lambda i: (0, i))
    return pl.pallas_call(
        _router_kernel,
        grid=(t // ts,),
        in_specs=[pl.BlockSpec((N_EXPERTS, ts), lambda i: (0, i)),
                  pl.BlockSpec((N_EXPERTS, 1), lambda i: (0, 0)),
                  pl.BlockSpec((ts, ts), lambda i: (0, 0))],
        out_specs=[tok, tok, tok, pl.BlockSpec((N_EXPERTS, LANES), lambda i: (0, 0))],
        out_shape=[jax.ShapeDtypeStruct((TOPK_PAD, t), jnp.int32),
                   jax.ShapeDtypeStruct((TOPK_PAD, t), F32),
                   jax.ShapeDtypeStruct((TOPK_PAD, t), jnp.int32),
                   jax.ShapeDtypeStruct((N_EXPERTS, LANES), F32)],
        scratch_shapes=[pltpu.VMEM((N_EXPERTS, LANES), F32)],
        compiler_params=_cparams(("arbitrary",)),
        name="router",
    )(logits_t, router_bias.reshape(N_EXPERTS, 1), tri)


def _dest_kernel(idx_ref, pos_ref, start_ref, dest_ref):
    ts = idx_ref.shape[1]
    eio = lax.broadcasted_iota(jnp.int32, (N_EXPERTS, ts), 0)
    start = start_ref[...]
    rows = [pos_ref[k:k + 1, :]
            + jnp.sum(jnp.where(eio == idx_ref[k:k + 1, :], start, 0), axis=0, keepdims=True)
            for k in range(TOPK_PAD)]
    dest_ref[...] = jnp.concatenate(rows, axis=0)


def _dest_rows(idx, pos, pad_start, ts):
    t = idx.shape[1]
    tok = pl.BlockSpec((TOPK_PAD, ts), lambda i: (0, i))
    return pl.pallas_call(
        _dest_kernel,
        grid=(t // ts,),
        in_specs=[tok, tok, pl.BlockSpec((N_EXPERTS, 1), lambda i: (0, 0))],
        out_specs=tok,
        out_shape=jax.ShapeDtypeStruct((TOPK_PAD, t), jnp.int32),
        compiler_params=_cparams(("parallel",)),
        name="dest_rows",
    )(idx, pos, pad_start.reshape(N_EXPERTS, 1))


def _pack_rows(x):
    bits = lax.bitcast_convert_type(x.astype(BF16).astype(F32), jnp.int32)
    return bits[:, :PACK_W] | lax.shift_right_logical(bits[:, PACK_W:], 16)


def _unpack_rows(words):
    hi = lax.bitcast_convert_type(words & jnp.int32(-65536), F32)
    lo = lax.bitcast_convert_type(lax.shift_left(words, 16), F32)
    return jnp.concatenate([hi, lo], axis=1)


def _sc_mesh():
    return plsc.VectorSubcoreMesh(core_axis_name="core", subcore_axis_name="subcore")


def _sc_worker_base(rows_per_worker):
    return (lax.axis_index("subcore") * SC_CORES + lax.axis_index("core")) * rows_per_worker


def _sc_scatter_rows(x, dest, n_rows):
    t, w = x.shape
    per_worker = t // SC_WORKERS
    assert per_worker * SC_WORKERS == t and per_worker % SC_CHUNK == 0

    @functools.partial(
        pl.kernel, out_type=jax.ShapeDtypeStruct((n_rows, w), x.dtype), mesh=_sc_mesh(),
        scratch_types=[pltpu.VMEM((TOPK_PAD, SC_CHUNK), jnp.int32), pltpu.VMEM((SC_CHUNK, w), x.dtype),
                       pltpu.SemaphoreType.DMA])
    def scatter(x_hbm, dest_hbm, out_hbm, idx_v, rows_v, sem):
        base = _sc_worker_base(per_worker)

        @pl.loop(0, per_worker // SC_CHUNK)
        def _(c):
            off = pl.multiple_of(base + c * SC_CHUNK, SC_CHUNK)
            pltpu.sync_copy(dest_hbm.at[:, pl.ds(off, SC_CHUNK)], idx_v)
            pltpu.sync_copy(x_hbm.at[pl.ds(off, SC_CHUNK)], rows_v)
            copies = [pltpu.async_copy(rows_v, out_hbm.at[idx_v.at[k]], sem) for k in range(TOP_K)]
            for cp in copies:
                cp.wait()

    return scatter(x, dest)


def _sc_gather_rows(table, idx):
    m = idx.shape[0]
    w = table.shape[1]
    per_worker = m // SC_WORKERS
    assert per_worker * SC_WORKERS == m and per_worker % SC_CHUNK == 0

    @functools.partial(
        pl.kernel, out_type=jax.ShapeDtypeStruct((m, w), table.dtype), mesh=_sc_mesh(),
        scratch_types=[pltpu.VMEM((SC_CHUNK,), jnp.int32), pltpu.VMEM((SC_CHUNK, w), table.dtype),
                       pltpu.SemaphoreType.DMA])
    def gather(table_hbm, idx_hbm, out_hbm, idx_v, rows_v, sem):
        base = _sc_worker_base(per_worker)

        @pl.loop(0, per_worker // SC_CHUNK)
        def _(c):
            off = pl.multiple_of(base + c * SC_CHUNK, SC_CHUNK)
            pltpu.sync_copy(idx_hbm.at[pl.ds(off, SC_CHUNK)], idx_v)
            pltpu.async_copy(table_hbm.at[idx_v], rows_v, sem).wait()
            pltpu.sync_copy(rows_v, out_hbm.at[pl.ds(off, SC_CHUNK)])

    return gather(table, idx)


def _experts_kernel(blk_exp_ref, n_used_ref, xs_hbm, wg_ref, wu_ref, wd_ref, ys_ref, ring, sems):
    i = pl.program_id(0)
    n_used = n_used_ref[0]
    rb = ring.shape[1]

    def block_copy(blk):
        slot = blk % XS_RING
        return pltpu.make_async_copy(xs_hbm.at[pl.ds(blk * rb, rb), :], ring.at[slot], sems.at[slot])

    @pl.when(i == 0)
    def _():
        for ahead in range(XS_RING - 1):
            @pl.when(ahead < n_used)
            def _():
                block_copy(ahead).start()

    @pl.when(i + (XS_RING - 1) < n_used)
    def _():
        block_copy(i + (XS_RING - 1)).start()

    @pl.when(i < n_used)
    def _():
        block_copy(i).wait()
        xb = _unpack_rows(ring[i % XS_RING]).astype(BF16)
        g = jnp.dot(xb, wg_ref[0].astype(BF16), preferred_element_type=F32)
        u = jnp.dot(xb, wu_ref[0].astype(BF16), preferred_element_type=F32)
        hb = (_silu(g) * u).astype(BF16)
        ys_ref[...] = _pack_rows(jnp.dot(hb, wd_ref[0].astype(BF16), preferred_element_type=F32))


def _row_block(t):
    per_expert = t * TOP_K // N_EXPERTS
    return 1024 if per_expert >= 4 * 1024 else (512 if per_expert >= 4 * 512 else 256)


def _experts(blk_exp, n_used, xs, w_gate, w_up, w_down, row_block):
    n_rows, w = xs.shape
    d = D_MODEL
    n_blocks = n_rows // row_block
    row = lambda i, be, nu: (jnp.minimum(i, nu[0] - 1), 0)
    wsel = lambda i, be, nu: (be[i], 0, 0)
    return pl.pallas_call(
        _experts_kernel,
        grid_spec=pltpu.PrefetchScalarGridSpec(
            num_scalar_prefetch=2,
            grid=(n_blocks,),
            in_specs=[pl.BlockSpec(memory_space=pl.ANY),
                      pl.BlockSpec((1, d, EXPERT_FF), wsel),
                      pl.BlockSpec((1, d, EXPERT_FF), wsel),
                      pl.BlockSpec((1, EXPERT_FF, d), wsel)],
            out_specs=pl.BlockSpec((row_block, w), row),
            scratch_shapes=[pltpu.VMEM((XS_RING, row_block, w), jnp.int32),
                            pltpu.SemaphoreType.DMA((XS_RING,))]),
        out_shape=jax.ShapeDtypeStruct((n_rows, w), jnp.int32),
        compiler_params=_cparams(("arbitrary",)),
        name="experts",
    )(blk_exp, n_used, xs, w_gate, w_up, w_down)


def _combine_kernel(g_hbm, w_ref, h2_ref, x1_ref, mod_ref, wsg_ref, wsu_ref, wsd_ref, out_ref,
                    ring, sems):
    i = pl.program_id(0)
    n_steps = pl.num_programs(0)
    tt = ring.shape[2]

    def tile_copy(step):
        slot = step % XS_RING
        return pltpu.make_async_copy(g_hbm.at[:, pl.ds(step * tt, tt), :], ring.at[slot], sems.at[slot])

    @pl.when(i == 0)
    def _():
        for ahead in range(XS_RING - 1):
            @pl.when(ahead < n_steps)
            def _():
                tile_copy(ahead).start()

    @pl.when(i + (XS_RING - 1) < n_steps)
    def _():
        tile_copy(i + (XS_RING - 1)).start()

    tile_copy(i).wait()
    g_ref = ring.at[i % XS_RING]
    hb = _unpack_rows(h2_ref[...]).astype(BF16)
    g = jnp.dot(hb, wsg_ref[...], preferred_element_type=F32)
    u = jnp.dot(hb, wsu_ref[...], preferred_element_type=F32)
    acc = jnp.dot((_silu(g) * u).astype(BF16), wsd_ref[...], preferred_element_type=F32)
    w = w_ref[...]
    for k in range(TOP_K):
        acc = acc + _unpack_rows(g_ref[k]) * w[:, k:k + 1]
    out_ref[...] = x1_ref[...] + mod_ref[0, 5:6, :] * acc


def _combine(gathered, wts_t, h2p, x1, mod, p, tt, seq):
    t, d = x1.shape
    per_seq = seq // tt
    tok = pl.BlockSpec((tt, d), lambda i: (i, 0))
    const = lambda shape: pl.BlockSpec(shape, lambda i: (0,) * len(shape))
    return pl.pallas_call(
        _combine_kernel,
        grid=(t // tt,),
        in_specs=[pl.BlockSpec(memory_space=pl.ANY),
                  pl.BlockSpec((tt, TOPK_PAD), lambda i: (i, 0)),
                  pl.BlockSpec((tt, PACK_W), lambda i: (i, 0)),
                  tok,
                  pl.BlockSpec((1, 6, d), lambda i: (i // per_seq, 0, 0)),
                  const((d, SHARED_FF)), const((d, SHARED_FF)), const((SHARED_FF, d))],
        out_specs=tok,
        out_shape=jax.ShapeDtypeStruct((t, d), F32),
        scratch_shapes=[pltpu.VMEM((XS_RING, TOP_K, tt, PACK_W), jnp.int32),
                        pltpu.SemaphoreType.DMA((XS_RING,))],
        compiler_params=_cparams(("arbitrary",)),
        name="combine",
    )(gathered, wts_t, h2p, x1, mod, p["ws_gate"], p["ws_up"], p["ws_down"])


def _rope_tables(s):
    half = QK_ROPE_DIM // 2
    inv_freq = ROPE_THETA ** (-jnp.arange(half, dtype=F32) / half)
    ang = jnp.arange(s, dtype=F32)[:, None] * inv_freq[None, :]
    cos, sin = jnp.cos(ang), jnp.sin(ang)
    z = lambda n: jnp.zeros((s, n), F32)
    tab_cos = jnp.concatenate([jnp.ones((s, QK_NOPE_DIM), F32), cos, cos, z(HEAD_PAD - QK_HEAD_DIM)], 1)
    tab_sa = jnp.concatenate([z(QK_NOPE_DIM), -sin, z(HEAD_PAD - QK_NOPE_DIM - half)], 1)
    tab_sb = jnp.concatenate([z(QK_NOPE_DIM + half), sin, z(HEAD_PAD - QK_HEAD_DIM)], 1)
    return dict(rope_cos=tab_cos, rope_sa=tab_sa, rope_sb=tab_sb, rope_cos_t=cos.T, rope_sin_t=sin.T)


def _prep_weights(norm1_w, w_in, pool_w, pool_scale, q_a_norm_w, w_q_b, kv_a_norm_w, w_kv_b,
                  q_norm_w, k_norm_w, w_o, norm2_w, w_router, w_gate, w_up, w_down,
                  ws_gate, ws_up, ws_down):
    d = D_MODEL
    c0 = POOL_WIDTH + Q_LORA_RANK + KV_LORA_RANK
    pad_h = HEAD_PAD - QK_HEAD_DIM
    w_in_p = jnp.concatenate(
        [w_in[:, :c0], jnp.zeros((d, QK_NOPE_DIM), F32), w_in[:, c0:], jnp.zeros((d, pad_h), F32)], 1)
    w_q = jnp.pad(w_q_b.reshape(Q_LORA_RANK, N_HEADS, QK_HEAD_DIM), ((0, 0), (0, 0), (0, pad_h)))
    kv = w_kv_b.reshape(KV_LORA_RANK, N_HEADS, QK_NOPE_DIM + V_HEAD_DIM)
    w_k = jnp.pad(kv[:, :, :QK_NOPE_DIM], ((0, 0), (0, 0), (0, HEAD_PAD - QK_NOPE_DIM)))
    w_v = kv[:, :, QK_NOPE_DIM:]
    w_r_t = w_router.T
    w_r_hi = w_r_t.astype(BF16)
    w_r_lo = (w_r_t - w_r_hi.astype(F32)).astype(BF16)
    q_gain = q_norm_w * (QK_HEAD_DIM ** -0.5 * LOG2E)
    bound = QK_HEAD_DIM * jnp.max(jnp.abs(q_gain)) * jnp.max(jnp.abs(k_norm_w)) * 1.02 + 0.25
    return dict(
        score_bound=bound,
        norm1_w=norm1_w.reshape(1, d), w_in=w_in_p.astype(BF16),
        q_a_norm_w=q_a_norm_w.reshape(1, -1), w_q_t=w_q.reshape(Q_LORA_RANK, -1).T.astype(BF16),
        kv_a_norm_w=kv_a_norm_w.reshape(1, -1), w_k=w_k.reshape(KV_LORA_RANK, -1).astype(BF16),
        w_v_t=w_v.reshape(KV_LORA_RANK, -1).T.astype(BF16),
        q_gain_col=jnp.broadcast_to(jnp.pad(q_gain, (0, pad_h))[:, None], (HEAD_PAD, LANES)),
        k_norm_w=jnp.pad(k_norm_w, (0, pad_h)).reshape(1, HEAD_PAD),
        pool_w=pool_w.astype(BF16), pool_scale=pool_scale.reshape(1, -1), w_o=w_o.astype(BF16),
        norm2_w=norm2_w.reshape(1, d), w_router_hi=w_r_hi, w_router_lo=w_r_lo,
        w_gate=w_gate, w_up=w_up, w_down=w_down,
        ws_gate=ws_gate.astype(BF16), ws_up=ws_up.astype(BF16), ws_down=ws_down.astype(BF16))


def _tile(n, pref):
    return pref if n % pref == 0 else n


def _mixer_and_routing(x, mod, router_bias, p):
    b, s, d = x.shape
    t = b * s
    ts = _tile(s, 512)
    p = dict(p, **_rope_tables(s))

    u, qt, k, vt = _inproj(x, mod, p, ts)
    online = (p["score_bound"] > MAX_UNSHIFTED_SCORE).astype(jnp.int32).reshape(1)
    attn = _attention(online, qt, k, vt, _tile(s, 1024), _tile(s, 2048), _tile(s, 512))
    x1, h2p, logits_t = _postmix(u, attn, x, mod, p, _tile(s, 1024))
    tr = _tile(t, 1024)
    idx, wts, pos, cnt = _router(logits_t, router_bias, tr)

    rb = _row_block(t)
    counts = cnt[:, 0].astype(jnp.int32)
    padded = (counts + rb - 1) // rb * rb
    pad_end = jnp.cumsum(padded)
    dest = _dest_rows(idx, pos, pad_end - padded, _tile(t, 2048))
    n_blocks = -(-t * TOP_K // rb) + N_EXPERTS
    blk_row = jnp.arange(n_blocks, dtype=jnp.int32)[:, None] * rb
    blk_exp = jnp.minimum(jnp.sum((pad_end[None, :] <= blk_row).astype(jnp.int32), axis=1),
                          N_EXPERTS - 1)
    n_used = (pad_end[-1:] // rb).astype(jnp.int32)
    h2f = h2p.reshape(t, PACK_W)
    xs = _sc_scatter_rows(h2f, dest, n_blocks * rb)
    return dict(xs=xs, dest=dest, blk_exp=blk_exp, n_used=n_used, wts_t=wts.T, h2f=h2f,
                x1=x1.reshape(t, d), mod=mod, shape=(b, s, d), row_block=rb)


def _gather_and_combine(st, ys, p):
    b, s, d = st["shape"]
    t = b * s
    gathered = _sc_gather_rows(ys, st["dest"][:TOP_K].reshape(TOP_K * t)).reshape(TOP_K, t, PACK_W)
    out = _combine(gathered, st["wts_t"], st["h2f"], st["x1"], st["mod"], p, _tile(s, 512), s)
    return out.reshape(b, s, d)


def kernel(x_prompt, x_sample, c_prompt, c_sample, w_ada, b_ada, norm1_w, w_in, pool_w, pool_scale,
           q_a_norm_w, w_q_b, kv_a_norm_w, w_kv_b, q_norm_w, k_norm_w, w_o, norm2_w, w_router,
           router_bias, w_gate, w_up, w_down, ws_gate, ws_up, ws_down):
    assert w_ada.shape[0] == 1, "single-layer encoder"
    p = _prep_weights(norm1_w[0], w_in[0], pool_w[0], pool_scale[0], q_a_norm_w[0], w_q_b[0],
                      kv_a_norm_w[0], w_kv_b[0], q_norm_w[0], k_norm_w[0], w_o[0], norm2_w[0],
                      w_router[0], w_gate[0], w_up[0], w_down[0], ws_gate[0], ws_up[0], ws_down[0])
    nb = x_prompt.shape[0]
    c = jnp.concatenate([c_prompt, c_sample], axis=0).astype(F32)
    mod = _adaln(c, w_ada[0], b_ada[0]).reshape(c.shape[0], 6, D_MODEL)
    experts = lambda st, xs: _experts(st["blk_exp"], st["n_used"], xs, p["w_gate"], p["w_up"],
                                      p["w_down"], st["row_block"])

    sp = _mixer_and_routing(x_prompt, mod[:nb], router_bias[0], p)
    dest_p, x_sample = lax.optimization_barrier((sp["dest"], x_sample))
    sp["dest"] = dest_p
    ss = _mixer_and_routing(x_sample, mod[nb:], router_bias[0], p)
    ys_p = experts(sp, sp["xs"])
    ys_p, xs_s = lax.optimization_barrier((ys_p, ss["xs"]))
    ys_s = experts(ss, xs_s)
    y_prompt = _gather_and_combine(sp, ys_p, p)
    y_sample = _gather_and_combine(ss, ys_s, p)
    return (y_prompt, y_sample)
```

```python
import functools

import jax
import jax.numpy as jnp
from jax import lax
from jax.experimental import pallas as pl
from jax.experimental.pallas import tpu as pltpu
from jax.experimental.pallas import tpu_sc as plsc

D_MODEL = 1024
POOL_WIDTH = 512
POOL_WINDOWS = (2, 4, 8, 16)
POOL_GROUP = 128
N_HEADS = 8
V_HEAD_DIM = 64
QK_NOPE_DIM = 64
QK_ROPE_DIM = 32
QK_HEAD_DIM = 96
Q_LORA_RANK = 256
KV_LORA_RANK = 128
ROPE_THETA = 10000.0
N_EXPERTS = 64
TOP_K = 6
N_GROUPS = 8
TOPK_GROUPS = 4
EXPERT_FF = 256
SHARED_FF = 256
ROUTED_SCALE = 2.5
EPS = 1e-6

LANES = 128
HEAD_PAD = 128
HALO = 16
TOPK_PAD = 8
V_AUG = 80
ATTN_HEADS = 4
XS_RING = 4
MAX_UNSHIFTED_SCORE = 40.0
LOG2E = 1.4426950408889634
PACK_W = D_MODEL // 2
SC_CORES = 2
SC_WORKERS = 32
SC_CHUNK = 128
VMEM_LIMIT = 48 * 1024 * 1024

F32 = jnp.float32
BF16 = jnp.bfloat16
QK_DTYPE = jnp.bfloat16


def _cparams(sem):
    return pltpu.CompilerParams(dimension_semantics=sem, vmem_limit_bytes=VMEM_LIMIT)


def _silu(x):
    return x * (1.0 / (1.0 + jnp.exp(-x)))


def _adaln_kernel(c_ref, w_ref, b_ref, o_ref):
    c = c_ref[...]
    o_ref[...] = jnp.dot(_silu(c), w_ref[...], preferred_element_type=F32,
                         precision=lax.Precision.HIGHEST) + b_ref[...]


def _adaln(c, w_ada, b_ada):
    nb, d = c.shape
    n = w_ada.shape[1]
    tn = 1536
    return pl.pallas_call(
        _adaln_kernel,
        grid=(n // tn,),
        in_specs=[pl.BlockSpec((nb, d), lambda j: (0, 0)),
                  pl.BlockSpec((d, tn), lambda j: (0, j)),
                  pl.BlockSpec((1, tn), lambda j: (0, j))],
        out_specs=pl.BlockSpec((nb, tn), lambda j: (0, j)),
        out_shape=jax.ShapeDtypeStruct((nb, n), F32),
        compiler_params=_cparams(("arbitrary",)),
        name="adaln",
    )(c, w_ada, b_ada.reshape(1, n))


def _inproj_kernel(x_ref, mod_ref, n1w_ref, win_ref, qan_ref, wqt_ref, kvan_ref, wk_ref, wvt_ref,
                   qg_ref, kg_ref, cos_ref, sa_ref, sb_ref, cost_ref, sint_ref,
                   u_ref, qt_ref, k_ref, vt_ref):
    ts = x_ref.shape[1]
    half = QK_ROPE_DIM // 2
    x = x_ref[0]
    shift1 = mod_ref[0, 0:1, :]
    gain1 = n1w_ref[...] * (1.0 + mod_ref[0, 1:2, :])
    r = lax.rsqrt(jnp.mean(x * x, axis=-1, keepdims=True) + EPS)
    h = x * r * gain1 + shift1
    z = jnp.dot(h.astype(BF16), win_ref[...], preferred_element_type=F32)
    u_ref[0] = z[:, :POOL_WIDTH].astype(BF16)

    cq = z[:, POOL_WIDTH:POOL_WIDTH + Q_LORA_RANK]
    cqn = cq * lax.rsqrt(jnp.mean(cq * cq, axis=-1, keepdims=True) + EPS) * qan_ref[...]
    qt = jnp.dot(wqt_ref[...], cqn.T.astype(BF16), preferred_element_type=F32)
    reps = ts // LANES
    qg = jnp.concatenate([qg_ref[...]] * reps, axis=1)
    cost = cost_ref[...]
    sint = sint_ref[...]
    spare = jnp.zeros((HEAD_PAD - QK_HEAD_DIM, ts), F32)
    for hd in range(N_HEADS):
        t = qt[hd * HEAD_PAD:(hd + 1) * HEAD_PAD]
        rn = lax.rsqrt(jnp.sum(t * t, axis=0, keepdims=True) * (1.0 / QK_HEAD_DIM) + EPS)
        tn = t * rn * qg
        t1 = tn[QK_NOPE_DIM:QK_NOPE_DIM + half]
        t2 = tn[QK_NOPE_DIM + half:QK_HEAD_DIM]
        out = jnp.concatenate([tn[:QK_NOPE_DIM], t1 * cost - t2 * sint, t1 * sint + t2 * cost, spare],
                              axis=0)
        qt_ref[0, hd * HEAD_PAD:(hd + 1) * HEAD_PAD, :] = out.astype(QK_DTYPE)

    c0 = POOL_WIDTH + Q_LORA_RANK
    ckv = z[:, c0:c0 + KV_LORA_RANK]
    ckvn = ckv * lax.rsqrt(jnp.mean(ckv * ckv, axis=-1, keepdims=True) + EPS) * kvan_ref[...]

    vt = jnp.dot(wvt_ref[...], ckvn.T.astype(BF16), preferred_element_type=F32)
    ones = jnp.ones((V_AUG - V_HEAD_DIM, ts), BF16)
    for hd in range(N_HEADS):
        vt_ref[0, hd * V_AUG:hd * V_AUG + V_HEAD_DIM, :] = (
            vt[hd * V_HEAD_DIM:(hd + 1) * V_HEAD_DIM].astype(BF16))
        vt_ref[0, hd * V_AUG + V_HEAD_DIM:(hd + 1) * V_AUG, :] = ones

    kk = jnp.dot(ckvn.astype(BF16), wk_ref[...], preferred_element_type=F32)
    kpe = z[:, c0 + KV_LORA_RANK:]
    kg = kg_ref[...]
    pe_ssq = jnp.sum(kpe * kpe, axis=-1, keepdims=True)
    pg = kpe * kg
    pe_rot = (pg * cos_ref[...] + pltpu.roll(pg, HEAD_PAD - half, 1) * sa_ref[...]
              + pltpu.roll(pg, half, 1) * sb_ref[...])
    for hd in range(N_HEADS):
        t = kk[:, hd * HEAD_PAD:(hd + 1) * HEAD_PAD]
        ssq = jnp.sum(t * t, axis=-1, keepdims=True) + pe_ssq
        rn = lax.rsqrt(ssq * (1.0 / QK_HEAD_DIM) + EPS)
        k_ref[0, :, hd * HEAD_PAD:(hd + 1) * HEAD_PAD] = ((t * kg + pe_rot) * rn).astype(QK_DTYPE)


def _inproj(x, mod, p, ts):
    b, s, d = x.shape
    qk_w = N_HEADS * HEAD_PAD
    half = QK_ROPE_DIM // 2
    const = lambda shape: pl.BlockSpec(shape, lambda bi, i: (0,) * len(shape))
    row_tab = pl.BlockSpec((ts, HEAD_PAD), lambda bi, i: (i, 0))
    col_tab = pl.BlockSpec((half, ts), lambda bi, i: (0, i))
    return pl.pallas_call(
        _inproj_kernel,
        grid=(b, s // ts),
        in_specs=[pl.BlockSpec((1, ts, d), lambda bi, i: (bi, i, 0)),
                  pl.BlockSpec((1, 6, d), lambda bi, i: (bi, 0, 0)),
                  const((1, d)), const((d, d)), const((1, Q_LORA_RANK)),
                  const((qk_w, Q_LORA_RANK)), const((1, KV_LORA_RANK)),
                  const((KV_LORA_RANK, qk_w)), const((N_HEADS * V_HEAD_DIM, KV_LORA_RANK)),
                  const((HEAD_PAD, LANES)), const((1, HEAD_PAD)),
                  row_tab, row_tab, row_tab, col_tab, col_tab],
        out_specs=[pl.BlockSpec((1, ts, POOL_WIDTH), lambda bi, i: (bi, i, 0)),
                   pl.BlockSpec((1, qk_w, ts), lambda bi, i: (bi, 0, i)),
                   pl.BlockSpec((1, ts, qk_w), lambda bi, i: (bi, i, 0)),
                   pl.BlockSpec((1, N_HEADS * V_AUG, ts), lambda bi, i: (bi, 0, i))],
        out_shape=[jax.ShapeDtypeStruct((b, s, POOL_WIDTH), BF16),
                   jax.ShapeDtypeStruct((b, qk_w, s), QK_DTYPE),
                   jax.ShapeDtypeStruct((b, s, qk_w), QK_DTYPE),
                   jax.ShapeDtypeStruct((b, N_HEADS * V_AUG, s), BF16)],
        compiler_params=_cparams(("parallel", "parallel")),
        name="inproj",
    )(x, mod, p["norm1_w"], p["w_in"], p["q_a_norm_w"], p["w_q_t"], p["kv_a_norm_w"], p["w_k"],
      p["w_v_t"], p["q_gain_col"], p["k_norm_w"],
      p["rope_cos"], p["rope_sa"], p["rope_sb"], p["rope_cos_t"], p["rope_sin_t"])


def _attn_kernel(online_ref, qt_ref, k_ref, vt_ref, o_ref, *scratch, tk, tk_online):
    accs, m_ref = scratch[:ATTN_HEADS], scratch[ATTN_HEADS]
    s_len = k_ref.shape[1]
    for acc in accs:
        acc[...] = jnp.zeros_like(acc)
    heads = range(ATTN_HEADS)
    qs = [qt_ref[0, h * HEAD_PAD:(h + 1) * HEAD_PAD, :] for h in heads]

    @pl.when(online_ref[0] == 0)
    def _():
        def body(c, _):
            off = pl.multiple_of(c * tk, tk)
            ks = k_ref[0, pl.ds(off, tk), :]
            vts = vt_ref[0, :, pl.ds(off, tk)]
            for h in heads:
                s = jnp.dot(ks[:, h * HEAD_PAD:(h + 1) * HEAD_PAD], qs[h], preferred_element_type=F32)
                accs[h][...] += jnp.dot(vts[h * V_AUG:(h + 1) * V_AUG], jnp.exp2(s).astype(BF16),
                                        preferred_element_type=F32)
            return 0

        lax.fori_loop(0, s_len // tk, body, 0)

    @pl.when(online_ref[0] != 0)
    def _():
        m_ref[...] = jnp.full_like(m_ref, -jnp.inf)

        def body(c, _):
            off = pl.multiple_of(c * tk_online, tk_online)
            ks = k_ref[0, pl.ds(off, tk_online), :]
            vts = vt_ref[0, :, pl.ds(off, tk_online)]
            for h in heads:
                s = jnp.dot(ks[:, h * HEAD_PAD:(h + 1) * HEAD_PAD], qs[h], preferred_element_type=F32)
                m_old = m_ref[h:h + 1, :]
                m_new = jnp.maximum(m_old, jnp.max(s, axis=0, keepdims=True))
                p = jnp.exp2(s - m_new).astype(BF16)
                accs[h][...] = (accs[h][...] * jnp.exp2(m_old - m_new)
                                + jnp.dot(vts[h * V_AUG:(h + 1) * V_AUG], p, preferred_element_type=F32))
                m_ref[h:h + 1, :] = m_new
            return 0

        lax.fori_loop(0, s_len // tk_online, body, 0)

    outs = [acc[:V_HEAD_DIM] / acc[V_HEAD_DIM:V_HEAD_DIM + 1] for acc in accs]
    o_ref[0] = jnp.concatenate(outs, axis=0).T.astype(BF16)


def _attention(online, qt, k, vt, tq, tk, tk_online):
    b, _, s = qt.shape
    hps = ATTN_HEADS
    return pl.pallas_call(
        functools.partial(_attn_kernel, tk=tk, tk_online=tk_online),
        grid_spec=pltpu.PrefetchScalarGridSpec(
            num_scalar_prefetch=1,
            grid=(b, N_HEADS // hps, s // tq),
            in_specs=[pl.BlockSpec((1, hps * HEAD_PAD, tq), lambda bi, j, i, on: (bi, j, i)),
                      pl.BlockSpec((1, s, hps * HEAD_PAD), lambda bi, j, i, on: (bi, 0, j)),
                      pl.BlockSpec((1, hps * V_AUG, s), lambda bi, j, i, on: (bi, j, 0))],
            out_specs=pl.BlockSpec((1, tq, hps * V_HEAD_DIM), lambda bi, j, i, on: (bi, i, j)),
            scratch_shapes=[pltpu.VMEM((V_AUG, tq), F32)] * hps + [pltpu.VMEM((8, tq), F32)]),
        out_shape=jax.ShapeDtypeStruct((b, s, N_HEADS * V_HEAD_DIM), BF16),
        compiler_params=_cparams(("parallel", "parallel", "arbitrary")),
        name="attn",
    )(online, qt, k, vt)


def _postmix_kernel(u_ref, up_ref, un_ref, a_ref, x_ref, mod_ref, pw_ref, ps_ref, wo_ref, n2w_ref,
                    wrh_ref, wrl_ref, x1_ref, h2_ref, lg_ref, *, seq):
    i = pl.program_id(1)
    ts = u_ref.shape[1]
    ext_rows = ts + 2 * HALO
    ext = jnp.concatenate([up_ref[0], u_ref[0], un_ref[0]], axis=0).astype(F32)
    pos = i * ts - HALO + lax.broadcasted_iota(jnp.int32, (ext_rows, 1), 0)
    ext = jnp.where((pos >= 0) & (pos < seq), ext, 0.0)
    p = i * ts + lax.broadcasted_iota(jnp.int32, (ts, 1), 0)

    outs = []
    for g, w in enumerate(POOL_WINDOWS):
        left = w // 2
        right = w - 1 - left
        t = ext[:, g * POOL_GROUP:(g + 1) * POOL_GROUP]
        step = 1
        while step < w:
            t = t + pltpu.roll(t, ext_rows - step, 0)
            step *= 2
        win = pltpu.roll(t, left, 0)[HALO:HALO + ts]
        cnt = (jnp.minimum(p + right + 1, seq) - jnp.maximum(p - left, 0)).astype(F32)
        d = win * (1.0 / cnt) - ext[HALO:HALO + ts, g * POOL_GROUP:(g + 1) * POOL_GROUP]
        outs.append(jnp.dot(d.astype(BF16), pw_ref[g], preferred_element_type=F32))
    pool = (jnp.concatenate(outs, axis=-1) * ps_ref[...]).astype(BF16)

    mix = (jnp.dot(pool, wo_ref[:POOL_WIDTH, :], preferred_element_type=F32)
           + jnp.dot(a_ref[0], wo_ref[POOL_WIDTH:, :], preferred_element_type=F32))
    x1 = x_ref[0] + mod_ref[0, 2:3, :] * mix
    x1_ref[0] = x1
    r = lax.rsqrt(jnp.mean(x1 * x1, axis=-1, keepdims=True) + EPS)
    gain2 = n2w_ref[...] * (1.0 + mod_ref[0, 4:5, :])
    h2 = x1 * r * gain2 + mod_ref[0, 3:4, :]
    h2_ref[0] = _pack_rows(h2)
    hi = h2.astype(BF16)
    lo = (h2 - hi.astype(F32)).astype(BF16)
    nt = (((1,), (1,)), ((), ()))
    lg_ref[...] = (lax.dot_general(wrh_ref[...], hi, nt, preferred_element_type=F32)
                   + lax.dot_general(wrl_ref[...], hi, nt, preferred_element_type=F32)
                   + lax.dot_general(wrh_ref[...], lo, nt, preferred_element_type=F32))


def _postmix(u, attn, x, mod, p, ts):
    b, s, d = x.shape
    nt = s // ts
    hb = ts // HALO
    const = lambda shape: pl.BlockSpec(shape, lambda bi, i: (0,) * len(shape))
    return pl.pallas_call(
        functools.partial(_postmix_kernel, seq=s),
        grid=(b, nt),
        in_specs=[pl.BlockSpec((1, ts, POOL_WIDTH), lambda bi, i: (bi, i, 0)),
                  pl.BlockSpec((1, HALO, POOL_WIDTH), lambda bi, i: (bi, jnp.maximum(i * hb - 1, 0), 0)),
                  pl.BlockSpec((1, HALO, POOL_WIDTH),
                               lambda bi, i: (bi, jnp.minimum((i + 1) * hb, s // HALO - 1), 0)),
                  pl.BlockSpec((1, ts, POOL_WIDTH), lambda bi, i: (bi, i, 0)),
                  pl.BlockSpec((1, ts, d), lambda bi, i: (bi, i, 0)),
                  pl.BlockSpec((1, 6, d), lambda bi, i: (bi, 0, 0)),
                  const((len(POOL_WINDOWS), POOL_GROUP, POOL_GROUP)), const((1, POOL_WIDTH)),
                  const((d, d)), const((1, d)), const((N_EXPERTS, d)), const((N_EXPERTS, d))],
        out_specs=[pl.BlockSpec((1, ts, d), lambda bi, i: (bi, i, 0)),
                   pl.BlockSpec((1, ts, PACK_W), lambda bi, i: (bi, i, 0)),
                   pl.BlockSpec((N_EXPERTS, ts), lambda bi, i: (0, bi * nt + i))],
        out_shape=[jax.ShapeDtypeStruct((b, s, d), F32),
                   jax.ShapeDtypeStruct((b, s, PACK_W), jnp.int32),
                   jax.ShapeDtypeStruct((N_EXPERTS, b * s), F32)],
        compiler_params=_cparams(("parallel", "parallel")),
        name="postmix",
    )(u, u, u, attn, x, mod, p["pool_w"], p["pool_scale"], p["w_o"], p["norm2_w"],
      p["w_router_hi"], p["w_router_lo"])


def _router_kernel(lg_ref, bias_ref, tri_ref, idx_ref, wts_ref, pos_ref, cnt_ref, carry_ref):
    @pl.when(pl.program_id(0) == 0)
    def _():
        carry_ref[...] = jnp.zeros_like(carry_ref)

    ts = lg_ref.shape[1]
    gsz = N_EXPERTS // N_GROUPS
    ninf = -jnp.inf
    scores = 1.0 / (1.0 + jnp.exp(-lg_ref[...]))
    choice = scores + bias_ref[...]
    sub = lax.broadcasted_iota(jnp.int32, (gsz, ts), 0)

    gs_rows = []
    for g in range(N_GROUPS):
        grp = choice[g * gsz:(g + 1) * gsz]
        m1 = jnp.max(grp, axis=0, keepdims=True)
        i1 = jnp.min(jnp.where(grp == m1, sub, gsz), axis=0, keepdims=True)
        m2 = jnp.max(jnp.where(sub == i1, ninf, grp), axis=0, keepdims=True)
        gs_rows.append(m1 + m2)
    gs = jnp.concatenate(gs_rows, axis=0)

    rank = jnp.zeros((N_GROUPS, ts), jnp.int32)
    for g in range(N_GROUPS):
        row = gs[g:g + 1]
        beats = (row > gs) | ((row == gs) & (sub > g))
        rank = rank + beats.astype(jnp.int32)
    gsel = rank < TOPK_GROUPS

    masked = jnp.concatenate(
        [jnp.where(gsel[g:g + 1], choice[g * gsz:(g + 1) * gsz], ninf) for g in range(N_GROUPS)],
        axis=0)
    eio = lax.broadcasted_iota(jnp.int32, (N_EXPERTS, ts), 0)
    idx_rows, w_rows, hits = [], [], []
    for _ in range(TOP_K):
        m = jnp.max(masked, axis=0, keepdims=True)
        i = jnp.min(jnp.where(masked == m, eio, N_EXPERTS), axis=0, keepdims=True)
        hit = eio == i
        w_rows.append(jnp.sum(jnp.where(hit, scores, 0.0), axis=0, keepdims=True))
        masked = jnp.where(hit, ninf, masked)
        idx_rows.append(i)
        hits.append(hit)

    wsum = functools.reduce(lambda a, c: a + c, w_rows)
    pad_i = [jnp.zeros((1, ts), jnp.int32)] * (TOPK_PAD - TOP_K)
    pad_f = [jnp.zeros((1, ts), F32)] * (TOPK_PAD - TOP_K)
    idx_ref[...] = jnp.concatenate(idx_rows + pad_i, axis=0)
    wts_ref[...] = jnp.concatenate([w / wsum * ROUTED_SCALE for w in w_rows] + pad_f, axis=0)

    sel = functools.reduce(lambda a, c: a | c, hits)
    onehot = jnp.where(sel, 1.0, 0.0).astype(BF16)
    run = jnp.dot(onehot, tri_ref[...], preferred_element_type=F32) + carry_ref[:, 0:1]
    pos_rows = [jnp.sum(jnp.where(h, run - 1.0, 0.0), axis=0, keepdims=True).astype(jnp.int32)
                for h in hits]
    pos_ref[...] = jnp.concatenate(pos_rows + pad_i, axis=0)
    total = run[:, ts - 1:ts]
    carry_ref[...] = jnp.broadcast_to(total, carry_ref.shape)
    cnt_ref[...] = jnp.broadcast_to(total, cnt_ref.shape)


def _router(logits_t, router_bias, ts):
    t = logits_t.shape[1]
    tri = jnp.triu(jnp.ones((ts, ts), BF16))
    tok = pl.BlockSpec((TOPK_PAD, ts), lambda i: (0, i))
    return pl.pallas_call(
        _router_kernel,
        grid=(t // ts,),
        in_specs=[pl.BlockSpec((N_EXPERTS, ts), lambda i: (0, i)),
                  pl.BlockSpec((N_EXPERTS, 1), lambda i: (0, 0)),
                  pl.BlockSpec((ts, ts), lambda i: (0, 0))],
        out_specs=[tok, tok, tok, pl.BlockSpec((N_EXPERTS, LANES), lambda i: (0, 0))],
        out_shape=[jax.ShapeDtypeStruct((TOPK_PAD, t), jnp.int32),
                   jax.ShapeDtypeStruct((TOPK_PAD, t), F32),
                   jax.ShapeDtypeStruct((TOPK_PAD, t), jnp.int32),
                   jax.ShapeDtypeStruct((N_EXPERTS, LANES), F32)],
        scratch_shapes=[pltpu.VMEM((N_EXPERTS, LANES), F32)],
        compiler_params=_cparams(("arbitrary",)),
        name="router",
    )(logits_t, router_bias.reshape(N_EXPERTS, 1), tri)


def _dest_kernel(idx_ref, pos_ref, start_ref, dest_ref):
    ts = idx_ref.shape[1]
    eio = lax.broadcasted_iota(jnp.int32, (N_EXPERTS, ts), 0)
    start = start_ref[...]
    rows = [pos_ref[k:k + 1, :]
            + jnp.sum(jnp.where(eio == idx_ref[k:k + 1, :], start, 0), axis=0, keepdims=True)
            for k in range(TOPK_PAD)]
    dest_ref[...] = jnp.concatenate(rows, axis=0)


def _dest_rows(idx, pos, pad_start, ts):
    t = idx.shape[1]
    tok = pl.BlockSpec((TOPK_PAD, ts), lambda i: (0, i))
    return pl.pallas_call(
        _dest_kernel,
        grid=(t // ts,),
        in_specs=[tok, tok, pl.BlockSpec((N_EXPERTS, 1), lambda i: (0, 0))],
        out_specs=tok,
        out_shape=jax.ShapeDtypeStruct((TOPK_PAD, t), jnp.int32),
        compiler_params=_cparams(("parallel",)),
        name="dest_rows",
    )(idx, pos, pad_start.reshape(N_EXPERTS, 1))


def _pack_rows(x):
    bits = lax.bitcast_convert_type(x.astype(BF16).astype(F32), jnp.int32)
    return bits[:, :PACK_W] | lax.shift_right_logical(bits[:, PACK_W:], 16)


def _unpack_rows(words):
    hi = lax.bitcast_convert_type(words & jnp.int32(-65536), F32)
    lo = lax.bitcast_convert_type(lax.shift_left(words, 16), F32)
    return jnp.concatenate([hi, lo], axis=1)


def _sc_mesh():
    return plsc.VectorSubcoreMesh(core_axis_name="core", subcore_axis_name="subcore")


def _sc_worker_base(rows_per_worker):
    return (lax.axis_index("subcore") * SC_CORES + lax.axis_index("core")) * rows_per_worker


def _sc_scatter_rows(x, dest, n_rows):
    t, w = x.shape
    per_worker = t // SC_WORKERS
    assert per_worker * SC_WORKERS == t and per_worker % SC_CHUNK == 0

    @functools.partial(
        pl.kernel, out_type=jax.ShapeDtypeStruct((n_rows, w), x.dtype), mesh=_sc_mesh(),
        scratch_types=[pltpu.VMEM((TOPK_PAD, SC_CHUNK), jnp.int32), pltpu.VMEM((SC_CHUNK, w), x.dtype),
                       pltpu.SemaphoreType.DMA])
    def scatter(x_hbm, dest_hbm, out_hbm, idx_v, rows_v, sem):
        base = _sc_worker_base(per_worker)

        @pl.loop(0, per_worker // SC_CHUNK)
        def _(c):
            off = pl.multiple_of(base + c * SC_CHUNK, SC_CHUNK)
            pltpu.sync_copy(dest_hbm.at[:, pl.ds(off, SC_CHUNK)], idx_v)
            pltpu.sync_copy(x_hbm.at[pl.ds(off, SC_CHUNK)], rows_v)
            copies = [pltpu.async_copy(rows_v, out_hbm.at[idx_v.at[k]], sem) for k in range(TOP_K)]
            for cp in copies:
                cp.wait()

    return scatter(x, dest)


def _sc_gather_rows(table, idx):
    m = idx.shape[0]
    w = table.shape[1]
    per_worker = m // SC_WORKERS
    assert per_worker * SC_WORKERS == m and per_worker % SC_CHUNK == 0

    @functools.partial(
        pl.kernel, out_type=jax.ShapeDtypeStruct((m, w), table.dtype), mesh=_sc_mesh(),
        scratch_types=[pltpu.VMEM((SC_CHUNK,), jnp.int32), pltpu.VMEM((SC_CHUNK, w), table.dtype),
                       pltpu.SemaphoreType.DMA])
    def gather(table_hbm, idx_hbm, out_hbm, idx_v, rows_v, sem):
        base = _sc_worker_base(per_worker)

        @pl.loop(0, per_worker // SC_CHUNK)
        def _(c):
            off = pl.multiple_of(base + c * SC_CHUNK, SC_CHUNK)
            pltpu.sync_copy(idx_hbm.at[pl.ds(off, SC_CHUNK)], idx_v)
            pltpu.async_copy(table_hbm.at[idx_v], rows_v, sem).wait()
            pltpu.sync_copy(rows_v, out_hbm.at[pl.ds(off, SC_CHUNK)])

    return gather(table, idx)


def _experts_kernel(blk_exp_ref, n_used_ref, xs_hbm, wg_ref, wu_ref, wd_ref, ys_ref, ring, sems):
    i = pl.program_id(0)
    n_used = n_used_ref[0]
    rb = ring.shape[1]

    def block_copy(blk):
        slot = blk % XS_RING
        return pltpu.make_async_copy(xs_hbm.at[pl.ds(blk * rb, rb), :], ring.at[slot], sems.at[slot])

    @pl.when(i == 0)
    def _():
        for ahead in range(XS_RING - 1):
            @pl.when(ahead < n_used)
            def _():
                block_copy(ahead).start()

    @pl.when(i + (XS_RING - 1) < n_used)
    def _():
        block_copy(i + (XS_RING - 1)).start()

    @pl.when(i < n_used)
    def _():
        block_copy(i).wait()
        xb = _unpack_rows(ring[i % XS_RING]).astype(BF16)
        g = jnp.dot(xb, wg_ref[0].astype(BF16), preferred_element_type=F32)
        u = jnp.dot(xb, wu_ref[0].astype(BF16), preferred_element_type=F32)
        hb = (_silu(g) * u).astype(BF16)
        ys_ref[...] = _pack_rows(jnp.dot(hb, wd_ref[0].astype(BF16), preferred_element_type=F32))


def _row_block(t):
    per_expert = t * TOP_K // N_EXPERTS
    return 1024 if per_expert >= 4 * 1024 else (512 if per_expert >= 4 * 512 else 256)


def _experts(blk_exp, n_used, xs, w_gate, w_up, w_down, row_block):
    n_rows, w = xs.shape
    d = D_MODEL
    n_blocks = n_rows // row_block
    row = lambda i, be, nu: (jnp.minimum(i, nu[0] - 1), 0)
    wsel = lambda i, be, nu: (be[i], 0, 0)
    return pl.pallas_call(
        _experts_kernel,
        grid_spec=pltpu.PrefetchScalarGridSpec(
            num_scalar_prefetch=2,
            grid=(n_blocks,),
            in_specs=[pl.BlockSpec(memory_space=pl.ANY),
                      pl.BlockSpec((1, d, EXPERT_FF), wsel),
                      pl.BlockSpec((1, d, EXPERT_FF), wsel),
                      pl.BlockSpec((1, EXPERT_FF, d), wsel)],
            out_specs=pl.BlockSpec((row_block, w), row),
            scratch_shapes=[pltpu.VMEM((XS_RING, row_block, w), jnp.int32),
                            pltpu.SemaphoreType.DMA((XS_RING,))]),
        out_shape=jax.ShapeDtypeStruct((n_rows, w), jnp.int32),
        compiler_params=_cparams(("arbitrary",)),
        name="experts",
    )(blk_exp, n_used, xs, w_gate, w_up, w_down)


def _combine_kernel(g_hbm, w_ref, h2_ref, x1_ref, mod_ref, wsg_ref, wsu_ref, wsd_ref, out_ref,
                    ring, sems):
    i = pl.program_id(0)
    n_steps = pl.num_programs(0)
    tt = ring.shape[2]

    def tile_copy(step):
        slot = step % XS_RING
        return pltpu.make_async_copy(g_hbm.at[:, pl.ds(step * tt, tt), :], ring.at[slot], sems.at[slot])

    @pl.when(i == 0)
    def _():
        for ahead in range(XS_RING - 1):
            @pl.when(ahead < n_steps)
            def _():
                tile_copy(ahead).start()

    @pl.when(i + (XS_RING - 1) < n_steps)
    def _():
        tile_copy(i + (XS_RING - 1)).start()

    tile_copy(i).wait()
    g_ref = ring.at[i % XS_RING]
    hb = _unpack_rows(h2_ref[...]).astype(BF16)
    g = jnp.dot(hb, wsg_ref[...], preferred_element_type=F32)
    u = jnp.dot(hb, wsu_ref[...], preferred_element_type=F32)
    acc = jnp.dot((_silu(g) * u).astype(BF16), wsd_ref[...], preferred_element_type=F32)
    w = w_ref[...]
    for k in range(TOP_K):
        acc = acc + _unpack_rows(g_ref[k]) * w[:, k:k + 1]
    out_ref[...] = x1_ref[...] + mod_ref[0, 5:6, :] * acc


def _combine(gathered, wts_t, h2p, x1, mod, p, tt, seq):
    t, d = x1.shape
    per_seq = seq // tt
    tok = pl.BlockSpec((tt, d), lambda i: (i, 0))
    const = lambda shape: pl.BlockSpec(shape, lambda i: (0,) * len(shape))
    return pl.pallas_call(
        _combine_kernel,
        grid=(t // tt,),
        in_specs=[pl.BlockSpec(memory_space=pl.ANY),
                  pl.BlockSpec((tt, TOPK_PAD), lambda i: (i, 0)),
                  pl.BlockSpec((tt, PACK_W), lambda i: (i, 0)),
                  tok,
                  pl.BlockSpec((1, 6, d), lambda i: (i // per_seq, 0, 0)),
                  const((d, SHARED_FF)), const((d, SHARED_FF)), const((SHARED_FF, d))],
        out_specs=tok,
        out_shape=jax.ShapeDtypeStruct((t, d), F32),
        scratch_shapes=[pltpu.VMEM((XS_RING, TOP_K, tt, PACK_W), jnp.int32),
                        pltpu.SemaphoreType.DMA((XS_RING,))],
        compiler_params=_cparams(("arbitrary",)),
        name="combine",
    )(gathered, wts_t, h2p, x1, mod, p["ws_gate"], p["ws_up"], p["ws_down"])


def _rope_tables(s):
    half = QK_ROPE_DIM // 2
    inv_freq = ROPE_THETA ** (-jnp.arange(half, dtype=F32) / half)
    ang = jnp.arange(s, dtype=F32)[:, None] * inv_freq[None, :]
    cos, sin = jnp.cos(ang), jnp.sin(ang)
    z = lambda n: jnp.zeros((s, n), F32)
    tab_cos = jnp.concatenate([jnp.ones((s, QK_NOPE_DIM), F32), cos, cos, z(HEAD_PAD - QK_HEAD_DIM)], 1)
    tab_sa = jnp.concatenate([z(QK_NOPE_DIM), -sin, z(HEAD_PAD - QK_NOPE_DIM - half)], 1)
    tab_sb = jnp.concatenate([z(QK_NOPE_DIM + half), sin, z(HEAD_PAD - QK_HEAD_DIM)], 1)
    return dict(rope_cos=tab_cos, rope_sa=tab_sa, rope_sb=tab_sb, rope_cos_t=cos.T, rope_sin_t=sin.T)


def _prep_weights(norm1_w, w_in, pool_w, pool_scale, q_a_norm_w, w_q_b, kv_a_norm_w, w_kv_b,
                  q_norm_w, k_norm_w, w_o, norm2_w, w_router, w_gate, w_up, w_down,
                  ws_gate, ws_up, ws_down):
    d = D_MODEL
    c0 = POOL_WIDTH + Q_LORA_RANK + KV_LORA_RANK
    pad_h = HEAD_PAD - QK_HEAD_DIM
    w_in_p = jnp.concatenate(
        [w_in[:, :c0], jnp.zeros((d, QK_NOPE_DIM), F32), w_in[:, c0:], jnp.zeros((d, pad_h), F32)], 1)
    w_q = jnp.pad(w_q_b.reshape(Q_LORA_RANK, N_HEADS, QK_HEAD_DIM), ((0, 0), (0, 0), (0, pad_h)))
    kv = w_kv_b.reshape(KV_LORA_RANK, N_HEADS, QK_NOPE_DIM + V_HEAD_DIM)
    w_k = jnp.pad(kv[:, :, :QK_NOPE_DIM], ((0, 0), (0, 0), (0, HEAD_PAD - QK_NOPE_DIM)))
    w_v = kv[:, :, QK_NOPE_DIM:]
    w_r_t = w_router.T
    w_r_hi = w_r_t.astype(BF16)
    w_r_lo = (w_r_t - w_r_hi.astype(F32)).astype(BF16)
    q_gain = q_norm_w * (QK_HEAD_DIM ** -0.5 * LOG2E)
    bound = QK_HEAD_DIM * jnp.max(jnp.abs(q_gain)) * jnp.max(jnp.abs(k_norm_w)) * 1.02 + 0.25
    return dict(
        score_bound=bound,
        norm1_w=norm1_w.reshape(1, d), w_in=w_in_p.astype(BF16),
        q_a_norm_w=q_a_norm_w.reshape(1, -1), w_q_t=w_q.reshape(Q_LORA_RANK, -1).T.astype(BF16),
        kv_a_norm_w=kv_a_norm_w.reshape(1, -1), w_k=w_k.reshape(KV_LORA_RANK, -1).astype(BF16),
        w_v_t=w_v.reshape(KV_LORA_RANK, -1).T.astype(BF16),
        q_gain_col=jnp.broadcast_to(jnp.pad(q_gain, (0, pad_h))[:, None], (HEAD_PAD, LANES)),
        k_norm_w=jnp.pad(k_norm_w, (0, pad_h)).reshape(1, HEAD_PAD),
        pool_w=pool_w.astype(BF16), pool_scale=pool_scale.reshape(1, -1), w_o=w_o.astype(BF16),
        norm2_w=norm2_w.reshape(1, d), w_router_hi=w_r_hi, w_router_lo=w_r_lo,
        w_gate=w_gate, w_up=w_up, w_down=w_down,
        ws_gate=ws_gate.astype(BF16), ws_up=ws_up.astype(BF16), ws_down=ws_down.astype(BF16))


def _tile(n, pref):
    return pref if n % pref == 0 else n


def _mixer_and_routing(x, mod, router_bias, p):
    b, s, d = x.shape
    t = b * s
    ts = _tile(s, 512)
    p = dict(p, **_rope_tables(s))

    u, qt, k, vt = _inproj(x, mod, p, ts)
    online = (p["score_bound"] > MAX_UNSHIFTED_SCORE).astype(jnp.int32).reshape(1)
    attn = _attention(online, qt, k, vt, _tile(s, 1024), _tile(s, 2048), _tile(s, 512))
    x1, h2p, logits_t = _postmix(u, attn, x, mod, p, _tile(s, 1024))
    tr = _tile(t, 1024)
    idx, wts, pos, cnt = _router(logits_t, router_bias, tr)

    rb = _row_block(t)
    counts = cnt[:, 0].astype(jnp.int32)
    padded = (counts + rb - 1) // rb * rb
    pad_end = jnp.cumsum(padded)
    dest = _dest_rows(idx, pos, pad_end - padded, _tile(t, 2048))
    n_blocks = -(-t * TOP_K // rb) + N_EXPERTS
    blk_row = jnp.arange(n_blocks, dtype=jnp.int32)[:, None] * rb
    blk_exp = jnp.minimum(jnp.sum((pad_end[None, :] <= blk_row).astype(jnp.int32), axis=1),
                          N_EXPERTS - 1)
    n_used = (pad_end[-1:] // rb).astype(jnp.int32)
    h2f = h2p.reshape(t, PACK_W)
    xs = _sc_scatter_rows(h2f, dest, n_blocks * rb)
    return dict(xs=xs, dest=dest, blk_exp=blk_exp, n_used=n_used, wts_t=wts.T, h2f=h2f,
                x1=x1.reshape(t, d), mod=mod, shape=(b, s, d), row_block=rb)


def _gather_and_combine(st, ys, p):
    b, s, d = st["shape"]
    t = b * s
    gathered = _sc_gather_rows(ys, st["dest"][:TOP_K].reshape(TOP_K * t)).reshape(TOP_K, t, PACK_W)
    out = _combine(gathered, st["wts_t"], st["h2f"], st["x1"], st["mod"], p, _tile(s, 512), s)
    return out.reshape(b, s, d)


def kernel(x_prompt, x_sample, c_prompt, c_sample, w_ada, b_ada, norm1_w, w_in, pool_w, pool_scale,
           q_a_norm_w, w_q_b, kv_a_norm_w, w_kv_b, q_norm_w, k_norm_w, w_o, norm2_w, w_router,
           router_bias, w_gate, w_up, w_down, ws_gate, ws_up, ws_down):
    assert w_ada.shape[0] == 1, "single-layer encoder"
    p = _prep_weights(norm1_w[0], w_in[0], pool_w[0], pool_scale[0], q_a_norm_w[0], w_q_b[0],
                      kv_a_norm_w[0], w_kv_b[0], q_norm_w[0], k_norm_w[0], w_o[0], norm2_w[0],
                      w_router[0], w_gate[0], w_up[0], w_down[0], ws_gate[0], ws_up[0], ws_down[0])
    nb = x_prompt.shape[0]
    c = jnp.concatenate([c_prompt, c_sample], axis=0).astype(F32)
    mod = _adaln(c, w_ada[0], b_ada[0]).reshape(c.shape[0], 6, D_MODEL)
    experts = lambda st, xs: _experts(st["blk_exp"], st["n_used"], xs, p["w_gate"], p["w_up"],
                                      p["w_down"], st["row_block"])

    sp = _mixer_and_routing(x_prompt, mod[:nb], router_bias[0], p)
    dest_p, x_sample = lax.optimization_barrier((sp["dest"], x_sample))
    sp["dest"] = dest_p
    ss = _mixer_and_routing(x_sample, mod[nb:], router_bias[0], p)
    ys_p = experts(sp, sp["xs"])
    ys_p, xs_s = lax.optimization_barrier((ys_p, ss["xs"]))
    ys_s = experts(ss, xs_s)
    y_prompt = _gather_and_combine(sp, ys_p, p)
    y_sample = _gather_and_combine(ss, ys_s, p)
    return (y_prompt, y_sample)
```

```python
import functools

import jax
import jax.numpy as jnp
from jax import lax
from jax.experimental import pallas as pl
from jax.experimental.pallas import tpu as pltpu
from jax.experimental.pallas import tpu_sc as plsc

D_MODEL = 1024
POOL_WIDTH = 512
POOL_WINDOWS = (2, 4, 8, 16)
POOL_GROUP = 128
N_HEADS = 8
V_HEAD_DIM = 64
QK_NOPE_DIM = 64
QK_ROPE_DIM = 32
QK_HEAD_DIM = 96
Q_LORA_RANK = 256
KV_LORA_RANK = 128
ROPE_THETA = 10000.0
N_EXPERTS = 64
TOP_K = 6
N_GROUPS = 8
TOPK_GROUPS = 4
EXPERT_FF = 256
SHARED_FF = 256
ROUTED_SCALE = 2.5
EPS = 1e-6

LANES = 128
HEAD_PAD = 128
HALO = 16
TOPK_PAD = 8
V_AUG = 80
ATTN_HEADS = 4
XS_RING = 3
MAX_UNSHIFTED_SCORE = 40.0
LOG2E = 1.4426950408889634
PACK_W = D_MODEL // 2
SC_CORES = 2
SC_WORKERS = 32
SC_CHUNK = 128
VMEM_LIMIT = 48 * 1024 * 1024

F32 = jnp.float32
BF16 = jnp.bfloat16
QK_DTYPE = jnp.bfloat16


def _cparams(sem):
    return pltpu.CompilerParams(dimension_semantics=sem, vmem_limit_bytes=VMEM_LIMIT)


def _silu(x):
    return x * (1.0 / (1.0 + jnp.exp(-x)))


def _adaln_kernel(c_ref, w_ref, b_ref, o_ref):
    c = c_ref[...]
    o_ref[...] = jnp.dot(_silu(c), w_ref[...], preferred_element_type=F32,
                         precision=lax.Precision.HIGHEST) + b_ref[...]


def _adaln(c, w_ada, b_ada):
    nb, d = c.shape
    n = w_ada.shape[1]
    tn = 1536
    return pl.pallas_call(
        _adaln_kernel,
        grid=(n // tn,),
        in_specs=[pl.BlockSpec((nb, d), lambda j: (0, 0)),
                  pl.BlockSpec((d, tn), lambda j: (0, j)),
                  pl.BlockSpec((1, tn), lambda j: (0, j))],
        out_specs=pl.BlockSpec((nb, tn), lambda j: (0, j)),
        out_shape=jax.ShapeDtypeStruct((nb, n), F32),
        compiler_params=_cparams(("arbitrary",)),
        name="adaln",
    )(c, w_ada, b_ada.reshape(1, n))


def _inproj_kernel(x_ref, mod_ref, n1w_ref, win_ref, qan_ref, wqt_ref, kvan_ref, wk_ref, wvt_ref,
                   qg_ref, kg_ref, cos_ref, sa_ref, sb_ref, cost_ref, sint_ref,
                   u_ref, qt_ref, k_ref, vt_ref):
    ts = x_ref.shape[1]
    half = QK_ROPE_DIM // 2
    x = x_ref[0]
    shift1 = mod_ref[0, 0:1, :]
    gain1 = n1w_ref[...] * (1.0 + mod_ref[0, 1:2, :])
    r = lax.rsqrt(jnp.mean(x * x, axis=-1, keepdims=True) + EPS)
    h = x * r * gain1 + shift1
    z = jnp.dot(h.astype(BF16), win_ref[...], preferred_element_type=F32)
    u_ref[0] = z[:, :POOL_WIDTH].astype(BF16)

    cq = z[:, POOL_WIDTH:POOL_WIDTH + Q_LORA_RANK]
    cqn = cq * lax.rsqrt(jnp.mean(cq * cq, axis=-1, keepdims=True) + EPS) * qan_ref[...]
    qt = jnp.dot(wqt_ref[...], cqn.T.astype(BF16), preferred_element_type=F32)
    reps = ts // LANES
    qg = jnp.concatenate([qg_ref[...]] * reps, axis=1)
    cost = cost_ref[...]
    sint = sint_ref[...]
    spare = jnp.zeros((HEAD_PAD - QK_HEAD_DIM, ts), F32)
    for hd in range(N_HEADS):
        t = qt[hd * HEAD_PAD:(hd + 1) * HEAD_PAD]
        rn = lax.rsqrt(jnp.sum(t * t, axis=0, keepdims=True) * (1.0 / QK_HEAD_DIM) + EPS)
        tn = t * rn * qg
        t1 = tn[QK_NOPE_DIM:QK_NOPE_DIM + half]
        t2 = tn[QK_NOPE_DIM + half:QK_HEAD_DIM]
        out = jnp.concatenate([tn[:QK_NOPE_DIM], t1 * cost - t2 * sint, t1 * sint + t2 * cost, spare],
                              axis=0)
        qt_ref[0, hd * HEAD_PAD:(hd + 1) * HEAD_PAD, :] = out.astype(QK_DTYPE)

    c0 = POOL_WIDTH + Q_LORA_RANK
    ckv = z[:, c0:c0 + KV_LORA_RANK]
    ckvn = ckv * lax.rsqrt(jnp.mean(ckv * ckv, axis=-1, keepdims=True) + EPS) * kvan_ref[...]

    vt = jnp.dot(wvt_ref[...], ckvn.T.astype(BF16), preferred_element_type=F32)
    ones = jnp.ones((V_AUG - V_HEAD_DIM, ts), BF16)
    for hd in range(N_HEADS):
        vt_ref[0, hd * V_AUG:hd * V_AUG + V_HEAD_DIM, :] = (
            vt[hd * V_HEAD_DIM:(hd + 1) * V_HEAD_DIM].astype(BF16))
        vt_ref[0, hd * V_AUG + V_HEAD_DIM:(hd + 1) * V_AUG, :] = ones

    kk = jnp.dot(ckvn.astype(BF16), wk_ref[...], preferred_element_type=F32)
    kpe = z[:, c0 + KV_LORA_RANK:]
    kg = kg_ref[...]
    pe_ssq = jnp.sum(kpe * kpe, axis=-1, keepdims=True)
    pg = kpe * kg
    pe_rot = (pg * cos_ref[...] + pltpu.roll(pg, HEAD_PAD - half, 1) * sa_ref[...]
              + pltpu.roll(pg, half, 1) * sb_ref[...])
    for hd in range(N_HEADS):
        t = kk[:, hd * HEAD_PAD:(hd + 1) * HEAD_PAD]
        ssq = jnp.sum(t * t, axis=-1, keepdims=True) + pe_ssq
        rn = lax.rsqrt(ssq * (1.0 / QK_HEAD_DIM) + EPS)
        k_ref[0, :, hd * HEAD_PAD:(hd + 1) * HEAD_PAD] = ((t * kg + pe_rot) * rn).astype(QK_DTYPE)


def _inproj(x, mod, p, ts):
    b, s, d = x.shape
    qk_w = N_HEADS * HEAD_PAD
    half = QK_ROPE_DIM // 2
    const = lambda shape: pl.BlockSpec(shape, lambda bi, i: (0,) * len(shape))
    row_tab = pl.BlockSpec((ts, HEAD_PAD), lambda bi, i: (i, 0))
    col_tab = pl.BlockSpec((half, ts), lambda bi, i: (0, i))
    return pl.pallas_call(
        _inproj_kernel,
        grid=(b, s // ts),
        in_specs=[pl.BlockSpec((1, ts, d), lambda bi, i: (bi, i, 0)),
                  pl.BlockSpec((1, 6, d), lambda bi, i: (bi, 0, 0)),
                  const((1, d)), const((d, d)), const((1, Q_LORA_RANK)),
                  const((qk_w, Q_LORA_RANK)), const((1, KV_LORA_RANK)),
                  const((KV_LORA_RANK, qk_w)), const((N_HEADS * V_HEAD_DIM, KV_LORA_RANK)),
                  const((HEAD_PAD, LANES)), const((1, HEAD_PAD)),
                  row_tab, row_tab, row_tab, col_tab, col_tab],
        out_specs=[pl.BlockSpec((1, ts, POOL_WIDTH), lambda bi, i: (bi, i, 0)),
                   pl.BlockSpec((1, qk_w, ts), lambda bi, i: (bi, 0, i)),
                   pl.BlockSpec((1, ts, qk_w), lambda bi, i: (bi, i, 0)),
                   pl.BlockSpec((1, N_HEADS * V_AUG, ts), lambda bi, i: (bi, 0, i))],
        out_shape=[jax.ShapeDtypeStruct((b, s, POOL_WIDTH), BF16),
                   jax.ShapeDtypeStruct((b, qk_w, s), QK_DTYPE),
                   jax.ShapeDtypeStruct((b, s, qk_w), QK_DTYPE),
                   jax.ShapeDtypeStruct((b, N_HEADS * V_AUG, s), BF16)],
        compiler_params=_cparams(("parallel", "parallel")),
        name="inproj",
    )(x, mod, p["norm1_w"], p["w_in"], p["q_a_norm_w"], p["w_q_t"], p["kv_a_norm_w"], p["w_k"],
      p["w_v_t"], p["q_gain_col"], p["k_norm_w"],
      p["rope_cos"], p["rope_sa"], p["rope_sb"], p["rope_cos_t"], p["rope_sin_t"])


def _attn_kernel(online_ref, qt_ref, k_ref, vt_ref, o_ref, *scratch, tk, tk_online):
    accs, m_ref = scratch[:ATTN_HEADS], scratch[ATTN_HEADS]
    s_len = k_ref.shape[1]
    for acc in accs:
        acc[...] = jnp.zeros_like(acc)
    heads = range(ATTN_HEADS)
    qs = [qt_ref[0, h * HEAD_PAD:(h + 1) * HEAD_PAD, :] for h in heads]

    @pl.when(online_ref[0] == 0)
    def _():
        def body(c, _):
            off = pl.multiple_of(c * tk, tk)
            ks = k_ref[0, pl.ds(off, tk), :]
            vts = vt_ref[0, :, pl.ds(off, tk)]
            for h in heads:
                s = jnp.dot(ks[:, h * HEAD_PAD:(h + 1) * HEAD_PAD], qs[h], preferred_element_type=F32)
                accs[h][...] += jnp.dot(vts[h * V_AUG:(h + 1) * V_AUG], jnp.exp2(s).astype(BF16),
                                        preferred_element_type=F32)
            return 0

        lax.fori_loop(0, s_len // tk, body, 0)

    @pl.when(online_ref[0] != 0)
    def _():
        m_ref[...] = jnp.full_like(m_ref, -jnp.inf)

        def body(c, _):
            off = pl.multiple_of(c * tk_online, tk_online)
            ks = k_ref[0, pl.ds(off, tk_online), :]
            vts = vt_ref[0, :, pl.ds(off, tk_online)]
            for h in heads:
                s = jnp.dot(ks[:, h * HEAD_PAD:(h + 1) * HEAD_PAD], qs[h], preferred_element_type=F32)
                m_old = m_ref[h:h + 1, :]
                m_new = jnp.maximum(m_old, jnp.max(s, axis=0, keepdims=True))
                p = jnp.exp2(s - m_new).astype(BF16)
                accs[h][...] = (accs[h][...] * jnp.exp2(m_old - m_new)
                                + jnp.dot(vts[h * V_AUG:(h + 1) * V_AUG], p, preferred_element_type=F32))
                m_ref[h:h + 1, :] = m_new
            return 0

        lax.fori_loop(0, s_len // tk_online, body, 0)

    outs = [acc[:V_HEAD_DIM] / acc[V_HEAD_DIM:V_HEAD_DIM + 1] for acc in accs]
    o_ref[0] = jnp.concatenate(outs, axis=0).T.astype(BF16)


def _attention(online, qt, k, vt, tq, tk, tk_online):
    b, _, s = qt.shape
    hps = ATTN_HEADS
    return pl.pallas_call(
        functools.partial(_attn_kernel, tk=tk, tk_online=tk_online),
        grid_spec=pltpu.PrefetchScalarGridSpec(
            num_scalar_prefetch=1,
            grid=(b, N_HEADS // hps, s // tq),
            in_specs=[pl.BlockSpec((1, hps * HEAD_PAD, tq), lambda bi, j, i, on: (bi, j, i)),
                      pl.BlockSpec((1, s, hps * HEAD_PAD), lambda bi, j, i, on: (bi, 0, j)),
                      pl.BlockSpec((1, hps * V_AUG, s), lambda bi, j, i, on: (bi, j, 0))],
            out_specs=pl.BlockSpec((1, tq, hps * V_HEAD_DIM), lambda bi, j, i, on: (bi, i, j)),
            scratch_shapes=[pltpu.VMEM((V_AUG, tq), F32)] * hps + [pltpu.VMEM((8, tq), F32)]),
        out_shape=jax.ShapeDtypeStruct((b, s, N_HEADS * V_HEAD_DIM), BF16),
        compiler_params=_cparams(("parallel", "parallel", "arbitrary")),
        name="attn",
    )(online, qt, k, vt)


def _postmix_kernel(u_ref, up_ref, un_ref, a_ref, x_ref, mod_ref, pw_ref, ps_ref, wo_ref, n2w_ref,
                    wrh_ref, wrl_ref, x1_ref, h2_ref, lg_ref, *, seq):
    i = pl.program_id(1)
    ts = u_ref.shape[1]
    ext_rows = ts + 2 * HALO
    ext = jnp.concatenate([up_ref[0], u_ref[0], un_ref[0]], axis=0).astype(F32)
    pos = i * ts - HALO + lax.broadcasted_iota(jnp.int32, (ext_rows, 1), 0)
    ext = jnp.where((pos >= 0) & (pos < seq), ext, 0.0)
    p = i * ts + lax.broadcasted_iota(jnp.int32, (ts, 1), 0)

    outs = []
    for g, w in enumerate(POOL_WINDOWS):
        left = w // 2
        right = w - 1 - left
        t = ext[:, g * POOL_GROUP:(g + 1) * POOL_GROUP]
        step = 1
        while step < w:
            t = t + pltpu.roll(t, ext_rows - step, 0)
            step *= 2
        win = pltpu.roll(t, left, 0)[HALO:HALO + ts]
        cnt = (jnp.minimum(p + right + 1, seq) - jnp.maximum(p - left, 0)).astype(F32)
        d = win * (1.0 / cnt) - ext[HALO:HALO + ts, g * POOL_GROUP:(g + 1) * POOL_GROUP]
        outs.append(jnp.dot(d.astype(BF16), pw_ref[g], preferred_element_type=F32))
    pool = (jnp.concatenate(outs, axis=-1) * ps_ref[...]).astype(BF16)

    mix = (jnp.dot(pool, wo_ref[:POOL_WIDTH, :], preferred_element_type=F32)
           + jnp.dot(a_ref[0], wo_ref[POOL_WIDTH:, :], preferred_element_type=F32))
    x1 = x_ref[0] + mod_ref[0, 2:3, :] * mix
    x1_ref[0] = x1
    r = lax.rsqrt(jnp.mean(x1 * x1, axis=-1, keepdims=True) + EPS)
    gain2 = n2w_ref[...] * (1.0 + mod_ref[0, 4:5, :])
    h2 = x1 * r * gain2 + mod_ref[0, 3:4, :]
    h2_ref[0] = _pack_rows(h2)
    hi = h2.astype(BF16)
    lo = (h2 - hi.astype(F32)).astype(BF16)
    nt = (((1,), (1,)), ((), ()))
    lg_ref[...] = (lax.dot_general(wrh_ref[...], hi, nt, preferred_element_type=F32)
                   + lax.dot_general(wrl_ref[...], hi, nt, preferred_element_type=F32)
                   + lax.dot_general(wrh_ref[...], lo, nt, preferred_element_type=F32))


def _postmix(u, attn, x, mod, p, ts):
    b, s, d = x.shape
    nt = s // ts
    hb = ts // HALO
    const = lambda shape: pl.BlockSpec(shape, lambda bi, i: (0,) * len(shape))
    return pl.pallas_call(
        functools.partial(_postmix_kernel, seq=s),
        grid=(b, nt),
        in_specs=[pl.BlockSpec((1, ts, POOL_WIDTH), lambda bi, i: (bi, i, 0)),
                  pl.BlockSpec((1, HALO, POOL_WIDTH), lambda bi, i: (bi, jnp.maximum(i * hb - 1, 0), 0)),
                  pl.BlockSpec((1, HALO, POOL_WIDTH),
                               lambda bi, i: (bi, jnp.minimum((i + 1) * hb, s // HALO - 1), 0)),
                  pl.BlockSpec((1, ts, POOL_WIDTH), lambda bi, i: (bi, i, 0)),
                  pl.BlockSpec((1, ts, d), lambda bi, i: (bi, i, 0)),
                  pl.BlockSpec((1, 6, d), lambda bi, i: (bi, 0, 0)),
                  const((len(POOL_WINDOWS), POOL_GROUP, POOL_GROUP)), const((1, POOL_WIDTH)),
                  const((d, d)), const((1, d)), const((N_EXPERTS, d)), const((N_EXPERTS, d))],
        out_specs=[pl.BlockSpec((1, ts, d), lambda bi, i: (bi, i, 0)),
                   pl.BlockSpec((1, ts, PACK_W), lambda bi, i: (bi, i, 0)),
                   pl.BlockSpec((N_EXPERTS, ts), lambda bi, i: (0, bi * nt + i))],
        out_shape=[jax.ShapeDtypeStruct((b, s, d), F32),
                   jax.ShapeDtypeStruct((b, s, PACK_W), jnp.int32),
                   jax.ShapeDtypeStruct((N_EXPERTS, b * s), F32)],
        compiler_params=_cparams(("parallel", "parallel")),
        name="postmix",
    )(u, u, u, attn, x, mod, p["pool_w"], p["pool_scale"], p["w_o"], p["norm2_w"],
      p["w_router_hi"], p["w_router_lo"])


def _router_kernel(lg_ref, bias_ref, tri_ref, idx_ref, wts_ref, pos_ref, cnt_ref, carry_ref):
    @pl.when(pl.program_id(0) == 0)
    def _():
        carry_ref[...] = jnp.zeros_like(carry_ref)

    ts = lg_ref.shape[1]
    gsz = N_EXPERTS // N_GROUPS
    ninf = -jnp.inf
    scores = 1.0 / (1.0 + jnp.exp(-lg_ref[...]))
    choice = scores + bias_ref[...]
    sub = lax.broadcasted_iota(jnp.int32, (gsz, ts), 0)

    gs_rows = []
    for g in range(N_GROUPS):
        grp = choice[g * gsz:(g + 1) * gsz]
        m1 = jnp.max(grp, axis=0, keepdims=True)
        i1 = jnp.min(jnp.where(grp == m1, sub, gsz), axis=0, keepdims=True)
        m2 = jnp.max(jnp.where(sub == i1, ninf, grp), axis=0, keepdims=True)
        gs_rows.append(m1 + m2)
    gs = jnp.concatenate(gs_rows, axis=0)

    rank = jnp.zeros((N_GROUPS, ts), jnp.int32)
    for g in range(N_GROUPS):
        row = gs[g:g + 1]
        beats = (row > gs) | ((row == gs) & (sub > g))
        rank = rank + beats.astype(jnp.int32)
    gsel = rank < TOPK_GROUPS

    masked = jnp.concatenate(
        [jnp.where(gsel[g:g + 1], choice[g * gsz:(g + 1) * gsz], ninf) for g in range(N_GROUPS)],
        axis=0)
    eio = lax.broadcasted_iota(jnp.int32, (N_EXPERTS, ts), 0)
    idx_rows, w_rows, hits = [], [], []
    for _ in range(TOP_K):
        m = jnp.max(masked, axis=0, keepdims=True)
        i = jnp.min(jnp.where(masked == m, eio, N_EXPERTS), axis=0, keepdims=True)
        hit = eio == i
        w_rows.append(jnp.sum(jnp.where(hit, scores, 0.0), axis=0, keepdims=True))
        masked = jnp.where(hit, ninf, masked)
        idx_rows.append(i)
        hits.append(hit)

    wsum = functools.reduce(lambda a, c: a + c, w_rows)
    pad_i = [jnp.zeros((1, ts), jnp.int32)] * (TOPK_PAD - TOP_K)
    pad_f = [jnp.zeros((1, ts), F32)] * (TOPK_PAD - TOP_K)
    idx_ref[...] = jnp.concatenate(idx_rows + pad_i, axis=0)
    wts_ref[...] = jnp.concatenate([w / wsum * ROUTED_SCALE for w in w_rows] + pad_f, axis=0)

    sel = functools.reduce(lambda a, c: a | c, hits)
    onehot = jnp.where(sel, 1.0, 0.0).astype(BF16)
    run = jnp.dot(onehot, tri_ref[...], preferred_element_type=F32) + carry_ref[:, 0:1]
    pos_rows = [jnp.sum(jnp.where(h, run - 1.0, 0.0), axis=0, keepdims=True).astype(jnp.int32)
                for h in hits]
    pos_ref[...] = jnp.concatenate(pos_rows + pad_i, axis=0)
    total = run[:, ts - 1:ts]
    carry_ref[...] = jnp.broadcast_to(total, carry_ref.shape)
    cnt_ref[...] = jnp.broadcast_to(total, cnt_ref.shape)


def _router(logits_t, router_bias, ts):
    t = logits_t.shape[1]
    tri = jnp.triu(jnp.ones((ts, ts), BF16))
    tok = pl.BlockSpec((TOPK_PAD, ts), lambda i: (0, i))
    return pl.pallas_call(
        _router_kernel,
        grid=(t // ts,),
        in_specs=[pl.BlockSpec((N_EXPERTS, ts), lambda i: (0, i)),
                  pl.BlockSpec((N_EXPERTS, 1), lambda i: (0, 0)),
                  pl.BlockSpec((ts, ts), lambda i: (0, 0))],
        out_specs=[tok, tok, tok, pl.BlockSpec((N_EXPERTS, LANES), lambda i: (0, 0))],
        out_shape=[jax.ShapeDtypeStruct((TOPK_PAD, t), jnp.int32),
                   jax.ShapeDtypeStruct((TOPK_PAD, t), F32),
                   jax.ShapeDtypeStruct((TOPK_PAD, t), jnp.int32),
                   jax.ShapeDtypeStruct((N_EXPERTS, LANES), F32)],
        scratch_shapes=[pltpu.VMEM((N_EXPERTS, LANES), F32)],
        compiler_params=_cparams(("arbitrary",)),
        name="router",
    )(logits_t, router_bias.reshape(N_EXPERTS, 1), tri)


def _dest_kernel(idx_ref, pos_ref, start_ref, dest_ref):
    ts = idx_ref.shape[1]
    eio = lax.broadcasted_iota(jnp.int32, (N_EXPERTS, ts), 0)
    start = start_ref[...]
    rows = [pos_ref[k:k + 1, :]
            + jnp.sum(jnp.where(eio == idx_ref[k:k + 1, :], start, 0), axis=0, keepdims=True)
            for k in range(TOPK_PAD)]
    dest_ref[...] = jnp.concatenate(rows, axis=0)


def _dest_rows(idx, pos, pad_start, ts):
    t = idx.shape[1]
    tok = pl.BlockSpec((TOPK_PAD, ts), lambda i: (0, i))
    return pl.pallas_call(
        _dest_kernel,
        grid=(t // ts,),
        in_specs=[tok, tok, pl.BlockSpec((N_EXPERTS, 1), lambda i: (0, 0))],
        out_specs=tok,
        out_shape=jax.ShapeDtypeStruct((TOPK_PAD, t), jnp.int32),
        compiler_params=_cparams(("parallel",)),
        name="dest_rows",
    )(idx, pos, pad_start.reshape(N_EXPERTS, 1))


def _pack_rows(x):
    bits = lax.bitcast_convert_type(x.astype(BF16).astype(F32), jnp.int32)
    return bits[:, :PACK_W] | lax.shift_right_logical(bits[:, PACK_W:], 16)


def _unpack_rows(words):
    hi = lax.bitcast_convert_type(words & jnp.int32(-65536), F32)
    lo = lax.bitcast_convert_type(lax.shift_left(words, 16), F32)
    return jnp.concatenate([hi, lo], axis=1)


def _sc_mesh():
    return plsc.VectorSubcoreMesh(core_axis_name="core", subcore_axis_name="subcore")


def _sc_worker_base(rows_per_worker):
    return (lax.axis_index("subcore") * SC_CORES + lax.axis_index("core")) * rows_per_worker


def _sc_scatter_rows(x, dest, n_rows):
    t, w = x.shape
    per_worker = t // SC_WORKERS
    assert per_worker * SC_WORKERS == t and per_worker % SC_CHUNK == 0

    @functools.partial(
        pl.kernel, out_type=jax.ShapeDtypeStruct((n_rows, w), x.dtype), mesh=_sc_mesh(),
        scratch_types=[pltpu.VMEM((TOPK_PAD, SC_CHUNK), jnp.int32), pltpu.VMEM((SC_CHUNK, w), x.dtype),
                       pltpu.SemaphoreType.DMA])
    def scatter(x_hbm, dest_hbm, out_hbm, idx_v, rows_v, sem):
        base = _sc_worker_base(per_worker)

        @pl.loop(0, per_worker // SC_CHUNK)
        def _(c):
            off = pl.multiple_of(base + c * SC_CHUNK, SC_CHUNK)
            pltpu.sync_copy(dest_hbm.at[:, pl.ds(off, SC_CHUNK)], idx_v)
            pltpu.sync_copy(x_hbm.at[pl.ds(off, SC_CHUNK)], rows_v)
            copies = [pltpu.async_copy(rows_v, out_hbm.at[idx_v.at[k]], sem) for k in range(TOP_K)]
            for cp in copies:
                cp.wait()

    return scatter(x, dest)


def _sc_gather_rows(table, idx):
    m = idx.shape[0]
    w = table.shape[1]
    per_worker = m // SC_WORKERS
    half = SC_CHUNK // 2
    assert per_worker * SC_WORKERS == m and per_worker % SC_CHUNK == 0

    @functools.partial(
        pl.kernel, out_type=jax.ShapeDtypeStruct((m, w), table.dtype), mesh=_sc_mesh(),
        scratch_types=[pltpu.VMEM((SC_CHUNK,), jnp.int32), pltpu.VMEM((half, w), table.dtype),
                       pltpu.VMEM((half, w), table.dtype), pltpu.SemaphoreType.DMA,
                       pltpu.SemaphoreType.DMA, pltpu.SemaphoreType.DMA])
    def gather(table_hbm, idx_hbm, out_hbm, idx_v, rows_a, rows_b, sem_a, sem_b, sem_w):
        base = _sc_worker_base(per_worker)

        @pl.loop(0, per_worker // SC_CHUNK)
        def _(c):
            off = pl.multiple_of(base + c * SC_CHUNK, SC_CHUNK)
            pltpu.sync_copy(idx_hbm.at[pl.ds(off, SC_CHUNK)], idx_v)
            ga = pltpu.async_copy(table_hbm.at[idx_v.at[pl.ds(0, half)]], rows_a, sem_a)
            gb = pltpu.async_copy(table_hbm.at[idx_v.at[pl.ds(half, half)]], rows_b, sem_b)
            ga.wait()
            wa = pltpu.async_copy(rows_a, out_hbm.at[pl.ds(off, half)], sem_w)
            gb.wait()
            wb = pltpu.async_copy(rows_b, out_hbm.at[pl.ds(off + half, half)], sem_w)
            wa.wait()
            wb.wait()

    return gather(table, idx)


def _experts_kernel(blk_exp_ref, n_used_ref, xs_hbm, wg_ref, wu_ref, wd_ref, ys_ref, ring, sems):
    i = pl.program_id(0)
    n_used = n_used_ref[0]
    rb = ring.shape[1]

    def block_copy(blk):
        slot = blk % XS_RING
        return pltpu.make_async_copy(xs_hbm.at[pl.ds(blk * rb, rb), :], ring.at[slot], sems.at[slot])

    @pl.when(i == 0)
    def _():
        for ahead in range(XS_RING - 1):
            @pl.when(ahead < n_used)
            def _():
                block_copy(ahead).start()

    @pl.when(i + (XS_RING - 1) < n_used)
    def _():
        block_copy(i + (XS_RING - 1)).start()

    @pl.when(i < n_used)
    def _():
        block_copy(i).wait()
        xb = _unpack_rows(ring[i % XS_RING]).astype(BF16)
        g = jnp.dot(xb, wg_ref[0].astype(BF16), preferred_element_type=F32)
        u = jnp.dot(xb, wu_ref[0].astype(BF16), preferred_element_type=F32)
        hb = (_silu(g) * u).astype(BF16)
        ys_ref[...] = _pack_rows(jnp.dot(hb, wd_ref[0].astype(BF16), preferred_element_type=F32))


def _row_block(t):
    per_expert = t * TOP_K // N_EXPERTS
    return 1024 if per_expert >= 4 * 1024 else (512 if per_expert >= 4 * 512 else 256)


def _experts(blk_exp, n_used, xs, w_gate, w_up, w_down, row_block):
    n_rows, w = xs.shape
    d = D_MODEL
    n_blocks = n_rows // row_block
    row = lambda i, be, nu: (jnp.minimum(i, nu[0] - 1), 0)
    wsel = lambda i, be, nu: (be[i], 0, 0)
    return pl.pallas_call(
        _experts_kernel,
        grid_spec=pltpu.PrefetchScalarGridSpec(
            num_scalar_prefetch=2,
            grid=(n_blocks,),
            in_specs=[pl.BlockSpec(memory_space=pl.ANY),
                      pl.BlockSpec((1, d, EXPERT_FF), wsel),
                      pl.BlockSpec((1, d, EXPERT_FF), wsel),
                      pl.BlockSpec((1, EXPERT_FF, d), wsel)],
            out_specs=pl.BlockSpec((row_block, w), row),
            scratch_shapes=[pltpu.VMEM((XS_RING, row_block, w), jnp.int32),
                            pltpu.SemaphoreType.DMA((XS_RING,))]),
        out_shape=jax.ShapeDtypeStruct((n_rows, w), jnp.int32),
        compiler_params=_cparams(("arbitrary",)),
        name="experts",
    )(blk_exp, n_used, xs, w_gate, w_up, w_down)


def _combine_kernel(g_hbm, w_ref, h2_ref, x1_ref, mod_ref, wsg_ref, wsu_ref, wsd_ref, out_ref,
                    ring, sems):
    i = pl.program_id(0)
    n_steps = pl.num_programs(0)
    tt = ring.shape[2]

    def tile_copy(step):
        slot = step % XS_RING
        return pltpu.make_async_copy(g_hbm.at[:, pl.ds(step * tt, tt), :], ring.at[slot], sems.at[slot])

    @pl.when(i == 0)
    def _():
        for ahead in range(XS_RING - 1):
            @pl.when(ahead < n_steps)
            def _():
                tile_copy(ahead).start()

    @pl.when(i + (XS_RING - 1) < n_steps)
    def _():
        tile_copy(i + (XS_RING - 1)).start()

    tile_copy(i).wait()
    g_ref = ring.at[i % XS_RING]
    hb = _unpack_rows(h2_ref[...]).astype(BF16)
    g = jnp.dot(hb, wsg_ref[...], preferred_element_type=F32)
    u = jnp.dot(hb, wsu_ref[...], preferred_element_type=F32)
    acc = jnp.dot((_silu(g) * u).astype(BF16), wsd_ref[...], preferred_element_type=F32)
    w = w_ref[...]
    for k in range(TOP_K):
        acc = acc + _unpack_rows(g_ref[k]) * w[:, k:k + 1]
    out_ref[...] = x1_ref[...] + mod_ref[0, 5:6, :] * acc


def _combine(gathered, wts_t, h2p, x1, mod, p, tt, seq):
    t, d = x1.shape
    per_seq = seq // tt
    tok = pl.BlockSpec((tt, d), lambda i: (i, 0))
    const = lambda shape: pl.BlockSpec(shape, lambda i: (0,) * len(shape))
    return pl.pallas_call(
        _combine_kernel,
        grid=(t // tt,),
        in_specs=[pl.BlockSpec(memory_space=pl.ANY),
                  pl.BlockSpec((tt, TOPK_PAD), lambda i: (i, 0)),
                  pl.BlockSpec((tt, PACK_W), lambda i: (i, 0)),
                  tok,
                  pl.BlockSpec((1, 6, d), lambda i: (i // per_seq, 0, 0)),
                  const((d, SHARED_FF)), const((d, SHARED_FF)), const((SHARED_FF, d))],
        out_specs=tok,
        out_shape=jax.ShapeDtypeStruct((t, d), F32),
        scratch_shapes=[pltpu.VMEM((XS_RING, TOP_K, tt, PACK_W), jnp.int32),
                        pltpu.SemaphoreType.DMA((XS_RING,))],
        compiler_params=_cparams(("arbitrary",)),
        name="combine",
    )(gathered, wts_t, h2p, x1, mod, p["ws_gate"], p["ws_up"], p["ws_down"])


def _rope_tables(s):
    half = QK_ROPE_DIM // 2
    inv_freq = ROPE_THETA ** (-jnp.arange(half, dtype=F32) / half)
    ang = jnp.arange(s, dtype=F32)[:, None] * inv_freq[None, :]
    cos, sin = jnp.cos(ang), jnp.sin(ang)
    z = lambda n: jnp.zeros((s, n), F32)
    tab_cos = jnp.concatenate([jnp.ones((s, QK_NOPE_DIM), F32), cos, cos, z(HEAD_PAD - QK_HEAD_DIM)], 1)
    tab_sa = jnp.concatenate([z(QK_NOPE_DIM), -sin, z(HEAD_PAD - QK_NOPE_DIM - half)], 1)
    tab_sb = jnp.concatenate([z(QK_NOPE_DIM + half), sin, z(HEAD_PAD - QK_HEAD_DIM)], 1)
    return dict(rope_cos=tab_cos, rope_sa=tab_sa, rope_sb=tab_sb, rope_cos_t=cos.T, rope_sin_t=sin.T)


def _prep_weights(norm1_w, w_in, pool_w, pool_scale, q_a_norm_w, w_q_b, kv_a_norm_w, w_kv_b,
                  q_norm_w, k_norm_w, w_o, norm2_w, w_router, w_gate, w_up, w_down,
                  ws_gate, ws_up, ws_down):
    d = D_MODEL
    c0 = POOL_WIDTH + Q_LORA_RANK + KV_LORA_RANK
    pad_h = HEAD_PAD - QK_HEAD_DIM
    w_in_p = jnp.concatenate(
        [w_in[:, :c0], jnp.zeros((d, QK_NOPE_DIM), F32), w_in[:, c0:], jnp.zeros((d, pad_h), F32)], 1)
    w_q = jnp.pad(w_q_b.reshape(Q_LORA_RANK, N_HEADS, QK_HEAD_DIM), ((0, 0), (0, 0), (0, pad_h)))
    kv = w_kv_b.reshape(KV_LORA_RANK, N_HEADS, QK_NOPE_DIM + V_HEAD_DIM)
    w_k = jnp.pad(kv[:, :, :QK_NOPE_DIM], ((0, 0), (0, 0), (0, HEAD_PAD - QK_NOPE_DIM)))
    w_v = kv[:, :, QK_NOPE_DIM:]
    w_r_t = w_router.T
    w_r_hi = w_r_t.astype(BF16)
    w_r_lo = (w_r_t - w_r_hi.astype(F32)).astype(BF16)
    q_gain = q_norm_w * (QK_HEAD_DIM ** -0.5 * LOG2E)
    bound = QK_HEAD_DIM * jnp.max(jnp.abs(q_gain)) * jnp.max(jnp.abs(k_norm_w)) * 1.02 + 0.25
    return dict(
        score_bound=bound,
        norm1_w=norm1_w.reshape(1, d), w_in=w_in_p.astype(BF16),
        q_a_norm_w=q_a_norm_w.reshape(1, -1), w_q_t=w_q.reshape(Q_LORA_RANK, -1).T.astype(BF16),
        kv_a_norm_w=kv_a_norm_w.reshape(1, -1), w_k=w_k.reshape(KV_LORA_RANK, -1).astype(BF16),
        w_v_t=w_v.reshape(KV_LORA_RANK, -1).T.astype(BF16),
        q_gain_col=jnp.broadcast_to(jnp.pad(q_gain, (0, pad_h))[:, None], (HEAD_PAD, LANES)),
        k_norm_w=jnp.pad(k_norm_w, (0, pad_h)).reshape(1, HEAD_PAD),
        pool_w=pool_w.astype(BF16), pool_scale=pool_scale.reshape(1, -1), w_o=w_o.astype(BF16),
        norm2_w=norm2_w.reshape(1, d), w_router_hi=w_r_hi, w_router_lo=w_r_lo,
        w_gate=w_gate, w_up=w_up, w_down=w_down,
        ws_gate=ws_gate.astype(BF16), ws_up=ws_up.astype(BF16), ws_down=ws_down.astype(BF16))


def _tile(n, pref):
    return pref if n % pref == 0 else n


def _mixer_and_routing(x, mod, router_bias, p):
    b, s, d = x.shape
    t = b * s
    ts = _tile(s, 512)
    p = dict(p, **_rope_tables(s))

    u, qt, k, vt = _inproj(x, mod, p, ts)
    online = (p["score_bound"] > MAX_UNSHIFTED_SCORE).astype(jnp.int32).reshape(1)
    attn = _attention(online, qt, k, vt, _tile(s, 1024), _tile(s, 2048), _tile(s, 512))
    x1, h2p, logits_t = _postmix(u, attn, x, mod, p, _tile(s, 1024))
    tr = _tile(t, 1024)
    idx, wts, pos, cnt = _router(logits_t, router_bias, tr)

    rb = _row_block(t)
    counts = cnt[:, 0].astype(jnp.int32)
    padded = (counts + rb - 1) // rb * rb
    pad_end = jnp.cumsum(padded)
    dest = _dest_rows(idx, pos, pad_end - padded, _tile(t, 2048))
    n_blocks = -(-t * TOP_K // rb) + N_EXPERTS
    blk_row = jnp.arange(n_blocks, dtype=jnp.int32)[:, None] * rb
    blk_exp = jnp.minimum(jnp.sum((pad_end[None, :] <= blk_row).astype(jnp.int32), axis=1),
                          N_EXPERTS - 1)
    n_used = (pad_end[-1:] // rb).astype(jnp.int32)
    h2f = h2p.reshape(t, PACK_W)
    xs = _sc_scatter_rows(h2f, dest, n_blocks * rb)
    return dict(xs=xs, dest=dest, blk_exp=blk_exp, n_used=n_used, wts_t=wts.T, h2f=h2f,
                x1=x1.reshape(t, d), mod=mod, shape=(b, s, d), row_block=rb)


def _gather_and_combine(st, ys, p):
    b, s, d = st["shape"]
    t = b * s
    gathered = _sc_gather_rows(ys, st["dest"][:TOP_K].reshape(TOP_K * t)).reshape(TOP_K, t, PACK_W)
    out = _combine(gathered, st["wts_t"], st["h2f"], st["x1"], st["mod"], p, _tile(s, 512), s)
    return out.reshape(b, s, d)


def kernel(x_prompt, x_sample, c_prompt, c_sample, w_ada, b_ada, norm1_w, w_in, pool_w, pool_scale,
           q_a_norm_w, w_q_b, kv_a_norm_w, w_kv_b, q_norm_w, k_norm_w, w_o, norm2_w, w_router,
           router_bias, w_gate, w_up, w_down, ws_gate, ws_up, ws_down):
    assert w_ada.shape[0] == 1, "single-layer encoder"
    p = _prep_weights(norm1_w[0], w_in[0], pool_w[0], pool_scale[0], q_a_norm_w[0], w_q_b[0],
                      kv_a_norm_w[0], w_kv_b[0], q_norm_w[0], k_norm_w[0], w_o[0], norm2_w[0],
                      w_router[0], w_gate[0], w_up[0], w_down[0], ws_gate[0], ws_up[0], ws_down[0])
    nb = x_prompt.shape[0]
    c = jnp.concatenate([c_prompt, c_sample], axis=0).astype(F32)
    mod = _adaln(c, w_ada[0], b_ada[0]).reshape(c.shape[0], 6, D_MODEL)
    experts = lambda st, xs: _experts(st["blk_exp"], st["n_used"], xs, p["w_gate"], p["w_up"],
                                      p["w_down"], st["row_block"])

    sp = _mixer_and_routing(x_prompt, mod[:nb], router_bias[0], p)
    dest_p, x_sample = lax.optimization_barrier((sp["dest"], x_sample))
    sp["dest"] = dest_p
    ss = _mixer_and_routing(x_sample, mod[nb:], router_bias[0], p)
    ys_p = experts(sp, sp["xs"])
    ys_p, xs_s = lax.optimization_barrier((ys_p, ss["xs"]))
    ys_s = experts(ss, xs_s)
    y_prompt = _gather_and_combine(sp, ys_p, p)
    y_sample = _gather_and_combine(ss, ys_s, p)
    return (y_prompt, y_sample)
```
